```python
import math
import jax
import jax.numpy as jnp
from jax import lax
import numpy as np

D_MODEL = 1024
BATCH = 16
SEQ = 256
DEPTH = 2
DEC_BATCH = 2
DEC_SEQ = 4096
PAST_LEN = 512

GRID_W = 64
HD = 64
SCALE = 1.0 / math.sqrt(HD)
BLK = 128
H_A = 8
KV_A = 2
G_A = H_A // KV_A
WINDOW = 128
D_A = H_A * HD
D_LRU = 512
LRU_BLOCKS = 8
LRU_BW = D_LRU // LRU_BLOCKS
CONV_W = 4
LRU_C = 8.0
H_C = 4
DV_C = 2 * HD
D_C = H_C * DV_C
N_BRANCH = 3
ROPE_BASE = 10000.0
ROPE_FREQS = HD // 4
EPS = 1e-6
SPLIT_SIZES = (D_A, KV_A * HD, KV_A * HD, D_A,
               D_LRU, D_LRU,
               H_C * 2 * HD, H_C * 2 * HD, D_C, D_C,
               N_BRANCH * D_MODEL)
IN_COLS = sum(SPLIT_SIZES)

kernel_name = "hybrid_dit_swa_rglru_diffattn_step"


def rmsnorm(x, g):
    xf = x.astype(jnp.float32)
    y = xf * lax.rsqrt(jnp.mean(xf * xf, axis=-1, keepdims=True) + EPS)
    return (y * g.astype(jnp.float32)).astype(x.dtype)


def split_cols(p):
    idx = [int(v) for v in np.cumsum(SPLIT_SIZES)[:-1]]
    return jnp.split(p, idx, axis=-1)


def modulation(cvec, mod_w, mod_b):
    m = jax.nn.silu(cvec) @ mod_w + mod_b
    m = m.reshape(-1, 1, 3 * D_MODEL)
    return jnp.split(m, 3, axis=-1)


def axial_rope_tables(T):
    n_rows = T // GRID_W
    row = jnp.repeat(jnp.arange(n_rows), GRID_W).astype(jnp.float32)
    col = jnp.tile(jnp.arange(GRID_W), n_rows).astype(jnp.float32)
    inv = jnp.power(ROPE_BASE, -jnp.arange(ROPE_FREQS, dtype=jnp.float32) / ROPE_FREQS)
    ang_r = row[:, None] * inv
    ang_c = col[:, None] * inv
    return (jnp.cos(ang_r), jnp.sin(ang_r), jnp.cos(ang_c), jnp.sin(ang_c))


def _rotate(x, cos, sin):
    x1, x2 = x[..., :ROPE_FREQS], x[..., ROPE_FREQS:]
    return jnp.concatenate([x1 * cos - x2 * sin, x2 * cos + x1 * sin], axis=-1)


def apply_axial_rope(x, tabs):
    shape = (x.shape[1],) + (1,) * (x.ndim - 3) + (ROPE_FREQS,)
    cr, sr, cc, sc = [t.reshape(shape).astype(x.dtype) for t in tabs]
    return jnp.concatenate([_rotate(x[..., :HD // 2], cr, sr),
                            _rotate(x[..., HD // 2:], cc, sc)], axis=-1)


def mixer_front(x, mod, norm_g, w_in):
    shift, scale, gate = mod
    h = rmsnorm(x, norm_g) * (1 + scale) + shift
    return split_cols(h @ w_in), gate


def mixer_tail(x, gate, oa, g_a, ob, g_b, oc, g_c, merge, P):
    B, T, _ = x.shape
    m = jax.nn.sigmoid(merge).reshape(B, T, N_BRANCH, D_MODEL)
    y = (m[:, :, 0] * ((oa * jax.nn.silu(g_a)) @ P['w_br_a'])
         + m[:, :, 1] * ((ob * jax.nn.silu(g_b)) @ P['w_br_b'])
         + m[:, :, 2] * ((oc * jax.nn.silu(g_c)) @ P['w_br_c']))
    return x + gate * (y @ P['w_out'])


def attn_a_context(q, k, v, sink):
    B, T = q.shape[:2]
    nb = T // BLK
    qb = jnp.moveaxis(q.reshape(B, nb, BLK, KV_A, G_A, HD), 1, 0)
    snk = sink.astype(jnp.float32).reshape(1, KV_A, G_A, 1)

    def block(qblk):
        s = jnp.einsum('bqkgd,bskd->bkgqs', qblk, k).astype(jnp.float32) * SCALE
        m = jnp.maximum(jnp.max(s, axis=-1), snk)[..., None]
        e = jnp.exp(s - m)
        p = e / (jnp.sum(e, axis=-1, keepdims=True) + jnp.exp(snk[..., None] - m))
        return jnp.einsum('bkgqs,bskd->bqkgd', p.astype(v.dtype), v)

    o = lax.map(block, qb)
    return jnp.moveaxis(o, 0, 1).reshape(B, T, D_A)


def attn_a_latent(q, k, v, k_ctx, v_ctx, sink):
    B, T = q.shape[:2]
    nb = T // BLK
    qb = jnp.moveaxis(q.reshape(B, nb, BLK, KV_A, G_A, HD), 1, 0)

    def band(t):
        tp = jnp.pad(t, ((0, 0), (BLK, BLK), (0, 0), (0, 0))).reshape(B, nb + 2, BLK, KV_A, HD)
        w = jnp.concatenate([tp[:, :-2], tp[:, 1:-1], tp[:, 2:]], axis=2)
        return jnp.moveaxis(w, 1, 0)

    kw, vw = band(k), band(v)
    kj = jnp.arange(3 * BLK) - BLK
    rel = kj[None, :] - jnp.arange(BLK)[:, None]
    pos = jnp.arange(nb)[:, None] * BLK + kj[None, :]
    mask = (jnp.abs(rel)[None] <= WINDOW) & ((pos >= 0) & (pos < T))[:, None, :]
    snk = sink.astype(jnp.float32).reshape(1, KV_A, G_A, 1)

    def block(args):
        qblk, kblk, vblk, mblk = args
        s_w = jnp.einsum('bqkgd,bskd->bkgqs', qblk, kblk).astype(jnp.float32) * SCALE
        s_w = jnp.where(mblk, s_w, -jnp.inf)
        s_c = jnp.einsum('bqkgd,bskd->bkgqs', qblk, k_ctx).astype(jnp.float32) * SCALE
        m = jnp.maximum(jnp.maximum(jnp.max(s_w, axis=-1), jnp.max(s_c, axis=-1)), snk)[..., None]
        e_w = jnp.exp(s_w - m)
        e_c = jnp.exp(s_c - m)
        denom = (jnp.sum(e_w, axis=-1, keepdims=True) + jnp.sum(e_c, axis=-1, keepdims=True)
                 + jnp.exp(snk[..., None] - m))
        return (jnp.einsum('bkgqs,bskd->bqkgd', (e_w / denom).astype(v.dtype), vblk)
                + jnp.einsum('bkgqs,bskd->bqkgd', (e_c / denom).astype(v.dtype), v_ctx))

    o = lax.map(block, (qb, kw, vw, mask))
    return jnp.moveaxis(o, 0, 1).reshape(B, T, D_A)


def centred_conv(x, w, b):
    T = x.shape[1]
    xp = jnp.pad(x, ((0, 0), (CONV_W // 2, CONV_W - 1 - CONV_W // 2), (0, 0)))
    y = b
    for j in range(CONV_W):
        y = y + xp[:, j:j + T] * w[j]
    return y


def rglru_coeffs(x, wa, ba, wx, bx, lam):
    B, T, _ = x.shape
    xb = x.reshape(B, T, LRU_BLOCKS, LRU_BW)
    r = jax.nn.sigmoid(jnp.einsum('btnc,ncd->btnd', xb, wa).reshape(B, T, D_LRU) + ba)
    i = jax.nn.sigmoid(jnp.einsum('btnc,ncd->btnd', xb, wx).reshape(B, T, D_LRU) + bx)
    log_a = -LRU_C * r.astype(jnp.float32) * jax.nn.softplus(-lam.astype(jnp.float32))
    a = jnp.exp(log_a)
    b = jnp.sqrt(-jnp.expm1(2.0 * log_a)) * (i * x).astype(jnp.float32)
    return a, b


def linear_scan(a, b, h0, reverse):
    def combine(left, right):
        a_l, b_l = left
        a_r, b_r = right
        return a_l * a_r, a_r * b_l + b_r
    a_cum, b_cum = lax.associative_scan(combine, (a, b), axis=1, reverse=reverse)
    return a_cum * h0[:, None] + b_cum


def rglru_bidir(x, h0, P):
    y = jnp.zeros(x.shape, jnp.float32)
    finals = []
    for d in range(2):
        a, b = rglru_coeffs(x, P['lru_wa'][d], P['lru_ba'][d], P['lru_wx'][d],
                            P['lru_bx'][d], P['lru_lam'][d])
        h = linear_scan(a, b, h0[:, d].astype(jnp.float32), reverse=(d == 1))
        y = y + h
        finals.append(h[:, -1] if d == 0 else h[:, 0])
    return y.astype(x.dtype), jnp.stack(finals, axis=1).astype(x.dtype)


def diff_lambda(lq1, lk1, lq2, lk2, lam_init):
    f = lambda t: t.astype(jnp.float32)
    return jnp.exp(jnp.sum(f(lq1) * f(lk1))) - jnp.exp(jnp.sum(f(lq2) * f(lk2))) + lam_init


def diff_attn(q, k, v, lam, subln, lam_init):
    B, T = q.shape[:2]
    nb = T // BLK
    qb = jnp.moveaxis(q.reshape(B, nb, BLK, H_C, 2, HD), 1, 0)

    def block(qblk):
        s = jnp.einsum('bqhmd,bshmd->bhmqs', qblk, k).astype(jnp.float32) * SCALE
        p = jax.nn.softmax(s, axis=-1)
        pd = p[:, :, 0] - lam * p[:, :, 1]
        return jnp.einsum('bhqs,bshe->bqhe', pd.astype(v.dtype), v)

    o = jnp.moveaxis(lax.map(block, qb), 0, 1).reshape(B, T, H_C, DV_C)
    o = rmsnorm(o, subln) * (1 - lam_init)
    return o.reshape(B, T, D_C)


def project_heads(x, q_a, k_a, v_a, q_c, k_c, v_c, P):
    B, T, _ = x.shape
    q_a = rmsnorm(q_a.reshape(B, T, H_A, HD), P['qn_a'])
    k_a = rmsnorm(k_a.reshape(B, T, KV_A, HD), P['kn_a'])
    v_a = v_a.reshape(B, T, KV_A, HD)
    q_c = rmsnorm(q_c.reshape(B, T, H_C, 2, HD), P['qn_c'])
    k_c = rmsnorm(k_c.reshape(B, T, H_C, 2, HD), P['kn_c'])
    v_c = v_c.reshape(B, T, H_C, DV_C)
    return q_a, k_a, v_a, q_c, k_c, v_c


def context_layer(x, mod, P, lam, lam_init):
    (q_a, k_a, v_a, g_a, x_b, g_b, q_c, k_c, v_c, g_c, merge), gate = mixer_front(x, mod, P['norm_g'], P['w_in'])
    q_a, k_a, v_a, q_c, k_c, v_c = project_heads(x, q_a, k_a, v_a, q_c, k_c, v_c, P)
    oa = attn_a_context(q_a, k_a, v_a, P['sink_a'])
    xb = centred_conv(x_b, P['conv_w'], P['conv_b'])
    h0 = jnp.zeros((x.shape[0], 2, D_LRU), jnp.float32)
    ob, st = rglru_bidir(xb, h0, P)
    oc = diff_attn(q_c, k_c, v_c, lam, P['subln_c'], lam_init)
    x = mixer_tail(x, gate, oa, g_a, ob, g_b, oc, g_c, merge, P)
    return x, (k_a, v_a, k_c, v_c, st)


def latent_layer(x, mod, P, lam, lam_init, tabs, ck_a, cv_a, ck_c, cv_c, st):
    (q_a, k_a, v_a, g_a, x_b, g_b, q_c, k_c, v_c, g_c, merge), gate = mixer_front(x, mod, P['norm_g'], P['w_in'])
    q_a, k_a, v_a, q_c, k_c, v_c = project_heads(x, q_a, k_a, v_a, q_c, k_c, v_c, P)
    q_a, k_a = apply_axial_rope(q_a, tabs), apply_axial_rope(k_a, tabs)
    oa = attn_a_latent(q_a, k_a, v_a, ck_a, cv_a, P['sink_a'])
    xb = centred_conv(x_b, P['conv_w'], P['conv_b'])
    ob, _ = rglru_bidir(xb, st, P)
    q_c, k_c = apply_axial_rope(q_c, tabs), apply_axial_rope(k_c, tabs)
    k_all = jnp.concatenate([k_c, ck_c.astype(k_c.dtype)], axis=1)
    v_all = jnp.concatenate([v_c, cv_c.astype(v_c.dtype)], axis=1)
    oc = diff_attn(q_c, k_all, v_all, lam, P['subln_c'], lam_init)
    return mixer_tail(x, gate, oa, g_a, ob, g_b, oc, g_c, merge, P)


def setup_inputs(seed: int = 0) -> dict:
    key = jax.random.key(seed)
    ks = jax.random.split(key, 40)
    n = lambda i, shape, s: jax.random.normal(ks[i], shape, jnp.float32) * s
    u = jax.random.uniform(ks[20], (DEPTH, 2, D_LRU), jnp.float32, 0.9, 0.999)
    s_lam = jnp.power(u, 1.0 / LRU_C)
    return {
        "x_prompt": n(0, (BATCH, SEQ, D_MODEL), 1.0),
        "x_sample": n(1, (DEC_BATCH, DEC_SEQ, D_MODEL), 1.0),
        "cache_a_k": n(2, (DEC_BATCH, DEPTH, PAST_LEN, KV_A, HD), 1.0),
        "cache_a_v": n(3, (DEC_BATCH, DEPTH, PAST_LEN, KV_A, HD), 1.0),
        "cache_c_k": n(4, (DEC_BATCH, DEPTH, PAST_LEN, H_C, 2, HD), 1.0),
        "cache_c_v": n(5, (DEC_BATCH, DEPTH, PAST_LEN, H_C, DV_C), 1.0),
        "state_lru": n(6, (DEC_BATCH, DEPTH, 2, D_LRU), 0.5),
        "c": n(7, (DEC_BATCH, D_MODEL), 1.0),
        "c_ctx": n(8, (D_MODEL,), 1.0),
        "norm_g": 1.0 + n(9, (DEPTH, D_MODEL), 0.02),
        "mod_w": n(10, (DEPTH, D_MODEL, 3 * D_MODEL), D_MODEL ** -0.5),
        "mod_b": n(11, (DEPTH, 3 * D_MODEL), 0.01),
        "w_in": n(12, (DEPTH, D_MODEL, IN_COLS), D_MODEL ** -0.5),
        "qn_a": 1.0 + n(13, (DEPTH, HD), 0.02),
        "kn_a": 1.0 + n(14, (DEPTH, HD), 0.02),
        "sink_a": n(15, (DEPTH, H_A), 0.5),
        "conv_w": n(16, (DEPTH, CONV_W, D_LRU), CONV_W ** -0.5),
        "conv_b": n(17, (DEPTH, D_LRU), 0.01),
        "lru_wa": n(18, (DEPTH, 2, LRU_BLOCKS, LRU_BW, LRU_BW), LRU_BW ** -0.5),
        "lru_ba": n(19, (DEPTH, 2, D_LRU), 0.01),
        "lru_wx": n(21, (DEPTH, 2, LRU_BLOCKS, LRU_BW, LRU_BW), LRU_BW ** -0.5),
        "lru_bx": n(22, (DEPTH, 2, D_LRU), 0.01),
        "lru_lam": jnp.log(s_lam) - jnp.log1p(-s_lam),
        "qn_c": 1.0 + n(23, (DEPTH, HD), 0.02),
        "kn_c": 1.0 + n(24, (DEPTH, HD), 0.02),
        "lam_q1": n(25, (DEPTH, HD), 0.1),
        "lam_k1": n(26, (DEPTH, HD), 0.1),
        "lam_q2": n(27, (DEPTH, HD), 0.1),
        "lam_k2": n(28, (DEPTH, HD), 0.1),
        "subln_c": 1.0 + n(29, (DEPTH, DV_C), 0.02),
        "w_br_a": n(30, (DEPTH, D_A, D_MODEL), D_A ** -0.5),
        "w_br_b": n(31, (DEPTH, D_LRU, D_MODEL), D_LRU ** -0.5),
        "w_br_c": n(32, (DEPTH, D_C, D_MODEL), D_C ** -0.5),
        "w_out": n(33, (DEPTH, D_MODEL, D_MODEL), D_MODEL ** -0.5),
    }


def reference(x_prompt, x_sample, cache_a_k, cache_a_v, cache_c_k, cache_c_v, state_lru,
              c, c_ctx, norm_g, mod_w, mod_b, w_in, qn_a, kn_a, sink_a, conv_w, conv_b,
              lru_wa, lru_ba, lru_wx, lru_bx, lru_lam, qn_c, kn_c, lam_q1, lam_k1,
              lam_q2, lam_k2, subln_c, w_br_a, w_br_b, w_br_c, w_out):
    tabs = axial_rope_tables(x_sample.shape[1])
    xp, xs = x_prompt, x_sample
    ka_l, va_l, kc_l, vc_l, st_l = [], [], [], [], []
    for l in range(DEPTH):
        P = {'norm_g': norm_g[l], 'w_in': w_in[l], 'qn_a': qn_a[l], 'kn_a': kn_a[l],
             'sink_a': sink_a[l], 'conv_w': conv_w[l], 'conv_b': conv_b[l],
             'lru_wa': lru_wa[l], 'lru_ba': lru_ba[l], 'lru_wx': lru_wx[l],
             'lru_bx': lru_bx[l], 'lru_lam': lru_lam[l], 'qn_c': qn_c[l], 'kn_c': kn_c[l],
             'subln_c': subln_c[l], 'w_br_a': w_br_a[l], 'w_br_b': w_br_b[l],
             'w_br_c': w_br_c[l], 'w_out': w_out[l]}
        lam_init = 0.8 - 0.6 * math.exp(-0.3 * l)
        lam = diff_lambda(lam_q1[l], lam_k1[l], lam_q2[l], lam_k2[l], lam_init)
        xp, (ka, va, kc, vc, st) = context_layer(xp, modulation(c_ctx, mod_w[l], mod_b[l]), P, lam, lam_init)
        ka_l.append(ka)
        va_l.append(va)
        kc_l.append(kc)
        vc_l.append(vc)
        st_l.append(st)
        xs = latent_layer(xs, modulation(c, mod_w[l], mod_b[l]), P, lam, lam_init, tabs,
                          cache_a_k[:, l], cache_a_v[:, l], cache_c_k[:, l], cache_c_v[:, l],
                          state_lru[:, l])
    return (xp, xs, jnp.stack(ka_l, axis=1), jnp.stack(va_l, axis=1), jnp.stack(kc_l, axis=1),
            jnp.stack(vc_l, axis=1), jnp.stack(st_l, axis=1))
```

```python
import functools
import math

import jax
import jax.numpy as jnp
import numpy as np
from jax import lax
from jax.experimental import pallas as pl
from jax.experimental.pallas import tpu as pltpu

F32 = jnp.float32
BF16 = jnp.bfloat16

D_MODEL = 1024
DEPTH = 2
GRID_W = 64
HD = 64
SCALE = 1.0 / math.sqrt(HD)
H_A = 8
KV_A = 2
WINDOW = 128
D_A = H_A * HD
D_LRU = 512
LRU_BLOCKS = 8
LRU_BW = D_LRU // LRU_BLOCKS
CONV_W = 4
LRU_C = 8.0
H_C = 4
DV_C = 2 * HD
D_C = H_C * DV_C
ROPE_BASE = 10000.0
ROPE_FREQS = HD // 4
EPS = 1e-6
LANES = 128
SUBLANES = 8
VMEM_LIMIT = 56 * 1024 * 1024

_OFF = np.cumsum([0, D_A, KV_A * HD, KV_A * HD, D_A, D_LRU, D_LRU,
                  H_C * 2 * HD, H_C * 2 * HD, D_C, D_C, 3 * D_MODEL])
(_QA, _KA, _VA, _GA, _XB, _GB, _QC, _KC, _VC, _GC, _MG) = [
    (int(_OFF[i]), int(_OFF[i + 1])) for i in range(11)]
N_NORM = D_A + KV_A * HD + 2 * H_C * 2 * HD
N_PLAIN = KV_A * HD + D_C + D_LRU


def _params(sem, vmem=VMEM_LIMIT):
    return pltpu.CompilerParams(dimension_semantics=sem, vmem_limit_bytes=vmem)


def _dot(a, b):
    return jnp.dot(a, b, preferred_element_type=F32)


def _dot_nt(a, b):
    return lax.dot_general(a, b, (((1,), (1,)), ((), ())), preferred_element_type=F32)


def _sigmoid(x):
    return 1.0 / (1.0 + jnp.exp(-x))


def _mod_kernel(c_ref, w_ref, b_ref, o_ref):
    c = c_ref[...]
    a = c * _sigmoid(c)
    w = w_ref[...]
    a_hi = a.astype(BF16)
    a_lo = (a - a_hi.astype(F32)).astype(BF16)
    w_hi = w.astype(BF16)
    w_lo = (w - w_hi.astype(F32)).astype(BF16)
    o_ref[...] = _dot(a_hi, w_hi) + _dot(a_hi, w_lo) + _dot(a_lo, w_hi) + b_ref[...]


def _modulation(cvecs, mod_w, mod_b):
    tn = 768
    return pl.pallas_call(
        _mod_kernel,
        out_shape=jax.ShapeDtypeStruct((DEPTH, SUBLANES, 3 * D_MODEL), F32),
        grid=(DEPTH, 3 * D_MODEL // tn),
        in_specs=[pl.BlockSpec((SUBLANES, D_MODEL), lambda l, j: (0, 0)),
                  pl.BlockSpec((None, D_MODEL, tn), lambda l, j: (l, 0, j)),
                  pl.BlockSpec((None, 1, tn), lambda l, j: (l, 0, j))],
        out_specs=pl.BlockSpec((None, SUBLANES, tn), lambda l, j: (l, 0, j)),
        compiler_params=_params(("arbitrary", "arbitrary")),
        name="modulation",
    )(cvecs, mod_w, mod_b.reshape(DEPTH, 1, 3 * D_MODEL))


def _normed_input(x, mod, ng):
    ms = jnp.mean(x * x, axis=-1, keepdims=True)
    shift = mod[:, 0:D_MODEL]
    scale = mod[:, D_MODEL:2 * D_MODEL]
    return (x * lax.rsqrt(ms + EPS) * ng) * (1.0 + scale) + shift


def _front_kernel(*refs, use_rope):
    if use_rope:
        (x_ref, mod_ref, ng_ref, w1_ref, w2_ref, gain_ref, cos_ref, sin_ref,
         qa_ref, ka_ref, qc_ref, kc_ref, va_ref, vc_ref, xb_ref) = refs
    else:
        (x_ref, mod_ref, ng_ref, w1_ref, w2_ref, gain_ref,
         qa_ref, ka_ref, qc_ref, kc_ref, va_ref, vc_ref, xb_ref) = refs
    x = x_ref[...]
    tm = x.shape[0]
    hb = _normed_input(x, mod_ref[...], ng_ref[...]).astype(BF16)
    p1 = _dot(hb, w1_ref[...])
    lane = lax.broadcasted_iota(jnp.int32, (tm, LANES), 1)
    lo = lane < HD
    first_half = (lane & ROPE_FREQS) == 0
    outs = ([(qa_ref, c) for c in range(4)] + [(ka_ref, 0)]
            + [(qc_ref, c) for c in range(4)] + [(kc_ref, c) for c in range(4)])
    for c, (o_ref, oc) in enumerate(outs):
        pc = p1[:, c * LANES:(c + 1) * LANES]
        sq = pc * pc
        s_lo = jnp.sum(jnp.where(lo, sq, 0.0), axis=-1, keepdims=True)
        s_hi = jnp.sum(jnp.where(lo, 0.0, sq), axis=-1, keepdims=True)
        msq = jnp.where(lo, s_lo, s_hi) * (1.0 / HD)
        y = pc * lax.rsqrt(msq + EPS) * gain_ref[:, c * LANES:(c + 1) * LANES]
        if use_rope:
            partner = jnp.where(first_half, pltpu.roll(y, LANES - ROPE_FREQS, 1),
                                pltpu.roll(y, ROPE_FREQS, 1))
            y = y * cos_ref[...] + partner * sin_ref[...]
        o_ref[:, oc * LANES:(oc + 1) * LANES] = y.astype(o_ref.dtype)
    p2 = _dot(hb, w2_ref[...])
    va_ref[...] = p2[:, 0:KV_A * HD].astype(va_ref.dtype)
    vc_ref[...] = p2[:, KV_A * HD:KV_A * HD + D_C].astype(vc_ref.dtype)
    xb_ref[...] = p2[:, KV_A * HD + D_C:]


def _front(x2d, mod_rows, ng, w1, w2, gains, rope, seq_len, kv_dtype, tm=256):
    n = x2d.shape[0]
    use_rope = rope is not None
    row = lambda i: (i, 0)
    const = lambda i: (0, 0)
    per_seq = seq_len // tm
    in_specs = [pl.BlockSpec((tm, D_MODEL), row),
                pl.BlockSpec((None, 1, 3 * D_MODEL), lambda i: (i // per_seq, 0, 0)),
                pl.BlockSpec((1, D_MODEL), const),
                pl.BlockSpec((D_MODEL, N_NORM), const),
                pl.BlockSpec((D_MODEL, N_PLAIN), const),
                pl.BlockSpec((1, N_NORM), const)]
    args = [x2d, mod_rows, ng, w1, w2, gains]
    if use_rope:
        in_specs += [pl.BlockSpec((tm, LANES), lambda i: (i % per_seq, 0))] * 2
        args += list(rope)
    widths = (D_A, KV_A * HD, H_C * 2 * HD, H_C * 2 * HD, KV_A * HD, D_C, D_LRU)
    dtypes = (BF16, kv_dtype, BF16, kv_dtype, kv_dtype, kv_dtype, F32)
    return pl.pallas_call(
        functools.partial(_front_kernel, use_rope=use_rope),
        out_shape=[jax.ShapeDtypeStruct((n, w), d) for w, d in zip(widths, dtypes)],
        grid=(n // tm,),
        in_specs=in_specs,
        out_specs=[pl.BlockSpec((tm, w), row) for w in widths],
        compiler_params=_params(("arbitrary",)),
        name="front_rope" if use_rope else "front",
    )(*args)


def _attn_a_kernel(*refs, tq, seq_len, banded, has_ctx):
    if has_ctx:
        sink_ref, q_ref, k_ref, v_ref, kc_ref, vc_ref, o_ref = refs
    else:
        sink_ref, q_ref, k_ref, v_ref, o_ref = refs
    i = pl.program_id(1)
    g = H_A // KV_A
    if banded:
        nwin = 3 * tq
        start = pl.multiple_of(jnp.clip((i - 1) * tq, 0, seq_len - nwin), tq)
        kw = k_ref[pl.ds(start, nwin), :].astype(F32)
        vw = v_ref[pl.ds(start, nwin), :].astype(F32)
        qpos = i * tq + (lax.broadcasted_iota(jnp.int32, (g * tq, nwin), 0) & (tq - 1))
        kpos = start + lax.broadcasted_iota(jnp.int32, (g * tq, nwin), 1)
        mask = jnp.abs(kpos - qpos) <= WINDOW
    else:
        kw = k_ref[...].astype(F32)
        vw = v_ref[...].astype(F32)
    nk = kw.shape[0]
    lo_k = lax.broadcasted_iota(jnp.int32, (nk, LANES), 1) < HD
    lo_q = lax.broadcasted_iota(jnp.int32, (tq, LANES), 1) < HD

    def dup(x, lo, kv):
        r = pltpu.roll(x, HD, 1)
        return (jnp.where(lo, x, r) if kv == 0 else jnp.where(lo, r, x)).astype(BF16)

    if has_ctx:
        kc = kc_ref[...].astype(F32)
        vc = vc_ref[...].astype(F32)
        lo_c = lax.broadcasted_iota(jnp.int32, (kc.shape[0], LANES), 1) < HD
    q = (q_ref[...].astype(F32) * SCALE).astype(BF16)
    zero = jnp.zeros((tq, LANES), BF16)
    for kv in range(KV_A):
        heads = list(range(g * kv, g * kv + g))
        qst = jnp.concatenate(
            [jnp.where(lo_q if h % 2 == 0 else jnp.logical_not(lo_q),
                       q[:, (h // 2) * LANES:(h // 2 + 1) * LANES], zero) for h in heads], axis=0)
        snk = jnp.concatenate([jnp.full((tq, 1), sink_ref[h], F32) for h in heads], axis=0)
        s_w = _dot_nt(qst, dup(kw, lo_k, kv))
        if banded:
            s_w = jnp.where(mask, s_w, -jnp.inf)
        m = jnp.maximum(jnp.max(s_w, axis=-1, keepdims=True), snk)
        if has_ctx:
            s_c = _dot_nt(qst, dup(kc, lo_c, kv))
            m = jnp.maximum(m, jnp.max(s_c, axis=-1, keepdims=True))
        e_w = jnp.exp(s_w - m)
        denom = jnp.sum(e_w, axis=-1, keepdims=True) + jnp.exp(snk - m)
        acc = _dot(e_w.astype(BF16), dup(vw, lo_k, kv))
        if has_ctx:
            e_c = jnp.exp(s_c - m)
            denom = denom + jnp.sum(e_c, axis=-1, keepdims=True)
            acc = acc + _dot(e_c.astype(BF16), dup(vc, lo_c, kv))
        o = acc / denom
        for c in range(g // 2):
            col = jnp.where(lo_q, o[(2 * c) * tq:(2 * c + 1) * tq], o[(2 * c + 1) * tq:(2 * c + 2) * tq])
            cc = (g // 2) * kv + c
            o_ref[:, cc * LANES:(cc + 1) * LANES] = col.astype(o_ref.dtype)


def _attn_a(sink, q, k, v, k_ctx, v_ctx, tq, banded):
    b, t, _ = q.shape
    has_ctx = k_ctx is not None
    full = lambda bi, i: (bi, 0, 0)
    in_specs = [pl.BlockSpec(memory_space=pltpu.SMEM),
                pl.BlockSpec((None, tq, D_A), lambda bi, i: (bi, i, 0)),
                pl.BlockSpec((None, t, KV_A * HD), full),
                pl.BlockSpec((None, t, KV_A * HD), full)]
    args = [sink, q, k, v]
    if has_ctx:
        s = k_ctx.shape[1]
        in_specs += [pl.BlockSpec((None, s, KV_A * HD), full)] * 2
        args += [k_ctx, v_ctx]
    return pl.pallas_call(
        functools.partial(_attn_a_kernel, tq=tq, seq_len=t, banded=banded, has_ctx=has_ctx),
        out_shape=jax.ShapeDtypeStruct((b, t, D_A), BF16),
        grid=(b, t // tq),
        in_specs=in_specs,
        out_specs=pl.BlockSpec((None, tq, D_A), lambda bi, i: (bi, i, 0)),
        compiler_params=_params(("arbitrary", "arbitrary")),
        name="attn_a_latent" if has_ctx else "attn_a_context",
    )(*args)


def _attn_c_kernel(*refs, tq, ts, n_loc, has_ctx, lam_init):
    if has_ctx:
        lq1, lk1, lq2, lk2, sub_ref, q_ref, k_ref, v_ref, kc_ref, vc_ref, o_ref = refs
    else:
        lq1, lk1, lq2, lk2, sub_ref, q_ref, k_ref, v_ref, o_ref = refs
    lam = (jnp.exp(jnp.sum(lq1[...] * lk1[...], axis=-1, keepdims=True))
           - jnp.exp(jnp.sum(lq2[...] * lk2[...], axis=-1, keepdims=True)) + lam_init)
    lo = lax.broadcasted_iota(jnp.int32, (tq, LANES), 1) < HD
    zero = jnp.zeros((tq, LANES), BF16)
    for h in range(H_C):
        cols = slice(h * LANES, (h + 1) * LANES)
        q12 = (q_ref[:, cols].astype(F32) * SCALE).astype(BF16)
        qz = jnp.concatenate([jnp.where(lo, q12, zero), jnp.where(lo, zero, q12)], axis=0)

        def update(carry, kt, vt):
            m, l, acc = carry
            s = _dot_nt(qz, kt)
            m_new = jnp.maximum(m, jnp.max(s, axis=-1, keepdims=True))
            alpha = jnp.exp(m - m_new)
            p = jnp.exp(s - m_new)
            l = alpha * l + jnp.sum(p, axis=-1, keepdims=True)
            acc = alpha * acc + _dot(p.astype(BF16), vt)
            return m_new, l, acc

        def body(j, carry):
            r0 = pl.multiple_of(j * ts, ts)
            return update(carry, k_ref[pl.ds(r0, ts), cols].astype(BF16), v_ref[pl.ds(r0, ts), cols].astype(BF16))

        carry = (jnp.full((2 * tq, 1), -jnp.inf, F32), jnp.zeros((2 * tq, 1), F32),
                 jnp.zeros((2 * tq, LANES), F32))
        carry = lax.fori_loop(0, n_loc, body, carry)
        if has_ctx:
            carry = update(carry, kc_ref[:, cols].astype(BF16), vc_ref[:, cols].astype(BF16))
        _, l, acc = carry
        o = acc[:tq] / l[:tq] - lam * (acc[tq:] / l[tq:])
        o = o * lax.rsqrt(jnp.mean(o * o, axis=-1, keepdims=True) + EPS) * sub_ref[...] * (1.0 - lam_init)
        o_ref[:, cols] = o.astype(o_ref.dtype)


def _attn_c(lam_vecs, subln, q, k, v, k_ctx, v_ctx, tq, ts, lam_init):
    b, t, _ = q.shape
    s_loc = k.shape[1]
    has_ctx = k_ctx is not None
    full = lambda bi, i: (bi, 0, 0)
    vec = lambda bi, i: (0, 0)
    in_specs = ([pl.BlockSpec((1, HD), vec)] * 4 + [pl.BlockSpec((1, DV_C), vec)]
                + [pl.BlockSpec((None, tq, D_C), lambda bi, i: (bi, i, 0)),
                   pl.BlockSpec((None, s_loc, D_C), full),
                   pl.BlockSpec((None, s_loc, D_C), full)])
    args = list(lam_vecs) + [subln, q, k, v]
    if has_ctx:
        sc = k_ctx.shape[1]
        in_specs += [pl.BlockSpec((None, sc, D_C), full)] * 2
        args += [k_ctx, v_ctx]
    return pl.pallas_call(
        functools.partial(_attn_c_kernel, tq=tq, ts=ts, n_loc=s_loc // ts, has_ctx=has_ctx, lam_init=lam_init),
        out_shape=jax.ShapeDtypeStruct((b, t, D_C), BF16),
        grid=(b, t // tq),
        in_specs=in_specs,
        out_specs=pl.BlockSpec((None, tq, D_C), lambda bi, i: (bi, i, 0)),
        compiler_params=_params(("arbitrary", "arbitrary")),
        name="attn_c_latent" if has_ctx else "attn_c_context",
    )(*args)


def _lru_kernel(xc_ref, xp_ref, xn_ref, cw_ref, cb_ref, wg_ref, bg_ref, lam_ref, h0_ref,
                ob_ref, st_ref, a_s, b_s, hs, hf_s, hcar, *, tc, nc):
    p = pl.program_id(1)
    c = pl.program_id(2)
    cidx = c + p * (nc - 1 - 2 * c)
    cur = xc_ref[...]
    prev = jnp.where(cidx > 0, xp_ref[...], 0.0)
    nxt = jnp.where(cidx < nc - 1, xn_ref[...], 0.0)
    ext = jnp.concatenate([prev, cur, nxt], axis=0)
    n_ext = tc + 2 * SUBLANES
    xb = jnp.broadcast_to(cb_ref[...], (tc, D_LRU))
    for j in range(CONV_W):
        off = j - CONV_W // 2
        sh = ext if off == 0 else pltpu.roll(ext, (-off) % n_ext, 0)
        xb = xb + sh[SUBLANES:SUBLANES + tc] * cw_ref[j:j + 1, :]
    gm = _dot(xb.astype(BF16), wg_ref[...]) + bg_ref[...]
    r = _sigmoid(gm[:, :D_LRU])
    ig = _sigmoid(gm[:, D_LRU:])
    nl = -lam_ref[...]
    softplus = jnp.maximum(nl, 0.0) + jnp.log1p(jnp.exp(-jnp.abs(nl)))
    log_a = -LRU_C * r * softplus
    a = jnp.exp(log_a)
    a_s[...] = a
    b_s[...] = jnp.sqrt(-jnp.tanh(log_a) * (a * a + 1.0)) * (ig * xb)

    @pl.when(c == 0)
    def _():
        hcar[...] = jnp.broadcast_to(h0_ref[pl.ds(p, 1), :], (SUBLANES, D_LRU))

    rows = lax.broadcasted_iota(jnp.int32, (SUBLANES, D_LRU), 0)
    ng = tc // SUBLANES

    def group(gi, h):
        gidx = gi + p * (ng - 1 - 2 * gi)
        base = pl.multiple_of(gidx * SUBLANES, SUBLANES)
        out8 = jnp.zeros((SUBLANES, D_LRU), F32)
        for rr in range(SUBLANES):
            r_in = rr + p * (SUBLANES - 1 - 2 * rr)
            t = base + r_in
            a_t = jnp.broadcast_to(a_s[pl.ds(t, 1), :], (SUBLANES, D_LRU))
            b_t = jnp.broadcast_to(b_s[pl.ds(t, 1), :], (SUBLANES, D_LRU))
            h = a_t * h + b_t
            out8 = jnp.where(rows == r_in, h, out8)
        hs[pl.ds(base, SUBLANES), :] = out8
        return h

    h = lax.fori_loop(0, ng, group, hcar[...])
    hcar[...] = h
    r0 = pl.multiple_of(cidx * tc, tc)

    @pl.when(p == 0)
    def _():
        hf_s[pl.ds(r0, tc), :] = hs[...]

    @pl.when(p == 1)
    def _():
        ob_ref[...] = (hf_s[pl.ds(r0, tc), :] + hs[...]).astype(ob_ref.dtype)

    @pl.when(c == nc - 1)
    def _():
        st_ref[pl.ds(p, 1), :] = h[0:1, :]


def _lru(xb, conv_w, conv_b, wg, bg, lam, h0, tc):
    b, t, _ = xb.shape
    nc = t // tc
    nb8 = t // SUBLANES
    per = tc // SUBLANES
    cidx = lambda p, c: c + p * (nc - 1 - 2 * c)
    vec = lambda bi, p, c: (0, 0)
    in_specs = [
        pl.BlockSpec((None, tc, D_LRU), lambda bi, p, c: (bi, cidx(p, c), 0)),
        pl.BlockSpec((None, SUBLANES, D_LRU), lambda bi, p, c: (bi, jnp.maximum(cidx(p, c) * per - 1, 0), 0)),
        pl.BlockSpec((None, SUBLANES, D_LRU), lambda bi, p, c: (bi, jnp.minimum((cidx(p, c) + 1) * per, nb8 - 1), 0)),
        pl.BlockSpec((CONV_W, D_LRU), vec),
        pl.BlockSpec((1, D_LRU), vec),
        pl.BlockSpec((None, D_LRU, 2 * D_LRU), lambda bi, p, c: (p, 0, 0)),
        pl.BlockSpec((None, 1, 2 * D_LRU), lambda bi, p, c: (p, 0, 0)),
        pl.BlockSpec((None, 1, D_LRU), lambda bi, p, c: (p, 0, 0)),
        pl.BlockSpec((None, 2, D_LRU), lambda bi, p, c: (bi, 0, 0)),
    ]
    return pl.pallas_call(
        functools.partial(_lru_kernel, tc=tc, nc=nc),
        out_shape=[jax.ShapeDtypeStruct((b, t, D_LRU), BF16), jax.ShapeDtypeStruct((b, 2, D_LRU), F32)],
        grid=(b, 2, nc),
        in_specs=in_specs,
        out_specs=[pl.BlockSpec((None, tc, D_LRU), lambda bi, p, c: (bi, nc - 1 - p * c, 0)),
                   pl.BlockSpec((None, 2, D_LRU), lambda bi, p, c: (bi, 0, 0))],
        scratch_shapes=[pltpu.VMEM((tc, D_LRU), F32), pltpu.VMEM((tc, D_LRU), F32), pltpu.VMEM((tc, D_LRU), F32),
                        pltpu.VMEM((t, D_LRU), F32), pltpu.VMEM((SUBLANES, D_LRU), F32)],
        compiler_params=_params(("arbitrary", "arbitrary", "arbitrary")),
        name="lru",
    )(xb, xb, xb, conv_w, conv_b, wg, bg, lam, h0)


def _tail_kernel(x_ref, mod_ref, ng_ref, oa_ref, ob_ref, oc_ref, wg_ref, wm_ref, wbr_ref, wout_ref, o_ref):
    x = x_ref[...]
    mod = mod_ref[...]
    hb = _normed_input(x, mod, ng_ref[...]).astype(BF16)
    y = None
    for br, br_ref in enumerate((oa_ref, ob_ref, oc_ref)):
        gt = _dot(hb, wg_ref[:, br * D_A:(br + 1) * D_A])
        u = (br_ref[...].astype(F32) * (gt * _sigmoid(gt))).astype(BF16)
        proj = _dot(u, wbr_ref[br])
        mg = _sigmoid(_dot(hb, wm_ref[:, br * D_MODEL:(br + 1) * D_MODEL]))
        y = mg * proj if y is None else y + mg * proj
    gate = mod[:, 2 * D_MODEL:]
    o_ref[...] = x + gate * _dot(y.astype(BF16), wout_ref[...])


def _tail(x2d, mod_rows, ng, oa, ob, oc, wg, wm, wbr, wout, seq_len, tm=256):
    n = x2d.shape[0]
    per_seq = seq_len // tm
    row = lambda i: (i, 0)
    const = lambda i: (0, 0)
    once = dict(pipeline_mode=pl.Buffered(1))
    return pl.pallas_call(
        _tail_kernel,
        out_shape=jax.ShapeDtypeStruct((n, D_MODEL), F32),
        grid=(n // tm,),
        in_specs=[pl.BlockSpec((tm, D_MODEL), row),
                  pl.BlockSpec((None, 1, 3 * D_MODEL), lambda i: (i // per_seq, 0, 0)),
                  pl.BlockSpec((1, D_MODEL), const),
                  pl.BlockSpec((tm, D_A), row),
                  pl.BlockSpec((tm, D_LRU), row),
                  pl.BlockSpec((tm, D_C), row),
                  pl.BlockSpec((D_MODEL, 3 * D_A), const, **once),
                  pl.BlockSpec((D_MODEL, 3 * D_MODEL), const, **once),
                  pl.BlockSpec((3, D_A, D_MODEL), lambda i: (0, 0, 0), **once),
                  pl.BlockSpec((D_MODEL, D_MODEL), const, **once)],
        out_specs=pl.BlockSpec((tm, D_MODEL), row),
        compiler_params=_params(("arbitrary",)),
        name="tail",
    )(x2d, mod_rows, ng, oa, ob, oc, wg, wm, wbr, wout)


def _rope_tables(seq_len):
    pos = jnp.arange(seq_len)
    row = (pos // GRID_W).astype(F32)
    col = (pos % GRID_W).astype(F32)
    inv = jnp.power(ROPE_BASE, -jnp.arange(ROPE_FREQS, dtype=F32) / ROPE_FREQS)
    ang_r = row[:, None] * inv
    ang_c = col[:, None] * inv
    cos = jnp.concatenate([jnp.cos(ang_r)] * 2 + [jnp.cos(ang_c)] * 2, axis=-1)
    sin = jnp.concatenate([-jnp.sin(ang_r), jnp.sin(ang_r), -jnp.sin(ang_c), jnp.sin(ang_c)], axis=-1)
    return jnp.tile(cos, (1, LANES // HD)), jnp.tile(sin, (1, LANES // HD))


def _block_diag(w):
    eye = jnp.eye(LRU_BLOCKS, dtype=w.dtype)
    return (w[:, :, None, :] * eye[:, None, :, None]).reshape(D_LRU, D_LRU)


def kernel(x_prompt, x_sample, cache_a_k, cache_a_v, cache_c_k, cache_c_v, state_lru, c, c_ctx, norm_g, mod_w, mod_b, w_in, qn_a, kn_a, sink_a, conv_w, conv_b, lru_wa, lru_ba, lru_wx, lru_bx, lru_lam, qn_c, kn_c, lam_q1, lam_k1, lam_q2, lam_k2, subln_c, w_br_a, w_br_b, w_br_c, w_out):
    bp, sp, _ = x_prompt.shape
    bs, ss, _ = x_sample.shape
    past = cache_a_k.shape[2]

    cvecs = jnp.concatenate([c, c_ctx[None, :], jnp.zeros((SUBLANES - bs - 1, D_MODEL), F32)], axis=0)
    mod = _modulation(cvecs, mod_w, mod_b)
    rope = _rope_tables(ss)

    sl = lambda w, ab: w[:, :, ab[0]:ab[1]]
    wb = w_in.astype(BF16)
    w1 = jnp.concatenate([sl(wb, _QA), sl(wb, _KA), sl(wb, _QC), sl(wb, _KC)], axis=-1)
    w2 = jnp.concatenate([sl(wb, _VA), sl(wb, _VC), sl(wb, _XB)], axis=-1)
    wg = jnp.concatenate([sl(wb, _GA), sl(wb, _GB), sl(wb, _GC)], axis=-1)
    wm = sl(wb, _MG)
    wbr = jnp.stack([w_br_a, w_br_b, w_br_c], axis=1).astype(BF16)
    wout = w_out.astype(BF16)
    gains = jnp.concatenate([jnp.tile(qn_a, (1, H_A)), jnp.tile(kn_a, (1, KV_A)),
                             jnp.tile(qn_c, (1, 2 * H_C)), jnp.tile(kn_c, (1, 2 * H_C))], axis=-1)

    xp = x_prompt.reshape(bp * sp, D_MODEL)
    xs = x_sample.reshape(bs * ss, D_MODEL)
    zeros_h0 = jnp.zeros((bp, 2, D_LRU), F32)
    ka_l, va_l, kc_l, vc_l, st_l = [], [], [], [], []
    for l in range(DEPTH):
        lam_init = 0.8 - 0.6 * math.exp(-0.3 * l)
        ng = norm_g[l][None, :]
        mod_ctx = mod[l, bs:bs + 1][:, None, :]
        mod_lat = mod[l, 0:bs][:, None, :]
        lru_wg = jnp.stack([jnp.concatenate([_block_diag(lru_wa[l, d]), _block_diag(lru_wx[l, d])], axis=-1)
                            for d in range(2)]).astype(BF16)
        lru_bg = jnp.concatenate([lru_ba[l], lru_bx[l]], axis=-1)[:, None, :]
        lam_vecs = [v[l][None, :] for v in (lam_q1, lam_k1, lam_q2, lam_k2)]
        subln = subln_c[l][None, :]
        g_l = gains[l][None, :]
        lru_args = (conv_w[l], conv_b[l][None, :], lru_wg, lru_bg, lru_lam[l][:, None, :])
        tail_w = (wg[l], wm[l], wbr[l], wout[l])

        qa, ka, qc, kc, va, vc, xb = _front(xp, mod_ctx, ng, w1[l], w2[l], g_l, None, bp * sp, F32)
        r3 = lambda a: a.reshape(bp, sp, a.shape[-1])
        oa = _attn_a(sink_a[l], r3(qa), r3(ka), r3(va), None, None, tq=sp, banded=False)
        ob, st = _lru(r3(xb), *lru_args, zeros_h0, tc=sp)
        oc = _attn_c(lam_vecs, subln, r3(qc), r3(kc), r3(vc), None, None, tq=sp, ts=sp, lam_init=lam_init)
        flat = lambda a: a.reshape(bp * sp, a.shape[-1])
        xp = _tail(xp, mod_ctx, ng, flat(oa), flat(ob), flat(oc), *tail_w, bp * sp)
        ka_l.append(ka.reshape(bp, sp, KV_A, HD))
        va_l.append(va.reshape(bp, sp, KV_A, HD))
        kc_l.append(kc.reshape(bp, sp, H_C, 2, HD))
        vc_l.append(vc.reshape(bp, sp, H_C, DV_C))
        st_l.append(st)

        qa, ka, qc, kc, va, vc, xb = _front(xs, mod_lat, ng, w1[l], w2[l], g_l, rope, ss, BF16)
        r3 = lambda a: a.reshape(bs, ss, a.shape[-1])
        cka = cache_a_k[:, l].reshape(bs, past, KV_A * HD)
        cva = cache_a_v[:, l].reshape(bs, past, KV_A * HD)
        ckc = cache_c_k[:, l].reshape(bs, past, D_C)
        cvc = cache_c_v[:, l].reshape(bs, past, D_C)
        oa = _attn_a(sink_a[l], r3(qa), r3(ka), r3(va), cka, cva, tq=WINDOW, banded=True)
        ob, _ = _lru(r3(xb), *lru_args, state_lru[:, l], tc=512)
        oc = _attn_c(lam_vecs, subln, r3(qc), r3(kc), r3(vc), ckc, cvc, tq=256, ts=512, lam_init=lam_init)
        flat = lambda a: a.reshape(bs * ss, a.shape[-1])
        xs = _tail(xs, mod_lat, ng, flat(oa), flat(ob), flat(oc), *tail_w, ss)

    return (xp.reshape(bp, sp, D_MODEL), xs.reshape(bs, ss, D_MODEL),
            jnp.stack(ka_l, axis=1), jnp.stack(va_l, axis=1), jnp.stack(kc_l, axis=1),
            jnp.stack(vc_l, axis=1), jnp.stack(st_l, axis=1))
```

```python
import functools
import math

import jax
import jax.numpy as jnp
import numpy as np
from jax import lax
from jax.experimental import pallas as pl
from jax.experimental.pallas import tpu as pltpu

F32 = jnp.float32
BF16 = jnp.bfloat16

D_MODEL = 1024
DEPTH = 2
GRID_W = 64
HD = 64
SCALE = 1.0 / math.sqrt(HD)
H_A = 8
KV_A = 2
WINDOW = 128
D_A = H_A * HD
D_LRU = 512
LRU_BLOCKS = 8
LRU_BW = D_LRU // LRU_BLOCKS
CONV_W = 4
LRU_C = 8.0
H_C = 4
DV_C = 2 * HD
D_C = H_C * DV_C
ROPE_BASE = 10000.0
ROPE_FREQS = HD // 4
EPS = 1e-6
LANES = 128
SUBLANES = 8
VMEM_LIMIT = 56 * 1024 * 1024

_OFF = np.cumsum([0, D_A, KV_A * HD, KV_A * HD, D_A, D_LRU, D_LRU,
                  H_C * 2 * HD, H_C * 2 * HD, D_C, D_C, 3 * D_MODEL])
(_QA, _KA, _VA, _GA, _XB, _GB, _QC, _KC, _VC, _GC, _MG) = [
    (int(_OFF[i]), int(_OFF[i + 1])) for i in range(11)]
N_NORM = D_A + KV_A * HD + 2 * H_C * 2 * HD
N_PLAIN = KV_A * HD + D_C + D_LRU


def _params(sem, vmem=VMEM_LIMIT):
    return pltpu.CompilerParams(dimension_semantics=sem, vmem_limit_bytes=vmem)


def _dot(a, b):
    return jnp.dot(a, b, preferred_element_type=F32)


def _dot_nt(a, b):
    return lax.dot_general(a, b, (((1,), (1,)), ((), ())), preferred_element_type=F32)


def _sigmoid(x):
    return 1.0 / (1.0 + jnp.exp(-x))


def _mod_kernel(c_ref, w_ref, b_ref, o_ref):
    c = c_ref[...]
    a = c * _sigmoid(c)
    w = w_ref[...]
    a_hi = a.astype(BF16)
    a_lo = (a - a_hi.astype(F32)).astype(BF16)
    w_hi = w.astype(BF16)
    w_lo = (w - w_hi.astype(F32)).astype(BF16)
    o_ref[...] = _dot(a_hi, w_hi) + _dot(a_hi, w_lo) + _dot(a_lo, w_hi) + b_ref[...]


def _modulation(cvecs, mod_w, mod_b):
    tn = 768
    return pl.pallas_call(
        _mod_kernel,
        out_shape=jax.ShapeDtypeStruct((DEPTH, SUBLANES, 3 * D_MODEL), F32),
        grid=(DEPTH, 3 * D_MODEL // tn),
        in_specs=[pl.BlockSpec((SUBLANES, D_MODEL), lambda l, j: (0, 0)),
                  pl.BlockSpec((None, D_MODEL, tn), lambda l, j: (l, 0, j)),
                  pl.BlockSpec((None, 1, tn), lambda l, j: (l, 0, j))],
        out_specs=pl.BlockSpec((None, SUBLANES, tn), lambda l, j: (l, 0, j)),
        compiler_params=_params(("arbitrary", "arbitrary")),
        name="modulation",
    )(cvecs, mod_w, mod_b.reshape(DEPTH, 1, 3 * D_MODEL))


def _normed_input(x, mod, ng):
    ms = jnp.mean(x * x, axis=-1, keepdims=True)
    shift = mod[:, 0:D_MODEL]
    scale = mod[:, D_MODEL:2 * D_MODEL]
    return (x * lax.rsqrt(ms + EPS) * ng) * (1.0 + scale) + shift


def _front_kernel(*refs, use_rope):
    if use_rope:
        (x_ref, mod_ref, ng_ref, w1_ref, w2_ref, gain_ref, cos_ref, sin_ref,
         qa_ref, ka_ref, qc_ref, kc_ref, va_ref, vc_ref, xb_ref) = refs
    else:
        (x_ref, mod_ref, ng_ref, w1_ref, w2_ref, gain_ref,
         qa_ref, ka_ref, qc_ref, kc_ref, va_ref, vc_ref, xb_ref) = refs
    x = x_ref[...]
    tm = x.shape[0]
    hb = _normed_input(x, mod_ref[...], ng_ref[...]).astype(BF16)
    p1 = _dot(hb, w1_ref[...])
    lane = lax.broadcasted_iota(jnp.int32, (tm, LANES), 1)
    lo = lane < HD
    first_half = (lane & ROPE_FREQS) == 0
    outs = ([(qa_ref, c) for c in range(4)] + [(ka_ref, 0)]
            + [(qc_ref, c) for c in range(4)] + [(kc_ref, c) for c in range(4)])
    for c, (o_ref, oc) in enumerate(outs):
        pc = p1[:, c * LANES:(c + 1) * LANES]
        sq = pc * pc
        s_lo = jnp.sum(jnp.where(lo, sq, 0.0), axis=-1, keepdims=True)
        s_hi = jnp.sum(jnp.where(lo, 0.0, sq), axis=-1, keepdims=True)
        msq = jnp.where(lo, s_lo, s_hi) * (1.0 / HD)
        y = pc * lax.rsqrt(msq + EPS) * gain_ref[:, c * LANES:(c + 1) * LANES]
        if use_rope:
            partner = jnp.where(first_half, pltpu.roll(y, LANES - ROPE_FREQS, 1),
                                pltpu.roll(y, ROPE_FREQS, 1))
            y = y * cos_ref[...] + partner * sin_ref[...]
        o_ref[:, oc * LANES:(oc + 1) * LANES] = y.astype(o_ref.dtype)
    p2 = _dot(hb, w2_ref[...])
    va_ref[...] = p2[:, 0:KV_A * HD].astype(va_ref.dtype)
    vc_ref[...] = p2[:, KV_A * HD:KV_A * HD + D_C].astype(vc_ref.dtype)
    xb_ref[...] = p2[:, KV_A * HD + D_C:]


def _front(x2d, mod_rows, ng, w1, w2, gains, rope, seq_len, kv_dtype, tm=256):
    n = x2d.shape[0]
    use_rope = rope is not None
    row = lambda i: (i, 0)
    const = lambda i: (0, 0)
    per_seq = seq_len // tm
    in_specs = [pl.BlockSpec((tm, D_MODEL), row),
                pl.BlockSpec((None, 1, 3 * D_MODEL), lambda i: (i // per_seq, 0, 0)),
                pl.BlockSpec((1, D_MODEL), const),
                pl.BlockSpec((D_MODEL, N_NORM), const),
                pl.BlockSpec((D_MODEL, N_PLAIN), const),
                pl.BlockSpec((1, N_NORM), const)]
    args = [x2d, mod_rows, ng, w1, w2, gains]
    if use_rope:
        in_specs += [pl.BlockSpec((tm, LANES), lambda i: (i % per_seq, 0))] * 2
        args += list(rope)
    widths = (D_A, KV_A * HD, H_C * 2 * HD, H_C * 2 * HD, KV_A * HD, D_C, D_LRU)
    dtypes = (BF16, kv_dtype, BF16, kv_dtype, kv_dtype, kv_dtype, F32)
    return pl.pallas_call(
        functools.partial(_front_kernel, use_rope=use_rope),
        out_shape=[jax.ShapeDtypeStruct((n, w), d) for w, d in zip(widths, dtypes)],
        grid=(n // tm,),
        in_specs=in_specs,
        out_specs=[pl.BlockSpec((tm, w), row) for w in widths],
        compiler_params=_params(("arbitrary",)),
        name="front_rope" if use_rope else "front",
    )(*args)


def _attn_a_kernel(*refs, tq, seq_len, banded, has_ctx):
    if has_ctx:
        sink_ref, q_ref, k_ref, v_ref, kc_ref, vc_ref, o_ref = refs
    else:
        sink_ref, q_ref, k_ref, v_ref, o_ref = refs
    i = pl.program_id(1)
    g = H_A // KV_A
    if banded:
        nwin = 3 * tq
        start = pl.multiple_of(jnp.clip((i - 1) * tq, 0, seq_len - nwin), tq)
        kw = k_ref[pl.ds(start, nwin), :].astype(F32)
        vw = v_ref[pl.ds(start, nwin), :].astype(F32)
        qpos = i * tq + (lax.broadcasted_iota(jnp.int32, (g * tq, nwin), 0) & (tq - 1))
        kpos = start + lax.broadcasted_iota(jnp.int32, (g * tq, nwin), 1)
        mask = jnp.abs(kpos - qpos) <= WINDOW
    else:
        kw = k_ref[...].astype(F32)
        vw = v_ref[...].astype(F32)
    nk = kw.shape[0]
    lo_k = lax.broadcasted_iota(jnp.int32, (nk, LANES), 1) < HD
    lo_q = lax.broadcasted_iota(jnp.int32, (tq, LANES), 1) < HD

    def dup(x, lo, kv):
        r = pltpu.roll(x, HD, 1)
        return (jnp.where(lo, x, r) if kv == 0 else jnp.where(lo, r, x)).astype(BF16)

    if has_ctx:
        kc = kc_ref[...].astype(F32)
        vc = vc_ref[...].astype(F32)
        lo_c = lax.broadcasted_iota(jnp.int32, (kc.shape[0], LANES), 1) < HD
    q = (q_ref[...].astype(F32) * SCALE).astype(BF16)
    zero = jnp.zeros((tq, LANES), BF16)
    for kv in range(KV_A):
        heads = list(range(g * kv, g * kv + g))
        qst = jnp.concatenate(
            [jnp.where(lo_q if h % 2 == 0 else jnp.logical_not(lo_q),
                       q[:, (h // 2) * LANES:(h // 2 + 1) * LANES], zero) for h in heads], axis=0)
        snk = jnp.concatenate([jnp.full((tq, 1), sink_ref[h], F32) for h in heads], axis=0)
        s_w = _dot_nt(qst, dup(kw, lo_k, kv))
        if banded:
            s_w = jnp.where(mask, s_w, -jnp.inf)
        m = jnp.maximum(jnp.max(s_w, axis=-1, keepdims=True), snk)
        if has_ctx:
            s_c = _dot_nt(qst, dup(kc, lo_c, kv))
            m = jnp.maximum(m, jnp.max(s_c, axis=-1, keepdims=True))
        e_w = jnp.exp(s_w - m)
        denom = jnp.sum(e_w, axis=-1, keepdims=True) + jnp.exp(snk - m)
        acc = _dot(e_w.astype(BF16), dup(vw, lo_k, kv))
        if has_ctx:
            e_c = jnp.exp(s_c - m)
            denom = denom + jnp.sum(e_c, axis=-1, keepdims=True)
            acc = acc + _dot(e_c.astype(BF16), dup(vc, lo_c, kv))
        o = acc / denom
        for c in range(g // 2):
            col = jnp.where(lo_q, o[(2 * c) * tq:(2 * c + 1) * tq], o[(2 * c + 1) * tq:(2 * c + 2) * tq])
            cc = (g // 2) * kv + c
            o_ref[:, cc * LANES:(cc + 1) * LANES] = col.astype(o_ref.dtype)


def _attn_a(sink, q, k, v, k_ctx, v_ctx, tq, banded):
    b, t, _ = q.shape
    has_ctx = k_ctx is not None
    full = lambda bi, i: (bi, 0, 0)
    in_specs = [pl.BlockSpec(memory_space=pltpu.SMEM),
                pl.BlockSpec((None, tq, D_A), lambda bi, i: (bi, i, 0)),
                pl.BlockSpec((None, t, KV_A * HD), full),
                pl.BlockSpec((None, t, KV_A * HD), full)]
    args = [sink, q, k, v]
    if has_ctx:
        s = k_ctx.shape[1]
        in_specs += [pl.BlockSpec((None, s, KV_A * HD), full)] * 2
        args += [k_ctx, v_ctx]
    return pl.pallas_call(
        functools.partial(_attn_a_kernel, tq=tq, seq_len=t, banded=banded, has_ctx=has_ctx),
        out_shape=jax.ShapeDtypeStruct((b, t, D_A), BF16),
        grid=(b, t // tq),
        in_specs=in_specs,
        out_specs=pl.BlockSpec((None, tq, D_A), lambda bi, i: (bi, i, 0)),
        compiler_params=_params(("arbitrary", "arbitrary")),
        name="attn_a_latent" if has_ctx else "attn_a_context",
    )(*args)


def _attn_c_kernel(*refs, tq, ts, n_loc, has_ctx, lam_init):
    if has_ctx:
        lq1, lk1, lq2, lk2, sub_ref, q_ref, k_ref, v_ref, kc_ref, vc_ref, o_ref = refs
    else:
        lq1, lk1, lq2, lk2, sub_ref, q_ref, k_ref, v_ref, o_ref = refs
    lam = (jnp.exp(jnp.sum(lq1[...] * lk1[...], axis=-1, keepdims=True))
           - jnp.exp(jnp.sum(lq2[...] * lk2[...], axis=-1, keepdims=True)) + lam_init)
    lo = lax.broadcasted_iota(jnp.int32, (tq, LANES), 1) < HD
    zero = jnp.zeros((tq, LANES), BF16)
    heads = [slice(h * LANES, (h + 1) * LANES) for h in range(H_C)]
    qz = []
    for cols in heads:
        q12 = (q_ref[:, cols].astype(F32) * SCALE).astype(BF16)
        qz.append(jnp.concatenate([jnp.where(lo, q12, zero), jnp.where(lo, zero, q12)], axis=0))

    def update(carry, kt, vt):
        new = []
        for h, cols in enumerate(heads):
            m, l, acc = carry[h]
            s = _dot_nt(qz[h], kt[:, cols].astype(BF16))
            m_new = jnp.maximum(m, jnp.max(s, axis=-1, keepdims=True))
            alpha = jnp.exp(m - m_new)
            p = jnp.exp(s - m_new)
            psum = p[:, 0:LANES]
            for j in range(1, p.shape[1] // LANES):
                psum = psum + p[:, j * LANES:(j + 1) * LANES]
            l = alpha * l + psum
            acc = alpha * acc + _dot(p.astype(BF16), vt[:, cols].astype(BF16))
            new.append((m_new, l, acc))
        return tuple(new)

    def body(j, carry):
        r0 = pl.multiple_of(j * ts, ts)
        return update(carry, k_ref[pl.ds(r0, ts), :], v_ref[pl.ds(r0, ts), :])

    carry = tuple((jnp.full((2 * tq, 1), -jnp.inf, F32), jnp.zeros((2 * tq, LANES), F32),
                   jnp.zeros((2 * tq, LANES), F32)) for _ in heads)
    carry = lax.fori_loop(0, n_loc, body, carry, unroll=True)
    if has_ctx:
        carry = update(carry, kc_ref[...], vc_ref[...])
    for h, cols in enumerate(heads):
        _, l, acc = carry[h]
        l = jnp.sum(l, axis=-1, keepdims=True)
        o = acc[:tq] / l[:tq] - lam * (acc[tq:] / l[tq:])
        o = o * lax.rsqrt(jnp.mean(o * o, axis=-1, keepdims=True) + EPS) * sub_ref[...] * (1.0 - lam_init)
        o_ref[:, cols] = o.astype(o_ref.dtype)


def _attn_c(lam_vecs, subln, q, k, v, k_ctx, v_ctx, tq, ts, lam_init):
    b, t, _ = q.shape
    s_loc = k.shape[1]
    has_ctx = k_ctx is not None
    full = lambda bi, i: (bi, 0, 0)
    vec = lambda bi, i: (0, 0)
    in_specs = ([pl.BlockSpec((1, HD), vec)] * 4 + [pl.BlockSpec((1, DV_C), vec)]
                + [pl.BlockSpec((None, tq, D_C), lambda bi, i: (bi, i, 0)),
                   pl.BlockSpec((None, s_loc, D_C), full),
                   pl.BlockSpec((None, s_loc, D_C), full)])
    args = list(lam_vecs) + [subln, q, k, v]
    if has_ctx:
        sc = k_ctx.shape[1]
        in_specs += [pl.BlockSpec((None, sc, D_C), full)] * 2
        args += [k_ctx, v_ctx]
    return pl.pallas_call(
        functools.partial(_attn_c_kernel, tq=tq, ts=ts, n_loc=s_loc // ts, has_ctx=has_ctx, lam_init=lam_init),
        out_shape=jax.ShapeDtypeStruct((b, t, D_C), BF16),
        grid=(b, t // tq),
        in_specs=in_specs,
        out_specs=pl.BlockSpec((None, tq, D_C), lambda bi, i: (bi, i, 0)),
        compiler_params=_params(("arbitrary", "arbitrary")),
        name="attn_c_latent" if has_ctx else "attn_c_context",
    )(*args)


def _lru_kernel(xc_ref, xp_ref, xn_ref, cw_ref, cb_ref, wg_ref, bg_ref, lam_ref, h0_ref,
                ob_ref, st_ref, a_s, b_s, hs, hf_s, hcar, *, tc, nc):
    p = pl.program_id(1)
    c = pl.program_id(2)
    cidx = c + p * (nc - 1 - 2 * c)
    cur = xc_ref[...]
    prev = jnp.where(cidx > 0, xp_ref[...], 0.0)
    nxt = jnp.where(cidx < nc - 1, xn_ref[...], 0.0)
    ext = jnp.concatenate([prev, cur, nxt], axis=0)
    n_ext = tc + 2 * SUBLANES
    xb = jnp.broadcast_to(cb_ref[...], (tc, D_LRU))
    for j in range(CONV_W):
        off = j - CONV_W // 2
        sh = ext if off == 0 else pltpu.roll(ext, (-off) % n_ext, 0)
        xb = xb + sh[SUBLANES:SUBLANES + tc] * cw_ref[j:j + 1, :]
    gm = _dot(xb.astype(BF16), wg_ref[...]) + bg_ref[...]
    r = _sigmoid(gm[:, :D_LRU])
    ig = _sigmoid(gm[:, D_LRU:])
    nl = -lam_ref[...]
    softplus = jnp.maximum(nl, 0.0) + jnp.log1p(jnp.exp(-jnp.abs(nl)))
    log_a = -LRU_C * r * softplus
    a = jnp.exp(log_a)
    bb = jnp.sqrt(-jnp.tanh(log_a) * (a * a + 1.0)) * (ig * xb)

    ng = tc // SUBLANES
    a3 = a.reshape(ng, SUBLANES, D_LRU)
    b3 = bb.reshape(ng, SUBLANES, D_LRU)
    row = lax.broadcasted_iota(jnp.int32, (ng, SUBLANES, D_LRU), 1)
    order = row + p * (SUBLANES - 1 - 2 * row)
    for sh in (1, 2, 4):
        shift = sh + p * (SUBLANES - 2 * sh)
        a_prev = pltpu.roll(a3, shift, 1)
        b_prev = pltpu.roll(b3, shift, 1)
        valid = order >= sh
        b3 = jnp.where(valid, a3 * b_prev + b3, b3)
        a3 = jnp.where(valid, a3 * a_prev, a3)
    a_tot = jnp.where(p == 0, a3[:, SUBLANES - 1:SUBLANES, :], a3[:, 0:1, :])
    b_tot = jnp.where(p == 0, b3[:, SUBLANES - 1:SUBLANES, :], b3[:, 0:1, :])
    a_s[...] = jnp.broadcast_to(a_tot, a3.shape).reshape(tc, D_LRU)
    b_s[...] = jnp.broadcast_to(b_tot, b3.shape).reshape(tc, D_LRU)

    @pl.when(c == 0)
    def _():
        hcar[...] = jnp.broadcast_to(h0_ref[pl.ds(p, 1), :], (SUBLANES, D_LRU))

    def group(gi, h):
        gidx = gi + p * (ng - 1 - 2 * gi)
        base = pl.multiple_of(gidx * SUBLANES, SUBLANES)
        hs[pl.ds(base, SUBLANES), :] = h
        return a_s[pl.ds(base, SUBLANES), :] * h + b_s[pl.ds(base, SUBLANES), :]

    h = lax.fori_loop(0, ng, group, hcar[...], unroll=8)
    hcar[...] = h
    hloc = (a3 * hs[...].reshape(ng, SUBLANES, D_LRU) + b3).reshape(tc, D_LRU)
    r0 = pl.multiple_of(cidx * tc, tc)

    @pl.when(p == 0)
    def _():
        hf_s[pl.ds(r0, tc), :] = hloc

    @pl.when(p == 1)
    def _():
        ob_ref[...] = (hf_s[pl.ds(r0, tc), :] + hloc).astype(ob_ref.dtype)

    @pl.when(c == nc - 1)
    def _():
        st_ref[pl.ds(p, 1), :] = h[0:1, :]


def _lru(xb, conv_w, conv_b, wg, bg, lam, h0, tc):
    b, t, _ = xb.shape
    nc = t // tc
    nb8 = t // SUBLANES
    per = tc // SUBLANES
    cidx = lambda p, c: c + p * (nc - 1 - 2 * c)
    vec = lambda bi, p, c: (0, 0)
    in_specs = [
        pl.BlockSpec((None, tc, D_LRU), lambda bi, p, c: (bi, cidx(p, c), 0)),
        pl.BlockSpec((None, SUBLANES, D_LRU), lambda bi, p, c: (bi, jnp.maximum(cidx(p, c) * per - 1, 0), 0)),
        pl.BlockSpec((None, SUBLANES, D_LRU), lambda bi, p, c: (bi, jnp.minimum((cidx(p, c) + 1) * per, nb8 - 1), 0)),
        pl.BlockSpec((CONV_W, D_LRU), vec),
        pl.BlockSpec((1, D_LRU), vec),
        pl.BlockSpec((None, D_LRU, 2 * D_LRU), lambda bi, p, c: (p, 0, 0)),
        pl.BlockSpec((None, 1, 2 * D_LRU), lambda bi, p, c: (p, 0, 0)),
        pl.BlockSpec((None, 1, D_LRU), lambda bi, p, c: (p, 0, 0)),
        pl.BlockSpec((None, 2, D_LRU), lambda bi, p, c: (bi, 0, 0)),
    ]
    return pl.pallas_call(
        functools.partial(_lru_kernel, tc=tc, nc=nc),
        out_shape=[jax.ShapeDtypeStruct((b, t, D_LRU), BF16), jax.ShapeDtypeStruct((b, 2, D_LRU), F32)],
        grid=(b, 2, nc),
        in_specs=in_specs,
        out_specs=[pl.BlockSpec((None, tc, D_LRU), lambda bi, p, c: (bi, nc - 1 - p * c, 0)),
                   pl.BlockSpec((None, 2, D_LRU), lambda bi, p, c: (bi, 0, 0))],
        scratch_shapes=[pltpu.VMEM((tc, D_LRU), F32), pltpu.VMEM((tc, D_LRU), F32), pltpu.VMEM((tc, D_LRU), F32),
                        pltpu.VMEM((t, D_LRU), F32), pltpu.VMEM((SUBLANES, D_LRU), F32)],
        compiler_params=_params(("arbitrary", "arbitrary", "arbitrary")),
        name="lru",
    )(xb, xb, xb, conv_w, conv_b, wg, bg, lam, h0)


def _tail_kernel(x_ref, mod_ref, ng_ref, oa_ref, ob_ref, oc_ref, wg_ref, wm_ref, wbr_ref, wout_ref, o_ref):
    x = x_ref[...]
    mod = mod_ref[...]
    hb = _normed_input(x, mod, ng_ref[...]).astype(BF16)
    y = None
    for br, br_ref in enumerate((oa_ref, ob_ref, oc_ref)):
        gt = _dot(hb, wg_ref[:, br * D_A:(br + 1) * D_A])
        u = (br_ref[...].astype(F32) * (gt * _sigmoid(gt))).astype(BF16)
        proj = _dot(u, wbr_ref[br])
        mg = _sigmoid(_dot(hb, wm_ref[:, br * D_MODEL:(br + 1) * D_MODEL]))
        y = mg * proj if y is None else y + mg * proj
    gate = mod[:, 2 * D_MODEL:]
    o_ref[...] = x + gate * _dot(y.astype(BF16), wout_ref[...])


def _tail(x2d, mod_rows, ng, oa, ob, oc, wg, wm, wbr, wout, seq_len, tm=256):
    n = x2d.shape[0]
    per_seq = seq_len // tm
    row = lambda i: (i, 0)
    const = lambda i: (0, 0)
    once = dict(pipeline_mode=pl.Buffered(1))
    return pl.pallas_call(
        _tail_kernel,
        out_shape=jax.ShapeDtypeStruct((n, D_MODEL), F32),
        grid=(n // tm,),
        in_specs=[pl.BlockSpec((tm, D_MODEL), row),
                  pl.BlockSpec((None, 1, 3 * D_MODEL), lambda i: (i // per_seq, 0, 0)),
                  pl.BlockSpec((1, D_MODEL), const),
                  pl.BlockSpec((tm, D_A), row),
                  pl.BlockSpec((tm, D_LRU), row),
                  pl.BlockSpec((tm, D_C), row),
                  pl.BlockSpec((D_MODEL, 3 * D_A), const, **once),
                  pl.BlockSpec((D_MODEL, 3 * D_MODEL), const, **once),
                  pl.BlockSpec((3, D_A, D_MODEL), lambda i: (0, 0, 0), **once),
                  pl.BlockSpec((D_MODEL, D_MODEL), const, **once)],
        out_specs=pl.BlockSpec((tm, D_MODEL), row),
        compiler_params=_params(("arbitrary",)),
        name="tail",
    )(x2d, mod_rows, ng, oa, ob, oc, wg, wm, wbr, wout)


def _rope_tables(seq_len):
    pos = jnp.arange(seq_len)
    row = (pos // GRID_W).astype(F32)
    col = (pos % GRID_W).astype(F32)
    inv = jnp.power(ROPE_BASE, -jnp.arange(ROPE_FREQS, dtype=F32) / ROPE_FREQS)
    ang_r = row[:, None] * inv
    ang_c = col[:, None] * inv
    cos = jnp.concatenate([jnp.cos(ang_r)] * 2 + [jnp.cos(ang_c)] * 2, axis=-1)
    sin = jnp.concatenate([-jnp.sin(ang_r), jnp.sin(ang_r), -jnp.sin(ang_c), jnp.sin(ang_c)], axis=-1)
    return jnp.tile(cos, (1, LANES // HD)), jnp.tile(sin, (1, LANES // HD))


def _block_diag(w):
    eye = jnp.eye(LRU_BLOCKS, dtype=w.dtype)
    return (w[:, :, None, :] * eye[:, None, :, None]).reshape(D_LRU, D_LRU)


def kernel(x_prompt, x_sample, cache_a_k, cache_a_v, cache_c_k, cache_c_v, state_lru, c, c_ctx, norm_g, mod_w, mod_b, w_in, qn_a, kn_a, sink_a, conv_w, conv_b, lru_wa, lru_ba, lru_wx, lru_bx, lru_lam, qn_c, kn_c, lam_q1, lam_k1, lam_q2, lam_k2, subln_c, w_br_a, w_br_b, w_br_c, w_out):
    bp, sp, _ = x_prompt.shape
    bs, ss, _ = x_sample.shape
    past = cache_a_k.shape[2]

    cvecs = jnp.concatenate([c, c_ctx[None, :], jnp.zeros((SUBLANES - bs - 1, D_MODEL), F32)], axis=0)
    mod = _modulation(cvecs, mod_w, mod_b)
    rope = _rope_tables(ss)

    sl = lambda w, ab: w[:, :, ab[0]:ab[1]]
    wb = w_in.astype(BF16)
    w1 = jnp.concatenate([sl(wb, _QA), sl(wb, _KA), sl(wb, _QC), sl(wb, _KC)], axis=-1)
    w2 = jnp.concatenate([sl(wb, _VA), sl(wb, _VC), sl(wb, _XB)], axis=-1)
    wg = jnp.concatenate([sl(wb, _GA), sl(wb, _GB), sl(wb, _GC)], axis=-1)
    wm = sl(wb, _MG)
    wbr = jnp.stack([w_br_a, w_br_b, w_br_c], axis=1).astype(BF16)
    wout = w_out.astype(BF16)
    gains = jnp.concatenate([jnp.tile(qn_a, (1, H_A)), jnp.tile(kn_a, (1, KV_A)),
                             jnp.tile(qn_c, (1, 2 * H_C)), jnp.tile(kn_c, (1, 2 * H_C))], axis=-1)

    xp = x_prompt.reshape(bp * sp, D_MODEL)
    xs = x_sample.reshape(bs * ss, D_MODEL)
    zeros_h0 = jnp.zeros((bp, 2, D_LRU), F32)
    ka_l, va_l, kc_l, vc_l, st_l = [], [], [], [], []
    for l in range(DEPTH):
        lam_init = 0.8 - 0.6 * math.exp(-0.3 * l)
        ng = norm_g[l][None, :]
        mod_ctx = mod[l, bs:bs + 1][:, None, :]
        mod_lat = mod[l, 0:bs][:, None, :]
        lru_wg = jnp.stack([jnp.concatenate([_block_diag(lru_wa[l, d]), _block_diag(lru_wx[l, d])], axis=-1)
                            for d in range(2)]).astype(BF16)
        lru_bg = jnp.concatenate([lru_ba[l], lru_bx[l]], axis=-1)[:, None, :]
        lam_vecs = [v[l][None, :] for v in (lam_q1, lam_k1, lam_q2, lam_k2)]
        subln = subln_c[l][None, :]
        g_l = gains[l][None, :]
        lru_args = (conv_w[l], conv_b[l][None, :], lru_wg, lru_bg, lru_lam[l][:, None, :])
        tail_w = (wg[l], wm[l], wbr[l], wout[l])

        qa, ka, qc, kc, va, vc, xb = _front(xp, mod_ctx, ng, w1[l], w2[l], g_l, None, bp * sp, F32)
        r3 = lambda a: a.reshape(bp, sp, a.shape[-1])
        oa = _attn_a(sink_a[l], r3(qa), r3(ka), r3(va), None, None, tq=sp, banded=False)
        ob, st = _lru(r3(xb), *lru_args, zeros_h0, tc=sp)
        oc = _attn_c(lam_vecs, subln, r3(qc), r3(kc), r3(vc), None, None, tq=sp, ts=sp, lam_init=lam_init)
        flat = lambda a: a.reshape(bp * sp, a.shape[-1])
        xp = _tail(xp, mod_ctx, ng, flat(oa), flat(ob), flat(oc), *tail_w, bp * sp)
        ka_l.append(ka.reshape(bp, sp, KV_A, HD))
        va_l.append(va.reshape(bp, sp, KV_A, HD))
        kc_l.append(kc.reshape(bp, sp, H_C, 2, HD))
        vc_l.append(vc.reshape(bp, sp, H_C, DV_C))
        st_l.append(st)

        qa, ka, qc, kc, va, vc, xb = _front(xs, mod_lat, ng, w1[l], w2[l], g_l, rope, ss, BF16)
        r3 = lambda a: a.reshape(bs, ss, a.shape[-1])
        cka = cache_a_k[:, l].reshape(bs, past, KV_A * HD)
        cva = cache_a_v[:, l].reshape(bs, past, KV_A * HD)
        ckc = cache_c_k[:, l].reshape(bs, past, D_C)
        cvc = cache_c_v[:, l].reshape(bs, past, D_C)
        oa = _attn_a(sink_a[l], r3(qa), r3(ka), r3(va), cka, cva, tq=WINDOW, banded=True)
        ob, _ = _lru(r3(xb), *lru_args, state_lru[:, l], tc=512)
        oc = _attn_c(lam_vecs, subln, r3(qc), r3(kc), r3(vc), ckc, cvc, tq=256, ts=512, lam_init=lam_init)
        flat = lambda a: a.reshape(bs * ss, a.shape[-1])
        xs = _tail(xs, mod_lat, ng, flat(oa), flat(ob), flat(oc), *tail_w, ss)

    return (xp.reshape(bp, sp, D_MODEL), xs.reshape(bs, ss, D_MODEL),
            jnp.stack(ka_l, axis=1), jnp.stack(va_l, axis=1), jnp.stack(kc_l, axis=1),
            jnp.stack(vc_l, axis=1), jnp.stack(st_l, axis=1))
```

```python
import functools
import math

import jax
import jax.numpy as jnp
import numpy as np
from jax import lax
from jax.experimental import pallas as pl
from jax.experimental.pallas import tpu as pltpu

F32 = jnp.float32
BF16 = jnp.bfloat16

D_MODEL = 1024
DEPTH = 2
GRID_W = 64
HD = 64
SCALE = 1.0 / math.sqrt(HD)
LOG2E = math.log2(math.e)
Q_SCALE = SCALE * LOG2E
H_A = 8
KV_A = 2
WINDOW = 128
D_A = H_A * HD
D_LRU = 512
LRU_BLOCKS = 8
LRU_BW = D_LRU // LRU_BLOCKS
CONV_W = 4
LRU_C = 8.0
H_C = 4
DV_C = 2 * HD
D_C = H_C * DV_C
ROPE_BASE = 10000.0
ROPE_FREQS = HD // 4
EPS = 1e-6
LANES = 128
SUBLANES = 8
VMEM_LIMIT = 56 * 1024 * 1024

_OFF = np.cumsum([0, D_A, KV_A * HD, KV_A * HD, D_A, D_LRU, D_LRU,
                  H_C * 2 * HD, H_C * 2 * HD, D_C, D_C, 3 * D_MODEL])
(_QA, _KA, _VA, _GA, _XB, _GB, _QC, _KC, _VC, _GC, _MG) = [
    (int(_OFF[i]), int(_OFF[i + 1])) for i in range(11)]
N_NORM = D_A + KV_A * HD + 2 * H_C * 2 * HD
N_PLAIN = KV_A * HD + D_C + D_LRU


def _params(sem, vmem=VMEM_LIMIT):
    return pltpu.CompilerParams(dimension_semantics=sem, vmem_limit_bytes=vmem)


def _dot(a, b):
    return jnp.dot(a, b, preferred_element_type=F32)


def _dot_nt(a, b):
    return lax.dot_general(a, b, (((1,), (1,)), ((), ())), preferred_element_type=F32)


def _sigmoid(x):
    return 1.0 / (1.0 + jnp.exp(-x))


def _mod_kernel(c_ref, w_ref, b_ref, o_ref):
    c = c_ref[...]
    a = c * _sigmoid(c)
    w = w_ref[...]
    a_hi = a.astype(BF16)
    a_lo = (a - a_hi.astype(F32)).astype(BF16)
    w_hi = w.astype(BF16)
    w_lo = (w - w_hi.astype(F32)).astype(BF16)
    o_ref[...] = _dot(a_hi, w_hi) + _dot(a_hi, w_lo) + _dot(a_lo, w_hi) + b_ref[...]


def _modulation(cvecs, mod_w, mod_b):
    tn = 768
    return pl.pallas_call(
        _mod_kernel,
        out_shape=jax.ShapeDtypeStruct((DEPTH, SUBLANES, 3 * D_MODEL), F32),
        grid=(DEPTH, 3 * D_MODEL // tn),
        in_specs=[pl.BlockSpec((SUBLANES, D_MODEL), lambda l, j: (0, 0)),
                  pl.BlockSpec((None, D_MODEL, tn), lambda l, j: (l, 0, j)),
                  pl.BlockSpec((None, 1, tn), lambda l, j: (l, 0, j))],
        out_specs=pl.BlockSpec((None, SUBLANES, tn), lambda l, j: (l, 0, j)),
        compiler_params=_params(("arbitrary", "arbitrary")),
        name="modulation",
    )(cvecs, mod_w, mod_b.reshape(DEPTH, 1, 3 * D_MODEL))


def _normed_input(x, mod, ng):
    ms = jnp.mean(x * x, axis=-1, keepdims=True)
    shift = mod[:, 0:D_MODEL]
    scale = mod[:, D_MODEL:2 * D_MODEL]
    return (x * lax.rsqrt(ms + EPS) * ng) * (1.0 + scale) + shift


def _front_kernel(*refs, use_rope):
    if use_rope:
        (x_ref, mod_ref, ng_ref, w1_ref, w2_ref, gain_ref, cos_ref, sin_ref,
         qa_ref, ka_ref, qc_ref, kc_ref, va_ref, vc_ref, xb_ref) = refs
    else:
        (x_ref, mod_ref, ng_ref, w1_ref, w2_ref, gain_ref,
         qa_ref, ka_ref, qc_ref, kc_ref, va_ref, vc_ref, xb_ref) = refs
    x = x_ref[...]
    tm = x.shape[0]
    hb = _normed_input(x, mod_ref[...], ng_ref[...]).astype(BF16)
    p1 = _dot(hb, w1_ref[...])
    lane = lax.broadcasted_iota(jnp.int32, (tm, LANES), 1)
    first_half = (lane & ROPE_FREQS) == 0
    wide = 2 * LANES
    same_head = (lax.broadcasted_iota(jnp.int32, (wide, wide), 0) // HD
                 == lax.broadcasted_iota(jnp.int32, (wide, wide), 1) // HD)
    ones_bd = jnp.where(same_head, 1.0, 0.0).astype(BF16)
    sums = []
    for c0 in range(0, N_NORM, wide):
        w = min(wide, N_NORM - c0)
        sq = p1[:, c0:c0 + w] * p1[:, c0:c0 + w]
        sq_hi = sq.astype(BF16)
        sq_lo = (sq - sq_hi.astype(F32)).astype(BF16)
        sums.append(_dot(sq_hi, ones_bd[:w, :w]) + _dot(sq_lo, ones_bd[:w, :w]))
    outs = ([(qa_ref, c) for c in range(4)] + [(ka_ref, 0)]
            + [(qc_ref, c) for c in range(4)] + [(kc_ref, c) for c in range(4)])
    for c, (o_ref, oc) in enumerate(outs):
        pc = p1[:, c * LANES:(c + 1) * LANES]
        msq = sums[c // 2][:, (c % 2) * LANES:(c % 2 + 1) * LANES] * (1.0 / HD)
        y = pc * lax.rsqrt(msq + EPS) * gain_ref[:, c * LANES:(c + 1) * LANES]
        if use_rope:
            partner = jnp.where(first_half, pltpu.roll(y, LANES - ROPE_FREQS, 1),
                                pltpu.roll(y, ROPE_FREQS, 1))
            y = y * cos_ref[...] + partner * sin_ref[...]
        o_ref[:, oc * LANES:(oc + 1) * LANES] = y.astype(o_ref.dtype)
    p2 = _dot(hb, w2_ref[...])
    va_ref[...] = p2[:, 0:KV_A * HD].astype(va_ref.dtype)
    vc_ref[...] = p2[:, KV_A * HD:KV_A * HD + D_C].astype(vc_ref.dtype)
    xb_ref[...] = p2[:, KV_A * HD + D_C:]


def _front(x2d, mod_rows, ng, w1, w2, gains, rope, seq_len, kv_dtype, tm=256):
    n = x2d.shape[0]
    use_rope = rope is not None
    row = lambda i: (i, 0)
    const = lambda i: (0, 0)
    per_seq = seq_len // tm
    in_specs = [pl.BlockSpec((tm, D_MODEL), row),
                pl.BlockSpec((None, 1, 3 * D_MODEL), lambda i: (i // per_seq, 0, 0)),
                pl.BlockSpec((1, D_MODEL), const),
                pl.BlockSpec((D_MODEL, N_NORM), const),
                pl.BlockSpec((D_MODEL, N_PLAIN), const),
                pl.BlockSpec((1, N_NORM), const)]
    args = [x2d, mod_rows, ng, w1, w2, gains]
    if use_rope:
        in_specs += [pl.BlockSpec((tm, LANES), lambda i: (i % per_seq, 0))] * 2
        args += list(rope)
    widths = (D_A, KV_A * HD, H_C * 2 * HD, H_C * 2 * HD, KV_A * HD, D_C, D_LRU)
    dtypes = (BF16, kv_dtype, BF16, kv_dtype, kv_dtype, kv_dtype, F32)
    return pl.pallas_call(
        functools.partial(_front_kernel, use_rope=use_rope),
        out_shape=[jax.ShapeDtypeStruct((n, w), d) for w, d in zip(widths, dtypes)],
        grid=(n // tm,),
        in_specs=in_specs,
        out_specs=[pl.BlockSpec((tm, w), row) for w in widths],
        compiler_params=_params(("arbitrary",)),
        name="front_rope" if use_rope else "front",
    )(*args)


def _attn_a_kernel(*refs, tq, seq_len, banded, has_ctx):
    if has_ctx:
        sink_ref, q_ref, k_ref, v_ref, kc_ref, vc_ref, o_ref = refs
    else:
        sink_ref, q_ref, k_ref, v_ref, o_ref = refs
    i = pl.program_id(1)
    g = H_A // KV_A
    if banded:
        nwin = tq + 2 * WINDOW
        start = pl.multiple_of(jnp.clip(i * tq - WINDOW, 0, seq_len - nwin), WINDOW)
        kall = k_ref[pl.ds(start, nwin), :].astype(F32)
        vall = v_ref[pl.ds(start, nwin), :].astype(F32)
    else:
        nwin = seq_len
        kall = k_ref[...].astype(F32)
        vall = v_ref[...].astype(F32)
    if has_ctx:
        kall = jnp.concatenate([kall, kc_ref[...].astype(F32)], axis=0)
        vall = jnp.concatenate([vall, vc_ref[...].astype(F32)], axis=0)
    nk = kall.shape[0]
    if banded:
        kpos = start + lax.broadcasted_iota(jnp.int32, (tq, nwin), 1)
        qpos = i * tq + lax.broadcasted_iota(jnp.int32, (tq, nwin), 0)
        bias = jnp.where(jnp.abs(kpos - qpos) <= WINDOW, 0.0, -jnp.inf)
    lo_k = lax.broadcasted_iota(jnp.int32, (nk, LANES), 1) < HD
    lo_q = lax.broadcasted_iota(jnp.int32, (tq, LANES), 1) < HD
    k_sw = pltpu.roll(kall, HD, 1)
    v_sw = pltpu.roll(vall, HD, 1)
    kd = (jnp.where(lo_k, kall, k_sw).astype(BF16), jnp.where(lo_k, k_sw, kall).astype(BF16))
    vd = [[jnp.where(lo_k, vall, 1.0).astype(BF16), jnp.where(lo_k, 1.0, v_sw).astype(BF16)],
          [jnp.where(lo_k, v_sw, 1.0).astype(BF16), jnp.where(lo_k, 1.0, vall).astype(BF16)]]
    q = q_ref[...]
    zero = jnp.zeros((tq, LANES), BF16)
    outs = []
    for h in range(H_A):
        kv, half = h // g, h % 2
        qz = jnp.where(lo_q if half == 0 else jnp.logical_not(lo_q), q[:, (h // 2) * LANES:(h // 2 + 1) * LANES], zero)
        s = _dot_nt(qz, kd[kv])
        if banded:
            s = jnp.concatenate([s[:, :nwin] + bias, s[:, nwin:]], axis=1)
        snk = sink_ref[h] * LOG2E
        m = jnp.maximum(jnp.max(s, axis=-1, keepdims=True), snk)
        e = jnp.exp2(s - m)
        pv = _dot(e.astype(BF16), vd[kv][half])
        outs.append(pv / (pltpu.roll(pv, HD, 1) + jnp.exp2(snk - m)))
    for c in range(H_A // 2):
        pair = jnp.where(lo_q, outs[2 * c], outs[2 * c + 1])
        o_ref[:, c * LANES:(c + 1) * LANES] = pair.astype(o_ref.dtype)


def _attn_a(sink, q, k, v, k_ctx, v_ctx, tq, banded):
    b, t, _ = q.shape
    has_ctx = k_ctx is not None
    full = lambda bi, i: (bi, 0, 0)
    in_specs = [pl.BlockSpec(memory_space=pltpu.SMEM),
                pl.BlockSpec((None, tq, D_A), lambda bi, i: (bi, i, 0)),
                pl.BlockSpec((None, t, KV_A * HD), full),
                pl.BlockSpec((None, t, KV_A * HD), full)]
    args = [sink, q, k, v]
    if has_ctx:
        s = k_ctx.shape[1]
        in_specs += [pl.BlockSpec((None, s, KV_A * HD), full)] * 2
        args += [k_ctx, v_ctx]
    return pl.pallas_call(
        functools.partial(_attn_a_kernel, tq=tq, seq_len=t, banded=banded, has_ctx=has_ctx),
        out_shape=jax.ShapeDtypeStruct((b, t, D_A), BF16),
        grid=(b, t // tq),
        in_specs=in_specs,
        out_specs=pl.BlockSpec((None, tq, D_A), lambda bi, i: (bi, i, 0)),
        compiler_params=_params(("arbitrary", "arbitrary")),
        name="attn_a_latent" if has_ctx else "attn_a_context",
    )(*args)


def _attn_c_kernel(*refs, tq, ts, n_loc, has_ctx, lam_init):
    if has_ctx:
        lq1, lk1, lq2, lk2, sub_ref, q_ref, k_ref, v_ref, kc_ref, vc_ref, o_ref = refs
    else:
        lq1, lk1, lq2, lk2, sub_ref, q_ref, k_ref, v_ref, o_ref = refs
    lam = (jnp.exp(jnp.sum(lq1[...] * lk1[...], axis=-1, keepdims=True))
           - jnp.exp(jnp.sum(lq2[...] * lk2[...], axis=-1, keepdims=True)) + lam_init)
    lo = lax.broadcasted_iota(jnp.int32, (tq, LANES), 1) < HD
    zero = jnp.zeros((tq, LANES), BF16)
    heads = [slice(h * LANES, (h + 1) * LANES) for h in range(H_C)]
    qz = []
    for cols in heads:
        q12 = q_ref[:, cols]
        qz.append(jnp.concatenate([jnp.where(lo, q12, zero), jnp.where(lo, zero, q12)], axis=0))

    def update(carry, kt, vt):
        new = []
        for h, cols in enumerate(heads):
            m, l, acc = carry[h]
            s = _dot_nt(qz[h], kt[:, cols].astype(BF16))
            m_new = jnp.maximum(m, jnp.max(s, axis=-1, keepdims=True))
            alpha = jnp.exp2(m - m_new)
            p = jnp.exp2(s - m_new)
            psum = p[:, 0:LANES]
            for j in range(1, p.shape[1] // LANES):
                psum = psum + p[:, j * LANES:(j + 1) * LANES]
            l = alpha * l + psum
            acc = alpha * acc + _dot(p.astype(BF16), vt[:, cols].astype(BF16))
            new.append((m_new, l, acc))
        return tuple(new)

    def body(j, carry):
        r0 = pl.multiple_of(j * ts, ts)
        return update(carry, k_ref[pl.ds(r0, ts), :], v_ref[pl.ds(r0, ts), :])

    carry = tuple((jnp.full((2 * tq, 1), -jnp.inf, F32), jnp.zeros((2 * tq, LANES), F32),
                   jnp.zeros((2 * tq, LANES), F32)) for _ in heads)
    carry = lax.fori_loop(0, n_loc, body, carry, unroll=True)
    if has_ctx:
        carry = update(carry, kc_ref[...], vc_ref[...])
    for h, cols in enumerate(heads):
        _, l, acc = carry[h]
        l = jnp.sum(l, axis=-1, keepdims=True)
        o = acc[:tq] / l[:tq] - lam * (acc[tq:] / l[tq:])
        o = o * lax.rsqrt(jnp.mean(o * o, axis=-1, keepdims=True) + EPS) * sub_ref[...] * (1.0 - lam_init)
        o_ref[:, cols] = o.astype(o_ref.dtype)


def _attn_c(lam_vecs, subln, q, k, v, k_ctx, v_ctx, tq, ts, lam_init):
    b, t, _ = q.shape
    s_loc = k.shape[1]
    has_ctx = k_ctx is not None
    full = lambda bi, i: (bi, 0, 0)
    vec = lambda bi, i: (0, 0)
    in_specs = ([pl.BlockSpec((1, HD), vec)] * 4 + [pl.BlockSpec((1, DV_C), vec)]
                + [pl.BlockSpec((None, tq, D_C), lambda bi, i: (bi, i, 0)),
                   pl.BlockSpec((None, s_loc, D_C), full),
                   pl.BlockSpec((None, s_loc, D_C), full)])
    args = list(lam_vecs) + [subln, q, k, v]
    if has_ctx:
        sc = k_ctx.shape[1]
        in_specs += [pl.BlockSpec((None, sc, D_C), full)] * 2
        args += [k_ctx, v_ctx]
    return pl.pallas_call(
        functools.partial(_attn_c_kernel, tq=tq, ts=ts, n_loc=s_loc // ts, has_ctx=has_ctx, lam_init=lam_init),
        out_shape=jax.ShapeDtypeStruct((b, t, D_C), BF16),
        grid=(b, t // tq),
        in_specs=in_specs,
        out_specs=pl.BlockSpec((None, tq, D_C), lambda bi, i: (bi, i, 0)),
        compiler_params=_params(("arbitrary", "arbitrary")),
        name="attn_c_latent" if has_ctx else "attn_c_context",
    )(*args)


def _lru_kernel(xc_ref, xp_ref, xn_ref, cw_ref, cb_ref, wg_ref, bg_ref, lam_ref, h0_ref,
                ob_ref, st_ref, a_s, b_s, hs, hf_s, hcar, *, tc, nc):
    p = pl.program_id(1)
    c = pl.program_id(2)
    cidx = c + p * (nc - 1 - 2 * c)
    cur = xc_ref[...]
    prev = jnp.where(cidx > 0, xp_ref[...], 0.0)
    nxt = jnp.where(cidx < nc - 1, xn_ref[...], 0.0)
    ext = jnp.concatenate([prev, cur, nxt], axis=0)
    n_ext = tc + 2 * SUBLANES
    xb = jnp.broadcast_to(cb_ref[...], (tc, D_LRU))
    for j in range(CONV_W):
        off = j - CONV_W // 2
        sh = ext if off == 0 else pltpu.roll(ext, (-off) % n_ext, 0)
        xb = xb + sh[SUBLANES:SUBLANES + tc] * cw_ref[j:j + 1, :]
    gm = _dot(xb.astype(BF16), wg_ref[...]) + bg_ref[...]
    r = _sigmoid(gm[:, :D_LRU])
    ig = _sigmoid(gm[:, D_LRU:])
    nl = -lam_ref[...]
    softplus = jnp.maximum(nl, 0.0) + jnp.log1p(jnp.exp(-jnp.abs(nl)))
    log_a = -LRU_C * r * softplus
    a = jnp.exp(log_a)
    bb = jnp.sqrt(-jnp.tanh(log_a) * (a * a + 1.0)) * (ig * xb)

    ng = tc // SUBLANES
    a3 = a.reshape(ng, SUBLANES, D_LRU)
    b3 = bb.reshape(ng, SUBLANES, D_LRU)
    row = lax.broadcasted_iota(jnp.int32, (ng, SUBLANES, D_LRU), 1)
    order = row + p * (SUBLANES - 1 - 2 * row)
    for sh in (1, 2, 4):
        shift = sh + p * (SUBLANES - 2 * sh)
        a_prev = pltpu.roll(a3, shift, 1)
        b_prev = pltpu.roll(b3, shift, 1)
        valid = order >= sh
        b3 = jnp.where(valid, a3 * b_prev + b3, b3)
        a3 = jnp.where(valid, a3 * a_prev, a3)
    a_tot = jnp.where(p == 0, a3[:, SUBLANES - 1:SUBLANES, :], a3[:, 0:1, :])
    b_tot = jnp.where(p == 0, b3[:, SUBLANES - 1:SUBLANES, :], b3[:, 0:1, :])
    a_s[...] = jnp.broadcast_to(a_tot, a3.shape).reshape(tc, D_LRU)
    b_s[...] = jnp.broadcast_to(b_tot, b3.shape).reshape(tc, D_LRU)

    @pl.when(c == 0)
    def _():
        hcar[...] = jnp.broadcast_to(h0_ref[pl.ds(p, 1), :], (SUBLANES, D_LRU))

    def group(gi, h):
        gidx = gi + p * (ng - 1 - 2 * gi)
        base = pl.multiple_of(gidx * SUBLANES, SUBLANES)
        hs[pl.ds(base, SUBLANES), :] = h
        return a_s[pl.ds(base, SUBLANES), :] * h + b_s[pl.ds(base, SUBLANES), :]

    h = lax.fori_loop(0, ng, group, hcar[...], unroll=8)
    hcar[...] = h
    hloc = (a3 * hs[...].reshape(ng, SUBLANES, D_LRU) + b3).reshape(tc, D_LRU)
    r0 = pl.multiple_of(cidx * tc, tc)

    @pl.when(p == 0)
    def _():
        hf_s[pl.ds(r0, tc), :] = hloc

    @pl.when(p == 1)
    def _():
        ob_ref[...] = (hf_s[pl.ds(r0, tc), :] + hloc).astype(ob_ref.dtype)

    @pl.when(c == nc - 1)
    def _():
        st_ref[pl.ds(p, 1), :] = h[0:1, :]


def _lru(xb, conv_w, conv_b, wg, bg, lam, h0, tc):
    b, t, _ = xb.shape
    nc = t // tc
    nb8 = t // SUBLANES
    per = tc // SUBLANES
    cidx = lambda p, c: c + p * (nc - 1 - 2 * c)
    vec = lambda bi, p, c: (0, 0)
    in_specs = [
        pl.BlockSpec((None, tc, D_LRU), lambda bi, p, c: (bi, cidx(p, c), 0)),
        pl.BlockSpec((None, SUBLANES, D_LRU), lambda bi, p, c: (bi, jnp.maximum(cidx(p, c) * per - 1, 0), 0)),
        pl.BlockSpec((None, SUBLANES, D_LRU), lambda bi, p, c: (bi, jnp.minimum((cidx(p, c) + 1) * per, nb8 - 1), 0)),
        pl.BlockSpec((CONV_W, D_LRU), vec),
        pl.BlockSpec((1, D_LRU), vec),
        pl.BlockSpec((None, D_LRU, 2 * D_LRU), lambda bi, p, c: (p, 0, 0)),
        pl.BlockSpec((None, 1, 2 * D_LRU), lambda bi, p, c: (p, 0, 0)),
        pl.BlockSpec((None, 1, D_LRU), lambda bi, p, c: (p, 0, 0)),
        pl.BlockSpec((None, 2, D_LRU), lambda bi, p, c: (bi, 0, 0)),
    ]
    return pl.pallas_call(
        functools.partial(_lru_kernel, tc=tc, nc=nc),
        out_shape=[jax.ShapeDtypeStruct((b, t, D_LRU), BF16), jax.ShapeDtypeStruct((b, 2, D_LRU), F32)],
        grid=(b, 2, nc),
        in_specs=in_specs,
        out_specs=[pl.BlockSpec((None, tc, D_LRU), lambda bi, p, c: (bi, nc - 1 - p * c, 0)),
                   pl.BlockSpec((None, 2, D_LRU), lambda bi, p, c: (bi, 0, 0))],
        scratch_shapes=[pltpu.VMEM((tc, D_LRU), F32), pltpu.VMEM((tc, D_LRU), F32), pltpu.VMEM((tc, D_LRU), F32),
                        pltpu.VMEM((t, D_LRU), F32), pltpu.VMEM((SUBLANES, D_LRU), F32)],
        compiler_params=_params(("arbitrary", "arbitrary", "arbitrary")),
        name="lru",
    )(xb, xb, xb, conv_w, conv_b, wg, bg, lam, h0)


def _tail_kernel(x_ref, mod_ref, ng_ref, oa_ref, ob_ref, oc_ref, wg_ref, wm_ref, wbr_ref, wout_ref, o_ref):
    x = x_ref[...]
    mod = mod_ref[...]
    hb = _normed_input(x, mod, ng_ref[...]).astype(BF16)
    y = None
    for br, br_ref in enumerate((oa_ref, ob_ref, oc_ref)):
        gt = _dot(hb, wg_ref[:, br * D_A:(br + 1) * D_A])
        u = (br_ref[...].astype(F32) * (gt * _sigmoid(gt))).astype(BF16)
        proj = _dot(u, wbr_ref[br])
        mg = _sigmoid(_dot(hb, wm_ref[:, br * D_MODEL:(br + 1) * D_MODEL]))
        y = mg * proj if y is None else y + mg * proj
    gate = mod[:, 2 * D_MODEL:]
    o_ref[...] = x + gate * _dot(y.astype(BF16), wout_ref[...])


def _tail(x2d, mod_rows, ng, oa, ob, oc, wg, wm, wbr, wout, seq_len, tm=256):
    n = x2d.shape[0]
    per_seq = seq_len // tm
    row = lambda i: (i, 0)
    const = lambda i: (0, 0)
    once = dict(pipeline_mode=pl.Buffered(1))
    return pl.pallas_call(
        _tail_kernel,
        out_shape=jax.ShapeDtypeStruct((n, D_MODEL), F32),
        grid=(n // tm,),
        in_specs=[pl.BlockSpec((tm, D_MODEL), row),
                  pl.BlockSpec((None, 1, 3 * D_MODEL), lambda i: (i // per_seq, 0, 0)),
                  pl.BlockSpec((1, D_MODEL), const),
                  pl.BlockSpec((tm, D_A), row),
                  pl.BlockSpec((tm, D_LRU), row),
                  pl.BlockSpec((tm, D_C), row),
                  pl.BlockSpec((D_MODEL, 3 * D_A), const, **once),
                  pl.BlockSpec((D_MODEL, 3 * D_MODEL), const, **once),
                  pl.BlockSpec((3, D_A, D_MODEL), lambda i: (0, 0, 0), **once),
                  pl.BlockSpec((D_MODEL, D_MODEL), const, **once)],
        out_specs=pl.BlockSpec((tm, D_MODEL), row),
        compiler_params=_params(("arbitrary",)),
        name="tail",
    )(x2d, mod_rows, ng, oa, ob, oc, wg, wm, wbr, wout)


def _rope_tables(seq_len):
    pos = jnp.arange(seq_len)
    row = (pos // GRID_W).astype(F32)
    col = (pos % GRID_W).astype(F32)
    inv = jnp.power(ROPE_BASE, -jnp.arange(ROPE_FREQS, dtype=F32) / ROPE_FREQS)
    ang_r = row[:, None] * inv
    ang_c = col[:, None] * inv
    cos = jnp.concatenate([jnp.cos(ang_r)] * 2 + [jnp.cos(ang_c)] * 2, axis=-1)
    sin = jnp.concatenate([-jnp.sin(ang_r), jnp.sin(ang_r), -jnp.sin(ang_c), jnp.sin(ang_c)], axis=-1)
    return jnp.tile(cos, (1, LANES // HD)), jnp.tile(sin, (1, LANES // HD))


def _block_diag(w):
    eye = jnp.eye(LRU_BLOCKS, dtype=w.dtype)
    return (w[:, :, None, :] * eye[:, None, :, None]).reshape(D_LRU, D_LRU)


def kernel(x_prompt, x_sample, cache_a_k, cache_a_v, cache_c_k, cache_c_v, state_lru, c, c_ctx, norm_g, mod_w, mod_b, w_in, qn_a, kn_a, sink_a, conv_w, conv_b, lru_wa, lru_ba, lru_wx, lru_bx, lru_lam, qn_c, kn_c, lam_q1, lam_k1, lam_q2, lam_k2, subln_c, w_br_a, w_br_b, w_br_c, w_out):
    bp, sp, _ = x_prompt.shape
    bs, ss, _ = x_sample.shape
    past = cache_a_k.shape[2]

    cvecs = jnp.concatenate([c, c_ctx[None, :], jnp.zeros((SUBLANES - bs - 1, D_MODEL), F32)], axis=0)
    mod = _modulation(cvecs, mod_w, mod_b)
    rope = _rope_tables(ss)

    cols = lambda *spans: jnp.concatenate([w_in[:, :, a:b] for a, b in spans], axis=-1).astype(BF16)
    w1 = cols(_QA, _KA, _QC, _KC)
    w2 = cols(_VA, _VC, _XB)
    wg = cols(_GA, _GB, _GC)
    wm = cols(_MG)
    wbr = jnp.stack([w_br_a, w_br_b, w_br_c], axis=1).astype(BF16)
    wout = w_out.astype(BF16)
    gains = jnp.concatenate([jnp.tile(qn_a * Q_SCALE, (1, H_A)), jnp.tile(kn_a, (1, KV_A)),
                             jnp.tile(qn_c * Q_SCALE, (1, 2 * H_C)), jnp.tile(kn_c, (1, 2 * H_C))], axis=-1)

    xp = x_prompt.reshape(bp * sp, D_MODEL)
    xs = x_sample.reshape(bs * ss, D_MODEL)
    zeros_h0 = jnp.zeros((bp, 2, D_LRU), F32)
    ka_l, va_l, kc_l, vc_l, st_l = [], [], [], [], []
    for l in range(DEPTH):
        lam_init = 0.8 - 0.6 * math.exp(-0.3 * l)
        ng = norm_g[l][None, :]
        mod_ctx = mod[l, bs:bs + 1][:, None, :]
        mod_lat = mod[l, 0:bs][:, None, :]
        lru_wg = jnp.stack([jnp.concatenate([_block_diag(lru_wa[l, d]), _block_diag(lru_wx[l, d])], axis=-1)
                            for d in range(2)]).astype(BF16)
        lru_bg = jnp.concatenate([lru_ba[l], lru_bx[l]], axis=-1)[:, None, :]
        lam_vecs = [v[l][None, :] for v in (lam_q1, lam_k1, lam_q2, lam_k2)]
        subln = subln_c[l][None, :]
        g_l = gains[l][None, :]
        lru_args = (conv_w[l], conv_b[l][None, :], lru_wg, lru_bg, lru_lam[l][:, None, :])
        tail_w = (wg[l], wm[l], wbr[l], wout[l])

        qa, ka, qc, kc, va, vc, xb = _front(xp, mod_ctx, ng, w1[l], w2[l], g_l, None, bp * sp, F32)
        r3 = lambda a: a.reshape(bp, sp, a.shape[-1])
        oa = _attn_a(sink_a[l], r3(qa), r3(ka), r3(va), None, None, tq=sp, banded=False)
        ob, st = _lru(r3(xb), *lru_args, zeros_h0, tc=sp)
        oc = _attn_c(lam_vecs, subln, r3(qc), r3(kc), r3(vc), None, None, tq=sp, ts=sp, lam_init=lam_init)
        flat = lambda a: a.reshape(bp * sp, a.shape[-1])
        xp = _tail(xp, mod_ctx, ng, flat(oa), flat(ob), flat(oc), *tail_w, bp * sp)
        ka_l.append(ka.reshape(bp, sp, KV_A, HD))
        va_l.append(va.reshape(bp, sp, KV_A, HD))
        kc_l.append(kc.reshape(bp, sp, H_C, 2, HD))
        vc_l.append(vc.reshape(bp, sp, H_C, DV_C))
        st_l.append(st)

        qa, ka, qc, kc, va, vc, xb = _front(xs, mod_lat, ng, w1[l], w2[l], g_l, rope, ss, BF16)
        r3 = lambda a: a.reshape(bs, ss, a.shape[-1])
        cka = cache_a_k[:, l].reshape(bs, past, KV_A * HD)
        cva = cache_a_v[:, l].reshape(bs, past, KV_A * HD)
        ckc = cache_c_k[:, l].reshape(bs, past, D_C)
        cvc = cache_c_v[:, l].reshape(bs, past, D_C)
        oa = _attn_a(sink_a[l], r3(qa), r3(ka), r3(va), cka, cva, tq=2 * WINDOW, banded=True)
        ob, _ = _lru(r3(xb), *lru_args, state_lru[:, l], tc=512)
        oc = _attn_c(lam_vecs, subln, r3(qc), r3(kc), r3(vc), ckc, cvc, tq=256, ts=512, lam_init=lam_init)
        flat = lambda a: a.reshape(bs * ss, a.shape[-1])
        xs = _tail(xs, mod_lat, ng, flat(oa), flat(ob), flat(oc), *tail_w, ss)

    return (xp.reshape(bp, sp, D_MODEL), xs.reshape(bs, ss, D_MODEL),
            jnp.stack(ka_l, axis=1), jnp.stack(va_l, axis=1), jnp.stack(kc_l, axis=1),
            jnp.stack(vc_l, axis=1), jnp.stack(st_l, axis=1))
```

```python
import functools
import math

import jax
import jax.numpy as jnp
import numpy as np
from jax import lax
from jax.experimental import pallas as pl
from jax.experimental.pallas import tpu as pltpu

F32 = jnp.float32
BF16 = jnp.bfloat16

D_MODEL = 1024
DEPTH = 2
GRID_W = 64
HD = 64
SCALE = 1.0 / math.sqrt(HD)
LOG2E = math.log2(math.e)
Q_SCALE = SCALE * LOG2E
H_A = 8
KV_A = 2
WINDOW = 128
D_A = H_A * HD
D_LRU = 512
LRU_BLOCKS = 8
LRU_BW = D_LRU // LRU_BLOCKS
CONV_W = 4
LRU_C = 8.0
H_C = 4
DV_C = 2 * HD
D_C = H_C * DV_C
ROPE_BASE = 10000.0
ROPE_FREQS = HD // 4
EPS = 1e-6
LANES = 128
SUBLANES = 8
VMEM_LIMIT = 56 * 1024 * 1024

_OFF = np.cumsum([0, D_A, KV_A * HD, KV_A * HD, D_A, D_LRU, D_LRU,
                  H_C * 2 * HD, H_C * 2 * HD, D_C, D_C, 3 * D_MODEL])
(_QA, _KA, _VA, _GA, _XB, _GB, _QC, _KC, _VC, _GC, _MG) = [
    (int(_OFF[i]), int(_OFF[i + 1])) for i in range(11)]
N_NORM = D_A + KV_A * HD + 2 * H_C * 2 * HD
N_PLAIN = KV_A * HD + D_C + D_LRU

TM_FRONT = 512
TM_TAIL = 512
TQ_A = 2 * WINDOW
TQ_C = 512
TS_C = 512
TC_LRU = 512


def _params(sem, vmem=VMEM_LIMIT):
    return pltpu.CompilerParams(dimension_semantics=sem, vmem_limit_bytes=vmem)


def _dot(a, b):
    return jnp.dot(a, b, preferred_element_type=F32)


def _dot_nt(a, b):
    return lax.dot_general(a, b, (((1,), (1,)), ((), ())), preferred_element_type=F32)


def _sigmoid(x):
    return 1.0 / (1.0 + jnp.exp(-x))


def _mod_kernel(c_ref, w_ref, b_ref, o_ref):
    c = c_ref[...]
    a = c * _sigmoid(c)
    w = w_ref[...]
    a_hi = a.astype(BF16)
    a_lo = (a - a_hi.astype(F32)).astype(BF16)
    w_hi = w.astype(BF16)
    w_lo = (w - w_hi.astype(F32)).astype(BF16)
    o_ref[...] = _dot(a_hi, w_hi) + _dot(a_hi, w_lo) + _dot(a_lo, w_hi) + b_ref[...]


def _modulation(cvecs, mod_w, mod_b):
    tn = 768
    return pl.pallas_call(
        _mod_kernel,
        out_shape=jax.ShapeDtypeStruct((DEPTH, SUBLANES, 3 * D_MODEL), F32),
        grid=(DEPTH, 3 * D_MODEL // tn),
        in_specs=[pl.BlockSpec((SUBLANES, D_MODEL), lambda l, j: (0, 0)),
                  pl.BlockSpec((None, D_MODEL, tn), lambda l, j: (l, 0, j)),
                  pl.BlockSpec((None, 1, tn), lambda l, j: (l, 0, j))],
        out_specs=pl.BlockSpec((None, SUBLANES, tn), lambda l, j: (l, 0, j)),
        compiler_params=_params(("arbitrary", "arbitrary")),
        name="modulation",
    )(cvecs, mod_w, mod_b.reshape(DEPTH, 1, 3 * D_MODEL))


def _normed_input(x, mod, ng):
    ms = jnp.mean(x * x, axis=-1, keepdims=True)
    shift = mod[:, 0:D_MODEL]
    scale = mod[:, D_MODEL:2 * D_MODEL]
    return (x * lax.rsqrt(ms + EPS) * ng) * (1.0 + scale) + shift


def _layer_spec(shape, layer):
    zeros = (0,) * len(shape)
    return pl.BlockSpec((None,) + tuple(shape), lambda *_: (layer,) + zeros)


def _mod_spec(layer, row0, tiles_per_row):
    return pl.BlockSpec((None, None, 1, 3 * D_MODEL), lambda i: (layer, row0 + i // tiles_per_row, 0, 0))


def _front_kernel(*refs, use_rope):
    n_in = 8 if use_rope else 6
    ins, outs = refs[:n_in], refs[n_in:]
    x_ref, mod_ref, ng_ref, w1_ref, w2_ref, gain_ref = ins[:6]
    qa_ref, ka_ref, qc_ref, kc_ref, va_ref, vc_ref, xb_ref = outs
    x = x_ref[...]
    tm = x.shape[0]
    hb = _normed_input(x, mod_ref[...], ng_ref[...]).astype(BF16)
    p1 = _dot(hb, w1_ref[...])
    lane = lax.broadcasted_iota(jnp.int32, (tm, LANES), 1)
    first_half = (lane & ROPE_FREQS) == 0
    wide = 2 * LANES
    same_head = (lax.broadcasted_iota(jnp.int32, (wide, wide), 0) // HD
                 == lax.broadcasted_iota(jnp.int32, (wide, wide), 1) // HD)
    ones_bd = jnp.where(same_head, 1.0, 0.0).astype(BF16)
    sums = []
    for c0 in range(0, N_NORM, wide):
        w = min(wide, N_NORM - c0)
        sq = p1[:, c0:c0 + w] * p1[:, c0:c0 + w]
        sq_hi = sq.astype(BF16)
        sq_lo = (sq - sq_hi.astype(F32)).astype(BF16)
        sums.append(_dot(sq_hi, ones_bd[:w, :w]) + _dot(sq_lo, ones_bd[:w, :w]))
    dests = ([(qa_ref, c) for c in range(4)] + [(ka_ref, 0)]
             + [(qc_ref, c) for c in range(4)] + [(kc_ref, c) for c in range(4)])
    for c, (o_ref, oc) in enumerate(dests):
        pc = p1[:, c * LANES:(c + 1) * LANES]
        msq = sums[c // 2][:, (c % 2) * LANES:(c % 2 + 1) * LANES] * (1.0 / HD)
        y = pc * lax.rsqrt(msq + EPS) * gain_ref[:, c * LANES:(c + 1) * LANES]
        if use_rope:
            cos_ref, sin_ref = ins[6:8]
            partner = jnp.where(first_half, pltpu.roll(y, LANES - ROPE_FREQS, 1),
                                pltpu.roll(y, ROPE_FREQS, 1))
            y = y * cos_ref[...] + partner * sin_ref[...]
        o_ref[:, oc * LANES:(oc + 1) * LANES] = y.astype(o_ref.dtype)
    p2 = _dot(hb, w2_ref[...])
    va_ref[...] = p2[:, 0:KV_A * HD].astype(va_ref.dtype)
    vc_ref[...] = p2[:, KV_A * HD:KV_A * HD + D_C].astype(vc_ref.dtype)
    xb_ref[...] = p2[:, KV_A * HD + D_C:]


def _front(x2d, mod4, ng3, w1, w2, gains3, rope, *, layer, mod_row0, tokens_per_mod, kv_dtype):
    n = x2d.shape[0]
    tm = TM_FRONT
    use_rope = rope is not None
    row = lambda i: (i, 0)
    in_specs = [pl.BlockSpec((tm, D_MODEL), row),
                _mod_spec(layer, mod_row0, tokens_per_mod // tm),
                _layer_spec((1, D_MODEL), layer),
                _layer_spec((D_MODEL, N_NORM), layer),
                _layer_spec((D_MODEL, N_PLAIN), layer),
                _layer_spec((1, N_NORM), layer)]
    args = [x2d, mod4, ng3, w1, w2, gains3]
    if use_rope:
        per_seq = rope[0].shape[0] // tm
        in_specs += [pl.BlockSpec((tm, LANES), lambda i: (i % per_seq, 0))] * 2
        args += list(rope)
    widths = (D_A, KV_A * HD, H_C * 2 * HD, H_C * 2 * HD, KV_A * HD, D_C, D_LRU)
    dtypes = (BF16, kv_dtype, BF16, kv_dtype, kv_dtype, kv_dtype, F32)
    return pl.pallas_call(
        functools.partial(_front_kernel, use_rope=use_rope),
        out_shape=[jax.ShapeDtypeStruct((n, w), d) for w, d in zip(widths, dtypes)],
        grid=(n // tm,),
        in_specs=in_specs,
        out_specs=[pl.BlockSpec((tm, w), row) for w in widths],
        compiler_params=_params(("arbitrary",)),
        name="front_rope" if use_rope else "front",
    )(*args)


def _attn_a_kernel(*refs, tq, seq_len, banded, has_ctx, layer):
    if has_ctx:
        sink_ref, q_ref, k_ref, v_ref, kc_ref, vc_ref, o_ref = refs
    else:
        sink_ref, q_ref, k_ref, v_ref, o_ref = refs
    i = pl.program_id(1)
    g = H_A // KV_A
    if banded:
        nwin = tq + 2 * WINDOW
        start = pl.multiple_of(jnp.clip(i * tq - WINDOW, 0, seq_len - nwin), WINDOW)
        kall = k_ref[pl.ds(start, nwin), :].astype(F32)
        vall = v_ref[pl.ds(start, nwin), :].astype(F32)
    else:
        nwin = seq_len
        kall = k_ref[...].astype(F32)
        vall = v_ref[...].astype(F32)
    if has_ctx:
        kall = jnp.concatenate([kall, kc_ref[...].astype(F32)], axis=0)
        vall = jnp.concatenate([vall, vc_ref[...].astype(F32)], axis=0)
    nk = kall.shape[0]
    if banded:
        kpos = start + lax.broadcasted_iota(jnp.int32, (tq, nwin), 1)
        qpos = i * tq + lax.broadcasted_iota(jnp.int32, (tq, nwin), 0)
        bias = jnp.where(jnp.abs(kpos - qpos) <= WINDOW, 0.0, -jnp.inf)
    lo_k = lax.broadcasted_iota(jnp.int32, (nk, LANES), 1) < HD
    lo_q = lax.broadcasted_iota(jnp.int32, (tq, LANES), 1) < HD
    k_sw = pltpu.roll(kall, HD, 1)
    v_sw = pltpu.roll(vall, HD, 1)
    kd = (jnp.where(lo_k, kall, k_sw).astype(BF16), jnp.where(lo_k, k_sw, kall).astype(BF16))
    vd = [[jnp.where(lo_k, vall, 1.0).astype(BF16), jnp.where(lo_k, 1.0, v_sw).astype(BF16)],
          [jnp.where(lo_k, v_sw, 1.0).astype(BF16), jnp.where(lo_k, 1.0, vall).astype(BF16)]]
    q = q_ref[...]
    zero = jnp.zeros((tq, LANES), BF16)
    outs = []
    for h in range(H_A):
        kv, half = h // g, h % 2
        qz = jnp.where(lo_q if half == 0 else jnp.logical_not(lo_q), q[:, (h // 2) * LANES:(h // 2 + 1) * LANES], zero)
        s = _dot_nt(qz, kd[kv])
        if banded:
            s = jnp.concatenate([s[:, :nwin] + bias, s[:, nwin:]], axis=1)
        snk = sink_ref[layer, h] * LOG2E
        m = jnp.maximum(jnp.max(s, axis=-1, keepdims=True), snk)
        e = jnp.exp2(s - m)
        pv = _dot(e.astype(BF16), vd[kv][half])
        outs.append(pv / (pltpu.roll(pv, HD, 1) + jnp.exp2(snk - m)))
    for c in range(H_A // 2):
        pair = jnp.where(lo_q, outs[2 * c], outs[2 * c + 1])
        o_ref[:, c * LANES:(c + 1) * LANES] = pair.astype(o_ref.dtype)


def _attn_a(sink, q, k, v, k_ctx, v_ctx, *, layer, tq, banded):
    b, t, _ = q.shape
    has_ctx = k_ctx is not None
    kv_spec = pl.BlockSpec((None, t, KV_A * HD), lambda bi, i: (bi, 0, 0))
    in_specs = [pl.BlockSpec(memory_space=pltpu.SMEM),
                pl.BlockSpec((None, tq, D_A), lambda bi, i: (bi, i, 0)), kv_spec, kv_spec]
    args = [sink, q, k, v]
    if has_ctx:
        s = k_ctx.shape[2]
        in_specs += [pl.BlockSpec((None, None, s, KV_A * HD), lambda bi, i: (bi, layer, 0, 0))] * 2
        args += [k_ctx, v_ctx]
    return pl.pallas_call(
        functools.partial(_attn_a_kernel, tq=tq, seq_len=t, banded=banded, has_ctx=has_ctx, layer=layer),
        out_shape=jax.ShapeDtypeStruct((b, t, D_A), BF16),
        grid=(b, t // tq),
        in_specs=in_specs,
        out_specs=pl.BlockSpec((None, tq, D_A), lambda bi, i: (bi, i, 0)),
        compiler_params=_params(("arbitrary", "arbitrary")),
        name="attn_a_latent" if has_ctx else "attn_a_context",
    )(*args)


def _attn_c_kernel(*refs, tq, ts, n_loc, has_ctx, lam_init):
    if has_ctx:
        lq1, lk1, lq2, lk2, sub_ref, q_ref, k_ref, v_ref, kc_ref, vc_ref, o_ref = refs
    else:
        lq1, lk1, lq2, lk2, sub_ref, q_ref, k_ref, v_ref, o_ref = refs
    lam = (jnp.exp(jnp.sum(lq1[...] * lk1[...], axis=-1, keepdims=True))
           - jnp.exp(jnp.sum(lq2[...] * lk2[...], axis=-1, keepdims=True)) + lam_init)
    lo = lax.broadcasted_iota(jnp.int32, (tq, LANES), 1) < HD
    zero = jnp.zeros((tq, LANES), BF16)
    heads = [slice(h * LANES, (h + 1) * LANES) for h in range(H_C)]
    qz = []
    for cols in heads:
        q12 = q_ref[:, cols]
        qz.append(jnp.concatenate([jnp.where(lo, q12, zero), jnp.where(lo, zero, q12)], axis=0))

    def update(carry, kt, vt):
        new = []
        for h, cols in enumerate(heads):
            m, l, acc = carry[h]
            s = _dot_nt(qz[h], kt[:, cols].astype(BF16))
            m_new = jnp.maximum(m, jnp.max(s, axis=-1, keepdims=True))
            alpha = jnp.exp2(m - m_new)
            p = jnp.exp2(s - m_new)
            psum = p[:, 0:LANES]
            for j in range(1, p.shape[1] // LANES):
                psum = psum + p[:, j * LANES:(j + 1) * LANES]
            l = alpha * l + psum
            acc = alpha * acc + _dot(p.astype(BF16), vt[:, cols].astype(BF16))
            new.append((m_new, l, acc))
        return tuple(new)

    carry = tuple((jnp.full((2 * tq, 1), -jnp.inf, F32), jnp.zeros((2 * tq, LANES), F32),
                   jnp.zeros((2 * tq, LANES), F32)) for _ in heads)
    for j in range(n_loc):
        carry = update(carry, k_ref[j * ts:(j + 1) * ts, :], v_ref[j * ts:(j + 1) * ts, :])
    if has_ctx:
        carry = update(carry, kc_ref[...], vc_ref[...])
    for h, cols in enumerate(heads):
        _, l, acc = carry[h]
        l = jnp.sum(l, axis=-1, keepdims=True)
        o = acc[:tq] / l[:tq] - lam * (acc[tq:] / l[tq:])
        o = o * lax.rsqrt(jnp.mean(o * o, axis=-1, keepdims=True) + EPS) * sub_ref[...] * (1.0 - lam_init)
        o_ref[:, cols] = o.astype(o_ref.dtype)


def _attn_c(lam_vecs, subln3, q, k, v, k_ctx, v_ctx, *, layer, tq, ts, lam_init):
    b, t, _ = q.shape
    s_loc = k.shape[1]
    has_ctx = k_ctx is not None
    once = dict(pipeline_mode=pl.Buffered(1))
    kv_spec = pl.BlockSpec((None, s_loc, D_C), lambda bi, i: (bi, 0, 0), **once)
    in_specs = ([_layer_spec((1, HD), layer)] * 4 + [_layer_spec((1, DV_C), layer)]
                + [pl.BlockSpec((None, tq, D_C), lambda bi, i: (bi, i, 0)), kv_spec, kv_spec])
    args = list(lam_vecs) + [subln3, q, k, v]
    if has_ctx:
        sc = k_ctx.shape[2]
        in_specs += [pl.BlockSpec((None, None, sc, D_C), lambda bi, i: (bi, layer, 0, 0), **once)] * 2
        args += [k_ctx, v_ctx]
    return pl.pallas_call(
        functools.partial(_attn_c_kernel, tq=tq, ts=ts, n_loc=s_loc // ts, has_ctx=has_ctx, lam_init=lam_init),
        out_shape=jax.ShapeDtypeStruct((b, t, D_C), BF16),
        grid=(b, t // tq),
        in_specs=in_specs,
        out_specs=pl.BlockSpec((None, tq, D_C), lambda bi, i: (bi, i, 0)),
        compiler_params=_params(("arbitrary", "arbitrary")),
        name="attn_c_latent" if has_ctx else "attn_c_context",
    )(*args)


def _lru_kernel(xc_ref, xp_ref, xn_ref, cw_ref, cb_ref, wg_ref, bg_ref, lam_ref, h0_ref,
                ob_ref, st_ref, a_s, b_s, hs, hf_s, hcar, *, tc, nc):
    p = pl.program_id(1)
    c = pl.program_id(2)
    cidx = c + p * (nc - 1 - 2 * c)
    cur = xc_ref[...]
    prev = jnp.where(cidx > 0, xp_ref[...], 0.0)
    nxt = jnp.where(cidx < nc - 1, xn_ref[...], 0.0)
    ext = jnp.concatenate([prev, cur, nxt], axis=0)
    n_ext = tc + 2 * SUBLANES
    xb = jnp.broadcast_to(cb_ref[...], (tc, D_LRU))
    for j in range(CONV_W):
        off = j - CONV_W // 2
        sh = ext if off == 0 else pltpu.roll(ext, (-off) % n_ext, 0)
        xb = xb + sh[SUBLANES:SUBLANES + tc] * cw_ref[j:j + 1, :]
    gm = _dot(xb.astype(BF16), wg_ref[...]) + bg_ref[...]
    r = _sigmoid(gm[:, :D_LRU])
    ig = _sigmoid(gm[:, D_LRU:])
    nl = -lam_ref[...]
    softplus = jnp.maximum(nl, 0.0) + jnp.log1p(jnp.exp(-jnp.abs(nl)))
    log_a = -LRU_C * r * softplus
    a = jnp.exp(log_a)
    bb = jnp.sqrt(-jnp.tanh(log_a) * (a * a + 1.0)) * (ig * xb)

    ng = tc // SUBLANES
    a3 = a.reshape(ng, SUBLANES, D_LRU)
    b3 = bb.reshape(ng, SUBLANES, D_LRU)
    row = lax.broadcasted_iota(jnp.int32, (ng, SUBLANES, D_LRU), 1)
    order = row + p * (SUBLANES - 1 - 2 * row)
    for sh in (1, 2, 4):
        shift = sh + p * (SUBLANES - 2 * sh)
        a_prev = pltpu.roll(a3, shift, 1)
        b_prev = pltpu.roll(b3, shift, 1)
        valid = order >= sh
        b3 = jnp.where(valid, a3 * b_prev + b3, b3)
        a3 = jnp.where(valid, a3 * a_prev, a3)
    a_tot = jnp.where(p == 0, a3[:, SUBLANES - 1:SUBLANES, :], a3[:, 0:1, :])
    b_tot = jnp.where(p == 0, b3[:, SUBLANES - 1:SUBLANES, :], b3[:, 0:1, :])
    a_s[...] = jnp.broadcast_to(a_tot, a3.shape).reshape(tc, D_LRU)
    b_s[...] = jnp.broadcast_to(b_tot, b3.shape).reshape(tc, D_LRU)

    @pl.when(c == 0)
    def _():
        hcar[...] = jnp.broadcast_to(h0_ref[pl.ds(p, 1), :], (SUBLANES, D_LRU))

    def group(gi, h):
        gidx = gi + p * (ng - 1 - 2 * gi)
        base = pl.multiple_of(gidx * SUBLANES, SUBLANES)
        hs[pl.ds(base, SUBLANES), :] = h
        return a_s[pl.ds(base, SUBLANES), :] * h + b_s[pl.ds(base, SUBLANES), :]

    h = lax.fori_loop(0, ng, group, hcar[...], unroll=8)
    hcar[...] = h
    hloc = (a3 * hs[...].reshape(ng, SUBLANES, D_LRU) + b3).reshape(tc, D_LRU)
    r0 = pl.multiple_of(cidx * tc, tc)

    @pl.when(p == 0)
    def _():
        hf_s[pl.ds(r0, tc), :] = hloc

    @pl.when(p == 1)
    def _():
        ob_ref[...] = (hf_s[pl.ds(r0, tc), :] + hloc).astype(ob_ref.dtype)

    @pl.when(c == nc - 1)
    def _():
        st_ref[pl.ds(p, 1), :] = h[0:1, :]


def _lru(xb, conv_w, conv_b3, wg, bg, lam4, h0, *, layer, h0_layer, tc):
    b, t, _ = xb.shape
    nc = t // tc
    nb8 = t // SUBLANES
    per = tc // SUBLANES
    cidx = lambda p, c: c + p * (nc - 1 - 2 * c)
    in_specs = [
        pl.BlockSpec((None, tc, D_LRU), lambda bi, p, c: (bi, cidx(p, c), 0)),
        pl.BlockSpec((None, SUBLANES, D_LRU), lambda bi, p, c: (bi, jnp.maximum(cidx(p, c) * per - 1, 0), 0)),
        pl.BlockSpec((None, SUBLANES, D_LRU), lambda bi, p, c: (bi, jnp.minimum((cidx(p, c) + 1) * per, nb8 - 1), 0)),
        _layer_spec((CONV_W, D_LRU), layer),
        _layer_spec((1, D_LRU), layer),
        pl.BlockSpec((None, None, D_LRU, 2 * D_LRU), lambda bi, p, c: (layer, p, 0, 0)),
        pl.BlockSpec((None, None, 1, 2 * D_LRU), lambda bi, p, c: (layer, p, 0, 0)),
        pl.BlockSpec((None, None, 1, D_LRU), lambda bi, p, c: (layer, p, 0, 0)),
        pl.BlockSpec((None, None, 2, D_LRU), lambda bi, p, c: (bi, h0_layer, 0, 0)),
    ]
    return pl.pallas_call(
        functools.partial(_lru_kernel, tc=tc, nc=nc),
        out_shape=[jax.ShapeDtypeStruct((b, t, D_LRU), BF16), jax.ShapeDtypeStruct((b, 2, D_LRU), F32)],
        grid=(b, 2, nc),
        in_specs=in_specs,
        out_specs=[pl.BlockSpec((None, tc, D_LRU), lambda bi, p, c: (bi, nc - 1 - p * c, 0)),
                   pl.BlockSpec((None, 2, D_LRU), lambda bi, p, c: (bi, 0, 0))],
        scratch_shapes=[pltpu.VMEM((tc, D_LRU), F32), pltpu.VMEM((tc, D_LRU), F32), pltpu.VMEM((tc, D_LRU), F32),
                        pltpu.VMEM((t, D_LRU), F32), pltpu.VMEM((SUBLANES, D_LRU), F32)],
        compiler_params=_params(("arbitrary", "arbitrary", "arbitrary")),
        name="lru",
    )(xb, xb, xb, conv_w, conv_b3, wg, bg, lam4, h0)


def _tail_kernel(x_ref, mod_ref, ng_ref, oa_ref, ob_ref, oc_ref, wg_ref, wm_ref, wbr_ref, wout_ref, o_ref):
    x = x_ref[...]
    mod = mod_ref[...]
    hb = _normed_input(x, mod, ng_ref[...]).astype(BF16)
    y = None
    for br, br_ref in enumerate((oa_ref, ob_ref, oc_ref)):
        gt = _dot(hb, wg_ref[:, br * D_A:(br + 1) * D_A])
        u = (br_ref[...].astype(F32) * (gt * _sigmoid(gt))).astype(BF16)
        proj = _dot(u, wbr_ref[br])
        mg = _sigmoid(_dot(hb, wm_ref[:, br * D_MODEL:(br + 1) * D_MODEL]))
        y = mg * proj if y is None else y + mg * proj
    gate = mod[:, 2 * D_MODEL:]
    o_ref[...] = x + gate * _dot(y.astype(BF16), wout_ref[...])


def _tail(x2d, mod4, ng3, oa, ob, oc, wg, wm, wbr, wout, *, layer, mod_row0, tokens_per_mod):
    n = x2d.shape[0]
    tm = TM_TAIL
    row = lambda i: (i, 0)
    once = dict(pipeline_mode=pl.Buffered(1))

    def weight(shape):
        zeros = (0,) * len(shape)
        return pl.BlockSpec((None,) + shape, lambda i: (layer,) + zeros, **once)

    return pl.pallas_call(
        _tail_kernel,
        out_shape=jax.ShapeDtypeStruct((n, D_MODEL), F32),
        grid=(n // tm,),
        in_specs=[pl.BlockSpec((tm, D_MODEL), row),
                  _mod_spec(layer, mod_row0, tokens_per_mod // tm),
                  _layer_spec((1, D_MODEL), layer),
                  pl.BlockSpec((tm, D_A), row),
                  pl.BlockSpec((tm, D_LRU), row),
                  pl.BlockSpec((tm, D_C), row),
                  weight((D_MODEL, 3 * D_A)),
                  weight((D_MODEL, 3 * D_MODEL)),
                  weight((3, D_A, D_MODEL)),
                  weight((D_MODEL, D_MODEL))],
        out_specs=pl.BlockSpec((tm, D_MODEL), row),
        compiler_params=_params(("arbitrary",)),
        name="tail",
    )(x2d, mod4, ng3, oa, ob, oc, wg, wm, wbr, wout)


def _rope_tables(seq_len):
    pos = jnp.arange(seq_len)
    row = (pos // GRID_W).astype(F32)
    col = (pos % GRID_W).astype(F32)
    inv = jnp.power(ROPE_BASE, -jnp.arange(ROPE_FREQS, dtype=F32) / ROPE_FREQS)
    ang_r = row[:, None] * inv
    ang_c = col[:, None] * inv
    cos = jnp.concatenate([jnp.cos(ang_r)] * 2 + [jnp.cos(ang_c)] * 2, axis=-1)
    sin = jnp.concatenate([-jnp.sin(ang_r), jnp.sin(ang_r), -jnp.sin(ang_c), jnp.sin(ang_c)], axis=-1)
    return jnp.tile(cos, (1, LANES // HD)), jnp.tile(sin, (1, LANES // HD))


def _block_diag(w):
    eye = jnp.eye(LRU_BLOCKS, dtype=w.dtype)
    dense = w[..., :, :, None, :] * eye[:, None, :, None]
    return dense.reshape(w.shape[:-3] + (D_LRU, D_LRU))


def kernel(x_prompt, x_sample, cache_a_k, cache_a_v, cache_c_k, cache_c_v, state_lru, c, c_ctx, norm_g, mod_w, mod_b, w_in, qn_a, kn_a, sink_a, conv_w, conv_b, lru_wa, lru_ba, lru_wx, lru_bx, lru_lam, qn_c, kn_c, lam_q1, lam_k1, lam_q2, lam_k2, subln_c, w_br_a, w_br_b, w_br_c, w_out):
    bp, sp, _ = x_prompt.shape
    bs, ss, _ = x_sample.shape
    past = cache_a_k.shape[2]

    cvecs = jnp.concatenate([c, c_ctx[None, :], jnp.zeros((SUBLANES - bs - 1, D_MODEL), F32)], axis=0)
    mod = _modulation(cvecs, mod_w, mod_b)
    mod4 = mod.reshape(DEPTH, SUBLANES, 1, 3 * D_MODEL)
    rope = _rope_tables(ss)

    cols = lambda *spans: jnp.concatenate([w_in[:, :, a:b] for a, b in spans], axis=-1).astype(BF16)
    w1 = cols(_QA, _KA, _QC, _KC)
    w2 = cols(_VA, _VC, _XB)
    wg = cols(_GA, _GB, _GC)
    wm = cols(_MG)
    wbr = jnp.stack([w_br_a, w_br_b, w_br_c], axis=1).astype(BF16)
    wout = w_out.astype(BF16)
    gains3 = jnp.concatenate([jnp.tile(qn_a * Q_SCALE, (1, H_A)), jnp.tile(kn_a, (1, KV_A)),
                              jnp.tile(qn_c * Q_SCALE, (1, 2 * H_C)), jnp.tile(kn_c, (1, 2 * H_C))],
                             axis=-1)[:, None, :]
    ng3 = norm_g[:, None, :]
    lru_wg = jnp.concatenate([_block_diag(lru_wa), _block_diag(lru_wx)], axis=-1).astype(BF16)
    lru_bg = jnp.concatenate([lru_ba, lru_bx], axis=-1)[:, :, None, :]
    lru_lam4 = lru_lam[:, :, None, :]
    lru_p = (conv_w, conv_b[:, None, :], lru_wg, lru_bg, lru_lam4)
    lam_vecs = [v[:, None, :] for v in (lam_q1, lam_k1, lam_q2, lam_k2)]
    subln3 = subln_c[:, None, :]
    cka = cache_a_k.reshape(bs, DEPTH, past, KV_A * HD)
    cva = cache_a_v.reshape(bs, DEPTH, past, KV_A * HD)
    ckc = cache_c_k.reshape(bs, DEPTH, past, D_C)
    cvc = cache_c_v.reshape(bs, DEPTH, past, D_C)
    zeros_h0 = jnp.zeros((bp, 1, 2, D_LRU), F32)

    xp = x_prompt.reshape(bp * sp, D_MODEL)
    xs = x_sample.reshape(bs * ss, D_MODEL)
    new_caches, new_states = [], []
    for l in range(DEPTH):
        lam_init = 0.8 - 0.6 * math.exp(-0.3 * l)
        ctx_mod = dict(layer=l, mod_row0=bs, tokens_per_mod=bp * sp)
        lat_mod = dict(layer=l, mod_row0=0, tokens_per_mod=ss)

        qa, ka, qc, kc, va, vc, xb = _front(xp, mod4, ng3, w1, w2, gains3, None, kv_dtype=F32, **ctx_mod)
        r3 = lambda a: a.reshape(bp, sp, a.shape[-1])
        new_caches.append((r3(ka), r3(va), r3(kc), r3(vc)))
        oa = _attn_a(sink_a, r3(qa), r3(ka), r3(va), None, None, layer=l, tq=sp, banded=False)
        ob, st = _lru(r3(xb), *lru_p, zeros_h0, layer=l, h0_layer=0, tc=sp)
        new_states.append(st)
        oc = _attn_c(lam_vecs, subln3, r3(qc), r3(kc), r3(vc), None, None, layer=l, tq=sp, ts=sp, lam_init=lam_init)
        flat = lambda a: a.reshape(bp * sp, a.shape[-1])
        xp = _tail(xp, mod4, ng3, flat(oa), flat(ob), flat(oc), wg, wm, wbr, wout, **ctx_mod)

        qa, ka, qc, kc, va, vc, xb = _front(xs, mod4, ng3, w1, w2, gains3, rope, kv_dtype=BF16, **lat_mod)
        r3 = lambda a: a.reshape(bs, ss, a.shape[-1])
        oa = _attn_a(sink_a, r3(qa), r3(ka), r3(va), cka, cva, layer=l, tq=TQ_A, banded=True)
        ob, _ = _lru(r3(xb), *lru_p, state_lru, layer=l, h0_layer=l, tc=TC_LRU)
        oc = _attn_c(lam_vecs, subln3, r3(qc), r3(kc), r3(vc), ckc, cvc, layer=l, tq=TQ_C, ts=TS_C, lam_init=lam_init)
        flat = lambda a: a.reshape(bs * ss, a.shape[-1])
        xs = _tail(xs, mod4, ng3, flat(oa), flat(ob), flat(oc), wg, wm, wbr, wout, **lat_mod)

    ka, va, kc, vc = (jnp.stack(per_layer, axis=1) for per_layer in zip(*new_caches))
    return (xp.reshape(bp, sp, D_MODEL), xs.reshape(bs, ss, D_MODEL),
            ka.reshape(bp, DEPTH, sp, KV_A, HD), va.reshape(bp, DEPTH, sp, KV_A, HD),
            kc.reshape(bp, DEPTH, sp, H_C, 2, HD), vc.reshape(bp, DEPTH, sp, H_C, DV_C),
            jnp.stack(new_states, axis=1))
```

```python
import functools
import math

import jax
import jax.numpy as jnp
import numpy as np
from jax import lax
from jax.experimental import pallas as pl
from jax.experimental.pallas import tpu as pltpu

F32 = jnp.float32
BF16 = jnp.bfloat16

D_MODEL = 1024
DEPTH = 2
GRID_W = 64
HD = 64
SCALE = 1.0 / math.sqrt(HD)
LOG2E = math.log2(math.e)
Q_SCALE = SCALE * LOG2E
H_A = 8
KV_A = 2
WINDOW = 128
D_A = H_A * HD
D_LRU = 512
LRU_BLOCKS = 8
LRU_BW = D_LRU // LRU_BLOCKS
CONV_W = 4
LRU_C = 8.0
H_C = 4
DV_C = 2 * HD
D_C = H_C * DV_C
ROPE_BASE = 10000.0
ROPE_FREQS = HD // 4
EPS = 1e-6
LANES = 128
SUBLANES = 8
VMEM_LIMIT = 56 * 1024 * 1024

_OFF = np.cumsum([0, D_A, KV_A * HD, KV_A * HD, D_A, D_LRU, D_LRU,
                  H_C * 2 * HD, H_C * 2 * HD, D_C, D_C, 3 * D_MODEL])
(_QA, _KA, _VA, _GA, _XB, _GB, _QC, _KC, _VC, _GC, _MG) = [
    (int(_OFF[i]), int(_OFF[i + 1])) for i in range(11)]
N_NORM = D_A + KV_A * HD + 2 * H_C * 2 * HD
N_PLAIN = KV_A * HD + D_C + D_LRU

TM_FRONT = 512
TM_TAIL = 512
TQ_A = 2 * WINDOW
TQ_C = 512
TS_C = 512
TC_LRU = 512


def _params(sem, vmem=VMEM_LIMIT):
    return pltpu.CompilerParams(dimension_semantics=sem, vmem_limit_bytes=vmem)


def _dot(a, b):
    return jnp.dot(a, b, preferred_element_type=F32)


def _dot_nt(a, b):
    return lax.dot_general(a, b, (((1,), (1,)), ((), ())), preferred_element_type=F32)


def _sigmoid(x):
    return 1.0 / (1.0 + jnp.exp(-x))


def _mod_kernel(c_ref, w_ref, b_ref, o_ref):
    c = c_ref[...]
    a = c * _sigmoid(c)
    w = w_ref[...]
    a_hi = a.astype(BF16)
    a_lo = (a - a_hi.astype(F32)).astype(BF16)
    w_hi = w.astype(BF16)
    w_lo = (w - w_hi.astype(F32)).astype(BF16)
    o_ref[...] = _dot(a_hi, w_hi) + _dot(a_hi, w_lo) + _dot(a_lo, w_hi) + b_ref[...]


def _modulation(cvecs, mod_w, mod_b):
    tn = 768
    return pl.pallas_call(
        _mod_kernel,
        out_shape=jax.ShapeDtypeStruct((DEPTH, SUBLANES, 3 * D_MODEL), F32),
        grid=(DEPTH, 3 * D_MODEL // tn),
        in_specs=[pl.BlockSpec((SUBLANES, D_MODEL), lambda l, j: (0, 0)),
                  pl.BlockSpec((None, D_MODEL, tn), lambda l, j: (l, 0, j)),
                  pl.BlockSpec((None, 1, tn), lambda l, j: (l, 0, j))],
        out_specs=pl.BlockSpec((None, SUBLANES, tn), lambda l, j: (l, 0, j)),
        compiler_params=_params(("arbitrary", "arbitrary")),
        name="modulation",
    )(cvecs, mod_w, mod_b.reshape(DEPTH, 1, 3 * D_MODEL))


def _normed_input(x, mod, ng):
    ms = jnp.mean(x * x, axis=-1, keepdims=True)
    shift = mod[:, 0:D_MODEL]
    scale = mod[:, D_MODEL:2 * D_MODEL]
    return (x * lax.rsqrt(ms + EPS) * ng) * (1.0 + scale) + shift


def _layer_spec(shape, layer):
    zeros = (0,) * len(shape)
    return pl.BlockSpec((None,) + tuple(shape), lambda *_: (layer,) + zeros)


def _mod_spec(layer, row0, tiles_per_row):
    return pl.BlockSpec((None, None, 1, 3 * D_MODEL), lambda i: (layer, row0 + i // tiles_per_row, 0, 0))


class _Part:
    def __init__(self, name, body, grid, in_specs, args, out_shape, out_specs, scratch=()):
        self.name, self.body, self.grid = name, body, tuple(grid)
        self.in_specs, self.args = list(in_specs), list(args)
        self.out_shape, self.out_specs, self.scratch = list(out_shape), list(out_specs), list(scratch)


def _run(part):
    ni, no = len(part.args), len(part.out_shape)

    def kern(*refs):
        ids = tuple(pl.program_id(k) for k in range(len(part.grid)))
        part.body(refs[:ni], refs[ni:ni + no], refs[ni + no:], ids)

    return pl.pallas_call(
        kern, out_shape=part.out_shape, grid=part.grid, in_specs=part.in_specs, out_specs=part.out_specs,
        scratch_shapes=part.scratch, compiler_params=_params(("arbitrary",) * len(part.grid)), name=part.name)(*part.args)


def _run_pair(a, b, b_ids, name):
    assert len(a.grid) == 1 and a.grid[0] == math.prod(b.grid)

    def remap(spec):
        if spec.index_map is None:
            return spec
        return pl.BlockSpec(spec.block_shape, lambda i, f=spec.index_map: f(*b_ids(i)),
                            memory_space=spec.memory_space, pipeline_mode=spec.pipeline_mode)

    na, nb, oa, ob, sa = len(a.args), len(b.args), len(a.out_shape), len(b.out_shape), len(a.scratch)

    def kern(*refs):
        ins, outs, scr = refs[:na + nb], refs[na + nb:na + nb + oa + ob], refs[na + nb + oa + ob:]
        i = pl.program_id(0)
        b.body(ins[na:], outs[oa:], scr[sa:], b_ids(i))
        a.body(ins[:na], outs[:oa], scr[:sa], (i,))

    res = pl.pallas_call(
        kern, out_shape=a.out_shape + b.out_shape, grid=a.grid,
        in_specs=a.in_specs + [remap(s) for s in b.in_specs],
        out_specs=a.out_specs + [remap(s) for s in b.out_specs],
        scratch_shapes=a.scratch + b.scratch, compiler_params=_params(("arbitrary",)), name=name)(*a.args, *b.args)
    return res[:oa], res[oa:]


def _front_body(ins, outs, scratch, ids, *, use_rope):
    x_ref, mod_ref, ng_ref, w1_ref, w2_ref, gain_ref = ins[:6]
    qa_ref, ka_ref, qc_ref, kc_ref, va_ref, vc_ref, xb_ref = outs
    x = x_ref[...]
    tm = x.shape[0]
    hb = _normed_input(x, mod_ref[...], ng_ref[...]).astype(BF16)
    p1 = _dot(hb, w1_ref[...])
    lane = lax.broadcasted_iota(jnp.int32, (tm, LANES), 1)
    first_half = (lane & ROPE_FREQS) == 0
    wide = 2 * LANES
    same_head = (lax.broadcasted_iota(jnp.int32, (wide, wide), 0) // HD
                 == lax.broadcasted_iota(jnp.int32, (wide, wide), 1) // HD)
    ones_bd = jnp.where(same_head, 1.0, 0.0).astype(BF16)
    sums = []
    for c0 in range(0, N_NORM, wide):
        w = min(wide, N_NORM - c0)
        sq = p1[:, c0:c0 + w] * p1[:, c0:c0 + w]
        sq_hi = sq.astype(BF16)
        sq_lo = (sq - sq_hi.astype(F32)).astype(BF16)
        sums.append(_dot(sq_hi, ones_bd[:w, :w]) + _dot(sq_lo, ones_bd[:w, :w]))
    dests = ([(qa_ref, c) for c in range(4)] + [(ka_ref, 0)]
             + [(qc_ref, c) for c in range(4)] + [(kc_ref, c) for c in range(4)])
    for c, (o_ref, oc) in enumerate(dests):
        pc = p1[:, c * LANES:(c + 1) * LANES]
        msq = sums[c // 2][:, (c % 2) * LANES:(c % 2 + 1) * LANES] * (1.0 / HD)
        y = pc * lax.rsqrt(msq + EPS) * gain_ref[:, c * LANES:(c + 1) * LANES]
        if use_rope:
            cos_ref, sin_ref = ins[6:8]
            partner = jnp.where(first_half, pltpu.roll(y, LANES - ROPE_FREQS, 1),
                                pltpu.roll(y, ROPE_FREQS, 1))
            y = y * cos_ref[...] + partner * sin_ref[...]
        o_ref[:, oc * LANES:(oc + 1) * LANES] = y.astype(o_ref.dtype)
    p2 = _dot(hb, w2_ref[...])
    va_ref[...] = p2[:, 0:KV_A * HD].astype(va_ref.dtype)
    vc_ref[...] = p2[:, KV_A * HD:KV_A * HD + D_C].astype(vc_ref.dtype)
    xb_ref[...] = p2[:, KV_A * HD + D_C:]


def _front_part(x2d, mod4, ng3, w1, w2, gains3, rope, *, layer, mod_row0, tokens_per_mod, kv_dtype, tm=TM_FRONT):
    n = x2d.shape[0]
    use_rope = rope is not None
    row = lambda i: (i, 0)
    in_specs = [pl.BlockSpec((tm, D_MODEL), row),
                _mod_spec(layer, mod_row0, tokens_per_mod // tm),
                _layer_spec((1, D_MODEL), layer),
                _layer_spec((D_MODEL, N_NORM), layer),
                _layer_spec((D_MODEL, N_PLAIN), layer),
                _layer_spec((1, N_NORM), layer)]
    args = [x2d, mod4, ng3, w1, w2, gains3]
    if use_rope:
        per_seq = rope[0].shape[0] // tm
        in_specs += [pl.BlockSpec((tm, LANES), lambda i: (i % per_seq, 0))] * 2
        args += list(rope)
    widths = (D_A, KV_A * HD, H_C * 2 * HD, H_C * 2 * HD, KV_A * HD, D_C, D_LRU)
    dtypes = (BF16, kv_dtype, BF16, kv_dtype, kv_dtype, kv_dtype, F32)
    return _Part("front_rope" if use_rope else "front", functools.partial(_front_body, use_rope=use_rope),
                 (n // tm,), in_specs, args,
                 [jax.ShapeDtypeStruct((n, w), d) for w, d in zip(widths, dtypes)],
                 [pl.BlockSpec((tm, w), row) for w in widths])


def _attn_a_kernel(*refs, tq, seq_len, banded, has_ctx, layer):
    if has_ctx:
        sink_ref, q_ref, k_ref, v_ref, kc_ref, vc_ref, o_ref = refs
    else:
        sink_ref, q_ref, k_ref, v_ref, o_ref = refs
    i = pl.program_id(1)
    g = H_A // KV_A
    if banded:
        nwin = tq + 2 * WINDOW
        start = pl.multiple_of(jnp.clip(i * tq - WINDOW, 0, seq_len - nwin), WINDOW)
        kall = k_ref[pl.ds(start, nwin), :].astype(F32)
        vall = v_ref[pl.ds(start, nwin), :].astype(F32)
    else:
        nwin = seq_len
        kall = k_ref[...].astype(F32)
        vall = v_ref[...].astype(F32)
    if has_ctx:
        kall = jnp.concatenate([kall, kc_ref[...].astype(F32)], axis=0)
        vall = jnp.concatenate([vall, vc_ref[...].astype(F32)], axis=0)
    nk = kall.shape[0]
    if banded:
        kpos = start + lax.broadcasted_iota(jnp.int32, (tq, nwin), 1)
        qpos = i * tq + lax.broadcasted_iota(jnp.int32, (tq, nwin), 0)
        bias = jnp.where(jnp.abs(kpos - qpos) <= WINDOW, 0.0, -jnp.inf)
    lo_k = lax.broadcasted_iota(jnp.int32, (nk, LANES), 1) < HD
    lo_q = lax.broadcasted_iota(jnp.int32, (tq, LANES), 1) < HD
    k_sw = pltpu.roll(kall, HD, 1)
    v_sw = pltpu.roll(vall, HD, 1)
    kd = (jnp.where(lo_k, kall, k_sw).astype(BF16), jnp.where(lo_k, k_sw, kall).astype(BF16))
    vd = [[jnp.where(lo_k, vall, 1.0).astype(BF16), jnp.where(lo_k, 1.0, v_sw).astype(BF16)],
          [jnp.where(lo_k, v_sw, 1.0).astype(BF16), jnp.where(lo_k, 1.0, vall).astype(BF16)]]
    q = q_ref[...]
    zero = jnp.zeros((tq, LANES), BF16)
    outs = []
    for h in range(H_A):
        kv, half = h // g, h % 2
        qz = jnp.where(lo_q if half == 0 else jnp.logical_not(lo_q), q[:, (h // 2) * LANES:(h // 2 + 1) * LANES], zero)
        s = _dot_nt(qz, kd[kv])
        if banded:
            s = jnp.concatenate([s[:, :nwin] + bias, s[:, nwin:]], axis=1)
        snk = sink_ref[layer, h] * LOG2E
        m = jnp.maximum(jnp.max(s, axis=-1, keepdims=True), snk)
        e = jnp.exp2(s - m)
        pv = _dot(e.astype(BF16), vd[kv][half])
        outs.append(pv / (pltpu.roll(pv, HD, 1) + jnp.exp2(snk - m)))
    for c in range(H_A // 2):
        pair = jnp.where(lo_q, outs[2 * c], outs[2 * c + 1])
        o_ref[:, c * LANES:(c + 1) * LANES] = pair.astype(o_ref.dtype)


def _attn_a(sink, q, k, v, k_ctx, v_ctx, *, layer, tq, banded):
    b, t, _ = q.shape
    has_ctx = k_ctx is not None
    kv_spec = pl.BlockSpec((None, t, KV_A * HD), lambda bi, i: (bi, 0, 0))
    in_specs = [pl.BlockSpec(memory_space=pltpu.SMEM),
                pl.BlockSpec((None, tq, D_A), lambda bi, i: (bi, i, 0)), kv_spec, kv_spec]
    args = [sink, q, k, v]
    if has_ctx:
        s = k_ctx.shape[2]
        in_specs += [pl.BlockSpec((None, None, s, KV_A * HD), lambda bi, i: (bi, layer, 0, 0))] * 2
        args += [k_ctx, v_ctx]
    return pl.pallas_call(
        functools.partial(_attn_a_kernel, tq=tq, seq_len=t, banded=banded, has_ctx=has_ctx, layer=layer),
        out_shape=jax.ShapeDtypeStruct((b, t, D_A), BF16),
        grid=(b, t // tq),
        in_specs=in_specs,
        out_specs=pl.BlockSpec((None, tq, D_A), lambda bi, i: (bi, i, 0)),
        compiler_params=_params(("arbitrary", "arbitrary")),
        name="attn_a_latent" if has_ctx else "attn_a_context",
    )(*args)


def _attn_c_kernel(*refs, tq, ts, n_loc, has_ctx, lam_init):
    if has_ctx:
        lq1, lk1, lq2, lk2, sub_ref, q_ref, k_ref, v_ref, kc_ref, vc_ref, o_ref = refs
    else:
        lq1, lk1, lq2, lk2, sub_ref, q_ref, k_ref, v_ref, o_ref = refs
    lam = (jnp.exp(jnp.sum(lq1[...] * lk1[...], axis=-1, keepdims=True))
           - jnp.exp(jnp.sum(lq2[...] * lk2[...], axis=-1, keepdims=True)) + lam_init)
    lo = lax.broadcasted_iota(jnp.int32, (tq, LANES), 1) < HD
    zero = jnp.zeros((tq, LANES), BF16)
    heads = [slice(h * LANES, (h + 1) * LANES) for h in range(H_C)]
    qz = []
    for cols in heads:
        q12 = q_ref[:, cols]
        qz.append(jnp.concatenate([jnp.where(lo, q12, zero), jnp.where(lo, zero, q12)], axis=0))

    def update(carry, kt, vt):
        new = []
        for h, cols in enumerate(heads):
            m, l, acc = carry[h]
            s = _dot_nt(qz[h], kt[:, cols].astype(BF16))
            m_new = jnp.maximum(m, jnp.max(s, axis=-1, keepdims=True))
            alpha = jnp.exp2(m - m_new)
            p = jnp.exp2(s - m_new)
            psum = p[:, 0:LANES]
            for j in range(1, p.shape[1] // LANES):
                psum = psum + p[:, j * LANES:(j + 1) * LANES]
            l = alpha * l + psum
            acc = alpha * acc + _dot(p.astype(BF16), vt[:, cols].astype(BF16))
            new.append((m_new, l, acc))
        return tuple(new)

    carry = tuple((jnp.full((2 * tq, 1), -jnp.inf, F32), jnp.zeros((2 * tq, LANES), F32),
                   jnp.zeros((2 * tq, LANES), F32)) for _ in heads)
    for j in range(n_loc):
        carry = update(carry, k_ref[j * ts:(j + 1) * ts, :], v_ref[j * ts:(j + 1) * ts, :])
    if has_ctx:
        carry = update(carry, kc_ref[...], vc_ref[...])
    for h, cols in enumerate(heads):
        _, l, acc = carry[h]
        l = jnp.sum(l, axis=-1, keepdims=True)
        o = acc[:tq] / l[:tq] - lam * (acc[tq:] / l[tq:])
        o = o * lax.rsqrt(jnp.mean(o * o, axis=-1, keepdims=True) + EPS) * sub_ref[...] * (1.0 - lam_init)
        o_ref[:, cols] = o.astype(o_ref.dtype)


def _attn_c(lam_vecs, subln3, q, k, v, k_ctx, v_ctx, *, layer, tq, ts, lam_init):
    b, t, _ = q.shape
    s_loc = k.shape[1]
    has_ctx = k_ctx is not None
    once = dict(pipeline_mode=pl.Buffered(1)) if t // tq > 1 else {}
    kv_spec = pl.BlockSpec((None, s_loc, D_C), lambda bi, i: (bi, 0, 0), **once)
    in_specs = ([_layer_spec((1, HD), layer)] * 4 + [_layer_spec((1, DV_C), layer)]
                + [pl.BlockSpec((None, tq, D_C), lambda bi, i: (bi, i, 0)), kv_spec, kv_spec])
    args = list(lam_vecs) + [subln3, q, k, v]
    if has_ctx:
        sc = k_ctx.shape[2]
        in_specs += [pl.BlockSpec((None, None, sc, D_C), lambda bi, i: (bi, layer, 0, 0), **once)] * 2
        args += [k_ctx, v_ctx]
    return pl.pallas_call(
        functools.partial(_attn_c_kernel, tq=tq, ts=ts, n_loc=s_loc // ts, has_ctx=has_ctx, lam_init=lam_init),
        out_shape=jax.ShapeDtypeStruct((b, t, D_C), BF16),
        grid=(b, t // tq),
        in_specs=in_specs,
        out_specs=pl.BlockSpec((None, tq, D_C), lambda bi, i: (bi, i, 0)),
        compiler_params=_params(("arbitrary", "arbitrary")),
        name="attn_c_latent" if has_ctx else "attn_c_context",
    )(*args)


def _lru_body(ins, outs, scratch, ids, *, tc, nc):
    xc_ref, xp_ref, xn_ref, cw_ref, cb_ref, wg_ref, bg_ref, lam_ref, h0_ref = ins
    ob_ref, st_ref = outs
    a_s, b_s, hs, hf_s, hcar, xb_s = scratch
    _, p, c = ids
    cidx = c + p * (nc - 1 - 2 * c)
    r0 = pl.multiple_of(cidx * tc, tc)

    @pl.when(p == 0)
    def _():
        cur = xc_ref[...]
        prev = jnp.where(cidx > 0, xp_ref[...], 0.0)
        nxt = jnp.where(cidx < nc - 1, xn_ref[...], 0.0)
        ext = jnp.concatenate([prev, cur, nxt], axis=0)
        n_ext = tc + 2 * SUBLANES
        acc = jnp.broadcast_to(cb_ref[...], (tc, D_LRU))
        for j in range(CONV_W):
            off = j - CONV_W // 2
            sh = ext if off == 0 else pltpu.roll(ext, (-off) % n_ext, 0)
            acc = acc + sh[SUBLANES:SUBLANES + tc] * cw_ref[j:j + 1, :]
        xb_s[pl.ds(r0, tc), :] = acc

    xb = xb_s[pl.ds(r0, tc), :]
    gm = _dot(xb.astype(BF16), wg_ref[...]) + bg_ref[...]
    r = _sigmoid(gm[:, :D_LRU])
    ig = _sigmoid(gm[:, D_LRU:])
    nl = -lam_ref[...]
    softplus = jnp.maximum(nl, 0.0) + jnp.log1p(jnp.exp(-jnp.abs(nl)))
    log_a = -LRU_C * r * softplus
    a = jnp.exp(log_a)
    one_m_a2 = -jnp.tanh(log_a) * (a * a + 1.0)
    bb = jnp.where(one_m_a2 == 0.0, 0.0, one_m_a2 * lax.rsqrt(one_m_a2)) * (ig * xb)

    ng = tc // SUBLANES
    a3 = a.reshape(ng, SUBLANES, D_LRU)
    b3 = bb.reshape(ng, SUBLANES, D_LRU)
    row = lax.broadcasted_iota(jnp.int32, (ng, SUBLANES, D_LRU), 1)
    order = row + p * (SUBLANES - 1 - 2 * row)
    for sh in (1, 2, 4):
        shift = sh + p * (SUBLANES - 2 * sh)
        a_prev = pltpu.roll(a3, shift, 1)
        b_prev = pltpu.roll(b3, shift, 1)
        valid = order >= sh
        b3 = jnp.where(valid, a3 * b_prev + b3, b3)
        a3 = jnp.where(valid, a3 * a_prev, a3)
    a_tot = jnp.where(p == 0, a3[:, SUBLANES - 1:SUBLANES, :], a3[:, 0:1, :])
    b_tot = jnp.where(p == 0, b3[:, SUBLANES - 1:SUBLANES, :], b3[:, 0:1, :])
    a_s[...] = jnp.broadcast_to(a_tot, a3.shape).reshape(tc, D_LRU)
    b_s[...] = jnp.broadcast_to(b_tot, b3.shape).reshape(tc, D_LRU)

    @pl.when(c == 0)
    def _():
        hcar[...] = jnp.broadcast_to(h0_ref[pl.ds(p, 1), :], (SUBLANES, D_LRU))

    def group(gi, h):
        gidx = gi + p * (ng - 1 - 2 * gi)
        base = pl.multiple_of(gidx * SUBLANES, SUBLANES)
        hs[pl.ds(base, SUBLANES), :] = h
        return a_s[pl.ds(base, SUBLANES), :] * h + b_s[pl.ds(base, SUBLANES), :]

    h = lax.fori_loop(0, ng, group, hcar[...], unroll=8)
    hcar[...] = h
    hloc = (a3 * hs[...].reshape(ng, SUBLANES, D_LRU) + b3).reshape(tc, D_LRU)

    @pl.when(p == 0)
    def _():
        hf_s[pl.ds(r0, tc), :] = hloc

    @pl.when(p == 1)
    def _():
        ob_ref[...] = (hf_s[pl.ds(r0, tc), :] + hloc).astype(ob_ref.dtype)

    @pl.when(c == nc - 1)
    def _():
        st_ref[pl.ds(p, 1), :] = h[0:1, :]


def _lru_part(xb, conv_w, conv_b3, wg, bg, lam4, h0, *, layer, h0_layer, tc):
    b, t, _ = xb.shape
    nc = t // tc
    nb8 = t // SUBLANES
    per = tc // SUBLANES
    cidx = lambda p, c: c + p * (nc - 1 - 2 * c)
    in_specs = [
        pl.BlockSpec((None, tc, D_LRU), lambda bi, p, c: (bi, cidx(p, c), 0)),
        pl.BlockSpec((None, SUBLANES, D_LRU), lambda bi, p, c: (bi, jnp.maximum(cidx(p, c) * per - 1, 0), 0)),
        pl.BlockSpec((None, SUBLANES, D_LRU), lambda bi, p, c: (bi, jnp.minimum((cidx(p, c) + 1) * per, nb8 - 1), 0)),
        _layer_spec((CONV_W, D_LRU), layer),
        _layer_spec((1, D_LRU), layer),
        pl.BlockSpec((None, None, D_LRU, 2 * D_LRU), lambda bi, p, c: (layer, p, 0, 0)),
        pl.BlockSpec((None, None, 1, 2 * D_LRU), lambda bi, p, c: (layer, p, 0, 0)),
        pl.BlockSpec((None, None, 1, D_LRU), lambda bi, p, c: (layer, p, 0, 0)),
        pl.BlockSpec((None, None, 2, D_LRU), lambda bi, p, c: (bi, h0_layer, 0, 0)),
    ]
    return _Part("lru", functools.partial(_lru_body, tc=tc, nc=nc), (b, 2, nc), in_specs,
                 [xb, xb, xb, conv_w, conv_b3, wg, bg, lam4, h0],
                 [jax.ShapeDtypeStruct((b, t, D_LRU), BF16), jax.ShapeDtypeStruct((b, 2, D_LRU), F32)],
                 [pl.BlockSpec((None, tc, D_LRU), lambda bi, p, c: (bi, nc - 1 - p * c, 0)),
                  pl.BlockSpec((None, 2, D_LRU), lambda bi, p, c: (bi, 0, 0))],
                 [pltpu.VMEM((tc, D_LRU), F32), pltpu.VMEM((tc, D_LRU), F32), pltpu.VMEM((tc, D_LRU), F32),
                  pltpu.VMEM((t, D_LRU), F32), pltpu.VMEM((SUBLANES, D_LRU), F32), pltpu.VMEM((t, D_LRU), F32)])


def _tail_kernel(x_ref, mod_ref, ng_ref, oa_ref, ob_ref, oc_ref, wg_ref, wm_ref, wbr_ref, wout_ref, o_ref):
    x = x_ref[...]
    mod = mod_ref[...]
    hb = _normed_input(x, mod, ng_ref[...]).astype(BF16)
    y = None
    for br, br_ref in enumerate((oa_ref, ob_ref, oc_ref)):
        gt = _dot(hb, wg_ref[:, br * D_A:(br + 1) * D_A])
        u = (br_ref[...].astype(F32) * (gt * _sigmoid(gt))).astype(BF16)
        proj = _dot(u, wbr_ref[br])
        mg = _sigmoid(_dot(hb, wm_ref[:, br * D_MODEL:(br + 1) * D_MODEL]))
        y = mg * proj if y is None else y + mg * proj
    gate = mod[:, 2 * D_MODEL:]
    o_ref[...] = x + gate * _dot(y.astype(BF16), wout_ref[...])


def _tail_part(x2d, mod4, ng3, oa, ob, oc, wg, wm, wbr, wout, *, layer, mod_row0, tokens_per_mod, tm=TM_TAIL):
    n = x2d.shape[0]
    row = lambda i: (i, 0)
    once = dict(pipeline_mode=pl.Buffered(1))

    def weight(shape):
        zeros = (0,) * len(shape)
        return pl.BlockSpec((None,) + shape, lambda i: (layer,) + zeros, **once)

    in_specs = [pl.BlockSpec((tm, D_MODEL), row),
                _mod_spec(layer, mod_row0, tokens_per_mod // tm),
                _layer_spec((1, D_MODEL), layer),
                pl.BlockSpec((tm, D_A), row),
                pl.BlockSpec((tm, D_LRU), row),
                pl.BlockSpec((tm, D_C), row),
                weight((D_MODEL, 3 * D_A)),
                weight((D_MODEL, 3 * D_MODEL)),
                weight((3, D_A, D_MODEL)),
                weight((D_MODEL, D_MODEL))]
    return _Part("tail", lambda ins, outs, scratch, ids: _tail_kernel(*ins, *outs), (n // tm,), in_specs,
                 [x2d, mod4, ng3, oa, ob, oc, wg, wm, wbr, wout],
                 [jax.ShapeDtypeStruct((n, D_MODEL), F32)], [pl.BlockSpec((tm, D_MODEL), row)])


def _rope_tables(seq_len):
    pos = jnp.arange(seq_len)
    row = (pos // GRID_W).astype(F32)
    col = (pos % GRID_W).astype(F32)
    inv = jnp.power(ROPE_BASE, -jnp.arange(ROPE_FREQS, dtype=F32) / ROPE_FREQS)
    ang_r = row[:, None] * inv
    ang_c = col[:, None] * inv
    cos = jnp.concatenate([jnp.cos(ang_r)] * 2 + [jnp.cos(ang_c)] * 2, axis=-1)
    sin = jnp.concatenate([-jnp.sin(ang_r), jnp.sin(ang_r), -jnp.sin(ang_c), jnp.sin(ang_c)], axis=-1)
    return jnp.tile(cos, (1, LANES // HD)), jnp.tile(sin, (1, LANES // HD))


def _block_diag(w):
    eye = jnp.eye(LRU_BLOCKS, dtype=w.dtype)
    dense = w[..., :, :, None, :] * eye[:, None, :, None]
    return dense.reshape(w.shape[:-3] + (D_LRU, D_LRU))


def kernel(x_prompt, x_sample, cache_a_k, cache_a_v, cache_c_k, cache_c_v, state_lru, c, c_ctx, norm_g, mod_w, mod_b, w_in, qn_a, kn_a, sink_a, conv_w, conv_b, lru_wa, lru_ba, lru_wx, lru_bx, lru_lam, qn_c, kn_c, lam_q1, lam_k1, lam_q2, lam_k2, subln_c, w_br_a, w_br_b, w_br_c, w_out):
    bp, sp, _ = x_prompt.shape
    bs, ss, _ = x_sample.shape
    past = cache_a_k.shape[2]

    cvecs = jnp.concatenate([c, c_ctx[None, :], jnp.zeros((SUBLANES - bs - 1, D_MODEL), F32)], axis=0)
    mod = _modulation(cvecs, mod_w, mod_b)
    mod4 = mod.reshape(DEPTH, SUBLANES, 1, 3 * D_MODEL)
    rope = _rope_tables(ss)

    cols = lambda *spans: jnp.concatenate([w_in[:, :, a:b] for a, b in spans], axis=-1).astype(BF16)
    w1 = cols(_QA, _KA, _QC, _KC)
    w2 = cols(_VA, _VC, _XB)
    wg = cols(_GA, _GB, _GC)
    wm = cols(_MG)
    wbr = jnp.stack([w_br_a, w_br_b, w_br_c], axis=1).astype(BF16)
    wout = w_out.astype(BF16)
    gains3 = jnp.concatenate([jnp.tile(qn_a * Q_SCALE, (1, H_A)), jnp.tile(kn_a, (1, KV_A)),
                              jnp.tile(qn_c * Q_SCALE, (1, 2 * H_C)), jnp.tile(kn_c, (1, 2 * H_C))],
                             axis=-1)[:, None, :]
    ng3 = norm_g[:, None, :]
    lru_wg = jnp.concatenate([_block_diag(lru_wa), _block_diag(lru_wx)], axis=-1).astype(BF16)
    lru_bg = jnp.concatenate([lru_ba, lru_bx], axis=-1)[:, :, None, :]
    lru_lam4 = lru_lam[:, :, None, :]
    lru_p = (conv_w, conv_b[:, None, :], lru_wg, lru_bg, lru_lam4)
    lam_vecs = [v[:, None, :] for v in (lam_q1, lam_k1, lam_q2, lam_k2)]
    subln3 = subln_c[:, None, :]
    cka = cache_a_k.reshape(bs, DEPTH, past, KV_A * HD)
    cva = cache_a_v.reshape(bs, DEPTH, past, KV_A * HD)
    ckc = cache_c_k.reshape(bs, DEPTH, past, D_C)
    cvc = cache_c_v.reshape(bs, DEPTH, past, D_C)
    zeros_h0 = jnp.zeros((bp, 1, 2, D_LRU), F32)

    xp = x_prompt.reshape(bp * sp, D_MODEL)
    xs = x_sample.reshape(bs * ss, D_MODEL)
    new_caches, new_states = [], []
    for l in range(DEPTH):
        lam_init = 0.8 - 0.6 * math.exp(-0.3 * l)
        ctx_mod = dict(layer=l, mod_row0=bs, tokens_per_mod=bp * sp)
        lat_mod = dict(layer=l, mod_row0=0, tokens_per_mod=ss)

        r3c = lambda a: a.reshape(bp, sp, a.shape[-1])
        r3s = lambda a: a.reshape(bs, ss, a.shape[-1])
        qa_c, ka_c, qc_c, kc_c, va_c, vc_c, xb_c = _run(
            _front_part(xp, mod4, ng3, w1, w2, gains3, None, kv_dtype=F32, **ctx_mod))
        new_caches.append((r3c(ka_c), r3c(va_c), r3c(kc_c), r3c(vc_c)))

        lru_c = _lru_part(r3c(xb_c), *lru_p, zeros_h0, layer=l, h0_layer=0, tc=sp)
        front_s = _front_part(xs, mod4, ng3, w1, w2, gains3, rope, kv_dtype=BF16, tm=ss * bs // (2 * bp), **lat_mod)
        (qa_s, ka_s, qc_s, kc_s, va_s, vc_s, xb_s), (ob_c, st) = _run_pair(
            front_s, lru_c, lambda i: (i // 2, i % 2, i * 0), "front_rope_lru")
        new_states.append(st)

        oa_c = _attn_a(sink_a, r3c(qa_c), r3c(ka_c), r3c(va_c), None, None, layer=l, tq=sp, banded=False)
        oc_c = _attn_c(lam_vecs, subln3, r3c(qc_c), r3c(kc_c), r3c(vc_c), None, None, layer=l, tq=sp, ts=sp,
                       lam_init=lam_init)

        flat = lambda a: a.reshape(-1, a.shape[-1])
        nc_s = ss // TC_LRU
        lru_s = _lru_part(r3s(xb_s), *lru_p, state_lru, layer=l, h0_layer=l, tc=TC_LRU)
        tail_c = _tail_part(xp, mod4, ng3, flat(oa_c), flat(ob_c), flat(oc_c), wg, wm, wbr, wout,
                            tm=bp * sp // (bs * 2 * nc_s), **ctx_mod)
        (xp,), (ob_s, _) = _run_pair(
            tail_c, lru_s, lambda i: (i // (2 * nc_s), (i // nc_s) % 2, i % nc_s), "tail_lru")

        oa_s = _attn_a(sink_a, r3s(qa_s), r3s(ka_s), r3s(va_s), cka, cva, layer=l, tq=TQ_A, banded=True)
        oc_s = _attn_c(lam_vecs, subln3, r3s(qc_s), r3s(kc_s), r3s(vc_s), ckc, cvc, layer=l, tq=TQ_C, ts=TS_C,
                       lam_init=lam_init)
        (xs,) = _run(_tail_part(xs, mod4, ng3, flat(oa_s), flat(ob_s), flat(oc_s), wg, wm, wbr, wout, **lat_mod))

    ka, va, kc, vc = (jnp.stack(per_layer, axis=1) for per_layer in zip(*new_caches))
    return (xp.reshape(bp, sp, D_MODEL), xs.reshape(bs, ss, D_MODEL),
            ka.reshape(bp, DEPTH, sp, KV_A, HD), va.reshape(bp, DEPTH, sp, KV_A, HD),
            kc.reshape(bp, DEPTH, sp, H_C, 2, HD), vc.reshape(bp, DEPTH, sp, H_C, DV_C),
            jnp.stack(new_states, axis=1))
```

```python
import functools
import math

import jax
import jax.numpy as jnp
from jax import lax
from jax.experimental import pallas as pl
from jax.experimental.pallas import tpu as pltpu

F32 = jnp.float32
BF16 = jnp.bfloat16

D_MODEL = 1024
DEPTH = 2
GRID_W = 64
HD = 64
SCALE = 1.0 / math.sqrt(HD)
LOG2E = math.log2(math.e)
Q_SCALE = SCALE * LOG2E
H_A = 8
KV_A = 2
WINDOW = 128
D_A = H_A * HD
D_LRU = 512
LRU_BLOCKS = 8
LRU_BW = D_LRU // LRU_BLOCKS
CONV_W = 4
LRU_C = 8.0
H_C = 4
DV_C = 2 * HD
D_C = H_C * DV_C
ROPE_BASE = 10000.0
ROPE_FREQS = HD // 4
EPS = 1e-6
LANES = 128
SUBLANES = 8
VMEM_LIMIT = 56 * 1024 * 1024

_SECTIONS = (("qa", D_A), ("kava", 2 * KV_A * HD), ("ga", D_A), ("xb", D_LRU), ("gb", D_LRU),
             ("qc", H_C * 2 * HD), ("kc", H_C * 2 * HD), ("vc", D_C), ("gc", D_C), ("mg", 3 * D_MODEL))
W_BLOCK = 256


def _col_blocks(*names):
    start, spans = 0, {}
    for name, width in _SECTIONS:
        spans[name] = (start, start + width)
        start += width
    blocks = []
    for name in names:
        a, b = spans[name]
        assert a % W_BLOCK == 0 and b % W_BLOCK == 0
        blocks += range(a // W_BLOCK, b // W_BLOCK)
    return tuple(blocks)


FRONT_BLOCKS = _col_blocks("qa", "qc", "kc", "kava", "vc", "xb")
GATE_BLOCKS = _col_blocks("ga", "gb", "gc")
MERGE_BLOCKS = _col_blocks("mg")
N_NORM = D_A + KV_A * HD + 2 * H_C * 2 * HD
N_PLAIN = KV_A * HD + D_C + D_LRU

TM_FRONT = 512
TM_TAIL = 512
TQ_A = 2 * WINDOW
TQ_C = 512
TS_C = 512
TC_LRU = 512


def _params(sem, vmem=VMEM_LIMIT):
    return pltpu.CompilerParams(dimension_semantics=sem, vmem_limit_bytes=vmem)


def _dot(a, b):
    return jnp.dot(a, b, preferred_element_type=F32)


def _dot_nt(a, b):
    return lax.dot_general(a, b, (((1,), (1,)), ((), ())), preferred_element_type=F32)


def _sigmoid(x):
    return 1.0 / (1.0 + jnp.exp(-x))


def _permute_cast_kernel(perm_ref, w_ref, o_ref):
    o_ref[...] = w_ref[...].astype(o_ref.dtype)


def _permute_cast(w, blocks):
    nl, k, _ = w.shape
    grid_spec = pltpu.PrefetchScalarGridSpec(
        num_scalar_prefetch=1, grid=(len(blocks),),
        in_specs=[pl.BlockSpec((nl, k, W_BLOCK), lambda j, perm: (0, 0, perm[j]))],
        out_specs=pl.BlockSpec((nl, k, W_BLOCK), lambda j, perm: (0, 0, j)))
    return pl.pallas_call(
        _permute_cast_kernel, grid_spec=grid_spec,
        out_shape=jax.ShapeDtypeStruct((nl, k, len(blocks) * W_BLOCK), BF16),
        compiler_params=_params(("arbitrary",)), name="permute_cast",
    )(jnp.asarray(blocks, jnp.int32), w)


def _mod_kernel(c_ref, w_ref, b_ref, o_ref):
    c = c_ref[...]
    a = c * _sigmoid(c)
    w = w_ref[...]
    a_hi = a.astype(BF16)
    a_lo = (a - a_hi.astype(F32)).astype(BF16)
    w_hi = w.astype(BF16)
    w_lo = (w - w_hi.astype(F32)).astype(BF16)
    o_ref[...] = _dot(a_hi, w_hi) + _dot(a_hi, w_lo) + _dot(a_lo, w_hi) + b_ref[...]


def _modulation(cvecs, mod_w, mod_b):
    tn = 768
    return pl.pallas_call(
        _mod_kernel,
        out_shape=jax.ShapeDtypeStruct((DEPTH, SUBLANES, 3 * D_MODEL), F32),
        grid=(DEPTH, 3 * D_MODEL // tn),
        in_specs=[pl.BlockSpec((SUBLANES, D_MODEL), lambda l, j: (0, 0)),
                  pl.BlockSpec((None, D_MODEL, tn), lambda l, j: (l, 0, j)),
                  pl.BlockSpec((None, 1, tn), lambda l, j: (l, 0, j))],
        out_specs=pl.BlockSpec((None, SUBLANES, tn), lambda l, j: (l, 0, j)),
        compiler_params=_params(("arbitrary", "arbitrary")),
        name="modulation",
    )(cvecs, mod_w, mod_b.reshape(DEPTH, 1, 3 * D_MODEL))


def _normed_input(x, mod, ng):
    ms = jnp.mean(x * x, axis=-1, keepdims=True)
    shift = mod[:, 0:D_MODEL]
    scale = mod[:, D_MODEL:2 * D_MODEL]
    return (x * lax.rsqrt(ms + EPS) * ng) * (1.0 + scale) + shift


def _layer_spec(shape, layer):
    zeros = (0,) * len(shape)
    return pl.BlockSpec((None,) + tuple(shape), lambda *_: (layer,) + zeros)


def _mod_spec(layer, row0, tiles_per_row):
    return pl.BlockSpec((None, None, 1, 3 * D_MODEL), lambda i: (layer, row0 + i // tiles_per_row, 0, 0))


class _Part:
    def __init__(self, name, body, grid, in_specs, args, out_shape, out_specs, scratch=()):
        self.name, self.body, self.grid = name, body, tuple(grid)
        self.in_specs, self.args = list(in_specs), list(args)
        self.out_shape, self.out_specs, self.scratch = list(out_shape), list(out_specs), list(scratch)


def _run(part):
    ni, no = len(part.args), len(part.out_shape)

    def kern(*refs):
        ids = tuple(pl.program_id(k) for k in range(len(part.grid)))
        part.body(refs[:ni], refs[ni:ni + no], refs[ni + no:], ids)

    return pl.pallas_call(
        kern, out_shape=part.out_shape, grid=part.grid, in_specs=part.in_specs, out_specs=part.out_specs,
        scratch_shapes=part.scratch, compiler_params=_params(("arbitrary",) * len(part.grid)), name=part.name)(*part.args)


def _front_body(ins, outs, scratch, ids, *, use_rope):
    x_ref, mod_ref, ng_ref, w_ref, gain_ref = ins[:5]
    qa_ref, ka_ref, qc_ref, kc_ref, va_ref, vc_ref, xb_ref = outs
    x = x_ref[...]
    tm = x.shape[0]
    hb = _normed_input(x, mod_ref[...], ng_ref[...]).astype(BF16)
    p1 = _dot(hb, w_ref[:, :N_NORM])
    lane = lax.broadcasted_iota(jnp.int32, (tm, LANES), 1)
    first_half = (lane & ROPE_FREQS) == 0
    wide = 2 * LANES
    same_head = (lax.broadcasted_iota(jnp.int32, (wide, wide), 0) // HD
                 == lax.broadcasted_iota(jnp.int32, (wide, wide), 1) // HD)
    ones_bd = jnp.where(same_head, 1.0, 0.0).astype(BF16)
    sums = []
    for c0 in range(0, N_NORM, wide):
        w = min(wide, N_NORM - c0)
        sq = p1[:, c0:c0 + w] * p1[:, c0:c0 + w]
        sq_hi = sq.astype(BF16)
        sq_lo = (sq - sq_hi.astype(F32)).astype(BF16)
        sums.append(_dot(sq_hi, ones_bd[:w, :w]) + _dot(sq_lo, ones_bd[:w, :w]))
    dests = ([(qa_ref, c) for c in range(4)] + [(qc_ref, c) for c in range(4)]
             + [(kc_ref, c) for c in range(4)] + [(ka_ref, 0)])
    for c, (o_ref, oc) in enumerate(dests):
        pc = p1[:, c * LANES:(c + 1) * LANES]
        msq = sums[c // 2][:, (c % 2) * LANES:(c % 2 + 1) * LANES] * (1.0 / HD)
        y = pc * lax.rsqrt(msq + EPS) * gain_ref[:, c * LANES:(c + 1) * LANES]
        if use_rope:
            cos_ref, sin_ref = ins[5:7]
            partner = jnp.where(first_half, pltpu.roll(y, LANES - ROPE_FREQS, 1),
                                pltpu.roll(y, ROPE_FREQS, 1))
            y = y * cos_ref[...] + partner * sin_ref[...]
        o_ref[:, oc * LANES:(oc + 1) * LANES] = y.astype(o_ref.dtype)
    p2 = _dot(hb, w_ref[:, N_NORM:])
    va_ref[...] = p2[:, 0:KV_A * HD].astype(va_ref.dtype)
    vc_ref[...] = p2[:, KV_A * HD:KV_A * HD + D_C].astype(vc_ref.dtype)
    xb_ref[...] = p2[:, KV_A * HD + D_C:]


def _front_part(x2d, mod4, ng3, w_front, gains3, rope, *, layer, mod_row0, tokens_per_mod, kv_dtype, tm=TM_FRONT):
    n = x2d.shape[0]
    use_rope = rope is not None
    row = lambda i: (i, 0)
    in_specs = [pl.BlockSpec((tm, D_MODEL), row),
                _mod_spec(layer, mod_row0, tokens_per_mod // tm),
                _layer_spec((1, D_MODEL), layer),
                _layer_spec((D_MODEL, N_NORM + N_PLAIN), layer),
                _layer_spec((1, N_NORM), layer)]
    args = [x2d, mod4, ng3, w_front, gains3]
    if use_rope:
        per_seq = rope[0].shape[0] // tm
        in_specs += [pl.BlockSpec((tm, LANES), lambda i: (i % per_seq, 0))] * 2
        args += list(rope)
    widths = (D_A, KV_A * HD, H_C * 2 * HD, H_C * 2 * HD, KV_A * HD, D_C, D_LRU)
    dtypes = (BF16, kv_dtype, BF16, kv_dtype, kv_dtype, kv_dtype, F32)
    return _Part("front_rope" if use_rope else "front", functools.partial(_front_body, use_rope=use_rope),
                 (n // tm,), in_specs, args,
                 [jax.ShapeDtypeStruct((n, w), d) for w, d in zip(widths, dtypes)],
                 [pl.BlockSpec((tm, w), row) for w in widths])


def _attn_a_kernel(*refs, tq, seq_len, banded, has_ctx, layer):
    if has_ctx:
        sink_ref, q_ref, k_ref, v_ref, kc_ref, vc_ref, o_ref = refs
    else:
        sink_ref, q_ref, k_ref, v_ref, o_ref = refs
    i = pl.program_id(1)
    g = H_A // KV_A
    if banded:
        nwin = tq + 2 * WINDOW
        start = pl.multiple_of(jnp.clip(i * tq - WINDOW, 0, seq_len - nwin), WINDOW)
        kall = k_ref[pl.ds(start, nwin), :].astype(F32)
        vall = v_ref[pl.ds(start, nwin), :].astype(F32)
    else:
        nwin = seq_len
        kall = k_ref[...].astype(F32)
        vall = v_ref[...].astype(F32)
    if has_ctx:
        kall = jnp.concatenate([kall, kc_ref[...].astype(F32)], axis=0)
        vall = jnp.concatenate([vall, vc_ref[...].astype(F32)], axis=0)
    nk = kall.shape[0]
    if banded:
        kpos = start + lax.broadcasted_iota(jnp.int32, (tq, nwin), 1)
        qpos = i * tq + lax.broadcasted_iota(jnp.int32, (tq, nwin), 0)
        bias = jnp.where(jnp.abs(kpos - qpos) <= WINDOW, 0.0, -jnp.inf)
    lo_k = lax.broadcasted_iota(jnp.int32, (nk, LANES), 1) < HD
    lo_q = lax.broadcasted_iota(jnp.int32, (tq, LANES), 1) < HD
    k_sw = pltpu.roll(kall, HD, 1)
    v_sw = pltpu.roll(vall, HD, 1)
    kd = (jnp.where(lo_k, kall, k_sw).astype(BF16), jnp.where(lo_k, k_sw, kall).astype(BF16))
    vd = [[jnp.where(lo_k, vall, 1.0).astype(BF16), jnp.where(lo_k, 1.0, v_sw).astype(BF16)],
          [jnp.where(lo_k, v_sw, 1.0).astype(BF16), jnp.where(lo_k, 1.0, vall).astype(BF16)]]
    q = q_ref[...]
    zero = jnp.zeros((tq, LANES), BF16)
    outs = []
    for h in range(H_A):
        kv, half = h // g, h % 2
        qz = jnp.where(lo_q if half == 0 else jnp.logical_not(lo_q), q[:, (h // 2) * LANES:(h // 2 + 1) * LANES], zero)
        s = _dot_nt(qz, kd[kv])
        if banded:
            s = jnp.concatenate([s[:, :nwin] + bias, s[:, nwin:]], axis=1)
        snk = sink_ref[layer, h] * LOG2E
        m = jnp.maximum(jnp.max(s, axis=-1, keepdims=True), snk)
        e = jnp.exp2(s - m)
        pv = _dot(e.astype(BF16), vd[kv][half])
        outs.append(pv / (pltpu.roll(pv, HD, 1) + jnp.exp2(snk - m)))
    for c in range(H_A // 2):
        pair = jnp.where(lo_q, outs[2 * c], outs[2 * c + 1])
        o_ref[:, c * LANES:(c + 1) * LANES] = pair.astype(o_ref.dtype)


def _attn_a(sink, q, k, v, k_ctx, v_ctx, *, layer, tq, banded):
    b, t, _ = q.shape
    has_ctx = k_ctx is not None
    kv_spec = pl.BlockSpec((None, t, KV_A * HD), lambda bi, i: (bi, 0, 0))
    in_specs = [pl.BlockSpec(memory_space=pltpu.SMEM),
                pl.BlockSpec((None, tq, D_A), lambda bi, i: (bi, i, 0)), kv_spec, kv_spec]
    args = [sink, q, k, v]
    if has_ctx:
        s = k_ctx.shape[2]
        in_specs += [pl.BlockSpec((None, None, s, KV_A * HD), lambda bi, i: (bi, layer, 0, 0))] * 2
        args += [k_ctx, v_ctx]
    return pl.pallas_call(
        functools.partial(_attn_a_kernel, tq=tq, seq_len=t, banded=banded, has_ctx=has_ctx, layer=layer),
        out_shape=jax.ShapeDtypeStruct((b, t, D_A), BF16),
        grid=(b, t // tq),
        in_specs=in_specs,
        out_specs=pl.BlockSpec((None, tq, D_A), lambda bi, i: (bi, i, 0)),
        compiler_params=_params(("arbitrary", "arbitrary")),
        name="attn_a_latent" if has_ctx else "attn_a_context",
    )(*args)


def _attn_c_kernel(*refs, tq, ts, n_loc, has_ctx, lam_init):
    if has_ctx:
        lq1, lk1, lq2, lk2, sub_ref, q_ref, k_ref, v_ref, kc_ref, vc_ref, o_ref = refs
    else:
        lq1, lk1, lq2, lk2, sub_ref, q_ref, k_ref, v_ref, o_ref = refs
    lam = (jnp.exp(jnp.sum(lq1[...] * lk1[...], axis=-1, keepdims=True))
           - jnp.exp(jnp.sum(lq2[...] * lk2[...], axis=-1, keepdims=True)) + lam_init)
    lo = lax.broadcasted_iota(jnp.int32, (tq, LANES), 1) < HD
    zero = jnp.zeros((tq, LANES), BF16)
    heads = [slice(h * LANES, (h + 1) * LANES) for h in range(H_C)]
    qz = []
    for cols in heads:
        q12 = q_ref[:, cols]
        qz.append(jnp.concatenate([jnp.where(lo, q12, zero), jnp.where(lo, zero, q12)], axis=0))

    def update(carry, kt, vt):
        new = []
        for h, cols in enumerate(heads):
            m, l, acc = carry[h]
            s = _dot_nt(qz[h], kt[:, cols].astype(BF16))
            m_new = jnp.maximum(m, jnp.max(s, axis=-1, keepdims=True))
            alpha = jnp.exp2(m - m_new)
            p = jnp.exp2(s - m_new)
            psum = p[:, 0:LANES]
            for j in range(1, p.shape[1] // LANES):
                psum = psum + p[:, j * LANES:(j + 1) * LANES]
            l = alpha * l + psum
            acc = alpha * acc + _dot(p.astype(BF16), vt[:, cols].astype(BF16))
            new.append((m_new, l, acc))
        return tuple(new)

    carry = tuple((jnp.full((2 * tq, 1), -jnp.inf, F32), jnp.zeros((2 * tq, LANES), F32),
                   jnp.zeros((2 * tq, LANES), F32)) for _ in heads)
    for j in range(n_loc):
        carry = update(carry, k_ref[j * ts:(j + 1) * ts, :], v_ref[j * ts:(j + 1) * ts, :])
    if has_ctx:
        carry = update(carry, kc_ref[...], vc_ref[...])
    for h, cols in enumerate(heads):
        _, l, acc = carry[h]
        l = jnp.sum(l, axis=-1, keepdims=True)
        o = acc[:tq] / l[:tq] - lam * (acc[tq:] / l[tq:])
        o = o * lax.rsqrt(jnp.mean(o * o, axis=-1, keepdims=True) + EPS) * sub_ref[...] * (1.0 - lam_init)
        o_ref[:, cols] = o.astype(o_ref.dtype)


def _attn_c(lam_vecs, subln3, q, k, v, k_ctx, v_ctx, *, layer, tq, ts, lam_init):
    b, t, _ = q.shape
    s_loc = k.shape[1]
    has_ctx = k_ctx is not None
    once = dict(pipeline_mode=pl.Buffered(1)) if t // tq > 1 else {}
    kv_spec = pl.BlockSpec((None, s_loc, D_C), lambda bi, i: (bi, 0, 0), **once)
    in_specs = ([_layer_spec((1, HD), layer)] * 4 + [_layer_spec((1, DV_C), layer)]
                + [pl.BlockSpec((None, tq, D_C), lambda bi, i: (bi, i, 0)), kv_spec, kv_spec])
    args = list(lam_vecs) + [subln3, q, k, v]
    if has_ctx:
        sc = k_ctx.shape[2]
        in_specs += [pl.BlockSpec((None, None, sc, D_C), lambda bi, i: (bi, layer, 0, 0), **once)] * 2
        args += [k_ctx, v_ctx]
    return pl.pallas_call(
        functools.partial(_attn_c_kernel, tq=tq, ts=ts, n_loc=s_loc // ts, has_ctx=has_ctx, lam_init=lam_init),
        out_shape=jax.ShapeDtypeStruct((b, t, D_C), BF16),
        grid=(b, t // tq),
        in_specs=in_specs,
        out_specs=pl.BlockSpec((None, tq, D_C), lambda bi, i: (bi, i, 0)),
        compiler_params=_params(("arbitrary", "arbitrary")),
        name="attn_c_latent" if has_ctx else "attn_c_context",
    )(*args)


def _lru_body(ins, outs, scratch, ids, *, tc, nc):
    xc_ref, xp_ref, xn_ref, cw_ref, cb_ref, wg_ref, bg_ref, lam_ref, h0_ref = ins
    ob_ref, st_ref = outs
    a_s, b_s, hs, hf_s, hcar, xb_s = scratch
    _, p, c = ids
    cidx = c + p * (nc - 1 - 2 * c)
    r0 = pl.multiple_of(cidx * tc, tc)

    @pl.when(p == 0)
    def _():
        cur = xc_ref[...]
        prev = jnp.where(cidx > 0, xp_ref[...], 0.0)
        nxt = jnp.where(cidx < nc - 1, xn_ref[...], 0.0)
        ext = jnp.concatenate([prev, cur, nxt], axis=0)
        n_ext = tc + 2 * SUBLANES
        acc = jnp.broadcast_to(cb_ref[...], (tc, D_LRU))
        for j in range(CONV_W):
            off = j - CONV_W // 2
            sh = ext if off == 0 else pltpu.roll(ext, (-off) % n_ext, 0)
            acc = acc + sh[SUBLANES:SUBLANES + tc] * cw_ref[j:j + 1, :]
        xb_s[pl.ds(r0, tc), :] = acc

    xb = xb_s[pl.ds(r0, tc), :]
    gm = _dot(xb.astype(BF16), wg_ref[...]) + bg_ref[...]
    r = _sigmoid(gm[:, :D_LRU])
    ig = _sigmoid(gm[:, D_LRU:])
    nl = -lam_ref[...]
    softplus = jnp.maximum(nl, 0.0) + jnp.log1p(jnp.exp(-jnp.abs(nl)))
    log_a = -LRU_C * r * softplus
    a = jnp.exp(log_a)
    one_m_a2 = -jnp.tanh(log_a) * (a * a + 1.0)
    bb = jnp.where(one_m_a2 == 0.0, 0.0, one_m_a2 * lax.rsqrt(one_m_a2)) * (ig * xb)

    ng = tc // SUBLANES
    a3 = a.reshape(ng, SUBLANES, D_LRU)
    b3 = bb.reshape(ng, SUBLANES, D_LRU)
    row = lax.broadcasted_iota(jnp.int32, (ng, SUBLANES, D_LRU), 1)
    order = row + p * (SUBLANES - 1 - 2 * row)
    for sh in (1, 2, 4):
        shift = sh + p * (SUBLANES - 2 * sh)
        a_prev = pltpu.roll(a3, shift, 1)
        b_prev = pltpu.roll(b3, shift, 1)
        valid = order >= sh
        b3 = jnp.where(valid, a3 * b_prev + b3, b3)
        a3 = jnp.where(valid, a3 * a_prev, a3)
    a_tot = jnp.where(p == 0, a3[:, SUBLANES - 1:SUBLANES, :], a3[:, 0:1, :])
    b_tot = jnp.where(p == 0, b3[:, SUBLANES - 1:SUBLANES, :], b3[:, 0:1, :])
    a_s[...] = jnp.broadcast_to(a_tot, a3.shape).reshape(tc, D_LRU)
    b_s[...] = jnp.broadcast_to(b_tot, b3.shape).reshape(tc, D_LRU)

    @pl.when(c == 0)
    def _():
        hcar[...] = jnp.broadcast_to(h0_ref[pl.ds(p, 1), :], (SUBLANES, D_LRU))

    def group(gi, h):
        gidx = gi + p * (ng - 1 - 2 * gi)
        base = pl.multiple_of(gidx * SUBLANES, SUBLANES)
        hs[pl.ds(base, SUBLANES), :] = h
        return a_s[pl.ds(base, SUBLANES), :] * h + b_s[pl.ds(base, SUBLANES), :]

    h = lax.fori_loop(0, ng, group, hcar[...], unroll=8)
    hcar[...] = h
    hloc = (a3 * hs[...].reshape(ng, SUBLANES, D_LRU) + b3).reshape(tc, D_LRU)

    @pl.when(p == 0)
    def _():
        hf_s[pl.ds(r0, tc), :] = hloc

    @pl.when(p == 1)
    def _():
        ob_ref[...] = (hf_s[pl.ds(r0, tc), :] + hloc).astype(ob_ref.dtype)

    @pl.when(c == nc - 1)
    def _():
        st_ref[pl.ds(p, 1), :] = h[0:1, :]


def _lru_part(xb, conv_w, conv_b3, wg, bg, lam4, h0, *, layer, h0_layer, tc):
    b, t, _ = xb.shape
    nc = t // tc
    nb8 = t // SUBLANES
    per = tc // SUBLANES
    cidx = lambda p, c: c + p * (nc - 1 - 2 * c)
    in_specs = [
        pl.BlockSpec((None, tc, D_LRU), lambda bi, p, c: (bi, cidx(p, c), 0)),
        pl.BlockSpec((None, SUBLANES, D_LRU), lambda bi, p, c: (bi, jnp.maximum(cidx(p, c) * per - 1, 0), 0)),
        pl.BlockSpec((None, SUBLANES, D_LRU), lambda bi, p, c: (bi, jnp.minimum((cidx(p, c) + 1) * per, nb8 - 1), 0)),
        _layer_spec((CONV_W, D_LRU), layer),
        _layer_spec((1, D_LRU), layer),
        pl.BlockSpec((None, None, D_LRU, 2 * D_LRU), lambda bi, p, c: (layer, p, 0, 0)),
        pl.BlockSpec((None, None, 1, 2 * D_LRU), lambda bi, p, c: (layer, p, 0, 0)),
        pl.BlockSpec((None, None, 1, D_LRU), lambda bi, p, c: (layer, p, 0, 0)),
        pl.BlockSpec((None, None, 2, D_LRU), lambda bi, p, c: (bi, h0_layer, 0, 0)),
    ]
    return _Part("lru", functools.partial(_lru_body, tc=tc, nc=nc), (b, 2, nc), in_specs,
                 [xb, xb, xb, conv_w, conv_b3, wg, bg, lam4, h0],
                 [jax.ShapeDtypeStruct((b, t, D_LRU), BF16), jax.ShapeDtypeStruct((b, 2, D_LRU), F32)],
                 [pl.BlockSpec((None, tc, D_LRU), lambda bi, p, c: (bi, nc - 1 - p * c, 0)),
                  pl.BlockSpec((None, 2, D_LRU), lambda bi, p, c: (bi, 0, 0))],
                 [pltpu.VMEM((tc, D_LRU), F32), pltpu.VMEM((tc, D_LRU), F32), pltpu.VMEM((tc, D_LRU), F32),
                  pltpu.VMEM((t, D_LRU), F32), pltpu.VMEM((SUBLANES, D_LRU), F32), pltpu.VMEM((t, D_LRU), F32)])


def _tail_kernel(x_ref, mod_ref, ng_ref, oa_ref, ob_ref, oc_ref, wg_ref, wm_ref, wbr_ref, wout_ref, o_ref):
    x = x_ref[...]
    mod = mod_ref[...]
    hb = _normed_input(x, mod, ng_ref[...]).astype(BF16)
    y = None
    for br, br_ref in enumerate((oa_ref, ob_ref, oc_ref)):
        gt = _dot(hb, wg_ref[:, br * D_A:(br + 1) * D_A])
        u = (br_ref[...].astype(F32) * (gt * _sigmoid(gt))).astype(BF16)
        proj = _dot(u, wbr_ref[br])
        mg = _sigmoid(_dot(hb, wm_ref[:, br * D_MODEL:(br + 1) * D_MODEL]))
        y = mg * proj if y is None else y + mg * proj
    gate = mod[:, 2 * D_MODEL:]
    o_ref[...] = x + gate * _dot(y.astype(BF16), wout_ref[...])


def _tail_part(x2d, mod4, ng3, oa, ob, oc, wg, wm, wbr, wout, *, layer, mod_row0, tokens_per_mod, tm=TM_TAIL):
    n = x2d.shape[0]
    row = lambda i: (i, 0)
    once = dict(pipeline_mode=pl.Buffered(1))

    def weight(shape):
        zeros = (0,) * len(shape)
        return pl.BlockSpec((None,) + shape, lambda i: (layer,) + zeros, **once)

    in_specs = [pl.BlockSpec((tm, D_MODEL), row),
                _mod_spec(layer, mod_row0, tokens_per_mod // tm),
                _layer_spec((1, D_MODEL), layer),
                pl.BlockSpec((tm, D_A), row),
                pl.BlockSpec((tm, D_LRU), row),
                pl.BlockSpec((tm, D_C), row),
                weight((D_MODEL, 3 * D_A)),
                weight((D_MODEL, 3 * D_MODEL)),
                weight((3, D_A, D_MODEL)),
                weight((D_MODEL, D_MODEL))]
    return _Part("tail", lambda ins, outs, scratch, ids: _tail_kernel(*ins, *outs), (n // tm,), in_specs,
                 [x2d, mod4, ng3, oa, ob, oc, wg, wm, wbr, wout],
                 [jax.ShapeDtypeStruct((n, D_MODEL), F32)], [pl.BlockSpec((tm, D_MODEL), row)])


def _assemble_kernel(*refs, n_arrays):
    ins, outs = refs[:n_arrays * DEPTH], refs[n_arrays * DEPTH:]
    for a in range(n_arrays):
        for l in range(DEPTH):
            x = ins[a * DEPTH + l][...]
            outs[a][l] = x.T if a < n_arrays - 1 else x


def _assemble_caches(per_layer, batch):
    n_arrays = len(per_layer[0])
    seq = per_layer[0][0].shape[0] // batch
    args = [per_layer[l][a] for a in range(n_arrays) for l in range(DEPTH)]
    widths = [per_layer[0][a].shape[1] for a in range(n_arrays)]
    in_specs = [pl.BlockSpec((seq, widths[a]), lambda b: (b, 0)) for a in range(n_arrays) for _ in range(DEPTH)]
    shapes = [(DEPTH, w, seq) for w in widths[:-1]] + [(DEPTH, seq, widths[-1])]
    return pl.pallas_call(
        functools.partial(_assemble_kernel, n_arrays=n_arrays),
        out_shape=[jax.ShapeDtypeStruct((batch,) + s, F32) for s in shapes],
        grid=(batch,),
        in_specs=in_specs,
        out_specs=[pl.BlockSpec((None,) + s, lambda b: (b, 0, 0, 0)) for s in shapes],
        compiler_params=_params(("arbitrary",)),
        name="assemble_caches",
    )(*args)


def _rope_tables(seq_len):
    pos = jnp.arange(seq_len)
    row = (pos // GRID_W).astype(F32)
    col = (pos % GRID_W).astype(F32)
    inv = jnp.power(ROPE_BASE, -jnp.arange(ROPE_FREQS, dtype=F32) / ROPE_FREQS)
    ang_r = row[:, None] * inv
    ang_c = col[:, None] * inv
    cos = jnp.concatenate([jnp.cos(ang_r)] * 2 + [jnp.cos(ang_c)] * 2, axis=-1)
    sin = jnp.concatenate([-jnp.sin(ang_r), jnp.sin(ang_r), -jnp.sin(ang_c), jnp.sin(ang_c)], axis=-1)
    return jnp.tile(cos, (1, LANES // HD)), jnp.tile(sin, (1, LANES // HD))


def _block_diag(w):
    rows = w.reshape(w.shape[:-3] + (D_LRU, LRU_BW))
    tiled = jnp.tile(rows, (1,) * (rows.ndim - 1) + (LRU_BLOCKS,))
    blk = jnp.arange(D_LRU) // LRU_BW
    return jnp.where(blk[:, None] == blk[None, :], tiled, 0.0)


def kernel(x_prompt, x_sample, cache_a_k, cache_a_v, cache_c_k, cache_c_v, state_lru, c, c_ctx, norm_g, mod_w, mod_b, w_in, qn_a, kn_a, sink_a, conv_w, conv_b, lru_wa, lru_ba, lru_wx, lru_bx, lru_lam, qn_c, kn_c, lam_q1, lam_k1, lam_q2, lam_k2, subln_c, w_br_a, w_br_b, w_br_c, w_out):
    bp, sp, _ = x_prompt.shape
    bs, ss, _ = x_sample.shape
    past = cache_a_k.shape[2]

    cvecs = jnp.concatenate([c, c_ctx[None, :], jnp.zeros((SUBLANES - bs - 1, D_MODEL), F32)], axis=0)
    mod = _modulation(cvecs, mod_w, mod_b)
    mod4 = mod.reshape(DEPTH, SUBLANES, 1, 3 * D_MODEL)
    rope = _rope_tables(ss)

    w_front = _permute_cast(w_in, FRONT_BLOCKS)
    wg = _permute_cast(w_in, GATE_BLOCKS)
    wm = _permute_cast(w_in, MERGE_BLOCKS)
    wbr = jnp.stack([w_br_a, w_br_b, w_br_c], axis=1).astype(BF16)
    wout = w_out.astype(BF16)
    gains3 = jnp.concatenate([jnp.tile(qn_a * Q_SCALE, (1, H_A)), jnp.tile(qn_c * Q_SCALE, (1, 2 * H_C)),
                              jnp.tile(kn_c, (1, 2 * H_C)), jnp.tile(kn_a, (1, KV_A))],
                             axis=-1)[:, None, :]
    ng3 = norm_g[:, None, :]
    lru_wg = jnp.concatenate([_block_diag(lru_wa), _block_diag(lru_wx)], axis=-1).astype(BF16)
    lru_bg = jnp.concatenate([lru_ba, lru_bx], axis=-1)[:, :, None, :]
    lru_lam4 = lru_lam[:, :, None, :]
    lru_p = (conv_w, conv_b[:, None, :], lru_wg, lru_bg, lru_lam4)
    lam_vecs = [v[:, None, :] for v in (lam_q1, lam_k1, lam_q2, lam_k2)]
    subln3 = subln_c[:, None, :]
    cka = cache_a_k.reshape(bs, DEPTH, past, KV_A * HD)
    cva = cache_a_v.reshape(bs, DEPTH, past, KV_A * HD)
    ckc = cache_c_k.reshape(bs, DEPTH, past, D_C)
    cvc = cache_c_v.reshape(bs, DEPTH, past, D_C)
    zeros_h0 = jnp.zeros((bp, 1, 2, D_LRU), F32)

    xp = x_prompt.reshape(bp * sp, D_MODEL)
    xs = x_sample.reshape(bs * ss, D_MODEL)
    new_caches, new_states = [], []
    for l in range(DEPTH):
        lam_init = 0.8 - 0.6 * math.exp(-0.3 * l)
        ctx_mod = dict(layer=l, mod_row0=bs, tokens_per_mod=bp * sp)
        lat_mod = dict(layer=l, mod_row0=0, tokens_per_mod=ss)

        flat = lambda a: a.reshape(-1, a.shape[-1])
        r3 = lambda a: a.reshape(bp, sp, a.shape[-1])
        qa, ka, qc, kc, va, vc, xb = _run(_front_part(xp, mod4, ng3, w_front, gains3, None, kv_dtype=F32, **ctx_mod))
        new_caches.append((ka, va, kc, vc))
        oa = _attn_a(sink_a, r3(qa), r3(ka), r3(va), None, None, layer=l, tq=sp, banded=False)
        ob, st = _run(_lru_part(r3(xb), *lru_p, zeros_h0, layer=l, h0_layer=0, tc=sp))
        new_states.append(st)
        oc = _attn_c(lam_vecs, subln3, r3(qc), r3(kc), r3(vc), None, None, layer=l, tq=sp, ts=sp, lam_init=lam_init)
        (xp,) = _run(_tail_part(xp, mod4, ng3, flat(oa), flat(ob), flat(oc), wg, wm, wbr, wout, **ctx_mod))

        r3 = lambda a: a.reshape(bs, ss, a.shape[-1])
        qa, ka, qc, kc, va, vc, xb = _run(_front_part(xs, mod4, ng3, w_front, gains3, rope, kv_dtype=BF16, **lat_mod))
        oa = _attn_a(sink_a, r3(qa), r3(ka), r3(va), cka, cva, layer=l, tq=TQ_A, banded=True)
        ob, _ = _run(_lru_part(r3(xb), *lru_p, state_lru, layer=l, h0_layer=l, tc=TC_LRU))
        oc = _attn_c(lam_vecs, subln3, r3(qc), r3(kc), r3(vc), ckc, cvc, layer=l, tq=TQ_C, ts=TS_C, lam_init=lam_init)
        (xs,) = _run(_tail_part(xs, mod4, ng3, flat(oa), flat(ob), flat(oc), wg, wm, wbr, wout, **lat_mod))

    ka_t, va_t, kc_t, vc = _assemble_caches(new_caches, bp)
    return (xp.reshape(bp, sp, D_MODEL), xs.reshape(bs, ss, D_MODEL),
            ka_t.reshape(bp, DEPTH, KV_A, HD, sp).transpose(0, 1, 4, 2, 3),
            va_t.reshape(bp, DEPTH, KV_A, HD, sp).transpose(0, 1, 4, 2, 3),
            kc_t.reshape(bp, DEPTH, H_C, 2, HD, sp).transpose(0, 1, 5, 2, 3, 4),
            vc.reshape(bp, DEPTH, sp, H_C, DV_C),
            jnp.stack(new_states, axis=1))
```

```python
import functools
import math

import jax
import jax.numpy as jnp
from jax import lax
from jax.experimental import pallas as pl
from jax.experimental.pallas import tpu as pltpu

F32 = jnp.float32
BF16 = jnp.bfloat16

D_MODEL = 1024
DEPTH = 2
GRID_W = 64
HD = 64
SCALE = 1.0 / math.sqrt(HD)
LOG2E = math.log2(math.e)
Q_SCALE = SCALE * LOG2E
H_A = 8
KV_A = 2
WINDOW = 128
D_A = H_A * HD
D_LRU = 512
LRU_BLOCKS = 8
LRU_BW = D_LRU // LRU_BLOCKS
CONV_W = 4
LRU_C = 8.0
H_C = 4
DV_C = 2 * HD
D_C = H_C * DV_C
ROPE_BASE = 10000.0
ROPE_FREQS = HD // 4
EPS = 1e-6
LANES = 128
SUBLANES = 8
VMEM_LIMIT = 56 * 1024 * 1024

_SECTIONS = (("qa", D_A), ("kava", 2 * KV_A * HD), ("ga", D_A), ("xb", D_LRU), ("gb", D_LRU),
             ("qc", H_C * 2 * HD), ("kc", H_C * 2 * HD), ("vc", D_C), ("gc", D_C), ("mg", 3 * D_MODEL))
W_BLOCK = 256


def _col_blocks(*names):
    start, spans = 0, {}
    for name, width in _SECTIONS:
        spans[name] = (start, start + width)
        start += width
    blocks = []
    for name in names:
        a, b = spans[name]
        assert a % W_BLOCK == 0 and b % W_BLOCK == 0
        blocks += range(a // W_BLOCK, b // W_BLOCK)
    return tuple(blocks)


FRONT_BLOCKS = _col_blocks("qa", "qc", "kc", "kava", "vc", "xb")
GATE_BLOCKS = _col_blocks("ga", "gb", "gc")
MERGE_BLOCKS = _col_blocks("mg")
N_NORM = D_A + KV_A * HD + 2 * H_C * 2 * HD
N_PLAIN = KV_A * HD + D_C + D_LRU

TM_FRONT = 512
TM_TAIL = 512
TQ_A = 2 * WINDOW
TQ_C = 512
TS_C = 512
TC_LRU = 512


def _params(sem, vmem=VMEM_LIMIT):
    return pltpu.CompilerParams(dimension_semantics=sem, vmem_limit_bytes=vmem)


def _dot(a, b):
    return jnp.dot(a, b, preferred_element_type=F32)


def _dot_nt(a, b):
    return lax.dot_general(a, b, (((1,), (1,)), ((), ())), preferred_element_type=F32)


def _sigmoid(x):
    return 1.0 / (1.0 + jnp.exp2(x * (-LOG2E)))


def _permute_cast_kernel(perm_ref, w_ref, o_ref):
    o_ref[...] = w_ref[...].astype(o_ref.dtype)


def _permute_cast(w, blocks):
    nl, k, _ = w.shape
    grid_spec = pltpu.PrefetchScalarGridSpec(
        num_scalar_prefetch=1, grid=(len(blocks),),
        in_specs=[pl.BlockSpec((nl, k, W_BLOCK), lambda j, perm: (0, 0, perm[j]))],
        out_specs=pl.BlockSpec((nl, k, W_BLOCK), lambda j, perm: (0, 0, j)))
    return pl.pallas_call(
        _permute_cast_kernel, grid_spec=grid_spec,
        out_shape=jax.ShapeDtypeStruct((nl, k, len(blocks) * W_BLOCK), BF16),
        compiler_params=_params(("arbitrary",)), name="permute_cast",
    )(jnp.asarray(blocks, jnp.int32), w)


def _mod_kernel(c_ref, w_ref, b_ref, o_ref):
    c = c_ref[...]
    a = c * _sigmoid(c)
    w = w_ref[...]
    a_hi = a.astype(BF16)
    a_lo = (a - a_hi.astype(F32)).astype(BF16)
    w_hi = w.astype(BF16)
    w_lo = (w - w_hi.astype(F32)).astype(BF16)
    o_ref[...] = _dot(a_hi, w_hi) + _dot(a_hi, w_lo) + _dot(a_lo, w_hi) + b_ref[...]


def _modulation(cvecs, mod_w, mod_b):
    tn = 768
    return pl.pallas_call(
        _mod_kernel,
        out_shape=jax.ShapeDtypeStruct((DEPTH, SUBLANES, 3 * D_MODEL), F32),
        grid=(DEPTH, 3 * D_MODEL // tn),
        in_specs=[pl.BlockSpec((SUBLANES, D_MODEL), lambda l, j: (0, 0)),
                  pl.BlockSpec((None, D_MODEL, tn), lambda l, j: (l, 0, j)),
                  pl.BlockSpec((None, 1, tn), lambda l, j: (l, 0, j))],
        out_specs=pl.BlockSpec((None, SUBLANES, tn), lambda l, j: (l, 0, j)),
        compiler_params=_params(("arbitrary", "arbitrary")),
        name="modulation",
    )(cvecs, mod_w, mod_b.reshape(DEPTH, 1, 3 * D_MODEL))


def _normed_input(x, mod, ng):
    ms = jnp.mean(x * x, axis=-1, keepdims=True)
    shift = mod[:, 0:D_MODEL]
    scale = mod[:, D_MODEL:2 * D_MODEL]
    return (x * lax.rsqrt(ms + EPS) * ng) * (1.0 + scale) + shift


def _layer_spec(shape, layer):
    zeros = (0,) * len(shape)
    return pl.BlockSpec((None,) + tuple(shape), lambda *_: (layer,) + zeros)


def _mod_spec(layer, row0, tiles_per_row):
    return pl.BlockSpec((None, None, 1, 3 * D_MODEL), lambda i: (layer, row0 + i // tiles_per_row, 0, 0))


class _Part:
    def __init__(self, name, body, grid, in_specs, args, out_shape, out_specs, scratch=()):
        self.name, self.body, self.grid = name, body, tuple(grid)
        self.in_specs, self.args = list(in_specs), list(args)
        self.out_shape, self.out_specs, self.scratch = list(out_shape), list(out_specs), list(scratch)


def _run(part):
    ni, no = len(part.args), len(part.out_shape)

    def kern(*refs):
        ids = tuple(pl.program_id(k) for k in range(len(part.grid)))
        part.body(refs[:ni], refs[ni:ni + no], refs[ni + no:], ids)

    return pl.pallas_call(
        kern, out_shape=part.out_shape, grid=part.grid, in_specs=part.in_specs, out_specs=part.out_specs,
        scratch_shapes=part.scratch, compiler_params=_params(("arbitrary",) * len(part.grid)), name=part.name)(*part.args)


def _front_body(ins, outs, scratch, ids, *, use_rope):
    x_ref, mod_ref, ng_ref, w_ref, gain_ref = ins[:5]
    qa_ref, ka_ref, qc_ref, kc_ref, va_ref, vc_ref, xb_ref = outs
    x = x_ref[...]
    tm = x.shape[0]
    hb = _normed_input(x, mod_ref[...], ng_ref[...]).astype(BF16)
    p1 = _dot(hb, w_ref[:, :N_NORM])
    lane = lax.broadcasted_iota(jnp.int32, (tm, LANES), 1)
    first_half = (lane & ROPE_FREQS) == 0
    wide = 2 * LANES
    same_head = (lax.broadcasted_iota(jnp.int32, (wide, wide), 0) // HD
                 == lax.broadcasted_iota(jnp.int32, (wide, wide), 1) // HD)
    ones_bd = jnp.where(same_head, 1.0, 0.0).astype(BF16)
    sums = []
    for c0 in range(0, N_NORM, wide):
        w = min(wide, N_NORM - c0)
        sq = p1[:, c0:c0 + w] * p1[:, c0:c0 + w]
        sq_hi = sq.astype(BF16)
        sq_lo = (sq - sq_hi.astype(F32)).astype(BF16)
        sums.append(_dot(sq_hi, ones_bd[:w, :w]) + _dot(sq_lo, ones_bd[:w, :w]))
    dests = ([(qa_ref, c) for c in range(4)] + [(qc_ref, c) for c in range(4)]
             + [(kc_ref, c) for c in range(4)] + [(ka_ref, 0)])
    for c, (o_ref, oc) in enumerate(dests):
        pc = p1[:, c * LANES:(c + 1) * LANES]
        msq = sums[c // 2][:, (c % 2) * LANES:(c % 2 + 1) * LANES] * (1.0 / HD)
        y = pc * lax.rsqrt(msq + EPS) * gain_ref[:, c * LANES:(c + 1) * LANES]
        if use_rope:
            cos_ref, sin_ref = ins[5:7]
            partner = jnp.where(first_half, pltpu.roll(y, LANES - ROPE_FREQS, 1),
                                pltpu.roll(y, ROPE_FREQS, 1))
            y = y * cos_ref[...] + partner * sin_ref[...]
        o_ref[:, oc * LANES:(oc + 1) * LANES] = y.astype(o_ref.dtype)
    p2 = _dot(hb, w_ref[:, N_NORM:])
    va_ref[...] = p2[:, 0:KV_A * HD].astype(va_ref.dtype)
    vc_ref[...] = p2[:, KV_A * HD:KV_A * HD + D_C].astype(vc_ref.dtype)
    xb_ref[...] = p2[:, KV_A * HD + D_C:]


def _front_part(x2d, mod4, ng3, w_front, gains3, rope, *, layer, mod_row0, tokens_per_mod, kv_dtype, tm=TM_FRONT):
    n = x2d.shape[0]
    use_rope = rope is not None
    row = lambda i: (i, 0)
    in_specs = [pl.BlockSpec((tm, D_MODEL), row),
                _mod_spec(layer, mod_row0, tokens_per_mod // tm),
                _layer_spec((1, D_MODEL), layer),
                _layer_spec((D_MODEL, N_NORM + N_PLAIN), layer),
                _layer_spec((1, N_NORM), layer)]
    args = [x2d, mod4, ng3, w_front, gains3]
    if use_rope:
        per_seq = rope[0].shape[0] // tm
        in_specs += [pl.BlockSpec((tm, LANES), lambda i: (i % per_seq, 0))] * 2
        args += list(rope)
    widths = (D_A, KV_A * HD, H_C * 2 * HD, H_C * 2 * HD, KV_A * HD, D_C, D_LRU)
    dtypes = (BF16, kv_dtype, BF16, kv_dtype, kv_dtype, kv_dtype, F32)
    return _Part("front_rope" if use_rope else "front", functools.partial(_front_body, use_rope=use_rope),
                 (n // tm,), in_specs, args,
                 [jax.ShapeDtypeStruct((n, w), d) for w, d in zip(widths, dtypes)],
                 [pl.BlockSpec((tm, w), row) for w in widths])


def _attn_a_kernel(*refs, tq, seq_len, banded, has_ctx, layer):
    if has_ctx:
        sink_ref, q_ref, k_ref, v_ref, kc_ref, vc_ref, o_ref = refs
    else:
        sink_ref, q_ref, k_ref, v_ref, o_ref = refs
    i = pl.program_id(1)
    g = H_A // KV_A
    if banded:
        nwin = tq + 2 * WINDOW
        start = pl.multiple_of(jnp.clip(i * tq - WINDOW, 0, seq_len - nwin), WINDOW)
        kall = k_ref[pl.ds(start, nwin), :].astype(F32)
        vall = v_ref[pl.ds(start, nwin), :].astype(F32)
    else:
        nwin = seq_len
        kall = k_ref[...].astype(F32)
        vall = v_ref[...].astype(F32)
    if has_ctx:
        kall = jnp.concatenate([kall, kc_ref[...].astype(F32)], axis=0)
        vall = jnp.concatenate([vall, vc_ref[...].astype(F32)], axis=0)
    nk = kall.shape[0]
    if banded:
        kpos = start + lax.broadcasted_iota(jnp.int32, (tq, nwin), 1)
        qpos = i * tq + lax.broadcasted_iota(jnp.int32, (tq, nwin), 0)
        bias = jnp.where(jnp.abs(kpos - qpos) <= WINDOW, 0.0, -jnp.inf)
    lo_k = lax.broadcasted_iota(jnp.int32, (nk, LANES), 1) < HD
    lo_q = lax.broadcasted_iota(jnp.int32, (tq, LANES), 1) < HD
    k_sw = pltpu.roll(kall, HD, 1)
    v_sw = pltpu.roll(vall, HD, 1)
    kd = (jnp.where(lo_k, kall, k_sw).astype(BF16), jnp.where(lo_k, k_sw, kall).astype(BF16))
    vd = [[jnp.where(lo_k, vall, 1.0).astype(BF16), jnp.where(lo_k, 1.0, v_sw).astype(BF16)],
          [jnp.where(lo_k, v_sw, 1.0).astype(BF16), jnp.where(lo_k, 1.0, vall).astype(BF16)]]
    q = q_ref[...]
    zero = jnp.zeros((tq, LANES), BF16)
    outs = []
    for h in range(H_A):
        kv, half = h // g, h % 2
        qz = jnp.where(lo_q if half == 0 else jnp.logical_not(lo_q), q[:, (h // 2) * LANES:(h // 2 + 1) * LANES], zero)
        s = _dot_nt(qz, kd[kv])
        if banded:
            s = jnp.concatenate([s[:, :nwin] + bias, s[:, nwin:]], axis=1)
        snk = sink_ref[layer, h] * LOG2E
        m = jnp.maximum(jnp.max(s, axis=-1, keepdims=True), snk)
        e = jnp.exp2(s - m)
        pv = _dot(e.astype(BF16), vd[kv][half])
        outs.append(pv / (pltpu.roll(pv, HD, 1) + jnp.exp2(snk - m)))
    for c in range(H_A // 2):
        pair = jnp.where(lo_q, outs[2 * c], outs[2 * c + 1])
        o_ref[:, c * LANES:(c + 1) * LANES] = pair.astype(o_ref.dtype)


def _attn_a(sink, q, k, v, k_ctx, v_ctx, *, layer, tq, banded):
    b, t, _ = q.shape
    has_ctx = k_ctx is not None
    kv_spec = pl.BlockSpec((None, t, KV_A * HD), lambda bi, i: (bi, 0, 0))
    in_specs = [pl.BlockSpec(memory_space=pltpu.SMEM),
                pl.BlockSpec((None, tq, D_A), lambda bi, i: (bi, i, 0)), kv_spec, kv_spec]
    args = [sink, q, k, v]
    if has_ctx:
        s = k_ctx.shape[2]
        in_specs += [pl.BlockSpec((None, None, s, KV_A * HD), lambda bi, i: (bi, layer, 0, 0))] * 2
        args += [k_ctx, v_ctx]
    return pl.pallas_call(
        functools.partial(_attn_a_kernel, tq=tq, seq_len=t, banded=banded, has_ctx=has_ctx, layer=layer),
        out_shape=jax.ShapeDtypeStruct((b, t, D_A), BF16),
        grid=(b, t // tq),
        in_specs=in_specs,
        out_specs=pl.BlockSpec((None, tq, D_A), lambda bi, i: (bi, i, 0)),
        compiler_params=_params(("arbitrary", "arbitrary")),
        name="attn_a_latent" if has_ctx else "attn_a_context",
    )(*args)


def _attn_c_kernel(*refs, tq, ts, n_loc, has_ctx, lam_init):
    if has_ctx:
        lq1, lk1, lq2, lk2, sub_ref, q_ref, k_ref, v_ref, kc_ref, vc_ref, o_ref = refs
    else:
        lq1, lk1, lq2, lk2, sub_ref, q_ref, k_ref, v_ref, o_ref = refs
    lam = (jnp.exp(jnp.sum(lq1[...] * lk1[...], axis=-1, keepdims=True))
           - jnp.exp(jnp.sum(lq2[...] * lk2[...], axis=-1, keepdims=True)) + lam_init)
    lo = lax.broadcasted_iota(jnp.int32, (tq, LANES), 1) < HD
    zero = jnp.zeros((tq, LANES), BF16)
    heads = [slice(h * LANES, (h + 1) * LANES) for h in range(H_C)]
    qz = []
    for cols in heads:
        q12 = q_ref[:, cols]
        qz.append(jnp.concatenate([jnp.where(lo, q12, zero), jnp.where(lo, zero, q12)], axis=0))

    def update(carry, kt, vt):
        new = []
        for h, cols in enumerate(heads):
            m, l, acc = carry[h]
            s = _dot_nt(qz[h], kt[:, cols].astype(BF16))
            m_new = jnp.maximum(m, jnp.max(s, axis=-1, keepdims=True))
            alpha = jnp.exp2(m - m_new)
            p = jnp.exp2(s - m_new)
            psum = p[:, 0:LANES]
            for j in range(1, p.shape[1] // LANES):
                psum = psum + p[:, j * LANES:(j + 1) * LANES]
            l = alpha * l + psum
            acc = alpha * acc + _dot(p.astype(BF16), vt[:, cols].astype(BF16))
            new.append((m_new, l, acc))
        return tuple(new)

    carry = tuple((jnp.full((2 * tq, 1), -jnp.inf, F32), jnp.zeros((2 * tq, LANES), F32),
                   jnp.zeros((2 * tq, LANES), F32)) for _ in heads)
    for j in range(n_loc):
        carry = update(carry, k_ref[j * ts:(j + 1) * ts, :], v_ref[j * ts:(j + 1) * ts, :])
    if has_ctx:
        carry = update(carry, kc_ref[...], vc_ref[...])
    for h, cols in enumerate(heads):
        _, l, acc = carry[h]
        l = jnp.sum(l, axis=-1, keepdims=True)
        o = acc[:tq] / l[:tq] - lam * (acc[tq:] / l[tq:])
        o = o * lax.rsqrt(jnp.mean(o * o, axis=-1, keepdims=True) + EPS) * sub_ref[...] * (1.0 - lam_init)
        o_ref[:, cols] = o.astype(o_ref.dtype)


def _attn_c(lam_vecs, subln3, q, k, v, k_ctx, v_ctx, *, layer, tq, ts, lam_init):
    b, t, _ = q.shape
    s_loc = k.shape[1]
    has_ctx = k_ctx is not None
    once = dict(pipeline_mode=pl.Buffered(1)) if t // tq > 1 else {}
    kv_spec = pl.BlockSpec((None, s_loc, D_C), lambda bi, i: (bi, 0, 0), **once)
    in_specs = ([_layer_spec((1, HD), layer)] * 4 + [_layer_spec((1, DV_C), layer)]
                + [pl.BlockSpec((None, tq, D_C), lambda bi, i: (bi, i, 0)), kv_spec, kv_spec])
    args = list(lam_vecs) + [subln3, q, k, v]
    if has_ctx:
        sc = k_ctx.shape[2]
        in_specs += [pl.BlockSpec((None, None, sc, D_C), lambda bi, i: (bi, layer, 0, 0), **once)] * 2
        args += [k_ctx, v_ctx]
    return pl.pallas_call(
        functools.partial(_attn_c_kernel, tq=tq, ts=ts, n_loc=s_loc // ts, has_ctx=has_ctx, lam_init=lam_init),
        out_shape=jax.ShapeDtypeStruct((b, t, D_C), BF16),
        grid=(b, t // tq),
        in_specs=in_specs,
        out_specs=pl.BlockSpec((None, tq, D_C), lambda bi, i: (bi, i, 0)),
        compiler_params=_params(("arbitrary", "arbitrary")),
        name="attn_c_latent" if has_ctx else "attn_c_context",
    )(*args)


def _lru_body(ins, outs, scratch, ids, *, tc, nc):
    xc_ref, xp_ref, xn_ref, cw_ref, cb_ref, wg_ref, bg_ref, lam_ref, h0_ref, perm_ref, inv_ref = ins
    ob_ref, st_ref = outs
    a_s, b_s, hs, ps, hf_s, xb_s, hcar, last_h, last_p, cin = scratch
    _, p, c = ids
    cidx = c + p * (nc - 1 - 2 * c)
    r0 = pl.multiple_of(cidx * tc, tc)
    ng = tc // SUBLANES

    @pl.when(p == 0)
    def _():
        cur = xc_ref[...]
        prev = jnp.where(cidx > 0, xp_ref[...], 0.0)
        nxt = jnp.where(cidx < nc - 1, xn_ref[...], 0.0)
        hi = cur.astype(BF16)
        rest = cur - hi.astype(F32)
        mid = rest.astype(BF16)
        lo = (rest - mid.astype(F32)).astype(BF16)
        x3 = (_dot(perm_ref[...], hi) + _dot(perm_ref[...], mid) + _dot(perm_ref[...], lo)).reshape(ng, SUBLANES, D_LRU)
        rows = lax.broadcasted_iota(jnp.int32, (SUBLANES, D_LRU), 0)
        before1 = jnp.where(rows == 0, prev[SUBLANES - 1:SUBLANES, :], pltpu.roll(x3[ng - 1], 1, 0))
        before2 = jnp.where(rows == 0, prev[SUBLANES - 2:SUBLANES - 1, :], pltpu.roll(x3[ng - 2], 1, 0))
        after1 = jnp.where(rows == SUBLANES - 1, nxt[0:1, :], pltpu.roll(x3[0], SUBLANES - 1, 0))
        xpad = jnp.concatenate([before2[None], before1[None], x3, after1[None]], axis=0)
        acc = jnp.broadcast_to(cb_ref[...], (ng, SUBLANES, D_LRU))
        for j in range(CONV_W):
            acc = acc + xpad[j:j + ng] * cw_ref[j:j + 1, :]
        xb_s[pl.ds(r0, tc), :] = acc.reshape(tc, D_LRU)

    xb = xb_s[pl.ds(r0, tc), :]
    gm = _dot(xb.astype(BF16), wg_ref[...]) + bg_ref[...]
    r = _sigmoid(gm[:, :D_LRU])
    ig = _sigmoid(gm[:, D_LRU:])
    nl = -lam_ref[...]
    softplus = jnp.maximum(nl, 0.0) + jnp.log1p(jnp.exp(-jnp.abs(nl)))
    decay = LRU_C * softplus
    a = jnp.exp2(r * (decay * (-LOG2E)))
    one_m_a2 = jnp.tanh(r * decay) * (a * a + 1.0)
    bb = jnp.where(one_m_a2 == 0.0, 0.0, one_m_a2 * lax.rsqrt(one_m_a2)) * (ig * xb)

    a_s[...] = a
    b_s[...] = bb

    def step(gi, carry):
        h, pc = carry
        g = gi + p * (ng - 1 - 2 * gi)
        base = pl.multiple_of(g * SUBLANES, SUBLANES)
        a_g = a_s[pl.ds(base, SUBLANES), :]
        h = a_g * h + b_s[pl.ds(base, SUBLANES), :]
        pc = a_g * pc
        hs[pl.ds(base, SUBLANES), :] = h
        ps[pl.ds(base, SUBLANES), :] = pc
        return h, pc

    init = (jnp.zeros((SUBLANES, D_LRU), F32), jnp.ones((SUBLANES, D_LRU), F32))
    last_h[...], last_p[...] = lax.fori_loop(0, ng, step, init, unroll=8)

    @pl.when(c == 0)
    def _():
        hcar[0:1, :] = h0_ref[pl.ds(p, 1), :]

    state = hcar[0:1, :]
    for rr in range(SUBLANES):
        r_in = rr + p * (SUBLANES - 1 - 2 * rr)
        cin[pl.ds(r_in, 1), :] = state
        state = last_p[pl.ds(r_in, 1), :] * state + last_h[pl.ds(r_in, 1), :]
    hcar[0:1, :] = state
    hfull = (hs[...].reshape(ng, SUBLANES, D_LRU)
             + ps[...].reshape(ng, SUBLANES, D_LRU) * cin[...]).reshape(tc, D_LRU)

    @pl.when(p == 0)
    def _():
        hf_s[pl.ds(r0, tc), :] = hfull

    @pl.when(p == 1)
    def _():
        both = (hf_s[pl.ds(r0, tc), :] + hfull).astype(BF16)
        ob_ref[...] = _dot(inv_ref[...], both).astype(ob_ref.dtype)

    @pl.when(c == nc - 1)
    def _():
        st_ref[pl.ds(p, 1), :] = state


def _lru_part(xb, conv_w, conv_b3, wg, bg, lam4, h0, *, layer, h0_layer, tc):
    b, t, _ = xb.shape
    nc = t // tc
    nb8 = t // SUBLANES
    per = tc // SUBLANES
    cidx = lambda p, c: c + p * (nc - 1 - 2 * c)
    in_specs = [
        pl.BlockSpec((None, tc, D_LRU), lambda bi, p, c: (bi, cidx(p, c), 0)),
        pl.BlockSpec((None, SUBLANES, D_LRU), lambda bi, p, c: (bi, jnp.maximum(cidx(p, c) * per - 1, 0), 0)),
        pl.BlockSpec((None, SUBLANES, D_LRU), lambda bi, p, c: (bi, jnp.minimum((cidx(p, c) + 1) * per, nb8 - 1), 0)),
        _layer_spec((CONV_W, D_LRU), layer),
        _layer_spec((1, D_LRU), layer),
        pl.BlockSpec((None, None, D_LRU, 2 * D_LRU), lambda bi, p, c: (layer, p, 0, 0)),
        pl.BlockSpec((None, None, 1, 2 * D_LRU), lambda bi, p, c: (layer, p, 0, 0)),
        pl.BlockSpec((None, None, 1, D_LRU), lambda bi, p, c: (layer, p, 0, 0)),
        pl.BlockSpec((None, None, 2, D_LRU), lambda bi, p, c: (bi, h0_layer, 0, 0)),
        pl.BlockSpec((tc, tc), lambda bi, p, c: (0, 0)),
        pl.BlockSpec((tc, tc), lambda bi, p, c: (0, 0)),
    ]
    j = jnp.arange(tc)
    src = (j % SUBLANES) * (tc // SUBLANES) + j // SUBLANES
    perm = (src[:, None] == j[None, :]).astype(BF16)
    chunk = (tc, D_LRU)
    row8 = (SUBLANES, D_LRU)
    return _Part("lru", functools.partial(_lru_body, tc=tc, nc=nc), (b, 2, nc), in_specs,
                 [xb, xb, xb, conv_w, conv_b3, wg, bg, lam4, h0, perm, perm.T],
                 [jax.ShapeDtypeStruct((b, t, D_LRU), BF16), jax.ShapeDtypeStruct((b, 2, D_LRU), F32)],
                 [pl.BlockSpec((None, tc, D_LRU), lambda bi, p, c: (bi, nc - 1 - p * c, 0)),
                  pl.BlockSpec((None, 2, D_LRU), lambda bi, p, c: (bi, 0, 0))],
                 [pltpu.VMEM(chunk, F32), pltpu.VMEM(chunk, F32), pltpu.VMEM(chunk, F32), pltpu.VMEM(chunk, F32),
                  pltpu.VMEM((t, D_LRU), F32), pltpu.VMEM((t, D_LRU), F32),
                  pltpu.VMEM(row8, F32), pltpu.VMEM(row8, F32), pltpu.VMEM(row8, F32), pltpu.VMEM(row8, F32)])


def _tail_kernel(x_ref, mod_ref, ng_ref, oa_ref, ob_ref, oc_ref, wg_ref, wm_ref, wbr_ref, wout_ref, o_ref):
    x = x_ref[...]
    mod = mod_ref[...]
    hb = _normed_input(x, mod, ng_ref[...]).astype(BF16)
    y = None
    for br, br_ref in enumerate((oa_ref, ob_ref, oc_ref)):
        gt = _dot(hb, wg_ref[:, br * D_A:(br + 1) * D_A])
        u = (br_ref[...].astype(F32) * (gt * _sigmoid(gt))).astype(BF16)
        proj = _dot(u, wbr_ref[br])
        mg = _sigmoid(_dot(hb, wm_ref[:, br * D_MODEL:(br + 1) * D_MODEL]))
        y = mg * proj if y is None else y + mg * proj
    gate = mod[:, 2 * D_MODEL:]
    o_ref[...] = x + gate * _dot(y.astype(BF16), wout_ref[...])


def _tail_part(x2d, mod4, ng3, oa, ob, oc, wg, wm, wbr, wout, *, layer, mod_row0, tokens_per_mod, tm=TM_TAIL):
    n = x2d.shape[0]
    row = lambda i: (i, 0)
    once = dict(pipeline_mode=pl.Buffered(1))

    def weight(shape):
        zeros = (0,) * len(shape)
        return pl.BlockSpec((None,) + shape, lambda i: (layer,) + zeros, **once)

    in_specs = [pl.BlockSpec((tm, D_MODEL), row),
                _mod_spec(layer, mod_row0, tokens_per_mod // tm),
                _layer_spec((1, D_MODEL), layer),
                pl.BlockSpec((tm, D_A), row),
                pl.BlockSpec((tm, D_LRU), row),
                pl.BlockSpec((tm, D_C), row),
                weight((D_MODEL, 3 * D_A)),
                weight((D_MODEL, 3 * D_MODEL)),
                weight((3, D_A, D_MODEL)),
                weight((D_MODEL, D_MODEL))]
    return _Part("tail", lambda ins, outs, scratch, ids: _tail_kernel(*ins, *outs), (n // tm,), in_specs,
                 [x2d, mod4, ng3, oa, ob, oc, wg, wm, wbr, wout],
                 [jax.ShapeDtypeStruct((n, D_MODEL), F32)], [pl.BlockSpec((tm, D_MODEL), row)])


def _assemble_kernel(*refs, n_arrays):
    ins, outs = refs[:n_arrays * DEPTH], refs[n_arrays * DEPTH:]
    for a in range(n_arrays):
        for l in range(DEPTH):
            x = ins[a * DEPTH + l][...]
            outs[a][l] = x.T if a < n_arrays - 1 else x


def _assemble_caches(per_layer, batch):
    n_arrays = len(per_layer[0])
    seq = per_layer[0][0].shape[0] // batch
    args = [per_layer[l][a] for a in range(n_arrays) for l in range(DEPTH)]
    widths = [per_layer[0][a].shape[1] for a in range(n_arrays)]
    in_specs = [pl.BlockSpec((seq, widths[a]), lambda b: (b, 0)) for a in range(n_arrays) for _ in range(DEPTH)]
    shapes = [(DEPTH, w, seq) for w in widths[:-1]] + [(DEPTH, seq, widths[-1])]
    return pl.pallas_call(
        functools.partial(_assemble_kernel, n_arrays=n_arrays),
        out_shape=[jax.ShapeDtypeStruct((batch,) + s, F32) for s in shapes],
        grid=(batch,),
        in_specs=in_specs,
        out_specs=[pl.BlockSpec((None,) + s, lambda b: (b, 0, 0, 0)) for s in shapes],
        compiler_params=_params(("arbitrary",)),
        name="assemble_caches",
    )(*args)


def _rope_tables(seq_len):
    pos = jnp.arange(seq_len)
    row = (pos // GRID_W).astype(F32)
    col = (pos % GRID_W).astype(F32)
    inv = jnp.power(ROPE_BASE, -jnp.arange(ROPE_FREQS, dtype=F32) / ROPE_FREQS)
    ang_r = row[:, None] * inv
    ang_c = col[:, None] * inv
    cos = jnp.concatenate([jnp.cos(ang_r)] * 2 + [jnp.cos(ang_c)] * 2, axis=-1)
    sin = jnp.concatenate([-jnp.sin(ang_r), jnp.sin(ang_r), -jnp.sin(ang_c), jnp.sin(ang_c)], axis=-1)
    return jnp.tile(cos, (1, LANES // HD)), jnp.tile(sin, (1, LANES // HD))


def _block_diag(w):
    rows = w.reshape(w.shape[:-3] + (D_LRU, LRU_BW))
    tiled = jnp.tile(rows, (1,) * (rows.ndim - 1) + (LRU_BLOCKS,))
    blk = jnp.arange(D_LRU) // LRU_BW
    return jnp.where(blk[:, None] == blk[None, :], tiled, 0.0)


def kernel(x_prompt, x_sample, cache_a_k, cache_a_v, cache_c_k, cache_c_v, state_lru, c, c_ctx, norm_g, mod_w, mod_b, w_in, qn_a, kn_a, sink_a, conv_w, conv_b, lru_wa, lru_ba, lru_wx, lru_bx, lru_lam, qn_c, kn_c, lam_q1, lam_k1, lam_q2, lam_k2, subln_c, w_br_a, w_br_b, w_br_c, w_out):
    bp, sp, _ = x_prompt.shape
    bs, ss, _ = x_sample.shape
    past = cache_a_k.shape[2]

    cvecs = jnp.concatenate([c, c_ctx[None, :], jnp.zeros((SUBLANES - bs - 1, D_MODEL), F32)], axis=0)
    mod = _modulation(cvecs, mod_w, mod_b)
    mod4 = mod.reshape(DEPTH, SUBLANES, 1, 3 * D_MODEL)
    rope = _rope_tables(ss)

    w_front = _permute_cast(w_in, FRONT_BLOCKS)
    wg = _permute_cast(w_in, GATE_BLOCKS)
    wm = _permute_cast(w_in, MERGE_BLOCKS)
    wbr = jnp.stack([w_br_a, w_br_b, w_br_c], axis=1).astype(BF16)
    wout = w_out.astype(BF16)
    gains3 = jnp.concatenate([jnp.tile(qn_a * Q_SCALE, (1, H_A)), jnp.tile(qn_c * Q_SCALE, (1, 2 * H_C)),
                              jnp.tile(kn_c, (1, 2 * H_C)), jnp.tile(kn_a, (1, KV_A))],
                             axis=-1)[:, None, :]
    ng3 = norm_g[:, None, :]
    lru_wg = jnp.concatenate([_block_diag(lru_wa), _block_diag(lru_wx)], axis=-1).astype(BF16)
    lru_bg = jnp.concatenate([lru_ba, lru_bx], axis=-1)[:, :, None, :]
    lru_lam4 = lru_lam[:, :, None, :]
    lru_p = (conv_w, conv_b[:, None, :], lru_wg, lru_bg, lru_lam4)
    lam_vecs = [v[:, None, :] for v in (lam_q1, lam_k1, lam_q2, lam_k2)]
    subln3 = subln_c[:, None, :]
    cka = cache_a_k.reshape(bs, DEPTH, past, KV_A * HD)
    cva = cache_a_v.reshape(bs, DEPTH, past, KV_A * HD)
    ckc = cache_c_k.reshape(bs, DEPTH, past, D_C)
    cvc = cache_c_v.reshape(bs, DEPTH, past, D_C)
    zeros_h0 = jnp.zeros((bp, 1, 2, D_LRU), F32)

    xp = x_prompt.reshape(bp * sp, D_MODEL)
    xs = x_sample.reshape(bs * ss, D_MODEL)
    new_caches, new_states = [], []
    for l in range(DEPTH):
        lam_init = 0.8 - 0.6 * math.exp(-0.3 * l)
        ctx_mod = dict(layer=l, mod_row0=bs, tokens_per_mod=bp * sp)
        lat_mod = dict(layer=l, mod_row0=0, tokens_per_mod=ss)

        flat = lambda a: a.reshape(-1, a.shape[-1])
        r3 = lambda a: a.reshape(bp, sp, a.shape[-1])
        qa, ka, qc, kc, va, vc, xb = _run(_front_part(xp, mod4, ng3, w_front, gains3, None, kv_dtype=F32, **ctx_mod))
        new_caches.append((ka, va, kc, vc))
        oa = _attn_a(sink_a, r3(qa), r3(ka), r3(va), None, None, layer=l, tq=sp, banded=False)
        ob, st = _run(_lru_part(r3(xb), *lru_p, zeros_h0, layer=l, h0_layer=0, tc=sp))
        new_states.append(st)
        oc = _attn_c(lam_vecs, subln3, r3(qc), r3(kc), r3(vc), None, None, layer=l, tq=sp, ts=sp, lam_init=lam_init)
        (xp,) = _run(_tail_part(xp, mod4, ng3, flat(oa), flat(ob), flat(oc), wg, wm, wbr, wout, **ctx_mod))

        r3 = lambda a: a.reshape(bs, ss, a.shape[-1])
        qa, ka, qc, kc, va, vc, xb = _run(_front_part(xs, mod4, ng3, w_front, gains3, rope, kv_dtype=BF16, **lat_mod))
        oa = _attn_a(sink_a, r3(qa), r3(ka), r3(va), cka, cva, layer=l, tq=TQ_A, banded=True)
        ob, _ = _run(_lru_part(r3(xb), *lru_p, state_lru, layer=l, h0_layer=l, tc=TC_LRU))
        oc = _attn_c(lam_vecs, subln3, r3(qc), r3(kc), r3(vc), ckc, cvc, layer=l, tq=TQ_C, ts=TS_C, lam_init=lam_init)
        (xs,) = _run(_tail_part(xs, mod4, ng3, flat(oa), flat(ob), flat(oc), wg, wm, wbr, wout, **lat_mod))

    ka_t, va_t, kc_t, vc = _assemble_caches(new_caches, bp)
    return (xp.reshape(bp, sp, D_MODEL), xs.reshape(bs, ss, D_MODEL),
            ka_t.reshape(bp, DEPTH, KV_A, HD, sp).transpose(0, 1, 4, 2, 3),
            va_t.reshape(bp, DEPTH, KV_A, HD, sp).transpose(0, 1, 4, 2, 3),
            kc_t.reshape(bp, DEPTH, H_C, 2, HD, sp).transpose(0, 1, 5, 2, 3, 4),
            vc.reshape(bp, DEPTH, sp, H_C, DV_C),
            jnp.stack(new_states, axis=1))
```

```python
import functools
import math

import jax
import jax.numpy as jnp
from jax import lax
from jax.experimental import pallas as pl
from jax.experimental.pallas import tpu as pltpu

F32 = jnp.float32
BF16 = jnp.bfloat16

D_MODEL = 1024
DEPTH = 2
GRID_W = 64
HD = 64
SCALE = 1.0 / math.sqrt(HD)
LOG2E = math.log2(math.e)
Q_SCALE = SCALE * LOG2E
H_A = 8
KV_A = 2
WINDOW = 128
D_A = H_A * HD
D_LRU = 512
LRU_BLOCKS = 8
LRU_BW = D_LRU // LRU_BLOCKS
CONV_W = 4
LRU_C = 8.0
H_C = 4
DV_C = 2 * HD
D_C = H_C * DV_C
ROPE_BASE = 10000.0
ROPE_FREQS = HD // 4
EPS = 1e-6
LANES = 128
SUBLANES = 8
VMEM_LIMIT = 56 * 1024 * 1024

_SECTIONS = (("qa", D_A), ("kava", 2 * KV_A * HD), ("ga", D_A), ("xb", D_LRU), ("gb", D_LRU),
             ("qc", H_C * 2 * HD), ("kc", H_C * 2 * HD), ("vc", D_C), ("gc", D_C), ("mg", 3 * D_MODEL))
W_BLOCK = 256


def _col_blocks(*names):
    start, spans = 0, {}
    for name, width in _SECTIONS:
        spans[name] = (start, start + width)
        start += width
    blocks = []
    for name in names:
        a, b = spans[name]
        assert a % W_BLOCK == 0 and b % W_BLOCK == 0
        blocks += range(a // W_BLOCK, b // W_BLOCK)
    return tuple(blocks)


FRONT_BLOCKS = _col_blocks("qa", "qc", "kc", "kava", "vc", "xb")
GATE_BLOCKS = _col_blocks("ga", "gb", "gc")
MERGE_BLOCKS = _col_blocks("mg")
N_NORM = D_A + KV_A * HD + 2 * H_C * 2 * HD
N_PLAIN = KV_A * HD + D_C + D_LRU

TM_FRONT = 512
TM_TAIL = 512
TQ_A = 2 * WINDOW
NSUB_A = 4
TQ_C = 512
TS_C = 512
TC_LRU = 512


def _params(sem, vmem=VMEM_LIMIT):
    return pltpu.CompilerParams(dimension_semantics=sem, vmem_limit_bytes=vmem)


def _dot(a, b):
    return jnp.dot(a, b, preferred_element_type=F32)


def _dot_nt(a, b):
    return lax.dot_general(a, b, (((1,), (1,)), ((), ())), preferred_element_type=F32)


def _sigmoid(x):
    return 1.0 / (1.0 + jnp.exp2(x * (-LOG2E)))


def _permute_cast_kernel(perm_ref, w_ref, o_ref):
    o_ref[...] = w_ref[...].astype(o_ref.dtype)


def _permute_cast(w, blocks):
    nl, k, _ = w.shape
    grid_spec = pltpu.PrefetchScalarGridSpec(
        num_scalar_prefetch=1, grid=(len(blocks),),
        in_specs=[pl.BlockSpec((nl, k, W_BLOCK), lambda j, perm: (0, 0, perm[j]))],
        out_specs=pl.BlockSpec((nl, k, W_BLOCK), lambda j, perm: (0, 0, j)))
    return pl.pallas_call(
        _permute_cast_kernel, grid_spec=grid_spec,
        out_shape=jax.ShapeDtypeStruct((nl, k, len(blocks) * W_BLOCK), BF16),
        compiler_params=_params(("arbitrary",)), name="permute_cast",
    )(jnp.asarray(blocks, jnp.int32), w)


def _mod_kernel(c_ref, w_ref, b_ref, o_ref):
    c = c_ref[...]
    a = c * _sigmoid(c)
    w = w_ref[...]
    a_hi = a.astype(BF16)
    a_lo = (a - a_hi.astype(F32)).astype(BF16)
    w_hi = w.astype(BF16)
    w_lo = (w - w_hi.astype(F32)).astype(BF16)
    o_ref[...] = _dot(a_hi, w_hi) + _dot(a_hi, w_lo) + _dot(a_lo, w_hi) + b_ref[...]


def _modulation(cvecs, mod_w, mod_b):
    tn = 768
    return pl.pallas_call(
        _mod_kernel,
        out_shape=jax.ShapeDtypeStruct((DEPTH, SUBLANES, 3 * D_MODEL), F32),
        grid=(DEPTH, 3 * D_MODEL // tn),
        in_specs=[pl.BlockSpec((SUBLANES, D_MODEL), lambda l, j: (0, 0)),
                  pl.BlockSpec((None, D_MODEL, tn), lambda l, j: (l, 0, j)),
                  pl.BlockSpec((None, 1, tn), lambda l, j: (l, 0, j))],
        out_specs=pl.BlockSpec((None, SUBLANES, tn), lambda l, j: (l, 0, j)),
        compiler_params=_params(("arbitrary", "arbitrary")),
        name="modulation",
    )(cvecs, mod_w, mod_b.reshape(DEPTH, 1, 3 * D_MODEL))


def _normed_input(x, mod, ng):
    ms = jnp.mean(x * x, axis=-1, keepdims=True)
    shift = mod[:, 0:D_MODEL]
    scale = mod[:, D_MODEL:2 * D_MODEL]
    return (x * lax.rsqrt(ms + EPS) * ng) * (1.0 + scale) + shift


def _layer_spec(shape, layer):
    zeros = (0,) * len(shape)
    return pl.BlockSpec((None,) + tuple(shape), lambda *_: (layer,) + zeros)


def _mod_spec(layer, row0, tiles_per_row):
    return pl.BlockSpec((None, None, 1, 3 * D_MODEL), lambda i: (layer, row0 + i // tiles_per_row, 0, 0))


class _Part:
    def __init__(self, name, body, grid, in_specs, args, out_shape, out_specs, scratch=()):
        self.name, self.body, self.grid = name, body, tuple(grid)
        self.in_specs, self.args = list(in_specs), list(args)
        self.out_shape, self.out_specs, self.scratch = list(out_shape), list(out_specs), list(scratch)


def _run(part):
    ni, no = len(part.args), len(part.out_shape)

    def kern(*refs):
        ids = tuple(pl.program_id(k) for k in range(len(part.grid)))
        part.body(refs[:ni], refs[ni:ni + no], refs[ni + no:], ids)

    return pl.pallas_call(
        kern, out_shape=part.out_shape, grid=part.grid, in_specs=part.in_specs, out_specs=part.out_specs,
        scratch_shapes=part.scratch, compiler_params=_params(("arbitrary",) * len(part.grid)), name=part.name)(*part.args)


def _front_body(ins, outs, scratch, ids, *, use_rope):
    x_ref, mod_ref, ng_ref, w_ref, gain_ref = ins[:5]
    qa_ref, ka_ref, qc_ref, kc_ref, va_ref, vc_ref, xb_ref = outs
    x = x_ref[...]
    tm = x.shape[0]
    hb = _normed_input(x, mod_ref[...], ng_ref[...]).astype(BF16)
    p1 = _dot(hb, w_ref[:, :N_NORM])
    lane = lax.broadcasted_iota(jnp.int32, (tm, LANES), 1)
    first_half = (lane & ROPE_FREQS) == 0
    wide = 2 * LANES
    same_head = (lax.broadcasted_iota(jnp.int32, (wide, wide), 0) // HD
                 == lax.broadcasted_iota(jnp.int32, (wide, wide), 1) // HD)
    ones_bd = jnp.where(same_head, 1.0, 0.0).astype(BF16)
    sums = []
    for c0 in range(0, N_NORM, wide):
        w = min(wide, N_NORM - c0)
        sq = p1[:, c0:c0 + w] * p1[:, c0:c0 + w]
        sq_hi = sq.astype(BF16)
        sq_lo = (sq - sq_hi.astype(F32)).astype(BF16)
        sums.append(_dot(sq_hi, ones_bd[:w, :w]) + _dot(sq_lo, ones_bd[:w, :w]))
    dests = ([(qa_ref, c) for c in range(4)] + [(qc_ref, c) for c in range(4)]
             + [(kc_ref, c) for c in range(4)] + [(ka_ref, 0)])
    for c, (o_ref, oc) in enumerate(dests):
        pc = p1[:, c * LANES:(c + 1) * LANES]
        msq = sums[c // 2][:, (c % 2) * LANES:(c % 2 + 1) * LANES] * (1.0 / HD)
        y = pc * lax.rsqrt(msq + EPS) * gain_ref[:, c * LANES:(c + 1) * LANES]
        if use_rope:
            cos_ref, sin_ref = ins[5:7]
            partner = jnp.where(first_half, pltpu.roll(y, LANES - ROPE_FREQS, 1),
                                pltpu.roll(y, ROPE_FREQS, 1))
            y = y * cos_ref[...] + partner * sin_ref[...]
        o_ref[:, oc * LANES:(oc + 1) * LANES] = y.astype(o_ref.dtype)
    p2 = _dot(hb, w_ref[:, N_NORM:])
    va_ref[...] = p2[:, 0:KV_A * HD].astype(va_ref.dtype)
    vc_ref[...] = p2[:, KV_A * HD:KV_A * HD + D_C].astype(vc_ref.dtype)
    xb_ref[...] = p2[:, KV_A * HD + D_C:]


def _front_part(x2d, mod4, ng3, w_front, gains3, rope, *, layer, mod_row0, tokens_per_mod, kv_dtype, tm=TM_FRONT):
    n = x2d.shape[0]
    use_rope = rope is not None
    row = lambda i: (i, 0)
    in_specs = [pl.BlockSpec((tm, D_MODEL), row),
                _mod_spec(layer, mod_row0, tokens_per_mod // tm),
                _layer_spec((1, D_MODEL), layer),
                _layer_spec((D_MODEL, N_NORM + N_PLAIN), layer),
                _layer_spec((1, N_NORM), layer)]
    args = [x2d, mod4, ng3, w_front, gains3]
    if use_rope:
        per_seq = rope[0].shape[0] // tm
        in_specs += [pl.BlockSpec((tm, LANES), lambda i: (i % per_seq, 0))] * 2
        args += list(rope)
    widths = (D_A, KV_A * HD, H_C * 2 * HD, H_C * 2 * HD, KV_A * HD, D_C, D_LRU)
    dtypes = (BF16, kv_dtype, BF16, kv_dtype, kv_dtype, kv_dtype, F32)
    return _Part("front_rope" if use_rope else "front", functools.partial(_front_body, use_rope=use_rope),
                 (n // tm,), in_specs, args,
                 [jax.ShapeDtypeStruct((n, w), d) for w, d in zip(widths, dtypes)],
                 [pl.BlockSpec((tm, w), row) for w in widths])


def _attn_a_kernel(*refs, tq, nsub, seq_len, banded, has_ctx, layer):
    if has_ctx:
        sink_ref, q_ref, k_ref, v_ref, kc_ref, vc_ref, o_ref = refs
    else:
        sink_ref, q_ref, k_ref, v_ref, o_ref = refs
    i = pl.program_id(1)
    g = H_A // KV_A

    def spread(k, v):
        k = k.astype(F32)
        v = v.astype(F32)
        lo = lax.broadcasted_iota(jnp.int32, k.shape, 1) < HD
        k_sw = pltpu.roll(k, HD, 1)
        v_sw = pltpu.roll(v, HD, 1)
        kd = (jnp.where(lo, k, k_sw).astype(BF16), jnp.where(lo, k_sw, k).astype(BF16))
        vd = ((jnp.where(lo, v, 1.0).astype(BF16), jnp.where(lo, 1.0, v_sw).astype(BF16)),
              (jnp.where(lo, v_sw, 1.0).astype(BF16), jnp.where(lo, 1.0, v).astype(BF16)))
        return kd, vd

    if has_ctx:
        kd_c, vd_c = spread(kc_ref[...], vc_ref[...])
    nwin = tq + 2 * WINDOW if banded else seq_len
    lo_q = lax.broadcasted_iota(jnp.int32, (tq, LANES), 1) < HD
    zero = jnp.zeros((tq, LANES), BF16)
    for u in range(nsub):
        blk = i * nsub + u
        if banded:
            start = pl.multiple_of(jnp.clip(blk * tq - WINDOW, 0, seq_len - nwin), WINDOW)
            kd, vd = spread(k_ref[pl.ds(start, nwin), :], v_ref[pl.ds(start, nwin), :])
            kpos = start + lax.broadcasted_iota(jnp.int32, (tq, nwin), 1)
            qpos = blk * tq + lax.broadcasted_iota(jnp.int32, (tq, nwin), 0)
            bias = jnp.where(jnp.abs(kpos - qpos) <= WINDOW, 0.0, -jnp.inf)
        else:
            kd, vd = spread(k_ref[...], v_ref[...])
        if has_ctx:
            kd = tuple(jnp.concatenate([kd[kv], kd_c[kv]], axis=0) for kv in range(KV_A))
            vd = tuple(tuple(jnp.concatenate([vd[kv][half], vd_c[kv][half]], axis=0) for half in range(2))
                       for kv in range(KV_A))
        q = q_ref[u * tq:(u + 1) * tq, :]
        outs = []
        for h in range(H_A):
            kv, half = h // g, h % 2
            qz = jnp.where(lo_q if half == 0 else jnp.logical_not(lo_q),
                           q[:, (h // 2) * LANES:(h // 2 + 1) * LANES], zero)
            s = _dot_nt(qz, kd[kv])
            if banded:
                s = jnp.concatenate([s[:, :nwin] + bias, s[:, nwin:]], axis=1)
            snk = sink_ref[layer, h] * LOG2E
            m = jnp.maximum(jnp.max(s, axis=-1, keepdims=True), snk)
            e = jnp.exp2(s - m)
            pv = _dot(e.astype(BF16), vd[kv][half])
            outs.append(pv / (pltpu.roll(pv, HD, 1) + jnp.exp2(snk - m)))
        for c in range(H_A // 2):
            pair = jnp.where(lo_q, outs[2 * c], outs[2 * c + 1])
            o_ref[u * tq:(u + 1) * tq, c * LANES:(c + 1) * LANES] = pair.astype(o_ref.dtype)


def _attn_a(sink, q, k, v, k_ctx, v_ctx, *, layer, tq, nsub, banded):
    b, t, _ = q.shape
    has_ctx = k_ctx is not None
    rows = tq * nsub
    kv_spec = pl.BlockSpec((None, t, KV_A * HD), lambda bi, i: (bi, 0, 0))
    in_specs = [pl.BlockSpec(memory_space=pltpu.SMEM),
                pl.BlockSpec((None, rows, D_A), lambda bi, i: (bi, i, 0)), kv_spec, kv_spec]
    args = [sink, q, k, v]
    if has_ctx:
        s = k_ctx.shape[2]
        in_specs += [pl.BlockSpec((None, None, s, KV_A * HD), lambda bi, i: (bi, layer, 0, 0))] * 2
        args += [k_ctx, v_ctx]
    return pl.pallas_call(
        functools.partial(_attn_a_kernel, tq=tq, nsub=nsub, seq_len=t, banded=banded, has_ctx=has_ctx, layer=layer),
        out_shape=jax.ShapeDtypeStruct((b, t, D_A), BF16),
        grid=(b, t // rows),
        in_specs=in_specs,
        out_specs=pl.BlockSpec((None, rows, D_A), lambda bi, i: (bi, i, 0)),
        compiler_params=_params(("arbitrary", "arbitrary")),
        name="attn_a_latent" if has_ctx else "attn_a_context",
    )(*args)


def _attn_c_kernel(*refs, tq, ts, n_loc, has_ctx, lam_init):
    if has_ctx:
        lq1, lk1, lq2, lk2, sub_ref, q_ref, k_ref, v_ref, kc_ref, vc_ref, o_ref = refs
    else:
        lq1, lk1, lq2, lk2, sub_ref, q_ref, k_ref, v_ref, o_ref = refs
    lam = (jnp.exp(jnp.sum(lq1[...] * lk1[...], axis=-1, keepdims=True))
           - jnp.exp(jnp.sum(lq2[...] * lk2[...], axis=-1, keepdims=True)) + lam_init)
    lo = lax.broadcasted_iota(jnp.int32, (tq, LANES), 1) < HD
    zero = jnp.zeros((tq, LANES), BF16)
    heads = [slice(h * LANES, (h + 1) * LANES) for h in range(H_C)]
    qz = []
    for cols in heads:
        q12 = q_ref[:, cols]
        qz.append(jnp.concatenate([jnp.where(lo, q12, zero), jnp.where(lo, zero, q12)], axis=0))

    def update(carry, kt, vt):
        new = []
        for h, cols in enumerate(heads):
            m, l, acc = carry[h]
            s = _dot_nt(qz[h], kt[:, cols].astype(BF16))
            m_new = jnp.maximum(m, jnp.max(s, axis=-1, keepdims=True))
            alpha = jnp.exp2(m - m_new)
            p = jnp.exp2(s - m_new)
            psum = p[:, 0:LANES]
            for j in range(1, p.shape[1] // LANES):
                psum = psum + p[:, j * LANES:(j + 1) * LANES]
            l = alpha * l + psum
            acc = alpha * acc + _dot(p.astype(BF16), vt[:, cols].astype(BF16))
            new.append((m_new, l, acc))
        return tuple(new)

    carry = tuple((jnp.full((2 * tq, 1), -jnp.inf, F32), jnp.zeros((2 * tq, LANES), F32),
                   jnp.zeros((2 * tq, LANES), F32)) for _ in heads)
    for j in range(n_loc):
        carry = update(carry, k_ref[j * ts:(j + 1) * ts, :], v_ref[j * ts:(j + 1) * ts, :])
    if has_ctx:
        carry = update(carry, kc_ref[...], vc_ref[...])
    for h, cols in enumerate(heads):
        _, l, acc = carry[h]
        l = jnp.sum(l, axis=-1, keepdims=True)
        o = acc[:tq] / l[:tq] - lam * (acc[tq:] / l[tq:])
        o = o * lax.rsqrt(jnp.mean(o * o, axis=-1, keepdims=True) + EPS) * sub_ref[...] * (1.0 - lam_init)
        o_ref[:, cols] = o.astype(o_ref.dtype)


def _attn_c(lam_vecs, subln3, q, k, v, k_ctx, v_ctx, *, layer, tq, ts, lam_init):
    b, t, _ = q.shape
    s_loc = k.shape[1]
    has_ctx = k_ctx is not None
    once = dict(pipeline_mode=pl.Buffered(1)) if t // tq > 1 else {}
    kv_spec = pl.BlockSpec((None, s_loc, D_C), lambda bi, i: (bi, 0, 0), **once)
    in_specs = ([_layer_spec((1, HD), layer)] * 4 + [_layer_spec((1, DV_C), layer)]
                + [pl.BlockSpec((None, tq, D_C), lambda bi, i: (bi, i, 0)), kv_spec, kv_spec])
    args = list(lam_vecs) + [subln3, q, k, v]
    if has_ctx:
        sc = k_ctx.shape[2]
        in_specs += [pl.BlockSpec((None, None, sc, D_C), lambda bi, i: (bi, layer, 0, 0), **once)] * 2
        args += [k_ctx, v_ctx]
    return pl.pallas_call(
        functools.partial(_attn_c_kernel, tq=tq, ts=ts, n_loc=s_loc // ts, has_ctx=has_ctx, lam_init=lam_init),
        out_shape=jax.ShapeDtypeStruct((b, t, D_C), BF16),
        grid=(b, t // tq),
        in_specs=in_specs,
        out_specs=pl.BlockSpec((None, tq, D_C), lambda bi, i: (bi, i, 0)),
        compiler_params=_params(("arbitrary", "arbitrary")),
        name="attn_c_latent" if has_ctx else "attn_c_context",
    )(*args)


def _lru_body(ins, outs, scratch, ids, *, tc, nc):
    xc_ref, xp_ref, xn_ref, cw_ref, cb_ref, wg_ref, bg_ref, lam_ref, h0_ref, perm_ref, inv_ref = ins
    ob_ref, st_ref = outs
    a_s, b_s, hs, ps, hf_s, xb_s, hcar, last_h, last_p, cin = scratch
    _, p, c = ids
    cidx = c + p * (nc - 1 - 2 * c)
    r0 = pl.multiple_of(cidx * tc, tc)
    ng = tc // SUBLANES

    @pl.when(p == 0)
    def _():
        cur = xc_ref[...]
        prev = jnp.where(cidx > 0, xp_ref[...], 0.0)
        nxt = jnp.where(cidx < nc - 1, xn_ref[...], 0.0)
        hi = cur.astype(BF16)
        rest = cur - hi.astype(F32)
        mid = rest.astype(BF16)
        lo = (rest - mid.astype(F32)).astype(BF16)
        x3 = (_dot(perm_ref[...], hi) + _dot(perm_ref[...], mid) + _dot(perm_ref[...], lo)).reshape(ng, SUBLANES, D_LRU)
        rows = lax.broadcasted_iota(jnp.int32, (SUBLANES, D_LRU), 0)
        before1 = jnp.where(rows == 0, prev[SUBLANES - 1:SUBLANES, :], pltpu.roll(x3[ng - 1], 1, 0))
        before2 = jnp.where(rows == 0, prev[SUBLANES - 2:SUBLANES - 1, :], pltpu.roll(x3[ng - 2], 1, 0))
        after1 = jnp.where(rows == SUBLANES - 1, nxt[0:1, :], pltpu.roll(x3[0], SUBLANES - 1, 0))
        xpad = jnp.concatenate([before2[None], before1[None], x3, after1[None]], axis=0)
        acc = jnp.broadcast_to(cb_ref[...], (ng, SUBLANES, D_LRU))
        for j in range(CONV_W):
            acc = acc + xpad[j:j + ng] * cw_ref[j:j + 1, :]
        xb_s[pl.ds(r0, tc), :] = acc.reshape(tc, D_LRU)

    xb = xb_s[pl.ds(r0, tc), :]
    gm = _dot(xb.astype(BF16), wg_ref[...]) + bg_ref[...]
    r = _sigmoid(gm[:, :D_LRU])
    ig = _sigmoid(gm[:, D_LRU:])
    nl = -lam_ref[...]
    softplus = jnp.maximum(nl, 0.0) + jnp.log1p(jnp.exp(-jnp.abs(nl)))
    decay = LRU_C * softplus
    a = jnp.exp2(r * (decay * (-LOG2E)))
    one_m_a2 = jnp.tanh(r * decay) * (a * a + 1.0)
    bb = jnp.where(one_m_a2 == 0.0, 0.0, one_m_a2 * lax.rsqrt(one_m_a2)) * (ig * xb)

    a_s[...] = a
    b_s[...] = bb

    def step(gi, carry):
        h, pc = carry
        g = gi + p * (ng - 1 - 2 * gi)
        base = pl.multiple_of(g * SUBLANES, SUBLANES)
        a_g = a_s[pl.ds(base, SUBLANES), :]
        h = a_g * h + b_s[pl.ds(base, SUBLANES), :]
        pc = a_g * pc
        hs[pl.ds(base, SUBLANES), :] = h
        ps[pl.ds(base, SUBLANES), :] = pc
        return h, pc

    init = (jnp.zeros((SUBLANES, D_LRU), F32), jnp.ones((SUBLANES, D_LRU), F32))
    last_h[...], last_p[...] = lax.fori_loop(0, ng, step, init, unroll=8)

    @pl.when(c == 0)
    def _():
        hcar[0:1, :] = h0_ref[pl.ds(p, 1), :]

    state = hcar[0:1, :]
    for rr in range(SUBLANES):
        r_in = rr + p * (SUBLANES - 1 - 2 * rr)
        cin[pl.ds(r_in, 1), :] = state
        state = last_p[pl.ds(r_in, 1), :] * state + last_h[pl.ds(r_in, 1), :]
    hcar[0:1, :] = state
    hfull = (hs[...].reshape(ng, SUBLANES, D_LRU)
             + ps[...].reshape(ng, SUBLANES, D_LRU) * cin[...]).reshape(tc, D_LRU)

    @pl.when(p == 0)
    def _():
        hf_s[pl.ds(r0, tc), :] = hfull

    @pl.when(p == 1)
    def _():
        both = (hf_s[pl.ds(r0, tc), :] + hfull).astype(BF16)
        ob_ref[...] = _dot(inv_ref[...], both).astype(ob_ref.dtype)

    @pl.when(c == nc - 1)
    def _():
        st_ref[pl.ds(p, 1), :] = state


def _lru_part(xb, conv_w, conv_b3, wg, bg, lam4, h0, *, layer, h0_layer, tc):
    b, t, _ = xb.shape
    nc = t // tc
    nb8 = t // SUBLANES
    per = tc // SUBLANES
    cidx = lambda p, c: c + p * (nc - 1 - 2 * c)
    in_specs = [
        pl.BlockSpec((None, tc, D_LRU), lambda bi, p, c: (bi, cidx(p, c), 0)),
        pl.BlockSpec((None, SUBLANES, D_LRU), lambda bi, p, c: (bi, jnp.maximum(cidx(p, c) * per - 1, 0), 0)),
        pl.BlockSpec((None, SUBLANES, D_LRU), lambda bi, p, c: (bi, jnp.minimum((cidx(p, c) + 1) * per, nb8 - 1), 0)),
        _layer_spec((CONV_W, D_LRU), layer),
        _layer_spec((1, D_LRU), layer),
        pl.BlockSpec((None, None, D_LRU, 2 * D_LRU), lambda bi, p, c: (layer, p, 0, 0)),
        pl.BlockSpec((None, None, 1, 2 * D_LRU), lambda bi, p, c: (layer, p, 0, 0)),
        pl.BlockSpec((None, None, 1, D_LRU), lambda bi, p, c: (layer, p, 0, 0)),
        pl.BlockSpec((None, None, 2, D_LRU), lambda bi, p, c: (bi, h0_layer, 0, 0)),
        pl.BlockSpec((tc, tc), lambda bi, p, c: (0, 0)),
        pl.BlockSpec((tc, tc), lambda bi, p, c: (0, 0)),
    ]
    j = jnp.arange(tc)
    src = (j % SUBLANES) * (tc // SUBLANES) + j // SUBLANES
    perm = (src[:, None] == j[None, :]).astype(BF16)
    chunk = (tc, D_LRU)
    row8 = (SUBLANES, D_LRU)
    return _Part("lru", functools.partial(_lru_body, tc=tc, nc=nc), (b, 2, nc), in_specs,
                 [xb, xb, xb, conv_w, conv_b3, wg, bg, lam4, h0, perm, perm.T],
                 [jax.ShapeDtypeStruct((b, t, D_LRU), BF16), jax.ShapeDtypeStruct((b, 2, D_LRU), F32)],
                 [pl.BlockSpec((None, tc, D_LRU), lambda bi, p, c: (bi, nc - 1 - p * c, 0)),
                  pl.BlockSpec((None, 2, D_LRU), lambda bi, p, c: (bi, 0, 0))],
                 [pltpu.VMEM(chunk, F32), pltpu.VMEM(chunk, F32), pltpu.VMEM(chunk, F32), pltpu.VMEM(chunk, F32),
                  pltpu.VMEM((t, D_LRU), F32), pltpu.VMEM((t, D_LRU), F32),
                  pltpu.VMEM(row8, F32), pltpu.VMEM(row8, F32), pltpu.VMEM(row8, F32), pltpu.VMEM(row8, F32)])


def _tail_kernel(x_ref, mod_ref, ng_ref, oa_ref, ob_ref, oc_ref, wg_ref, wm_ref, wbr_ref, wout_ref, o_ref):
    x = x_ref[...]
    mod = mod_ref[...]
    hb = _normed_input(x, mod, ng_ref[...]).astype(BF16)
    y = None
    for br, br_ref in enumerate((oa_ref, ob_ref, oc_ref)):
        gt = _dot(hb, wg_ref[:, br * D_A:(br + 1) * D_A])
        u = (br_ref[...].astype(F32) * (gt * _sigmoid(gt))).astype(BF16)
        proj = _dot(u, wbr_ref[br])
        mg = _sigmoid(_dot(hb, wm_ref[:, br * D_MODEL:(br + 1) * D_MODEL]))
        y = mg * proj if y is None else y + mg * proj
    gate = mod[:, 2 * D_MODEL:]
    o_ref[...] = x + gate * _dot(y.astype(BF16), wout_ref[...])


def _tail_part(x2d, mod4, ng3, oa, ob, oc, wg, wm, wbr, wout, *, layer, mod_row0, tokens_per_mod, tm=TM_TAIL):
    n = x2d.shape[0]
    row = lambda i: (i, 0)
    once = dict(pipeline_mode=pl.Buffered(1))

    def weight(shape):
        zeros = (0,) * len(shape)
        return pl.BlockSpec((None,) + shape, lambda i: (layer,) + zeros, **once)

    in_specs = [pl.BlockSpec((tm, D_MODEL), row),
                _mod_spec(layer, mod_row0, tokens_per_mod // tm),
                _layer_spec((1, D_MODEL), layer),
                pl.BlockSpec((tm, D_A), row),
                pl.BlockSpec((tm, D_LRU), row),
                pl.BlockSpec((tm, D_C), row),
                weight((D_MODEL, 3 * D_A)),
                weight((D_MODEL, 3 * D_MODEL)),
                weight((3, D_A, D_MODEL)),
                weight((D_MODEL, D_MODEL))]
    return _Part("tail", lambda ins, outs, scratch, ids: _tail_kernel(*ins, *outs), (n // tm,), in_specs,
                 [x2d, mod4, ng3, oa, ob, oc, wg, wm, wbr, wout],
                 [jax.ShapeDtypeStruct((n, D_MODEL), F32)], [pl.BlockSpec((tm, D_MODEL), row)])


def _assemble_kernel(*refs, n_arrays):
    ins, outs = refs[:n_arrays * DEPTH], refs[n_arrays * DEPTH:]
    for a in range(n_arrays):
        for l in range(DEPTH):
            x = ins[a * DEPTH + l][...]
            outs[a][l] = x.T if a < n_arrays - 1 else x


def _assemble_caches(per_layer, batch):
    n_arrays = len(per_layer[0])
    seq = per_layer[0][0].shape[0] // batch
    args = [per_layer[l][a] for a in range(n_arrays) for l in range(DEPTH)]
    widths = [per_layer[0][a].shape[1] for a in range(n_arrays)]
    in_specs = [pl.BlockSpec((seq, widths[a]), lambda b: (b, 0)) for a in range(n_arrays) for _ in range(DEPTH)]
    shapes = [(DEPTH, w, seq) for w in widths[:-1]] + [(DEPTH, seq, widths[-1])]
    return pl.pallas_call(
        functools.partial(_assemble_kernel, n_arrays=n_arrays),
        out_shape=[jax.ShapeDtypeStruct((batch,) + s, F32) for s in shapes],
        grid=(batch,),
        in_specs=in_specs,
        out_specs=[pl.BlockSpec((None,) + s, lambda b: (b, 0, 0, 0)) for s in shapes],
        compiler_params=_params(("arbitrary",)),
        name="assemble_caches",
    )(*args)


def _rope_tables(seq_len):
    pos = jnp.arange(seq_len)
    row = (pos // GRID_W).astype(F32)
    col = (pos % GRID_W).astype(F32)
    inv = jnp.power(ROPE_BASE, -jnp.arange(ROPE_FREQS, dtype=F32) / ROPE_FREQS)
    ang_r = row[:, None] * inv
    ang_c = col[:, None] * inv
    cos = jnp.concatenate([jnp.cos(ang_r)] * 2 + [jnp.cos(ang_c)] * 2, axis=-1)
    sin = jnp.concatenate([-jnp.sin(ang_r), jnp.sin(ang_r), -jnp.sin(ang_c), jnp.sin(ang_c)], axis=-1)
    return jnp.tile(cos, (1, LANES // HD)), jnp.tile(sin, (1, LANES // HD))


def _block_diag(w):
    rows = w.reshape(w.shape[:-3] + (D_LRU, LRU_BW))
    tiled = jnp.tile(rows, (1,) * (rows.ndim - 1) + (LRU_BLOCKS,))
    blk = jnp.arange(D_LRU) // LRU_BW
    return jnp.where(blk[:, None] == blk[None, :], tiled, 0.0)


def kernel(x_prompt, x_sample, cache_a_k, cache_a_v, cache_c_k, cache_c_v, state_lru, c, c_ctx, norm_g, mod_w, mod_b, w_in, qn_a, kn_a, sink_a, conv_w, conv_b, lru_wa, lru_ba, lru_wx, lru_bx, lru_lam, qn_c, kn_c, lam_q1, lam_k1, lam_q2, lam_k2, subln_c, w_br_a, w_br_b, w_br_c, w_out):
    bp, sp, _ = x_prompt.shape
    bs, ss, _ = x_sample.shape
    past = cache_a_k.shape[2]

    cvecs = jnp.concatenate([c, c_ctx[None, :], jnp.zeros((SUBLANES - bs - 1, D_MODEL), F32)], axis=0)
    mod = _modulation(cvecs, mod_w, mod_b)
    mod4 = mod.reshape(DEPTH, SUBLANES, 1, 3 * D_MODEL)
    rope = _rope_tables(ss)

    w_front = _permute_cast(w_in, FRONT_BLOCKS)
    wg = _permute_cast(w_in, GATE_BLOCKS)
    wm = _permute_cast(w_in, MERGE_BLOCKS)
    wbr = jnp.stack([w_br_a, w_br_b, w_br_c], axis=1).astype(BF16)
    wout = w_out.astype(BF16)
    gains3 = jnp.concatenate([jnp.tile(qn_a * Q_SCALE, (1, H_A)), jnp.tile(qn_c * Q_SCALE, (1, 2 * H_C)),
                              jnp.tile(kn_c, (1, 2 * H_C)), jnp.tile(kn_a, (1, KV_A))],
                             axis=-1)[:, None, :]
    ng3 = norm_g[:, None, :]
    lru_wg = jnp.concatenate([_block_diag(lru_wa), _block_diag(lru_wx)], axis=-1).astype(BF16)
    lru_bg = jnp.concatenate([lru_ba, lru_bx], axis=-1)[:, :, None, :]
    lru_lam4 = lru_lam[:, :, None, :]
    lru_p = (conv_w, conv_b[:, None, :], lru_wg, lru_bg, lru_lam4)
    lam_vecs = [v[:, None, :] for v in (lam_q1, lam_k1, lam_q2, lam_k2)]
    subln3 = subln_c[:, None, :]
    cka = cache_a_k.reshape(bs, DEPTH, past, KV_A * HD)
    cva = cache_a_v.reshape(bs, DEPTH, past, KV_A * HD)
    ckc = cache_c_k.reshape(bs, DEPTH, past, D_C)
    cvc = cache_c_v.reshape(bs, DEPTH, past, D_C)
    zeros_h0 = jnp.zeros((bp, 1, 2, D_LRU), F32)

    xp = x_prompt.reshape(bp * sp, D_MODEL)
    xs = x_sample.reshape(bs * ss, D_MODEL)
    new_caches, new_states = [], []
    for l in range(DEPTH):
        lam_init = 0.8 - 0.6 * math.exp(-0.3 * l)
        ctx_mod = dict(layer=l, mod_row0=bs, tokens_per_mod=bp * sp)
        lat_mod = dict(layer=l, mod_row0=0, tokens_per_mod=ss)

        flat = lambda a: a.reshape(-1, a.shape[-1])
        r3 = lambda a: a.reshape(bp, sp, a.shape[-1])
        qa, ka, qc, kc, va, vc, xb = _run(_front_part(xp, mod4, ng3, w_front, gains3, None, kv_dtype=F32, **ctx_mod))
        new_caches.append((ka, va, kc, vc))
        oa = _attn_a(sink_a, r3(qa), r3(ka), r3(va), None, None, layer=l, tq=sp, nsub=1, banded=False)
        ob, st = _run(_lru_part(r3(xb), *lru_p, zeros_h0, layer=l, h0_layer=0, tc=sp))
        new_states.append(st)
        oc = _attn_c(lam_vecs, subln3, r3(qc), r3(kc), r3(vc), None, None, layer=l, tq=sp, ts=sp, lam_init=lam_init)
        (xp,) = _run(_tail_part(xp, mod4, ng3, flat(oa), flat(ob), flat(oc), wg, wm, wbr, wout, **ctx_mod))

        r3 = lambda a: a.reshape(bs, ss, a.shape[-1])
        qa, ka, qc, kc, va, vc, xb = _run(_front_part(xs, mod4, ng3, w_front, gains3, rope, kv_dtype=BF16, **lat_mod))
        oa = _attn_a(sink_a, r3(qa), r3(ka), r3(va), cka, cva, layer=l, tq=TQ_A, nsub=NSUB_A, banded=True)
        ob, _ = _run(_lru_part(r3(xb), *lru_p, state_lru, layer=l, h0_layer=l, tc=TC_LRU))
        oc = _attn_c(lam_vecs, subln3, r3(qc), r3(kc), r3(vc), ckc, cvc, layer=l, tq=TQ_C, ts=TS_C, lam_init=lam_init)
        (xs,) = _run(_tail_part(xs, mod4, ng3, flat(oa), flat(ob), flat(oc), wg, wm, wbr, wout, **lat_mod))

    ka_t, va_t, kc_t, vc = _assemble_caches(new_caches, bp)
    return (xp.reshape(bp, sp, D_MODEL), xs.reshape(bs, ss, D_MODEL),
            ka_t.reshape(bp, DEPTH, KV_A, HD, sp).transpose(0, 1, 4, 2, 3),
            va_t.reshape(bp, DEPTH, KV_A, HD, sp).transpose(0, 1, 4, 2, 3),
            kc_t.reshape(bp, DEPTH, H_C, 2, HD, sp).transpose(0, 1, 5, 2, 3, 4),
            vc.reshape(bp, DEPTH, sp, H_C, DV_C),
            jnp.stack(new_states, axis=1))
```

```python
import functools
import math

import jax
import jax.numpy as jnp
from jax import lax
from jax.experimental import pallas as pl
from jax.experimental.pallas import tpu as pltpu

F32 = jnp.float32
BF16 = jnp.bfloat16

D_MODEL = 1024
DEPTH = 2
GRID_W = 64
HD = 64
SCALE = 1.0 / math.sqrt(HD)
LOG2E = math.log2(math.e)
Q_SCALE = SCALE * LOG2E
H_A = 8
KV_A = 2
WINDOW = 128
D_A = H_A * HD
D_LRU = 512
LRU_BLOCKS = 8
LRU_BW = D_LRU // LRU_BLOCKS
CONV_W = 4
LRU_C = 8.0
H_C = 4
DV_C = 2 * HD
D_C = H_C * DV_C
ROPE_BASE = 10000.0
ROPE_FREQS = HD // 4
EPS = 1e-6
LANES = 128
SUBLANES = 8
VMEM_LIMIT = 56 * 1024 * 1024

_SECTIONS = (("qa", D_A), ("kava", 2 * KV_A * HD), ("ga", D_A), ("xb", D_LRU), ("gb", D_LRU),
             ("qc", H_C * 2 * HD), ("kc", H_C * 2 * HD), ("vc", D_C), ("gc", D_C), ("mg", 3 * D_MODEL))
W_BLOCK = 256


def _col_blocks(*names):
    start, spans = 0, {}
    for name, width in _SECTIONS:
        spans[name] = (start, start + width)
        start += width
    blocks = []
    for name in names:
        a, b = spans[name]
        assert a % W_BLOCK == 0 and b % W_BLOCK == 0
        blocks += range(a // W_BLOCK, b // W_BLOCK)
    return tuple(blocks)


FRONT_BLOCKS = _col_blocks("qa", "qc", "kc", "kava", "vc", "xb")
GATE_BLOCKS = _col_blocks("ga", "gb", "gc")
MERGE_BLOCKS = _col_blocks("mg")
N_NORM = D_A + KV_A * HD + 2 * H_C * 2 * HD
N_PLAIN = KV_A * HD + D_C + D_LRU

TM_FRONT = 512
TM_TAIL = 512
TQ_A = 2 * WINDOW
NSUB_A = 4
TQ_C = 512
TS_C = 512
TC_LRU = 512


def _params(sem, vmem=VMEM_LIMIT):
    return pltpu.CompilerParams(dimension_semantics=sem, vmem_limit_bytes=vmem)


def _dot(a, b):
    return jnp.dot(a, b, preferred_element_type=F32)


def _dot_nt(a, b):
    return lax.dot_general(a, b, (((1,), (1,)), ((), ())), preferred_element_type=F32)


def _sigmoid(x):
    return 0.5 * jnp.tanh(0.5 * x) + 0.5


def _permute_cast_kernel(perm_ref, w_ref, o_ref):
    o_ref[...] = w_ref[...].astype(o_ref.dtype)


def _permute_cast(w, blocks):
    nl, k, _ = w.shape
    grid_spec = pltpu.PrefetchScalarGridSpec(
        num_scalar_prefetch=1, grid=(len(blocks),),
        in_specs=[pl.BlockSpec((nl, k, W_BLOCK), lambda j, perm: (0, 0, perm[j]))],
        out_specs=pl.BlockSpec((nl, k, W_BLOCK), lambda j, perm: (0, 0, j)))
    return pl.pallas_call(
        _permute_cast_kernel, grid_spec=grid_spec,
        out_shape=jax.ShapeDtypeStruct((nl, k, len(blocks) * W_BLOCK), BF16),
        compiler_params=_params(("arbitrary",)), name="permute_cast",
    )(jnp.asarray(blocks, jnp.int32), w)


def _mod_kernel(c_ref, w_ref, b_ref, o_ref):
    c = c_ref[...]
    a = c * _sigmoid(c)
    w = w_ref[...]
    a_hi = a.astype(BF16)
    a_lo = (a - a_hi.astype(F32)).astype(BF16)
    w_hi = w.astype(BF16)
    w_lo = (w - w_hi.astype(F32)).astype(BF16)
    o_ref[...] = _dot(a_hi, w_hi) + _dot(a_hi, w_lo) + _dot(a_lo, w_hi) + b_ref[...]


def _modulation(cvecs, mod_w, mod_b):
    tn = 768
    return pl.pallas_call(
        _mod_kernel,
        out_shape=jax.ShapeDtypeStruct((DEPTH, SUBLANES, 3 * D_MODEL), F32),
        grid=(DEPTH, 3 * D_MODEL // tn),
        in_specs=[pl.BlockSpec((SUBLANES, D_MODEL), lambda l, j: (0, 0)),
                  pl.BlockSpec((None, D_MODEL, tn), lambda l, j: (l, 0, j)),
                  pl.BlockSpec((None, 1, tn), lambda l, j: (l, 0, j))],
        out_specs=pl.BlockSpec((None, SUBLANES, tn), lambda l, j: (l, 0, j)),
        compiler_params=_params(("arbitrary", "arbitrary")),
        name="modulation",
    )(cvecs, mod_w, mod_b.reshape(DEPTH, 1, 3 * D_MODEL))


def _normed_input(x, mod, ng):
    ms = jnp.mean(x * x, axis=-1, keepdims=True)
    shift = mod[:, 0:D_MODEL]
    scale = mod[:, D_MODEL:2 * D_MODEL]
    return (x * lax.rsqrt(ms + EPS) * ng) * (1.0 + scale) + shift


def _layer_spec(shape, layer):
    zeros = (0,) * len(shape)
    return pl.BlockSpec((None,) + tuple(shape), lambda *_: (layer,) + zeros)


def _mod_spec(layer, row0, tiles_per_row):
    return pl.BlockSpec((None, None, 1, 3 * D_MODEL), lambda i: (layer, row0 + i // tiles_per_row, 0, 0))


class _Part:
    def __init__(self, name, body, grid, in_specs, args, out_shape, out_specs, scratch=()):
        self.name, self.body, self.grid = name, body, tuple(grid)
        self.in_specs, self.args = list(in_specs), list(args)
        self.out_shape, self.out_specs, self.scratch = list(out_shape), list(out_specs), list(scratch)


def _run(part):
    ni, no = len(part.args), len(part.out_shape)

    def kern(*refs):
        ids = tuple(pl.program_id(k) for k in range(len(part.grid)))
        part.body(refs[:ni], refs[ni:ni + no], refs[ni + no:], ids)

    return pl.pallas_call(
        kern, out_shape=part.out_shape, grid=part.grid, in_specs=part.in_specs, out_specs=part.out_specs,
        scratch_shapes=part.scratch, compiler_params=_params(("arbitrary",) * len(part.grid)), name=part.name)(*part.args)


def _front_body(ins, outs, scratch, ids, *, use_rope, n_prev=0, seq=None):
    x_ref, mod_ref, ng_ref, w_ref, gain_ref = ins[:5]
    qa_ref, ka_ref, qc_ref, kc_ref, va_ref, vc_ref, xb_ref = outs[:7]
    x = x_ref[...]
    tm = x.shape[0]
    hb = _normed_input(x, mod_ref[...], ng_ref[...]).astype(BF16)
    p1 = _dot(hb, w_ref[:, :N_NORM])
    lane = lax.broadcasted_iota(jnp.int32, (tm, LANES), 1)
    first_half = (lane & ROPE_FREQS) == 0
    wide = 2 * LANES
    same_head = (lax.broadcasted_iota(jnp.int32, (wide, wide), 0) // HD
                 == lax.broadcasted_iota(jnp.int32, (wide, wide), 1) // HD)
    ones_bd = jnp.where(same_head, 1.0, 0.0).astype(BF16)
    sums = []
    for c0 in range(0, N_NORM, wide):
        w = min(wide, N_NORM - c0)
        sq = p1[:, c0:c0 + w] * p1[:, c0:c0 + w]
        sq_hi = sq.astype(BF16)
        sq_lo = (sq - sq_hi.astype(F32)).astype(BF16)
        sums.append(_dot(sq_hi, ones_bd[:w, :w]) + _dot(sq_lo, ones_bd[:w, :w]))
    dests = ([(qa_ref, c) for c in range(4)] + [(qc_ref, c) for c in range(4)]
             + [(kc_ref, c) for c in range(4)] + [(ka_ref, 0)])
    for c, (o_ref, oc) in enumerate(dests):
        pc = p1[:, c * LANES:(c + 1) * LANES]
        msq = sums[c // 2][:, (c % 2) * LANES:(c % 2 + 1) * LANES] * (1.0 / HD)
        y = pc * lax.rsqrt(msq + EPS) * gain_ref[:, c * LANES:(c + 1) * LANES]
        if use_rope:
            cos_ref, sin_ref = ins[5:7]
            partner = jnp.where(first_half, pltpu.roll(y, LANES - ROPE_FREQS, 1),
                                pltpu.roll(y, ROPE_FREQS, 1))
            y = y * cos_ref[...] + partner * sin_ref[...]
        o_ref[:, oc * LANES:(oc + 1) * LANES] = y.astype(o_ref.dtype)
    p2 = _dot(hb, w_ref[:, N_NORM:])
    va_ref[...] = p2[:, 0:KV_A * HD].astype(va_ref.dtype)
    vc_ref[...] = p2[:, KV_A * HD:KV_A * HD + D_C].astype(vc_ref.dtype)
    xb_ref[...] = p2[:, KV_A * HD + D_C:]
    if seq is not None:
        prev = ins[len(ins) - 4 * n_prev:]
        layers = [prev[4 * l:4 * l + 4] for l in range(n_prev)] + [(ka_ref, va_ref, kc_ref, vc_ref)]
        for a, st_ref in enumerate(outs[7:]):
            for l, arrays in enumerate(layers):
                for s in range(tm // seq):
                    rows = arrays[a][s * seq:(s + 1) * seq, :]
                    st_ref[s, l] = rows.T if a < 3 else rows


def _front_part(x2d, mod4, ng3, w_front, gains3, rope, *, layer, mod_row0, tokens_per_mod, kv_dtype, tm=TM_FRONT,
                stack_seq=None, prev_caches=()):
    n = x2d.shape[0]
    use_rope = rope is not None
    row = lambda i: (i, 0)
    in_specs = [pl.BlockSpec((tm, D_MODEL), row),
                _mod_spec(layer, mod_row0, tokens_per_mod // tm),
                _layer_spec((1, D_MODEL), layer),
                _layer_spec((D_MODEL, N_NORM + N_PLAIN), layer),
                _layer_spec((1, N_NORM), layer)]
    args = [x2d, mod4, ng3, w_front, gains3]
    if use_rope:
        per_seq = rope[0].shape[0] // tm
        in_specs += [pl.BlockSpec((tm, LANES), lambda i: (i % per_seq, 0))] * 2
        args += list(rope)
    widths = (D_A, KV_A * HD, H_C * 2 * HD, H_C * 2 * HD, KV_A * HD, D_C, D_LRU)
    dtypes = (BF16, kv_dtype, BF16, kv_dtype, kv_dtype, kv_dtype, F32)
    out_shape = [jax.ShapeDtypeStruct((n, w), d) for w, d in zip(widths, dtypes)]
    out_specs = [pl.BlockSpec((tm, w), row) for w in widths]
    if stack_seq is not None:
        assert len(prev_caches) == DEPTH - 1 and tm % stack_seq == 0
        for cache in prev_caches:
            in_specs += [pl.BlockSpec((tm, arr.shape[1]), row) for arr in cache]
            args += list(cache)
        per_tile = tm // stack_seq
        cache_w = (KV_A * HD, KV_A * HD, H_C * 2 * HD, D_C)
        shapes = [(DEPTH, w, stack_seq) for w in cache_w[:3]] + [(DEPTH, stack_seq, cache_w[3])]
        out_shape += [jax.ShapeDtypeStruct((n // stack_seq,) + s, F32) for s in shapes]
        out_specs += [pl.BlockSpec((per_tile,) + s, lambda i: (i, 0, 0, 0)) for s in shapes]
    body = functools.partial(_front_body, use_rope=use_rope, n_prev=len(prev_caches), seq=stack_seq)
    return _Part("front_rope" if use_rope else "front", body, (n // tm,), in_specs, args, out_shape, out_specs)


def _attn_a_kernel(*refs, tq, nsub, seq_len, banded, has_ctx, layer):
    if has_ctx:
        sink_ref, q_ref, k_ref, v_ref, kc_ref, vc_ref, o_ref = refs
    else:
        sink_ref, q_ref, k_ref, v_ref, o_ref = refs
    i = pl.program_id(1)
    g = H_A // KV_A

    def spread(k, v):
        k = k.astype(F32)
        v = v.astype(F32)
        lo = lax.broadcasted_iota(jnp.int32, k.shape, 1) < HD
        k_sw = pltpu.roll(k, HD, 1)
        v_sw = pltpu.roll(v, HD, 1)
        kd = (jnp.where(lo, k, k_sw).astype(BF16), jnp.where(lo, k_sw, k).astype(BF16))
        vd = ((jnp.where(lo, v, 1.0).astype(BF16), jnp.where(lo, 1.0, v_sw).astype(BF16)),
              (jnp.where(lo, v_sw, 1.0).astype(BF16), jnp.where(lo, 1.0, v).astype(BF16)))
        return kd, vd

    if has_ctx:
        kd_c, vd_c = spread(kc_ref[...], vc_ref[...])
    nwin = tq + 2 * WINDOW if banded else seq_len
    lo_q = lax.broadcasted_iota(jnp.int32, (tq, LANES), 1) < HD
    zero = jnp.zeros((tq, LANES), BF16)
    for u in range(nsub):
        blk = i * nsub + u
        if banded:
            start = pl.multiple_of(jnp.clip(blk * tq - WINDOW, 0, seq_len - nwin), WINDOW)
            kd, vd = spread(k_ref[pl.ds(start, nwin), :], v_ref[pl.ds(start, nwin), :])
            kpos = start + lax.broadcasted_iota(jnp.int32, (tq, nwin), 1)
            qpos = blk * tq + lax.broadcasted_iota(jnp.int32, (tq, nwin), 0)
            bias = jnp.where(jnp.abs(kpos - qpos) <= WINDOW, 0.0, -jnp.inf)
        else:
            kd, vd = spread(k_ref[...], v_ref[...])
        if has_ctx:
            kd = tuple(jnp.concatenate([kd[kv], kd_c[kv]], axis=0) for kv in range(KV_A))
            vd = tuple(tuple(jnp.concatenate([vd[kv][half], vd_c[kv][half]], axis=0) for half in range(2))
                       for kv in range(KV_A))
        q = q_ref[u * tq:(u + 1) * tq, :]
        outs = []
        for h in range(H_A):
            kv, half = h // g, h % 2
            qz = jnp.where(lo_q if half == 0 else jnp.logical_not(lo_q),
                           q[:, (h // 2) * LANES:(h // 2 + 1) * LANES], zero)
            s = _dot_nt(qz, kd[kv])
            if banded:
                s = jnp.concatenate([s[:, :nwin] + bias, s[:, nwin:]], axis=1)
            snk = sink_ref[layer, h] * LOG2E
            m = jnp.maximum(jnp.max(s, axis=-1, keepdims=True), snk)
            e = jnp.exp2(s - m)
            pv = _dot(e.astype(BF16), vd[kv][half])
            outs.append(pv / (pltpu.roll(pv, HD, 1) + jnp.exp2(snk - m)))
        for c in range(H_A // 2):
            pair = jnp.where(lo_q, outs[2 * c], outs[2 * c + 1])
            o_ref[u * tq:(u + 1) * tq, c * LANES:(c + 1) * LANES] = pair.astype(o_ref.dtype)


def _attn_a(sink, q, k, v, k_ctx, v_ctx, *, layer, tq, nsub, banded):
    b, t, _ = q.shape
    has_ctx = k_ctx is not None
    rows = tq * nsub
    kv_spec = pl.BlockSpec((None, t, KV_A * HD), lambda bi, i: (bi, 0, 0))
    in_specs = [pl.BlockSpec(memory_space=pltpu.SMEM),
                pl.BlockSpec((None, rows, D_A), lambda bi, i: (bi, i, 0)), kv_spec, kv_spec]
    args = [sink, q, k, v]
    if has_ctx:
        s = k_ctx.shape[2]
        in_specs += [pl.BlockSpec((None, None, s, KV_A * HD), lambda bi, i: (bi, layer, 0, 0))] * 2
        args += [k_ctx, v_ctx]
    return pl.pallas_call(
        functools.partial(_attn_a_kernel, tq=tq, nsub=nsub, seq_len=t, banded=banded, has_ctx=has_ctx, layer=layer),
        out_shape=jax.ShapeDtypeStruct((b, t, D_A), BF16),
        grid=(b, t // rows),
        in_specs=in_specs,
        out_specs=pl.BlockSpec((None, rows, D_A), lambda bi, i: (bi, i, 0)),
        compiler_params=_params(("arbitrary", "arbitrary")),
        name="attn_a_latent" if has_ctx else "attn_a_context",
    )(*args)


def _attn_c_kernel(*refs, tq, ts, n_loc, has_ctx, lam_init):
    if has_ctx:
        lq1, lk1, lq2, lk2, sub_ref, q_ref, k_ref, v_ref, kc_ref, vc_ref, o_ref = refs
    else:
        lq1, lk1, lq2, lk2, sub_ref, q_ref, k_ref, v_ref, o_ref = refs
    lam = (jnp.exp(jnp.sum(lq1[...] * lk1[...], axis=-1, keepdims=True))
           - jnp.exp(jnp.sum(lq2[...] * lk2[...], axis=-1, keepdims=True)) + lam_init)
    lo = lax.broadcasted_iota(jnp.int32, (tq, LANES), 1) < HD
    zero = jnp.zeros((tq, LANES), BF16)
    heads = [slice(h * LANES, (h + 1) * LANES) for h in range(H_C)]
    qz = []
    for cols in heads:
        q12 = q_ref[:, cols]
        qz.append(jnp.concatenate([jnp.where(lo, q12, zero), jnp.where(lo, zero, q12)], axis=0))

    def update(carry, kt, vt):
        new = []
        for h, cols in enumerate(heads):
            m, l, acc = carry[h]
            s = _dot_nt(qz[h], kt[:, cols].astype(BF16))
            m_new = jnp.maximum(m, jnp.max(s, axis=-1, keepdims=True))
            alpha = jnp.exp2(m - m_new)
            p = jnp.exp2(s - m_new)
            psum = p[:, 0:LANES]
            for j in range(1, p.shape[1] // LANES):
                psum = psum + p[:, j * LANES:(j + 1) * LANES]
            l = alpha * l + psum
            acc = alpha * acc + _dot(p.astype(BF16), vt[:, cols].astype(BF16))
            new.append((m_new, l, acc))
        return tuple(new)

    carry = tuple((jnp.full((2 * tq, 1), -jnp.inf, F32), jnp.zeros((2 * tq, LANES), F32),
                   jnp.zeros((2 * tq, LANES), F32)) for _ in heads)
    for j in range(n_loc):
        carry = update(carry, k_ref[j * ts:(j + 1) * ts, :], v_ref[j * ts:(j + 1) * ts, :])
    if has_ctx:
        carry = update(carry, kc_ref[...], vc_ref[...])
    for h, cols in enumerate(heads):
        _, l, acc = carry[h]
        l = jnp.sum(l, axis=-1, keepdims=True)
        o = acc[:tq] / l[:tq] - lam * (acc[tq:] / l[tq:])
        o = o * lax.rsqrt(jnp.mean(o * o, axis=-1, keepdims=True) + EPS) * sub_ref[...] * (1.0 - lam_init)
        o_ref[:, cols] = o.astype(o_ref.dtype)


def _attn_c(lam_vecs, subln3, q, k, v, k_ctx, v_ctx, *, layer, tq, ts, lam_init):
    b, t, _ = q.shape
    s_loc = k.shape[1]
    has_ctx = k_ctx is not None
    once = dict(pipeline_mode=pl.Buffered(1)) if t // tq > 1 else {}
    kv_spec = pl.BlockSpec((None, s_loc, D_C), lambda bi, i: (bi, 0, 0), **once)
    in_specs = ([_layer_spec((1, HD), layer)] * 4 + [_layer_spec((1, DV_C), layer)]
                + [pl.BlockSpec((None, tq, D_C), lambda bi, i: (bi, i, 0)), kv_spec, kv_spec])
    args = list(lam_vecs) + [subln3, q, k, v]
    if has_ctx:
        sc = k_ctx.shape[2]
        in_specs += [pl.BlockSpec((None, None, sc, D_C), lambda bi, i: (bi, layer, 0, 0), **once)] * 2
        args += [k_ctx, v_ctx]
    return pl.pallas_call(
        functools.partial(_attn_c_kernel, tq=tq, ts=ts, n_loc=s_loc // ts, has_ctx=has_ctx, lam_init=lam_init),
        out_shape=jax.ShapeDtypeStruct((b, t, D_C), BF16),
        grid=(b, t // tq),
        in_specs=in_specs,
        out_specs=pl.BlockSpec((None, tq, D_C), lambda bi, i: (bi, i, 0)),
        compiler_params=_params(("arbitrary", "arbitrary")),
        name="attn_c_latent" if has_ctx else "attn_c_context",
    )(*args)


def _lru_body(ins, outs, scratch, ids, *, tc, nc):
    xc_ref, xp_ref, xn_ref, cw_ref, cb_ref, wg_ref, bg_ref, lam_ref, h0_ref, perm_ref, inv_ref = ins
    ob_ref, st_ref = outs
    a_s, b_s, hs, ps, hf_s, xb_s, hcar, last_h, last_p, cin = scratch
    _, p, c = ids
    cidx = c + p * (nc - 1 - 2 * c)
    r0 = pl.multiple_of(cidx * tc, tc)
    ng = tc // SUBLANES

    @pl.when(p == 0)
    def _():
        cur = xc_ref[...]
        prev = jnp.where(cidx > 0, xp_ref[...], 0.0)
        nxt = jnp.where(cidx < nc - 1, xn_ref[...], 0.0)
        hi = cur.astype(BF16)
        rest = cur - hi.astype(F32)
        mid = rest.astype(BF16)
        lo = (rest - mid.astype(F32)).astype(BF16)
        x3 = (_dot(perm_ref[...], hi) + _dot(perm_ref[...], mid) + _dot(perm_ref[...], lo)).reshape(ng, SUBLANES, D_LRU)
        rows = lax.broadcasted_iota(jnp.int32, (SUBLANES, D_LRU), 0)
        before1 = jnp.where(rows == 0, prev[SUBLANES - 1:SUBLANES, :], pltpu.roll(x3[ng - 1], 1, 0))
        before2 = jnp.where(rows == 0, prev[SUBLANES - 2:SUBLANES - 1, :], pltpu.roll(x3[ng - 2], 1, 0))
        after1 = jnp.where(rows == SUBLANES - 1, nxt[0:1, :], pltpu.roll(x3[0], SUBLANES - 1, 0))
        xpad = jnp.concatenate([before2[None], before1[None], x3, after1[None]], axis=0)
        acc = jnp.broadcast_to(cb_ref[...], (ng, SUBLANES, D_LRU))
        for j in range(CONV_W):
            acc = acc + xpad[j:j + ng] * cw_ref[j:j + 1, :]
        xb_s[pl.ds(r0, tc), :] = acc.reshape(tc, D_LRU)

    xb = xb_s[pl.ds(r0, tc), :]
    gm = _dot(xb.astype(BF16), wg_ref[...]) + bg_ref[...]
    r = _sigmoid(gm[:, :D_LRU])
    ig = _sigmoid(gm[:, D_LRU:])
    nl = -lam_ref[...]
    softplus = jnp.maximum(nl, 0.0) + jnp.log1p(jnp.exp(-jnp.abs(nl)))
    decay = LRU_C * softplus
    a = jnp.exp2(r * (decay * (-LOG2E)))
    one_m_a2 = jnp.tanh(r * decay) * (a * a + 1.0)
    bb = jnp.where(one_m_a2 == 0.0, 0.0, one_m_a2 * lax.rsqrt(one_m_a2)) * (ig * xb)

    a_s[...] = a
    b_s[...] = bb

    def step(gi, carry):
        h, pc = carry
        g = gi + p * (ng - 1 - 2 * gi)
        base = pl.multiple_of(g * SUBLANES, SUBLANES)
        a_g = a_s[pl.ds(base, SUBLANES), :]
        h = a_g * h + b_s[pl.ds(base, SUBLANES), :]
        pc = a_g * pc
        hs[pl.ds(base, SUBLANES), :] = h
        ps[pl.ds(base, SUBLANES), :] = pc
        return h, pc

    init = (jnp.zeros((SUBLANES, D_LRU), F32), jnp.ones((SUBLANES, D_LRU), F32))
    last_h[...], last_p[...] = lax.fori_loop(0, ng, step, init, unroll=8)

    @pl.when(c == 0)
    def _():
        hcar[0:1, :] = h0_ref[pl.ds(p, 1), :]

    state = hcar[0:1, :]
    for rr in range(SUBLANES):
        r_in = rr + p * (SUBLANES - 1 - 2 * rr)
        cin[pl.ds(r_in, 1), :] = state
        state = last_p[pl.ds(r_in, 1), :] * state + last_h[pl.ds(r_in, 1), :]
    hcar[0:1, :] = state
    hfull = (hs[...].reshape(ng, SUBLANES, D_LRU)
             + ps[...].reshape(ng, SUBLANES, D_LRU) * cin[...]).reshape(tc, D_LRU)

    @pl.when(p == 0)
    def _():
        hf_s[pl.ds(r0, tc), :] = hfull

    @pl.when(p == 1)
    def _():
        both = (hf_s[pl.ds(r0, tc), :] + hfull).astype(BF16)
        ob_ref[...] = _dot(inv_ref[...], both).astype(ob_ref.dtype)

    @pl.when(c == nc - 1)
    def _():
        st_ref[pl.ds(p, 1), :] = state


def _lru_part(xb, conv_w, conv_b3, wg, bg, lam4, h0, *, layer, h0_layer, tc):
    b, t, _ = xb.shape
    nc = t // tc
    nb8 = t // SUBLANES
    per = tc // SUBLANES
    cidx = lambda p, c: c + p * (nc - 1 - 2 * c)
    in_specs = [
        pl.BlockSpec((None, tc, D_LRU), lambda bi, p, c: (bi, cidx(p, c), 0)),
        pl.BlockSpec((None, SUBLANES, D_LRU), lambda bi, p, c: (bi, jnp.maximum(cidx(p, c) * per - 1, 0), 0)),
        pl.BlockSpec((None, SUBLANES, D_LRU), lambda bi, p, c: (bi, jnp.minimum((cidx(p, c) + 1) * per, nb8 - 1), 0)),
        _layer_spec((CONV_W, D_LRU), layer),
        _layer_spec((1, D_LRU), layer),
        pl.BlockSpec((None, None, D_LRU, 2 * D_LRU), lambda bi, p, c: (layer, p, 0, 0)),
        pl.BlockSpec((None, None, 1, 2 * D_LRU), lambda bi, p, c: (layer, p, 0, 0)),
        pl.BlockSpec((None, None, 1, D_LRU), lambda bi, p, c: (layer, p, 0, 0)),
        pl.BlockSpec((None, None, 2, D_LRU), lambda bi, p, c: (bi, h0_layer, 0, 0)),
        pl.BlockSpec((tc, tc), lambda bi, p, c: (0, 0)),
        pl.BlockSpec((tc, tc), lambda bi, p, c: (0, 0)),
    ]
    j = jnp.arange(tc)
    src = (j % SUBLANES) * (tc // SUBLANES) + j // SUBLANES
    perm = (src[:, None] == j[None, :]).astype(BF16)
    chunk = (tc, D_LRU)
    row8 = (SUBLANES, D_LRU)
    return _Part("lru", functools.partial(_lru_body, tc=tc, nc=nc), (b, 2, nc), in_specs,
                 [xb, xb, xb, conv_w, conv_b3, wg, bg, lam4, h0, perm, perm.T],
                 [jax.ShapeDtypeStruct((b, t, D_LRU), BF16), jax.ShapeDtypeStruct((b, 2, D_LRU), F32)],
                 [pl.BlockSpec((None, tc, D_LRU), lambda bi, p, c: (bi, nc - 1 - p * c, 0)),
                  pl.BlockSpec((None, 2, D_LRU), lambda bi, p, c: (bi, 0, 0))],
                 [pltpu.VMEM(chunk, F32), pltpu.VMEM(chunk, F32), pltpu.VMEM(chunk, F32), pltpu.VMEM(chunk, F32),
                  pltpu.VMEM((t, D_LRU), F32), pltpu.VMEM((t, D_LRU), F32),
                  pltpu.VMEM(row8, F32), pltpu.VMEM(row8, F32), pltpu.VMEM(row8, F32), pltpu.VMEM(row8, F32)])


def _tail_kernel(x_ref, mod_ref, ng_ref, oa_ref, ob_ref, oc_ref, wg_ref, wm_ref, wbr_ref, wout_ref, o_ref):
    x = x_ref[...]
    mod = mod_ref[...]
    hb = _normed_input(x, mod, ng_ref[...]).astype(BF16)
    y = None
    for br, br_ref in enumerate((oa_ref, ob_ref, oc_ref)):
        gt = _dot(hb, wg_ref[:, br * D_A:(br + 1) * D_A])
        u = (br_ref[...].astype(F32) * (gt * _sigmoid(gt))).astype(BF16)
        proj = _dot(u, wbr_ref[br])
        mg = _sigmoid(_dot(hb, wm_ref[:, br * D_MODEL:(br + 1) * D_MODEL]))
        y = mg * proj if y is None else y + mg * proj
    gate = mod[:, 2 * D_MODEL:]
    o_ref[...] = x + gate * _dot(y.astype(BF16), wout_ref[...])


def _tail_part(x2d, mod4, ng3, oa, ob, oc, wg, wm, wbr, wout, *, layer, mod_row0, tokens_per_mod, tm=TM_TAIL):
    n = x2d.shape[0]
    row = lambda i: (i, 0)
    once = dict(pipeline_mode=pl.Buffered(1))

    def weight(shape):
        zeros = (0,) * len(shape)
        return pl.BlockSpec((None,) + shape, lambda i: (layer,) + zeros, **once)

    in_specs = [pl.BlockSpec((tm, D_MODEL), row),
                _mod_spec(layer, mod_row0, tokens_per_mod // tm),
                _layer_spec((1, D_MODEL), layer),
                pl.BlockSpec((tm, D_A), row),
                pl.BlockSpec((tm, D_LRU), row),
                pl.BlockSpec((tm, D_C), row),
                weight((D_MODEL, 3 * D_A)),
                weight((D_MODEL, 3 * D_MODEL)),
                weight((3, D_A, D_MODEL)),
                weight((D_MODEL, D_MODEL))]
    return _Part("tail", lambda ins, outs, scratch, ids: _tail_kernel(*ins, *outs), (n // tm,), in_specs,
                 [x2d, mod4, ng3, oa, ob, oc, wg, wm, wbr, wout],
                 [jax.ShapeDtypeStruct((n, D_MODEL), F32)], [pl.BlockSpec((tm, D_MODEL), row)])


def _rope_tables(seq_len):
    pos = jnp.arange(seq_len)
    row = (pos // GRID_W).astype(F32)
    col = (pos % GRID_W).astype(F32)
    inv = jnp.power(ROPE_BASE, -jnp.arange(ROPE_FREQS, dtype=F32) / ROPE_FREQS)
    ang_r = row[:, None] * inv
    ang_c = col[:, None] * inv
    cos = jnp.concatenate([jnp.cos(ang_r)] * 2 + [jnp.cos(ang_c)] * 2, axis=-1)
    sin = jnp.concatenate([-jnp.sin(ang_r), jnp.sin(ang_r), -jnp.sin(ang_c), jnp.sin(ang_c)], axis=-1)
    return jnp.tile(cos, (1, LANES // HD)), jnp.tile(sin, (1, LANES // HD))


def _block_diag(w):
    rows = w.reshape(w.shape[:-3] + (D_LRU, LRU_BW))
    tiled = jnp.tile(rows, (1,) * (rows.ndim - 1) + (LRU_BLOCKS,))
    blk = jnp.arange(D_LRU) // LRU_BW
    return jnp.where(blk[:, None] == blk[None, :], tiled, 0.0)


def kernel(x_prompt, x_sample, cache_a_k, cache_a_v, cache_c_k, cache_c_v, state_lru, c, c_ctx, norm_g, mod_w, mod_b, w_in, qn_a, kn_a, sink_a, conv_w, conv_b, lru_wa, lru_ba, lru_wx, lru_bx, lru_lam, qn_c, kn_c, lam_q1, lam_k1, lam_q2, lam_k2, subln_c, w_br_a, w_br_b, w_br_c, w_out):
    bp, sp, _ = x_prompt.shape
    bs, ss, _ = x_sample.shape
    past = cache_a_k.shape[2]

    cvecs = jnp.concatenate([c, c_ctx[None, :], jnp.zeros((SUBLANES - bs - 1, D_MODEL), F32)], axis=0)
    mod = _modulation(cvecs, mod_w, mod_b)
    mod4 = mod.reshape(DEPTH, SUBLANES, 1, 3 * D_MODEL)
    rope = _rope_tables(ss)

    w_front = _permute_cast(w_in, FRONT_BLOCKS)
    wg = _permute_cast(w_in, GATE_BLOCKS)
    wm = _permute_cast(w_in, MERGE_BLOCKS)
    wbr = jnp.stack([w_br_a, w_br_b, w_br_c], axis=1).astype(BF16)
    wout = w_out.astype(BF16)
    gains3 = jnp.concatenate([jnp.tile(qn_a * Q_SCALE, (1, H_A)), jnp.tile(qn_c * Q_SCALE, (1, 2 * H_C)),
                              jnp.tile(kn_c, (1, 2 * H_C)), jnp.tile(kn_a, (1, KV_A))],
                             axis=-1)[:, None, :]
    ng3 = norm_g[:, None, :]
    lru_wg = jnp.concatenate([_block_diag(lru_wa), _block_diag(lru_wx)], axis=-1).astype(BF16)
    lru_bg = jnp.concatenate([lru_ba, lru_bx], axis=-1)[:, :, None, :]
    lru_lam4 = lru_lam[:, :, None, :]
    lru_p = (conv_w, conv_b[:, None, :], lru_wg, lru_bg, lru_lam4)
    lam_vecs = [v[:, None, :] for v in (lam_q1, lam_k1, lam_q2, lam_k2)]
    subln3 = subln_c[:, None, :]
    cka = cache_a_k.reshape(bs, DEPTH, past, KV_A * HD)
    cva = cache_a_v.reshape(bs, DEPTH, past, KV_A * HD)
    ckc = cache_c_k.reshape(bs, DEPTH, past, D_C)
    cvc = cache_c_v.reshape(bs, DEPTH, past, D_C)
    zeros_h0 = jnp.zeros((bp, 1, 2, D_LRU), F32)

    xp = x_prompt.reshape(bp * sp, D_MODEL)
    xs = x_sample.reshape(bs * ss, D_MODEL)
    new_caches, new_states = [], []
    for l in range(DEPTH):
        lam_init = 0.8 - 0.6 * math.exp(-0.3 * l)
        ctx_mod = dict(layer=l, mod_row0=bs, tokens_per_mod=bp * sp)
        lat_mod = dict(layer=l, mod_row0=0, tokens_per_mod=ss)

        flat = lambda a: a.reshape(-1, a.shape[-1])
        r3 = lambda a: a.reshape(bp, sp, a.shape[-1])
        stacking = dict(stack_seq=sp, prev_caches=new_caches) if l == DEPTH - 1 else {}
        qa, ka, qc, kc, va, vc, xb, *stacked = _run(
            _front_part(xp, mod4, ng3, w_front, gains3, None, kv_dtype=F32, **ctx_mod, **stacking))
        new_caches.append((ka, va, kc, vc))
        oa = _attn_a(sink_a, r3(qa), r3(ka), r3(va), None, None, layer=l, tq=sp, nsub=1, banded=False)
        ob, st = _run(_lru_part(r3(xb), *lru_p, zeros_h0, layer=l, h0_layer=0, tc=sp))
        new_states.append(st)
        oc = _attn_c(lam_vecs, subln3, r3(qc), r3(kc), r3(vc), None, None, layer=l, tq=sp, ts=sp, lam_init=lam_init)
        (xp,) = _run(_tail_part(xp, mod4, ng3, flat(oa), flat(ob), flat(oc), wg, wm, wbr, wout, **ctx_mod))

        r3 = lambda a: a.reshape(bs, ss, a.shape[-1])
        qa, ka, qc, kc, va, vc, xb = _run(_front_part(xs, mod4, ng3, w_front, gains3, rope, kv_dtype=BF16, **lat_mod))
        oa = _attn_a(sink_a, r3(qa), r3(ka), r3(va), cka, cva, layer=l, tq=TQ_A, nsub=NSUB_A, banded=True)
        ob, _ = _run(_lru_part(r3(xb), *lru_p, state_lru, layer=l, h0_layer=l, tc=TC_LRU))
        oc = _attn_c(lam_vecs, subln3, r3(qc), r3(kc), r3(vc), ckc, cvc, layer=l, tq=TQ_C, ts=TS_C, lam_init=lam_init)
        (xs,) = _run(_tail_part(xs, mod4, ng3, flat(oa), flat(ob), flat(oc), wg, wm, wbr, wout, **lat_mod))

    ka_t, va_t, kc_t, vc = stacked
    return (xp.reshape(bp, sp, D_MODEL), xs.reshape(bs, ss, D_MODEL),
            ka_t.reshape(bp, DEPTH, KV_A, HD, sp).transpose(0, 1, 4, 2, 3),
            va_t.reshape(bp, DEPTH, KV_A, HD, sp).transpose(0, 1, 4, 2, 3),
            kc_t.reshape(bp, DEPTH, H_C, 2, HD, sp).transpose(0, 1, 5, 2, 3, 4),
            vc.reshape(bp, DEPTH, sp, H_C, DV_C),
            jnp.stack(new_states, axis=1))
```

```python
import functools
import math

import jax
import jax.numpy as jnp
from jax import lax
from jax.experimental import pallas as pl
from jax.experimental.pallas import tpu as pltpu

F32 = jnp.float32
BF16 = jnp.bfloat16

D_MODEL = 1024
DEPTH = 2
GRID_W = 64
HD = 64
SCALE = 1.0 / math.sqrt(HD)
LOG2E = math.log2(math.e)
Q_SCALE = SCALE * LOG2E
H_A = 8
KV_A = 2
WINDOW = 128
D_A = H_A * HD
D_LRU = 512
LRU_BLOCKS = 8
LRU_BW = D_LRU // LRU_BLOCKS
CONV_W = 4
LRU_C = 8.0
H_C = 4
DV_C = 2 * HD
D_C = H_C * DV_C
ROPE_BASE = 10000.0
ROPE_FREQS = HD // 4
EPS = 1e-6
LANES = 128
SUBLANES = 8
VMEM_LIMIT = 56 * 1024 * 1024

_SECTIONS = (("qa", D_A), ("kava", 2 * KV_A * HD), ("ga", D_A), ("xb", D_LRU), ("gb", D_LRU),
             ("qc", H_C * 2 * HD), ("kc", H_C * 2 * HD), ("vc", D_C), ("gc", D_C),
             ("mg_a", D_MODEL), ("mg_b", D_MODEL), ("mg_c", D_MODEL))
W_BLOCK = 256


def _col_blocks(*names):
    start, spans = 0, {}
    for name, width in _SECTIONS:
        spans[name] = (start, start + width)
        start += width
    blocks = []
    for name in names:
        a, b = spans[name]
        assert a % W_BLOCK == 0 and b % W_BLOCK == 0
        blocks += range(a // W_BLOCK, b // W_BLOCK)
    return tuple(blocks)


FRONT_BLOCKS = _col_blocks("qa", "qc", "kc", "kava", "vc", "xb")
TAIL_BLOCKS = _col_blocks("ga", "mg_a", "gb", "mg_b", "gc", "mg_c")
N_BRANCH_COLS = D_A + D_MODEL
N_NORM = D_A + KV_A * HD + 2 * H_C * 2 * HD
N_PLAIN = KV_A * HD + D_C + D_LRU

TM_FRONT = 512
TM_TAIL = 512
TQ_A = 2 * WINDOW
NSUB_A = 4
TQ_C = 512
TS_C = 512
TC_LRU = 512


def _params(sem, vmem=VMEM_LIMIT):
    return pltpu.CompilerParams(dimension_semantics=sem, vmem_limit_bytes=vmem)


def _dot(a, b):
    return jnp.dot(a, b, preferred_element_type=F32)


def _dot_nt(a, b):
    return lax.dot_general(a, b, (((1,), (1,)), ((), ())), preferred_element_type=F32)


def _sigmoid(x):
    return 0.5 * jnp.tanh(0.5 * x) + 0.5


def _permute_cast_kernel(perm_ref, w_ref, o_ref):
    o_ref[...] = w_ref[...].astype(o_ref.dtype)


def _permute_cast(w, blocks):
    nl, k, _ = w.shape
    grid_spec = pltpu.PrefetchScalarGridSpec(
        num_scalar_prefetch=1, grid=(len(blocks),),
        in_specs=[pl.BlockSpec((nl, k, W_BLOCK), lambda j, perm: (0, 0, perm[j]))],
        out_specs=pl.BlockSpec((nl, k, W_BLOCK), lambda j, perm: (0, 0, j)))
    return pl.pallas_call(
        _permute_cast_kernel, grid_spec=grid_spec,
        out_shape=jax.ShapeDtypeStruct((nl, k, len(blocks) * W_BLOCK), BF16),
        compiler_params=_params(("arbitrary",)), name="permute_cast",
    )(jnp.asarray(blocks, jnp.int32), w)


def _mod_kernel(c_ref, w_ref, b_ref, o_ref):
    c = c_ref[...]
    a = c * _sigmoid(c)
    w = w_ref[...]
    a_hi = a.astype(BF16)
    a_lo = (a - a_hi.astype(F32)).astype(BF16)
    w_hi = w.astype(BF16)
    w_lo = (w - w_hi.astype(F32)).astype(BF16)
    o_ref[...] = _dot(a_hi, w_hi) + _dot(a_hi, w_lo) + _dot(a_lo, w_hi) + b_ref[...]


def _modulation(cvecs, mod_w, mod_b):
    tn = 768
    return pl.pallas_call(
        _mod_kernel,
        out_shape=jax.ShapeDtypeStruct((DEPTH, SUBLANES, 3 * D_MODEL), F32),
        grid=(DEPTH, 3 * D_MODEL // tn),
        in_specs=[pl.BlockSpec((SUBLANES, D_MODEL), lambda l, j: (0, 0)),
                  pl.BlockSpec((None, D_MODEL, tn), lambda l, j: (l, 0, j)),
                  pl.BlockSpec((None, 1, tn), lambda l, j: (l, 0, j))],
        out_specs=pl.BlockSpec((None, SUBLANES, tn), lambda l, j: (l, 0, j)),
        compiler_params=_params(("arbitrary", "arbitrary")),
        name="modulation",
    )(cvecs, mod_w, mod_b.reshape(DEPTH, 1, 3 * D_MODEL))


def _normed_input(x, mod, ng):
    ms = jnp.mean(x * x, axis=-1, keepdims=True)
    shift = mod[:, 0:D_MODEL]
    scale = mod[:, D_MODEL:2 * D_MODEL]
    return (x * lax.rsqrt(ms + EPS) * ng) * (1.0 + scale) + shift


def _layer_spec(shape, layer):
    zeros = (0,) * len(shape)
    return pl.BlockSpec((None,) + tuple(shape), lambda *_: (layer,) + zeros)


def _mod_spec(layer, row0, tiles_per_row):
    return pl.BlockSpec((None, None, 1, 3 * D_MODEL), lambda i: (layer, row0 + i // tiles_per_row, 0, 0))


class _Part:
    def __init__(self, name, body, grid, in_specs, args, out_shape, out_specs, scratch=()):
        self.name, self.body, self.grid = name, body, tuple(grid)
        self.in_specs, self.args = list(in_specs), list(args)
        self.out_shape, self.out_specs, self.scratch = list(out_shape), list(out_specs), list(scratch)


def _run(part):
    ni, no = len(part.args), len(part.out_shape)

    def kern(*refs):
        ids = tuple(pl.program_id(k) for k in range(len(part.grid)))
        part.body(refs[:ni], refs[ni:ni + no], refs[ni + no:], ids)

    return pl.pallas_call(
        kern, out_shape=part.out_shape, grid=part.grid, in_specs=part.in_specs, out_specs=part.out_specs,
        scratch_shapes=part.scratch, compiler_params=_params(("arbitrary",) * len(part.grid)), name=part.name)(*part.args)


def _front_body(ins, outs, scratch, ids, *, use_rope, n_prev=0, seq=None):
    x_ref, mod_ref, ng_ref, w_ref, gain_ref = ins[:5]
    qa_ref, ka_ref, qc_ref, kc_ref, va_ref, vc_ref, xb_ref = outs[:7]
    x = x_ref[...]
    tm = x.shape[0]
    hb = _normed_input(x, mod_ref[...], ng_ref[...]).astype(BF16)
    p1 = _dot(hb, w_ref[:, :N_NORM])
    lane = lax.broadcasted_iota(jnp.int32, (tm, LANES), 1)
    first_half = (lane & ROPE_FREQS) == 0
    wide = 2 * LANES
    same_head = (lax.broadcasted_iota(jnp.int32, (wide, wide), 0) // HD
                 == lax.broadcasted_iota(jnp.int32, (wide, wide), 1) // HD)
    ones_bd = jnp.where(same_head, 1.0, 0.0).astype(BF16)
    sums = []
    for c0 in range(0, N_NORM, wide):
        w = min(wide, N_NORM - c0)
        sq = p1[:, c0:c0 + w] * p1[:, c0:c0 + w]
        sq_hi = sq.astype(BF16)
        sq_lo = (sq - sq_hi.astype(F32)).astype(BF16)
        sums.append(_dot(sq_hi, ones_bd[:w, :w]) + _dot(sq_lo, ones_bd[:w, :w]))
    dests = ([(qa_ref, c) for c in range(4)] + [(qc_ref, c) for c in range(4)]
             + [(kc_ref, c) for c in range(4)] + [(ka_ref, 0)])
    for c, (o_ref, oc) in enumerate(dests):
        pc = p1[:, c * LANES:(c + 1) * LANES]
        msq = sums[c // 2][:, (c % 2) * LANES:(c % 2 + 1) * LANES] * (1.0 / HD)
        y = pc * lax.rsqrt(msq + EPS) * gain_ref[:, c * LANES:(c + 1) * LANES]
        if use_rope:
            cos_ref, sin_ref = ins[5:7]
            partner = jnp.where(first_half, pltpu.roll(y, LANES - ROPE_FREQS, 1),
                                pltpu.roll(y, ROPE_FREQS, 1))
            y = y * cos_ref[...] + partner * sin_ref[...]
        o_ref[:, oc * LANES:(oc + 1) * LANES] = y.astype(o_ref.dtype)
    p2 = _dot(hb, w_ref[:, N_NORM:])
    va_ref[...] = p2[:, 0:KV_A * HD].astype(va_ref.dtype)
    vc_ref[...] = p2[:, KV_A * HD:KV_A * HD + D_C].astype(vc_ref.dtype)
    xb_ref[...] = p2[:, KV_A * HD + D_C:]
    if seq is not None:
        prev = ins[len(ins) - 4 * n_prev:]
        layers = [prev[4 * l:4 * l + 4] for l in range(n_prev)] + [(ka_ref, va_ref, kc_ref, vc_ref)]
        for a, st_ref in enumerate(outs[7:]):
            for l, arrays in enumerate(layers):
                for s in range(tm // seq):
                    rows = arrays[a][s * seq:(s + 1) * seq, :]
                    st_ref[s, l] = rows.T if a < 3 else rows


def _front_part(x2d, mod4, ng3, w_front, gains3, rope, *, layer, mod_row0, tokens_per_mod, kv_dtype, tm=TM_FRONT,
                stack_seq=None, prev_caches=()):
    n = x2d.shape[0]
    use_rope = rope is not None
    row = lambda i: (i, 0)
    in_specs = [pl.BlockSpec((tm, D_MODEL), row),
                _mod_spec(layer, mod_row0, tokens_per_mod // tm),
                _layer_spec((1, D_MODEL), layer),
                _layer_spec((D_MODEL, N_NORM + N_PLAIN), layer),
                _layer_spec((1, N_NORM), layer)]
    args = [x2d, mod4, ng3, w_front, gains3]
    if use_rope:
        per_seq = rope[0].shape[0] // tm
        in_specs += [pl.BlockSpec((tm, LANES), lambda i: (i % per_seq, 0))] * 2
        args += list(rope)
    widths = (D_A, KV_A * HD, H_C * 2 * HD, H_C * 2 * HD, KV_A * HD, D_C, D_LRU)
    dtypes = (BF16, kv_dtype, BF16, kv_dtype, kv_dtype, kv_dtype, F32)
    out_shape = [jax.ShapeDtypeStruct((n, w), d) for w, d in zip(widths, dtypes)]
    out_specs = [pl.BlockSpec((tm, w), row) for w in widths]
    if stack_seq is not None:
        assert len(prev_caches) == DEPTH - 1 and tm % stack_seq == 0
        for cache in prev_caches:
            in_specs += [pl.BlockSpec((tm, arr.shape[1]), row) for arr in cache]
            args += list(cache)
        per_tile = tm // stack_seq
        cache_w = (KV_A * HD, KV_A * HD, H_C * 2 * HD, D_C)
        shapes = [(DEPTH, w, stack_seq) for w in cache_w[:3]] + [(DEPTH, stack_seq, cache_w[3])]
        out_shape += [jax.ShapeDtypeStruct((n // stack_seq,) + s, F32) for s in shapes]
        out_specs += [pl.BlockSpec((per_tile,) + s, lambda i: (i, 0, 0, 0)) for s in shapes]
    body = functools.partial(_front_body, use_rope=use_rope, n_prev=len(prev_caches), seq=stack_seq)
    return _Part("front_rope" if use_rope else "front", body, (n // tm,), in_specs, args, out_shape, out_specs)


def _attn_a_kernel(*refs, tq, nsub, seq_len, banded, has_ctx, layer):
    if has_ctx:
        sink_ref, q_ref, k_ref, v_ref, kc_ref, vc_ref, o_ref = refs
    else:
        sink_ref, q_ref, k_ref, v_ref, o_ref = refs
    i = pl.program_id(1)
    g = H_A // KV_A

    def spread(k, v):
        k = k.astype(F32)
        v = v.astype(F32)
        lo = lax.broadcasted_iota(jnp.int32, k.shape, 1) < HD
        k_sw = pltpu.roll(k, HD, 1)
        v_sw = pltpu.roll(v, HD, 1)
        kd = (jnp.where(lo, k, k_sw).astype(BF16), jnp.where(lo, k_sw, k).astype(BF16))
        vd = ((jnp.where(lo, v, 1.0).astype(BF16), jnp.where(lo, 1.0, v_sw).astype(BF16)),
              (jnp.where(lo, v_sw, 1.0).astype(BF16), jnp.where(lo, 1.0, v).astype(BF16)))
        return kd, vd

    if has_ctx:
        kd_c, vd_c = spread(kc_ref[...], vc_ref[...])
    nwin = tq + 2 * WINDOW if banded else tq
    lo_q = lax.broadcasted_iota(jnp.int32, (tq, LANES), 1) < HD
    zero = jnp.zeros((tq, LANES), BF16)
    for u in range(nsub):
        blk = i * nsub + u
        if banded:
            start = pl.multiple_of(jnp.clip(blk * tq - WINDOW, 0, seq_len - nwin), WINDOW)
            kd, vd = spread(k_ref[pl.ds(start, nwin), :], v_ref[pl.ds(start, nwin), :])
            kpos = start + lax.broadcasted_iota(jnp.int32, (tq, nwin), 1)
            qpos = blk * tq + lax.broadcasted_iota(jnp.int32, (tq, nwin), 0)
            bias = jnp.where(jnp.abs(kpos - qpos) <= WINDOW, 0.0, -jnp.inf)
        else:
            own = pl.multiple_of(blk * tq, tq)
            kd, vd = spread(k_ref[pl.ds(own, tq), :], v_ref[pl.ds(own, tq), :])
        if has_ctx:
            kd = tuple(jnp.concatenate([kd[kv], kd_c[kv]], axis=0) for kv in range(KV_A))
            vd = tuple(tuple(jnp.concatenate([vd[kv][half], vd_c[kv][half]], axis=0) for half in range(2))
                       for kv in range(KV_A))
        q = q_ref[u * tq:(u + 1) * tq, :]
        outs = []
        for h in range(H_A):
            kv, half = h // g, h % 2
            qz = jnp.where(lo_q if half == 0 else jnp.logical_not(lo_q),
                           q[:, (h // 2) * LANES:(h // 2 + 1) * LANES], zero)
            s = _dot_nt(qz, kd[kv])
            if banded:
                s = jnp.concatenate([s[:, :nwin] + bias, s[:, nwin:]], axis=1)
            snk = sink_ref[layer, h] * LOG2E
            m = jnp.maximum(jnp.max(s, axis=-1, keepdims=True), snk)
            e = jnp.exp2(s - m)
            pv = _dot(e.astype(BF16), vd[kv][half])
            outs.append(pv / (pltpu.roll(pv, HD, 1) + jnp.exp2(snk - m)))
        for c in range(H_A // 2):
            pair = jnp.where(lo_q, outs[2 * c], outs[2 * c + 1])
            o_ref[u * tq:(u + 1) * tq, c * LANES:(c + 1) * LANES] = pair.astype(o_ref.dtype)


def _attn_a(sink, q, k, v, k_ctx, v_ctx, *, layer, tq, nsub, banded):
    b, t, _ = q.shape
    has_ctx = k_ctx is not None
    rows = tq * nsub
    kv_spec = pl.BlockSpec((None, t, KV_A * HD), lambda bi, i: (bi, 0, 0))
    in_specs = [pl.BlockSpec(memory_space=pltpu.SMEM),
                pl.BlockSpec((None, rows, D_A), lambda bi, i: (bi, i, 0)), kv_spec, kv_spec]
    args = [sink, q, k, v]
    if has_ctx:
        s = k_ctx.shape[2]
        in_specs += [pl.BlockSpec((None, None, s, KV_A * HD), lambda bi, i: (bi, layer, 0, 0))] * 2
        args += [k_ctx, v_ctx]
    return pl.pallas_call(
        functools.partial(_attn_a_kernel, tq=tq, nsub=nsub, seq_len=t, banded=banded, has_ctx=has_ctx, layer=layer),
        out_shape=jax.ShapeDtypeStruct((b, t, D_A), BF16),
        grid=(b, t // rows),
        in_specs=in_specs,
        out_specs=pl.BlockSpec((None, rows, D_A), lambda bi, i: (bi, i, 0)),
        compiler_params=_params(("arbitrary", "arbitrary")),
        name="attn_a_latent" if has_ctx else "attn_a_context",
    )(*args)


def _attn_c_kernel(*refs, tq, ts, n_loc, has_ctx, lam_init):
    if has_ctx:
        lq1, lk1, lq2, lk2, sub_ref, q_ref, k_ref, v_ref, kc_ref, vc_ref, o_ref = refs
    else:
        lq1, lk1, lq2, lk2, sub_ref, q_ref, k_ref, v_ref, o_ref = refs
    lam = (jnp.exp(jnp.sum(lq1[...] * lk1[...], axis=-1, keepdims=True))
           - jnp.exp(jnp.sum(lq2[...] * lk2[...], axis=-1, keepdims=True)) + lam_init)
    lo = lax.broadcasted_iota(jnp.int32, (tq, LANES), 1) < HD
    zero = jnp.zeros((tq, LANES), BF16)
    heads = [slice(h * LANES, (h + 1) * LANES) for h in range(H_C)]
    qz = []
    for cols in heads:
        q12 = q_ref[:, cols]
        qz.append(jnp.concatenate([jnp.where(lo, q12, zero), jnp.where(lo, zero, q12)], axis=0))

    def update(carry, kt, vt):
        new = []
        for h, cols in enumerate(heads):
            m, l, acc = carry[h]
            s = _dot_nt(qz[h], kt[:, cols].astype(BF16))
            m_new = jnp.maximum(m, jnp.max(s, axis=-1, keepdims=True))
            alpha = jnp.exp2(m - m_new)
            p = jnp.exp2(s - m_new)
            psum = p[:, 0:LANES]
            for j in range(1, p.shape[1] // LANES):
                psum = psum + p[:, j * LANES:(j + 1) * LANES]
            l = alpha * l + psum
            acc = alpha * acc + _dot(p.astype(BF16), vt[:, cols].astype(BF16))
            new.append((m_new, l, acc))
        return tuple(new)

    carry = tuple((jnp.full((2 * tq, 1), -jnp.inf, F32), jnp.zeros((2 * tq, LANES), F32),
                   jnp.zeros((2 * tq, LANES), F32)) for _ in heads)
    for j in range(n_loc):
        carry = update(carry, k_ref[j * ts:(j + 1) * ts, :], v_ref[j * ts:(j + 1) * ts, :])
    if has_ctx:
        carry = update(carry, kc_ref[...], vc_ref[...])
    for h, cols in enumerate(heads):
        _, l, acc = carry[h]
        l = jnp.sum(l, axis=-1, keepdims=True)
        o = acc[:tq] / l[:tq] - lam * (acc[tq:] / l[tq:])
        o = o * lax.rsqrt(jnp.mean(o * o, axis=-1, keepdims=True) + EPS) * sub_ref[...] * (1.0 - lam_init)
        o_ref[:, cols] = o.astype(o_ref.dtype)


def _attn_c(lam_vecs, subln3, q, k, v, k_ctx, v_ctx, *, layer, tq, ts, lam_init):
    b, t, _ = q.shape
    s_loc = k.shape[1]
    has_ctx = k_ctx is not None
    once = dict(pipeline_mode=pl.Buffered(1)) if t // tq > 1 else {}
    kv_spec = pl.BlockSpec((None, s_loc, D_C), lambda bi, i: (bi, 0, 0), **once)
    in_specs = ([_layer_spec((1, HD), layer)] * 4 + [_layer_spec((1, DV_C), layer)]
                + [pl.BlockSpec((None, tq, D_C), lambda bi, i: (bi, i, 0)), kv_spec, kv_spec])
    args = list(lam_vecs) + [subln3, q, k, v]
    if has_ctx:
        sc = k_ctx.shape[2]
        in_specs += [pl.BlockSpec((None, None, sc, D_C), lambda bi, i: (bi, layer, 0, 0), **once)] * 2
        args += [k_ctx, v_ctx]
    return pl.pallas_call(
        functools.partial(_attn_c_kernel, tq=tq, ts=ts, n_loc=s_loc // ts, has_ctx=has_ctx, lam_init=lam_init),
        out_shape=jax.ShapeDtypeStruct((b, t, D_C), BF16),
        grid=(b, t // tq),
        in_specs=in_specs,
        out_specs=pl.BlockSpec((None, tq, D_C), lambda bi, i: (bi, i, 0)),
        compiler_params=_params(("arbitrary", "arbitrary")),
        name="attn_c_latent" if has_ctx else "attn_c_context",
    )(*args)


def _lru_body(ins, outs, scratch, ids, *, tc, nc):
    xc_ref, xp_ref, xn_ref, cw_ref, cb_ref, wg_ref, bg_ref, lam_ref, h0_ref, perm_ref, inv_ref = ins
    ob_ref, st_ref = outs
    a_s, b_s, hs, ps, hf_s, xb_s, hcar, last_h, last_p, cin = scratch
    _, p, c = ids
    cidx = c + p * (nc - 1 - 2 * c)
    r0 = pl.multiple_of(cidx * tc, tc)
    ng = tc // SUBLANES

    @pl.when(p == 0)
    def _():
        cur = xc_ref[...]
        prev = jnp.where(cidx > 0, xp_ref[...], 0.0)
        nxt = jnp.where(cidx < nc - 1, xn_ref[...], 0.0)
        hi = cur.astype(BF16)
        rest = cur - hi.astype(F32)
        mid = rest.astype(BF16)
        lo = (rest - mid.astype(F32)).astype(BF16)
        x3 = (_dot(perm_ref[...], hi) + _dot(perm_ref[...], mid) + _dot(perm_ref[...], lo)).reshape(ng, SUBLANES, D_LRU)
        rows = lax.broadcasted_iota(jnp.int32, (SUBLANES, D_LRU), 0)
        before1 = jnp.where(rows == 0, prev[SUBLANES - 1:SUBLANES, :], pltpu.roll(x3[ng - 1], 1, 0))
        before2 = jnp.where(rows == 0, prev[SUBLANES - 2:SUBLANES - 1, :], pltpu.roll(x3[ng - 2], 1, 0))
        after1 = jnp.where(rows == SUBLANES - 1, nxt[0:1, :], pltpu.roll(x3[0], SUBLANES - 1, 0))
        xpad = jnp.concatenate([before2[None], before1[None], x3, after1[None]], axis=0)
        acc = jnp.broadcast_to(cb_ref[...], (ng, SUBLANES, D_LRU))
        for j in range(CONV_W):
            acc = acc + xpad[j:j + ng] * cw_ref[j:j + 1, :]
        xb_s[pl.ds(r0, tc), :] = acc.reshape(tc, D_LRU)

    xb = xb_s[pl.ds(r0, tc), :]
    gm = _dot(xb.astype(BF16), wg_ref[...]) + bg_ref[...]
    r = _sigmoid(gm[:, :D_LRU])
    ig = _sigmoid(gm[:, D_LRU:])
    nl = -lam_ref[...]
    softplus = jnp.maximum(nl, 0.0) + jnp.log1p(jnp.exp(-jnp.abs(nl)))
    decay = LRU_C * softplus
    a = jnp.exp2(r * (decay * (-LOG2E)))
    one_m_a2 = jnp.tanh(r * decay) * (a * a + 1.0)
    bb = jnp.where(one_m_a2 == 0.0, 0.0, one_m_a2 * lax.rsqrt(one_m_a2)) * (ig * xb)

    a_s[...] = a
    b_s[...] = bb

    def step(gi, carry):
        h, pc = carry
        g = gi + p * (ng - 1 - 2 * gi)
        base = pl.multiple_of(g * SUBLANES, SUBLANES)
        a_g = a_s[pl.ds(base, SUBLANES), :]
        h = a_g * h + b_s[pl.ds(base, SUBLANES), :]
        pc = a_g * pc
        hs[pl.ds(base, SUBLANES), :] = h
        ps[pl.ds(base, SUBLANES), :] = pc
        return h, pc

    init = (jnp.zeros((SUBLANES, D_LRU), F32), jnp.ones((SUBLANES, D_LRU), F32))
    last_h[...], last_p[...] = lax.fori_loop(0, ng, step, init, unroll=8)

    @pl.when(c == 0)
    def _():
        hcar[0:1, :] = h0_ref[pl.ds(p, 1), :]

    state = hcar[0:1, :]
    for rr in range(SUBLANES):
        r_in = rr + p * (SUBLANES - 1 - 2 * rr)
        cin[pl.ds(r_in, 1), :] = state
        state = last_p[pl.ds(r_in, 1), :] * state + last_h[pl.ds(r_in, 1), :]
    hcar[0:1, :] = state
    hfull = (hs[...].reshape(ng, SUBLANES, D_LRU)
             + ps[...].reshape(ng, SUBLANES, D_LRU) * cin[...]).reshape(tc, D_LRU)

    @pl.when(p == 0)
    def _():
        hf_s[pl.ds(r0, tc), :] = hfull

    @pl.when(p == 1)
    def _():
        both = (hf_s[pl.ds(r0, tc), :] + hfull).astype(BF16)
        ob_ref[...] = _dot(inv_ref[...], both).astype(ob_ref.dtype)

    @pl.when(c == nc - 1)
    def _():
        st_ref[pl.ds(p, 1), :] = state


def _lru_part(xb, conv_w, conv_b3, wg, bg, lam4, h0, *, layer, h0_layer, tc):
    b, t, _ = xb.shape
    nc = t // tc
    nb8 = t // SUBLANES
    per = tc // SUBLANES
    cidx = lambda p, c: c + p * (nc - 1 - 2 * c)
    in_specs = [
        pl.BlockSpec((None, tc, D_LRU), lambda bi, p, c: (bi, cidx(p, c), 0)),
        pl.BlockSpec((None, SUBLANES, D_LRU), lambda bi, p, c: (bi, jnp.maximum(cidx(p, c) * per - 1, 0), 0)),
        pl.BlockSpec((None, SUBLANES, D_LRU), lambda bi, p, c: (bi, jnp.minimum((cidx(p, c) + 1) * per, nb8 - 1), 0)),
        _layer_spec((CONV_W, D_LRU), layer),
        _layer_spec((1, D_LRU), layer),
        pl.BlockSpec((None, None, D_LRU, 2 * D_LRU), lambda bi, p, c: (layer, p, 0, 0)),
        pl.BlockSpec((None, None, 1, 2 * D_LRU), lambda bi, p, c: (layer, p, 0, 0)),
        pl.BlockSpec((None, None, 1, D_LRU), lambda bi, p, c: (layer, p, 0, 0)),
        pl.BlockSpec((None, None, 2, D_LRU), lambda bi, p, c: (bi, h0_layer, 0, 0)),
        pl.BlockSpec((tc, tc), lambda bi, p, c: (0, 0)),
        pl.BlockSpec((tc, tc), lambda bi, p, c: (0, 0)),
    ]
    j = jnp.arange(tc)
    src = (j % SUBLANES) * (tc // SUBLANES) + j // SUBLANES
    perm = (src[:, None] == j[None, :]).astype(BF16)
    chunk = (tc, D_LRU)
    row8 = (SUBLANES, D_LRU)
    return _Part("lru", functools.partial(_lru_body, tc=tc, nc=nc), (b, 2, nc), in_specs,
                 [xb, xb, xb, conv_w, conv_b3, wg, bg, lam4, h0, perm, perm.T],
                 [jax.ShapeDtypeStruct((b, t, D_LRU), BF16), jax.ShapeDtypeStruct((b, 2, D_LRU), F32)],
                 [pl.BlockSpec((None, tc, D_LRU), lambda bi, p, c: (bi, nc - 1 - p * c, 0)),
                  pl.BlockSpec((None, 2, D_LRU), lambda bi, p, c: (bi, 0, 0))],
                 [pltpu.VMEM(chunk, F32), pltpu.VMEM(chunk, F32), pltpu.VMEM(chunk, F32), pltpu.VMEM(chunk, F32),
                  pltpu.VMEM((t, D_LRU), F32), pltpu.VMEM((t, D_LRU), F32),
                  pltpu.VMEM(row8, F32), pltpu.VMEM(row8, F32), pltpu.VMEM(row8, F32), pltpu.VMEM(row8, F32)])


def _tail_kernel(x_ref, mod_ref, ng_ref, oa_ref, ob_ref, oc_ref, wgm_ref, wbr_ref, wout_ref, o_ref):
    x = x_ref[...]
    mod = mod_ref[...]
    hb = _normed_input(x, mod, ng_ref[...]).astype(BF16)
    y = None
    for br, br_ref in enumerate((oa_ref, ob_ref, oc_ref)):
        gm = _dot(hb, wgm_ref[:, br * N_BRANCH_COLS:(br + 1) * N_BRANCH_COLS])
        gt = gm[:, :D_A]
        u = (br_ref[...].astype(F32) * (gt * _sigmoid(gt))).astype(BF16)
        proj = _dot(u, wbr_ref[br])
        mg = _sigmoid(gm[:, D_A:])
        y = mg * proj if y is None else y + mg * proj
    gate = mod[:, 2 * D_MODEL:]
    o_ref[...] = x + gate * _dot(y.astype(BF16), wout_ref[...])


def _tail_part(x2d, mod4, ng3, oa, ob, oc, wgm, wbr, wout, *, layer, mod_row0, tokens_per_mod, tm=TM_TAIL):
    n = x2d.shape[0]
    row = lambda i: (i, 0)
    once = dict(pipeline_mode=pl.Buffered(1))

    def weight(shape):
        zeros = (0,) * len(shape)
        return pl.BlockSpec((None,) + shape, lambda i: (layer,) + zeros, **once)

    in_specs = [pl.BlockSpec((tm, D_MODEL), row),
                _mod_spec(layer, mod_row0, tokens_per_mod // tm),
                _layer_spec((1, D_MODEL), layer),
                pl.BlockSpec((tm, D_A), row),
                pl.BlockSpec((tm, D_LRU), row),
                pl.BlockSpec((tm, D_C), row),
                weight((D_MODEL, 3 * N_BRANCH_COLS)),
                weight((3, D_A, D_MODEL)),
                weight((D_MODEL, D_MODEL))]
    return _Part("tail", lambda ins, outs, scratch, ids: _tail_kernel(*ins, *outs), (n // tm,), in_specs,
                 [x2d, mod4, ng3, oa, ob, oc, wgm, wbr, wout],
                 [jax.ShapeDtypeStruct((n, D_MODEL), F32)], [pl.BlockSpec((tm, D_MODEL), row)])


def _rope_tables(seq_len):
    pos = jnp.arange(seq_len)
    row = (pos // GRID_W).astype(F32)
    col = (pos % GRID_W).astype(F32)
    inv = jnp.power(ROPE_BASE, -jnp.arange(ROPE_FREQS, dtype=F32) / ROPE_FREQS)
    ang_r = row[:, None] * inv
    ang_c = col[:, None] * inv
    cos = jnp.concatenate([jnp.cos(ang_r)] * 2 + [jnp.cos(ang_c)] * 2, axis=-1)
    sin = jnp.concatenate([-jnp.sin(ang_r), jnp.sin(ang_r), -jnp.sin(ang_c), jnp.sin(ang_c)], axis=-1)
    return jnp.tile(cos, (1, LANES // HD)), jnp.tile(sin, (1, LANES // HD))


def _block_diag(w):
    rows = w.reshape(w.shape[:-3] + (D_LRU, LRU_BW))
    tiled = jnp.tile(rows, (1,) * (rows.ndim - 1) + (LRU_BLOCKS,))
    blk = jnp.arange(D_LRU) // LRU_BW
    return jnp.where(blk[:, None] == blk[None, :], tiled, 0.0)


def kernel(x_prompt, x_sample, cache_a_k, cache_a_v, cache_c_k, cache_c_v, state_lru, c, c_ctx, norm_g, mod_w, mod_b, w_in, qn_a, kn_a, sink_a, conv_w, conv_b, lru_wa, lru_ba, lru_wx, lru_bx, lru_lam, qn_c, kn_c, lam_q1, lam_k1, lam_q2, lam_k2, subln_c, w_br_a, w_br_b, w_br_c, w_out):
    bp, sp, _ = x_prompt.shape
    bs, ss, _ = x_sample.shape
    past = cache_a_k.shape[2]

    cvecs = jnp.concatenate([c, c_ctx[None, :], jnp.zeros((SUBLANES - bs - 1, D_MODEL), F32)], axis=0)
    mod = _modulation(cvecs, mod_w, mod_b)
    mod4 = mod.reshape(DEPTH, SUBLANES, 1, 3 * D_MODEL)
    rope = _rope_tables(ss)

    w_front = _permute_cast(w_in, FRONT_BLOCKS)
    wgm = _permute_cast(w_in, TAIL_BLOCKS)
    wbr = jnp.stack([w_br_a, w_br_b, w_br_c], axis=1).astype(BF16)
    wout = w_out.astype(BF16)
    gains3 = jnp.concatenate([jnp.tile(qn_a * Q_SCALE, (1, H_A)), jnp.tile(qn_c * Q_SCALE, (1, 2 * H_C)),
                              jnp.tile(kn_c, (1, 2 * H_C)), jnp.tile(kn_a, (1, KV_A))],
                             axis=-1)[:, None, :]
    ng3 = norm_g[:, None, :]
    lru_wg = jnp.concatenate([_block_diag(lru_wa), _block_diag(lru_wx)], axis=-1).astype(BF16)
    lru_bg = jnp.concatenate([lru_ba, lru_bx], axis=-1)[:, :, None, :]
    lru_lam4 = lru_lam[:, :, None, :]
    lru_p = (conv_w, conv_b[:, None, :], lru_wg, lru_bg, lru_lam4)
    lam_vecs = [v[:, None, :] for v in (lam_q1, lam_k1, lam_q2, lam_k2)]
    subln3 = subln_c[:, None, :]
    cka = cache_a_k.reshape(bs, DEPTH, past, KV_A * HD)
    cva = cache_a_v.reshape(bs, DEPTH, past, KV_A * HD)
    ckc = cache_c_k.reshape(bs, DEPTH, past, D_C)
    cvc = cache_c_v.reshape(bs, DEPTH, past, D_C)
    zeros_h0 = jnp.zeros((bp, 1, 2, D_LRU), F32)

    xp = x_prompt.reshape(bp * sp, D_MODEL)
    xs = x_sample.reshape(bs * ss, D_MODEL)
    new_caches, new_states = [], []
    for l in range(DEPTH):
        lam_init = 0.8 - 0.6 * math.exp(-0.3 * l)
        ctx_mod = dict(layer=l, mod_row0=bs, tokens_per_mod=bp * sp)
        lat_mod = dict(layer=l, mod_row0=0, tokens_per_mod=ss)

        flat = lambda a: a.reshape(-1, a.shape[-1])
        r3 = lambda a: a.reshape(bp, sp, a.shape[-1])
        stacking = dict(stack_seq=sp, prev_caches=new_caches) if l == DEPTH - 1 else {}
        qa, ka, qc, kc, va, vc, xb, *stacked = _run(
            _front_part(xp, mod4, ng3, w_front, gains3, None, kv_dtype=F32, **ctx_mod, **stacking))
        new_caches.append((ka, va, kc, vc))
        oa = _attn_a(sink_a, qa[None], ka[None], va[None], None, None, layer=l, tq=sp, nsub=NSUB_A, banded=False)
        ob, st = _run(_lru_part(r3(xb), *lru_p, zeros_h0, layer=l, h0_layer=0, tc=sp))
        new_states.append(st)
        oc = _attn_c(lam_vecs, subln3, r3(qc), r3(kc), r3(vc), None, None, layer=l, tq=sp, ts=sp, lam_init=lam_init)
        (xp,) = _run(_tail_part(xp, mod4, ng3, flat(oa), flat(ob), flat(oc), wgm, wbr, wout, **ctx_mod))

        r3 = lambda a: a.reshape(bs, ss, a.shape[-1])
        qa, ka, qc, kc, va, vc, xb = _run(_front_part(xs, mod4, ng3, w_front, gains3, rope, kv_dtype=BF16, **lat_mod))
        oa = _attn_a(sink_a, r3(qa), r3(ka), r3(va), cka, cva, layer=l, tq=TQ_A, nsub=NSUB_A, banded=True)
        ob, _ = _run(_lru_part(r3(xb), *lru_p, state_lru, layer=l, h0_layer=l, tc=TC_LRU))
        oc = _attn_c(lam_vecs, subln3, r3(qc), r3(kc), r3(vc), ckc, cvc, layer=l, tq=TQ_C, ts=TS_C, lam_init=lam_init)
        (xs,) = _run(_tail_part(xs, mod4, ng3, flat(oa), flat(ob), flat(oc), wgm, wbr, wout, **lat_mod))

    ka_t, va_t, kc_t, vc = stacked
    return (xp.reshape(bp, sp, D_MODEL), xs.reshape(bs, ss, D_MODEL),
            ka_t.reshape(bp, DEPTH, KV_A, HD, sp).transpose(0, 1, 4, 2, 3),
            va_t.reshape(bp, DEPTH, KV_A, HD, sp).transpose(0, 1, 4, 2, 3),
            kc_t.reshape(bp, DEPTH, H_C, 2, HD, sp).transpose(0, 1, 5, 2, 3, 4),
            vc.reshape(bp, DEPTH, sp, H_C, DV_C),
            jnp.stack(new_states, axis=1))
```

```python
import functools
import math

import jax
import jax.numpy as jnp
from jax import lax
from jax.experimental import pallas as pl
from jax.experimental.pallas import tpu as pltpu

F32 = jnp.float32
BF16 = jnp.bfloat16

D_MODEL = 1024
DEPTH = 2
GRID_W = 64
HD = 64
SCALE = 1.0 / math.sqrt(HD)
LOG2E = math.log2(math.e)
Q_SCALE = SCALE * LOG2E
H_A = 8
KV_A = 2
WINDOW = 128
D_A = H_A * HD
D_LRU = 512
LRU_BLOCKS = 8
LRU_BW = D_LRU // LRU_BLOCKS
CONV_W = 4
LRU_C = 8.0
H_C = 4
DV_C = 2 * HD
D_C = H_C * DV_C
ROPE_BASE = 10000.0
ROPE_FREQS = HD // 4
EPS = 1e-6
LANES = 128
SUBLANES = 8
VMEM_LIMIT = 56 * 1024 * 1024

_SECTIONS = (("qa", D_A), ("kava", 2 * KV_A * HD), ("ga", D_A), ("xb", D_LRU), ("gb", D_LRU),
             ("qc", H_C * 2 * HD), ("kc", H_C * 2 * HD), ("vc", D_C), ("gc", D_C),
             ("mg_a", D_MODEL), ("mg_b", D_MODEL), ("mg_c", D_MODEL))
W_BLOCK = 256


def _col_blocks(*names):
    start, spans = 0, {}
    for name, width in _SECTIONS:
        spans[name] = (start, start + width)
        start += width
    blocks = []
    for name in names:
        a, b = spans[name]
        assert a % W_BLOCK == 0 and b % W_BLOCK == 0
        blocks += range(a // W_BLOCK, b // W_BLOCK)
    return tuple(blocks)


FRONT_BLOCKS = _col_blocks("qa", "qc", "kc", "kava", "vc", "xb")
TAIL_BLOCKS = _col_blocks("ga", "mg_a", "gb", "mg_b", "gc", "mg_c")
N_BRANCH_COLS = D_A + D_MODEL
N_NORM = D_A + KV_A * HD + 2 * H_C * 2 * HD
N_PLAIN = KV_A * HD + D_C + D_LRU

TM_FRONT = 512
TM_TAIL = 512
TQ_A = 2 * WINDOW
NSUB_A = 4
TQ_C = 512
TS_C = 512
TC_LRU = 512


def _params(sem, vmem=VMEM_LIMIT):
    return pltpu.CompilerParams(dimension_semantics=sem, vmem_limit_bytes=vmem)


def _dot(a, b):
    return jnp.dot(a, b, preferred_element_type=F32)


def _dot_nt(a, b):
    return lax.dot_general(a, b, (((1,), (1,)), ((), ())), preferred_element_type=F32)


def _sigmoid(x):
    return 0.5 * jnp.tanh(0.5 * x) + 0.5


def _permute_cast_kernel(perm_ref, w_ref, o_ref):
    o_ref[...] = w_ref[...].astype(o_ref.dtype)


def _permute_cast(w, blocks):
    nl, k, _ = w.shape
    grid_spec = pltpu.PrefetchScalarGridSpec(
        num_scalar_prefetch=1, grid=(len(blocks),),
        in_specs=[pl.BlockSpec((nl, k, W_BLOCK), lambda j, perm: (0, 0, perm[j]))],
        out_specs=pl.BlockSpec((nl, k, W_BLOCK), lambda j, perm: (0, 0, j)))
    return pl.pallas_call(
        _permute_cast_kernel, grid_spec=grid_spec,
        out_shape=jax.ShapeDtypeStruct((nl, k, len(blocks) * W_BLOCK), BF16),
        compiler_params=_params(("arbitrary",)), name="permute_cast",
    )(jnp.asarray(blocks, jnp.int32), w)


def _mod_kernel(c_ref, w_ref, b_ref, o_ref):
    c = c_ref[...]
    a = c * _sigmoid(c)
    w = w_ref[...]
    a_hi = a.astype(BF16)
    a_lo = (a - a_hi.astype(F32)).astype(BF16)
    w_hi = w.astype(BF16)
    w_lo = (w - w_hi.astype(F32)).astype(BF16)
    o_ref[...] = _dot(a_hi, w_hi) + _dot(a_hi, w_lo) + _dot(a_lo, w_hi) + b_ref[...]


def _modulation(cvecs, mod_w, mod_b):
    tn = 768
    return pl.pallas_call(
        _mod_kernel,
        out_shape=jax.ShapeDtypeStruct((DEPTH, SUBLANES, 3 * D_MODEL), F32),
        grid=(DEPTH, 3 * D_MODEL // tn),
        in_specs=[pl.BlockSpec((SUBLANES, D_MODEL), lambda l, j: (0, 0)),
                  pl.BlockSpec((None, D_MODEL, tn), lambda l, j: (l, 0, j)),
                  pl.BlockSpec((None, 1, tn), lambda l, j: (l, 0, j))],
        out_specs=pl.BlockSpec((None, SUBLANES, tn), lambda l, j: (l, 0, j)),
        compiler_params=_params(("arbitrary", "arbitrary")),
        name="modulation",
    )(cvecs, mod_w, mod_b.reshape(DEPTH, 1, 3 * D_MODEL))


def _normed_input(x, mod, ng):
    ms = jnp.mean(x * x, axis=-1, keepdims=True)
    shift = mod[:, 0:D_MODEL]
    scale = mod[:, D_MODEL:2 * D_MODEL]
    return (x * lax.rsqrt(ms + EPS) * ng) * (1.0 + scale) + shift


def _layer_spec(shape, layer):
    zeros = (0,) * len(shape)
    return pl.BlockSpec((None,) + tuple(shape), lambda *_: (layer,) + zeros)


def _mod_spec(layer, row0, tiles_per_row):
    return pl.BlockSpec((None, None, 1, 3 * D_MODEL), lambda i: (layer, row0 + i // tiles_per_row, 0, 0))


class _Part:
    def __init__(self, name, body, grid, in_specs, args, out_shape, out_specs, scratch=()):
        self.name, self.body, self.grid = name, body, tuple(grid)
        self.in_specs, self.args = list(in_specs), list(args)
        self.out_shape, self.out_specs, self.scratch = list(out_shape), list(out_specs), list(scratch)


def _run(part):
    ni, no = len(part.args), len(part.out_shape)

    def kern(*refs):
        ids = tuple(pl.program_id(k) for k in range(len(part.grid)))
        part.body(refs[:ni], refs[ni:ni + no], refs[ni + no:], ids)

    return pl.pallas_call(
        kern, out_shape=part.out_shape, grid=part.grid, in_specs=part.in_specs, out_specs=part.out_specs,
        scratch_shapes=part.scratch, compiler_params=_params(("arbitrary",) * len(part.grid)), name=part.name)(*part.args)


def _front_body(ins, outs, scratch, ids, *, use_rope, n_prev=0, seq=None):
    x_ref, mod_ref, ng_ref, w_ref, gain_ref = ins[:5]
    qa_ref, ka_ref, qc_ref, kc_ref, va_ref, vc_ref, xb_ref = outs[:7]
    x = x_ref[...]
    tm = x.shape[0]
    hb = _normed_input(x, mod_ref[...], ng_ref[...]).astype(BF16)
    p1 = _dot(hb, w_ref[:, :N_NORM])
    lane = lax.broadcasted_iota(jnp.int32, (tm, LANES), 1)
    first_half = (lane & ROPE_FREQS) == 0
    wide = 2 * LANES
    same_head = (lax.broadcasted_iota(jnp.int32, (wide, wide), 0) // HD
                 == lax.broadcasted_iota(jnp.int32, (wide, wide), 1) // HD)
    ones_bd = jnp.where(same_head, 1.0, 0.0).astype(BF16)
    sums = []
    for c0 in range(0, N_NORM, wide):
        w = min(wide, N_NORM - c0)
        sq = p1[:, c0:c0 + w] * p1[:, c0:c0 + w]
        sq_hi = sq.astype(BF16)
        sq_lo = (sq - sq_hi.astype(F32)).astype(BF16)
        sums.append(_dot(sq_hi, ones_bd[:w, :w]) + _dot(sq_lo, ones_bd[:w, :w]))
    dests = ([(qa_ref, c) for c in range(4)] + [(qc_ref, c) for c in range(4)]
             + [(kc_ref, c) for c in range(4)] + [(ka_ref, 0)])
    for c, (o_ref, oc) in enumerate(dests):
        pc = p1[:, c * LANES:(c + 1) * LANES]
        msq = sums[c // 2][:, (c % 2) * LANES:(c % 2 + 1) * LANES] * (1.0 / HD)
        y = pc * lax.rsqrt(msq + EPS) * gain_ref[:, c * LANES:(c + 1) * LANES]
        if use_rope:
            cos_ref, sin_ref = ins[5:7]
            partner = jnp.where(first_half, pltpu.roll(y, LANES - ROPE_FREQS, 1),
                                pltpu.roll(y, ROPE_FREQS, 1))
            y = y * cos_ref[...] + partner * sin_ref[...]
        o_ref[:, oc * LANES:(oc + 1) * LANES] = y.astype(o_ref.dtype)
    p2 = _dot(hb, w_ref[:, N_NORM:])
    va_ref[...] = p2[:, 0:KV_A * HD].astype(va_ref.dtype)
    vc_ref[...] = p2[:, KV_A * HD:KV_A * HD + D_C].astype(vc_ref.dtype)
    xb_ref[...] = p2[:, KV_A * HD + D_C:]
    if seq is not None:
        prev = ins[len(ins) - 4 * n_prev:]
        layers = [prev[4 * l:4 * l + 4] for l in range(n_prev)] + [(ka_ref, va_ref, kc_ref, vc_ref)]
        for a, st_ref in enumerate(outs[7:]):
            for l, arrays in enumerate(layers):
                for s in range(tm // seq):
                    rows = arrays[a][s * seq:(s + 1) * seq, :]
                    if a < 3:
                        st_ref[s, l] = rows.T
                    else:
                        for h in range(H_C):
                            st_ref[s, l, pl.ds(h, seq, stride=H_C), :] = rows[:, h * DV_C:(h + 1) * DV_C]


def _front_part(x2d, mod4, ng3, w_front, gains3, rope, *, layer, mod_row0, tokens_per_mod, kv_dtype, tm=TM_FRONT,
                stack_seq=None, prev_caches=()):
    n = x2d.shape[0]
    use_rope = rope is not None
    row = lambda i: (i, 0)
    in_specs = [pl.BlockSpec((tm, D_MODEL), row),
                _mod_spec(layer, mod_row0, tokens_per_mod // tm),
                _layer_spec((1, D_MODEL), layer),
                _layer_spec((D_MODEL, N_NORM + N_PLAIN), layer),
                _layer_spec((1, N_NORM), layer)]
    args = [x2d, mod4, ng3, w_front, gains3]
    if use_rope:
        per_seq = rope[0].shape[0] // tm
        in_specs += [pl.BlockSpec((tm, LANES), lambda i: (i % per_seq, 0))] * 2
        args += list(rope)
    widths = (D_A, KV_A * HD, H_C * 2 * HD, H_C * 2 * HD, KV_A * HD, D_C, D_LRU)
    dtypes = (BF16, kv_dtype, BF16, kv_dtype, kv_dtype, kv_dtype, F32)
    out_shape = [jax.ShapeDtypeStruct((n, w), d) for w, d in zip(widths, dtypes)]
    out_specs = [pl.BlockSpec((tm, w), row) for w in widths]
    if stack_seq is not None:
        assert len(prev_caches) == DEPTH - 1 and tm % stack_seq == 0
        for cache in prev_caches:
            in_specs += [pl.BlockSpec((tm, arr.shape[1]), row) for arr in cache]
            args += list(cache)
        per_tile = tm // stack_seq
        cache_w = (KV_A * HD, KV_A * HD, H_C * 2 * HD, D_C)
        shapes = [(DEPTH, w, stack_seq) for w in cache_w[:3]] + [(DEPTH, stack_seq * H_C, DV_C)]
        out_shape += [jax.ShapeDtypeStruct((n // stack_seq,) + s, F32) for s in shapes]
        out_specs += [pl.BlockSpec((per_tile,) + s, lambda i: (i, 0, 0, 0)) for s in shapes]
    body = functools.partial(_front_body, use_rope=use_rope, n_prev=len(prev_caches), seq=stack_seq)
    return _Part("front_rope" if use_rope else "front", body, (n // tm,), in_specs, args, out_shape, out_specs)


def _attn_a_kernel(*refs, tq, nsub, seq_len, banded, has_ctx, layer):
    if has_ctx:
        sink_ref, q_ref, k_ref, v_ref, kc_ref, vc_ref, o_ref = refs
    else:
        sink_ref, q_ref, k_ref, v_ref, o_ref = refs
    i = pl.program_id(1)
    g = H_A // KV_A

    def spread(k, v):
        k = k.astype(F32)
        v = v.astype(F32)
        lo = lax.broadcasted_iota(jnp.int32, k.shape, 1) < HD
        k_sw = pltpu.roll(k, HD, 1)
        v_sw = pltpu.roll(v, HD, 1)
        kd = (jnp.where(lo, k, k_sw).astype(BF16), jnp.where(lo, k_sw, k).astype(BF16))
        vd = ((jnp.where(lo, v, 1.0).astype(BF16), jnp.where(lo, 1.0, v_sw).astype(BF16)),
              (jnp.where(lo, v_sw, 1.0).astype(BF16), jnp.where(lo, 1.0, v).astype(BF16)))
        return kd, vd

    if has_ctx:
        kd_c, vd_c = spread(kc_ref[...], vc_ref[...])
    nwin = tq + 2 * WINDOW if banded else tq
    lo_q = lax.broadcasted_iota(jnp.int32, (tq, LANES), 1) < HD
    zero = jnp.zeros((tq, LANES), BF16)
    for u in range(nsub):
        blk = i * nsub + u
        if banded:
            start = pl.multiple_of(jnp.clip(blk * tq - WINDOW, 0, seq_len - nwin), WINDOW)
            kd, vd = spread(k_ref[pl.ds(start, nwin), :], v_ref[pl.ds(start, nwin), :])
            kpos = start + lax.broadcasted_iota(jnp.int32, (tq, nwin), 1)
            qpos = blk * tq + lax.broadcasted_iota(jnp.int32, (tq, nwin), 0)
            bias = jnp.where(jnp.abs(kpos - qpos) <= WINDOW, 0.0, -jnp.inf)
        else:
            own = pl.multiple_of(blk * tq, tq)
            kd, vd = spread(k_ref[pl.ds(own, tq), :], v_ref[pl.ds(own, tq), :])
        if has_ctx:
            kd = tuple(jnp.concatenate([kd[kv], kd_c[kv]], axis=0) for kv in range(KV_A))
            vd = tuple(tuple(jnp.concatenate([vd[kv][half], vd_c[kv][half]], axis=0) for half in range(2))
                       for kv in range(KV_A))
        q = q_ref[u * tq:(u + 1) * tq, :]
        outs = []
        for h in range(H_A):
            kv, half = h // g, h % 2
            qz = jnp.where(lo_q if half == 0 else jnp.logical_not(lo_q),
                           q[:, (h // 2) * LANES:(h // 2 + 1) * LANES], zero)
            s = _dot_nt(qz, kd[kv])
            if banded:
                s = jnp.concatenate([s[:, :nwin] + bias, s[:, nwin:]], axis=1)
            snk = sink_ref[layer, h] * LOG2E
            m = jnp.maximum(jnp.max(s, axis=-1, keepdims=True), snk)
            e = jnp.exp2(s - m)
            pv = _dot(e.astype(BF16), vd[kv][half])
            outs.append(pv / (pltpu.roll(pv, HD, 1) + jnp.exp2(snk - m)))
        for c in range(H_A // 2):
            pair = jnp.where(lo_q, outs[2 * c], outs[2 * c + 1])
            o_ref[u * tq:(u + 1) * tq, c * LANES:(c + 1) * LANES] = pair.astype(o_ref.dtype)


def _attn_a(sink, q, k, v, k_ctx, v_ctx, *, layer, tq, nsub, banded):
    b, t, _ = q.shape
    has_ctx = k_ctx is not None
    rows = tq * nsub
    kv_spec = pl.BlockSpec((None, t, KV_A * HD), lambda bi, i: (bi, 0, 0))
    in_specs = [pl.BlockSpec(memory_space=pltpu.SMEM),
                pl.BlockSpec((None, rows, D_A), lambda bi, i: (bi, i, 0)), kv_spec, kv_spec]
    args = [sink, q, k, v]
    if has_ctx:
        s = k_ctx.shape[2]
        in_specs += [pl.BlockSpec((None, None, s, KV_A * HD), lambda bi, i: (bi, layer, 0, 0))] * 2
        args += [k_ctx, v_ctx]
    return pl.pallas_call(
        functools.partial(_attn_a_kernel, tq=tq, nsub=nsub, seq_len=t, banded=banded, has_ctx=has_ctx, layer=layer),
        out_shape=jax.ShapeDtypeStruct((b, t, D_A), BF16),
        grid=(b, t // rows),
        in_specs=in_specs,
        out_specs=pl.BlockSpec((None, rows, D_A), lambda bi, i: (bi, i, 0)),
        compiler_params=_params(("arbitrary", "arbitrary")),
        name="attn_a_latent" if has_ctx else "attn_a_context",
    )(*args)


def _attn_c_kernel(*refs, tq, ts, n_loc, has_ctx, lam_init):
    if has_ctx:
        lq1, lk1, lq2, lk2, sub_ref, q_ref, k_ref, v_ref, kc_ref, vc_ref, o_ref = refs
    else:
        lq1, lk1, lq2, lk2, sub_ref, q_ref, k_ref, v_ref, o_ref = refs
    lam = (jnp.exp(jnp.sum(lq1[...] * lk1[...], axis=-1, keepdims=True))
           - jnp.exp(jnp.sum(lq2[...] * lk2[...], axis=-1, keepdims=True)) + lam_init)
    lo = lax.broadcasted_iota(jnp.int32, (tq, LANES), 1) < HD
    zero = jnp.zeros((tq, LANES), BF16)
    heads = [slice(h * LANES, (h + 1) * LANES) for h in range(H_C)]
    qz = []
    for cols in heads:
        q12 = q_ref[:, cols]
        qz.append(jnp.concatenate([jnp.where(lo, q12, zero), jnp.where(lo, zero, q12)], axis=0))

    def update(carry, kt, vt):
        new = []
        for h, cols in enumerate(heads):
            m, l, acc = carry[h]
            s = _dot_nt(qz[h], kt[:, cols].astype(BF16))
            m_new = jnp.maximum(m, jnp.max(s, axis=-1, keepdims=True))
            alpha = jnp.exp2(m - m_new)
            p = jnp.exp2(s - m_new)
            psum = p[:, 0:LANES]
            for j in range(1, p.shape[1] // LANES):
                psum = psum + p[:, j * LANES:(j + 1) * LANES]
            l = alpha * l + psum
            acc = alpha * acc + _dot(p.astype(BF16), vt[:, cols].astype(BF16))
            new.append((m_new, l, acc))
        return tuple(new)

    carry = tuple((jnp.full((2 * tq, 1), -jnp.inf, F32), jnp.zeros((2 * tq, LANES), F32),
                   jnp.zeros((2 * tq, LANES), F32)) for _ in heads)
    for j in range(n_loc):
        carry = update(carry, k_ref[j * ts:(j + 1) * ts, :], v_ref[j * ts:(j + 1) * ts, :])
    if has_ctx:
        carry = update(carry, kc_ref[...], vc_ref[...])
    for h, cols in enumerate(heads):
        _, l, acc = carry[h]
        l = jnp.sum(l, axis=-1, keepdims=True)
        o = acc[:tq] / l[:tq] - lam * (acc[tq:] / l[tq:])
        o = o * lax.rsqrt(jnp.mean(o * o, axis=-1, keepdims=True) + EPS) * sub_ref[...] * (1.0 - lam_init)
        o_ref[:, cols] = o.astype(o_ref.dtype)


def _attn_c(lam_vecs, subln3, q, k, v, k_ctx, v_ctx, *, layer, tq, ts, lam_init):
    b, t, _ = q.shape
    s_loc = k.shape[1]
    has_ctx = k_ctx is not None
    once = dict(pipeline_mode=pl.Buffered(1)) if t // tq > 1 else {}
    kv_spec = pl.BlockSpec((None, s_loc, D_C), lambda bi, i: (bi, 0, 0), **once)
    in_specs = ([_layer_spec((1, HD), layer)] * 4 + [_layer_spec((1, DV_C), layer)]
                + [pl.BlockSpec((None, tq, D_C), lambda bi, i: (bi, i, 0)), kv_spec, kv_spec])
    args = list(lam_vecs) + [subln3, q, k, v]
    if has_ctx:
        sc = k_ctx.shape[2]
        in_specs += [pl.BlockSpec((None, None, sc, D_C), lambda bi, i: (bi, layer, 0, 0), **once)] * 2
        args += [k_ctx, v_ctx]
    return pl.pallas_call(
        functools.partial(_attn_c_kernel, tq=tq, ts=ts, n_loc=s_loc // ts, has_ctx=has_ctx, lam_init=lam_init),
        out_shape=jax.ShapeDtypeStruct((b, t, D_C), BF16),
        grid=(b, t // tq),
        in_specs=in_specs,
        out_specs=pl.BlockSpec((None, tq, D_C), lambda bi, i: (bi, i, 0)),
        compiler_params=_params(("arbitrary", "arbitrary")),
        name="attn_c_latent" if has_ctx else "attn_c_context",
    )(*args)


def _lru_body(ins, outs, scratch, ids, *, tc, nc):
    xc_ref, xp_ref, xn_ref, cw_ref, cb_ref, wg_ref, bg_ref, lam_ref, h0_ref, perm_ref, inv_ref = ins
    ob_ref, st_ref = outs
    a_s, b_s, hs, ps, hf_s, xb_s, hcar, last_h, last_p, cin = scratch
    _, p, c = ids
    cidx = c + p * (nc - 1 - 2 * c)
    r0 = pl.multiple_of(cidx * tc, tc)
    ng = tc // SUBLANES

    @pl.when(p == 0)
    def _():
        cur = xc_ref[...]
        prev = jnp.where(cidx > 0, xp_ref[...], 0.0)
        nxt = jnp.where(cidx < nc - 1, xn_ref[...], 0.0)
        hi = cur.astype(BF16)
        rest = cur - hi.astype(F32)
        mid = rest.astype(BF16)
        lo = (rest - mid.astype(F32)).astype(BF16)
        x3 = (_dot(perm_ref[...], hi) + _dot(perm_ref[...], mid) + _dot(perm_ref[...], lo)).reshape(ng, SUBLANES, D_LRU)
        rows = lax.broadcasted_iota(jnp.int32, (SUBLANES, D_LRU), 0)
        before1 = jnp.where(rows == 0, prev[SUBLANES - 1:SUBLANES, :], pltpu.roll(x3[ng - 1], 1, 0))
        before2 = jnp.where(rows == 0, prev[SUBLANES - 2:SUBLANES - 1, :], pltpu.roll(x3[ng - 2], 1, 0))
        after1 = jnp.where(rows == SUBLANES - 1, nxt[0:1, :], pltpu.roll(x3[0], SUBLANES - 1, 0))
        xpad = jnp.concatenate([before2[None], before1[None], x3, after1[None]], axis=0)
        acc = jnp.broadcast_to(cb_ref[...], (ng, SUBLANES, D_LRU))
        for j in range(CONV_W):
            acc = acc + xpad[j:j + ng] * cw_ref[j:j + 1, :]
        xb_s[pl.ds(r0, tc), :] = acc.reshape(tc, D_LRU)

    xb = xb_s[pl.ds(r0, tc), :]
    gm = _dot(xb.astype(BF16), wg_ref[...]) + bg_ref[...]
    r = _sigmoid(gm[:, :D_LRU])
    ig = _sigmoid(gm[:, D_LRU:])
    nl = -lam_ref[...]
    softplus = jnp.maximum(nl, 0.0) + jnp.log1p(jnp.exp(-jnp.abs(nl)))
    decay = LRU_C * softplus
    a = jnp.exp2(r * (decay * (-LOG2E)))
    one_m_a2 = jnp.tanh(r * decay) * (a * a + 1.0)
    bb = jnp.where(one_m_a2 == 0.0, 0.0, one_m_a2 * lax.rsqrt(one_m_a2)) * (ig * xb)

    a_s[...] = a
    b_s[...] = bb

    def step(gi, carry):
        h, pc = carry
        g = gi + p * (ng - 1 - 2 * gi)
        base = pl.multiple_of(g * SUBLANES, SUBLANES)
        a_g = a_s[pl.ds(base, SUBLANES), :]
        h = a_g * h + b_s[pl.ds(base, SUBLANES), :]
        pc = a_g * pc
        hs[pl.ds(base, SUBLANES), :] = h
        ps[pl.ds(base, SUBLANES), :] = pc
        return h, pc

    init = (jnp.zeros((SUBLANES, D_LRU), F32), jnp.ones((SUBLANES, D_LRU), F32))
    last_h[...], last_p[...] = lax.fori_loop(0, ng, step, init, unroll=8)

    @pl.when(c == 0)
    def _():
        hcar[0:1, :] = h0_ref[pl.ds(p, 1), :]

    state = hcar[0:1, :]
    for rr in range(SUBLANES):
        r_in = rr + p * (SUBLANES - 1 - 2 * rr)
        cin[pl.ds(r_in, 1), :] = state
        state = last_p[pl.ds(r_in, 1), :] * state + last_h[pl.ds(r_in, 1), :]
    hcar[0:1, :] = state
    hfull = (hs[...].reshape(ng, SUBLANES, D_LRU)
             + ps[...].reshape(ng, SUBLANES, D_LRU) * cin[...]).reshape(tc, D_LRU)

    @pl.when(p == 0)
    def _():
        hf_s[pl.ds(r0, tc), :] = hfull

    @pl.when(p == 1)
    def _():
        both = (hf_s[pl.ds(r0, tc), :] + hfull).astype(BF16)
        ob_ref[...] = _dot(inv_ref[...], both).astype(ob_ref.dtype)

    @pl.when(c == nc - 1)
    def _():
        st_ref[pl.ds(p, 1), :] = state


def _lru_part(xb, conv_w, conv_b3, wg, bg, lam4, h0, *, layer, h0_layer, tc):
    b, t, _ = xb.shape
    nc = t // tc
    nb8 = t // SUBLANES
    per = tc // SUBLANES
    cidx = lambda p, c: c + p * (nc - 1 - 2 * c)
    in_specs = [
        pl.BlockSpec((None, tc, D_LRU), lambda bi, p, c: (bi, cidx(p, c), 0)),
        pl.BlockSpec((None, SUBLANES, D_LRU), lambda bi, p, c: (bi, jnp.maximum(cidx(p, c) * per - 1, 0), 0)),
        pl.BlockSpec((None, SUBLANES, D_LRU), lambda bi, p, c: (bi, jnp.minimum((cidx(p, c) + 1) * per, nb8 - 1), 0)),
        _layer_spec((CONV_W, D_LRU), layer),
        _layer_spec((1, D_LRU), layer),
        pl.BlockSpec((None, None, D_LRU, 2 * D_LRU), lambda bi, p, c: (layer, p, 0, 0)),
        pl.BlockSpec((None, None, 1, 2 * D_LRU), lambda bi, p, c: (layer, p, 0, 0)),
        pl.BlockSpec((None, None, 1, D_LRU), lambda bi, p, c: (layer, p, 0, 0)),
        pl.BlockSpec((None, None, 2, D_LRU), lambda bi, p, c: (bi, h0_layer, 0, 0)),
        pl.BlockSpec((tc, tc), lambda bi, p, c: (0, 0)),
        pl.BlockSpec((tc, tc), lambda bi, p, c: (0, 0)),
    ]
    j = jnp.arange(tc)
    src = (j % SUBLANES) * (tc // SUBLANES) + j // SUBLANES
    perm = (src[:, None] == j[None, :]).astype(BF16)
    chunk = (tc, D_LRU)
    row8 = (SUBLANES, D_LRU)
    return _Part("lru", functools.partial(_lru_body, tc=tc, nc=nc), (b, 2, nc), in_specs,
                 [xb, xb, xb, conv_w, conv_b3, wg, bg, lam4, h0, perm, perm.T],
                 [jax.ShapeDtypeStruct((b, t, D_LRU), BF16), jax.ShapeDtypeStruct((b, 2, D_LRU), F32)],
                 [pl.BlockSpec((None, tc, D_LRU), lambda bi, p, c: (bi, nc - 1 - p * c, 0)),
                  pl.BlockSpec((None, 2, D_LRU), lambda bi, p, c: (bi, 0, 0))],
                 [pltpu.VMEM(chunk, F32), pltpu.VMEM(chunk, F32), pltpu.VMEM(chunk, F32), pltpu.VMEM(chunk, F32),
                  pltpu.VMEM((t, D_LRU), F32), pltpu.VMEM((t, D_LRU), F32),
                  pltpu.VMEM(row8, F32), pltpu.VMEM(row8, F32), pltpu.VMEM(row8, F32), pltpu.VMEM(row8, F32)])


def _tail_kernel(x_ref, mod_ref, ng_ref, oa_ref, ob_ref, oc_ref, wgm_ref, wbr_ref, wout_ref, o_ref):
    x = x_ref[...]
    mod = mod_ref[...]
    hb = _normed_input(x, mod, ng_ref[...]).astype(BF16)
    y = None
    for br, br_ref in enumerate((oa_ref, ob_ref, oc_ref)):
        gm = _dot(hb, wgm_ref[:, br * N_BRANCH_COLS:(br + 1) * N_BRANCH_COLS])
        gt = gm[:, :D_A]
        u = (br_ref[...].astype(F32) * (gt * _sigmoid(gt))).astype(BF16)
        proj = _dot(u, wbr_ref[br])
        mg = _sigmoid(gm[:, D_A:])
        y = mg * proj if y is None else y + mg * proj
    gate = mod[:, 2 * D_MODEL:]
    o_ref[...] = x + gate * _dot(y.astype(BF16), wout_ref[...])


def _tail_part(x2d, mod4, ng3, oa, ob, oc, wgm, wbr, wout, *, layer, mod_row0, tokens_per_mod, tm=TM_TAIL):
    n = x2d.shape[0]
    row = lambda i: (i, 0)
    once = dict(pipeline_mode=pl.Buffered(1))

    def weight(shape):
        zeros = (0,) * len(shape)
        return pl.BlockSpec((None,) + shape, lambda i: (layer,) + zeros, **once)

    in_specs = [pl.BlockSpec((tm, D_MODEL), row),
                _mod_spec(layer, mod_row0, tokens_per_mod // tm),
                _layer_spec((1, D_MODEL), layer),
                pl.BlockSpec((tm, D_A), row),
                pl.BlockSpec((tm, D_LRU), row),
                pl.BlockSpec((tm, D_C), row),
                weight((D_MODEL, 3 * N_BRANCH_COLS)),
                weight((3, D_A, D_MODEL)),
                weight((D_MODEL, D_MODEL))]
    return _Part("tail", lambda ins, outs, scratch, ids: _tail_kernel(*ins, *outs), (n // tm,), in_specs,
                 [x2d, mod4, ng3, oa, ob, oc, wgm, wbr, wout],
                 [jax.ShapeDtypeStruct((n, D_MODEL), F32)], [pl.BlockSpec((tm, D_MODEL), row)])


def _rope_tables(seq_len):
    pos = jnp.arange(seq_len)
    row = (pos // GRID_W).astype(F32)
    col = (pos % GRID_W).astype(F32)
    inv = jnp.power(ROPE_BASE, -jnp.arange(ROPE_FREQS, dtype=F32) / ROPE_FREQS)
    ang_r = row[:, None] * inv
    ang_c = col[:, None] * inv
    cos = jnp.concatenate([jnp.cos(ang_r)] * 2 + [jnp.cos(ang_c)] * 2, axis=-1)
    sin = jnp.concatenate([-jnp.sin(ang_r), jnp.sin(ang_r), -jnp.sin(ang_c), jnp.sin(ang_c)], axis=-1)
    return jnp.tile(cos, (1, LANES // HD)), jnp.tile(sin, (1, LANES // HD))


def _block_diag(w):
    rows = w.reshape(w.shape[:-3] + (D_LRU, LRU_BW))
    tiled = jnp.tile(rows, (1,) * (rows.ndim - 1) + (LRU_BLOCKS,))
    blk = jnp.arange(D_LRU) // LRU_BW
    return jnp.where(blk[:, None] == blk[None, :], tiled, 0.0)


def kernel(x_prompt, x_sample, cache_a_k, cache_a_v, cache_c_k, cache_c_v, state_lru, c, c_ctx, norm_g, mod_w, mod_b, w_in, qn_a, kn_a, sink_a, conv_w, conv_b, lru_wa, lru_ba, lru_wx, lru_bx, lru_lam, qn_c, kn_c, lam_q1, lam_k1, lam_q2, lam_k2, subln_c, w_br_a, w_br_b, w_br_c, w_out):
    bp, sp, _ = x_prompt.shape
    bs, ss, _ = x_sample.shape
    past = cache_a_k.shape[2]

    cvecs = jnp.concatenate([c, c_ctx[None, :], jnp.zeros((SUBLANES - bs - 1, D_MODEL), F32)], axis=0)
    mod = _modulation(cvecs, mod_w, mod_b)
    mod4 = mod.reshape(DEPTH, SUBLANES, 1, 3 * D_MODEL)
    rope = _rope_tables(ss)

    w_front = _permute_cast(w_in, FRONT_BLOCKS)
    wgm = _permute_cast(w_in, TAIL_BLOCKS)
    wbr = jnp.stack([w_br_a, w_br_b, w_br_c], axis=1).astype(BF16)
    wout = w_out.astype(BF16)
    gains3 = jnp.concatenate([jnp.tile(qn_a * Q_SCALE, (1, H_A)), jnp.tile(qn_c * Q_SCALE, (1, 2 * H_C)),
                              jnp.tile(kn_c, (1, 2 * H_C)), jnp.tile(kn_a, (1, KV_A))],
                             axis=-1)[:, None, :]
    ng3 = norm_g[:, None, :]
    lru_wg = jnp.concatenate([_block_diag(lru_wa), _block_diag(lru_wx)], axis=-1).astype(BF16)
    lru_bg = jnp.concatenate([lru_ba, lru_bx], axis=-1)[:, :, None, :]
    lru_lam4 = lru_lam[:, :, None, :]
    lru_p = (conv_w, conv_b[:, None, :], lru_wg, lru_bg, lru_lam4)
    lam_vecs = [v[:, None, :] for v in (lam_q1, lam_k1, lam_q2, lam_k2)]
    subln3 = subln_c[:, None, :]
    cka = cache_a_k.reshape(bs, DEPTH, past, KV_A * HD)
    cva = cache_a_v.reshape(bs, DEPTH, past, KV_A * HD)
    ckc = cache_c_k.reshape(bs, DEPTH, past, D_C)
    cvc = cache_c_v.reshape(bs, DEPTH, past, D_C)
    zeros_h0 = jnp.zeros((bp, 1, 2, D_LRU), F32)

    xp = x_prompt.reshape(bp * sp, D_MODEL)
    xs = x_sample.reshape(bs * ss, D_MODEL)
    new_caches, new_states = [], []
    for l in range(DEPTH):
        lam_init = 0.8 - 0.6 * math.exp(-0.3 * l)
        ctx_mod = dict(layer=l, mod_row0=bs, tokens_per_mod=bp * sp)
        lat_mod = dict(layer=l, mod_row0=0, tokens_per_mod=ss)

        flat = lambda a: a.reshape(-1, a.shape[-1])
        r3 = lambda a: a.reshape(bp, sp, a.shape[-1])
        stacking = dict(stack_seq=sp, prev_caches=new_caches) if l == DEPTH - 1 else {}
        qa, ka, qc, kc, va, vc, xb, *stacked = _run(
            _front_part(xp, mod4, ng3, w_front, gains3, None, kv_dtype=F32, **ctx_mod, **stacking))
        new_caches.append((ka, va, kc, vc))
        oa = _attn_a(sink_a, r3(qa), r3(ka), r3(va), None, None, layer=l, tq=sp, nsub=1, banded=False)
        ob, st = _run(_lru_part(r3(xb), *lru_p, zeros_h0, layer=l, h0_layer=0, tc=sp))
        new_states.append(st)
        oc = _attn_c(lam_vecs, subln3, r3(qc), r3(kc), r3(vc), None, None, layer=l, tq=sp, ts=sp, lam_init=lam_init)
        (xp,) = _run(_tail_part(xp, mod4, ng3, flat(oa), flat(ob), flat(oc), wgm, wbr, wout, **ctx_mod))

        r3 = lambda a: a.reshape(bs, ss, a.shape[-1])
        qa, ka, qc, kc, va, vc, xb = _run(_front_part(xs, mod4, ng3, w_front, gains3, rope, kv_dtype=BF16, **lat_mod))
        oa = _attn_a(sink_a, r3(qa), r3(ka), r3(va), cka, cva, layer=l, tq=TQ_A, nsub=NSUB_A, banded=True)
        ob, _ = _run(_lru_part(r3(xb), *lru_p, state_lru, layer=l, h0_layer=l, tc=TC_LRU))
        oc = _attn_c(lam_vecs, subln3, r3(qc), r3(kc), r3(vc), ckc, cvc, layer=l, tq=TQ_C, ts=TS_C, lam_init=lam_init)
        (xs,) = _run(_tail_part(xs, mod4, ng3, flat(oa), flat(ob), flat(oc), wgm, wbr, wout, **lat_mod))

    ka_t, va_t, kc_t, vc = stacked
    return (xp.reshape(bp, sp, D_MODEL), xs.reshape(bs, ss, D_MODEL),
            ka_t.reshape(bp, DEPTH, KV_A, HD, sp).transpose(0, 1, 4, 2, 3),
            va_t.reshape(bp, DEPTH, KV_A, HD, sp).transpose(0, 1, 4, 2, 3),
            kc_t.reshape(bp, DEPTH, H_C, 2, HD, sp).transpose(0, 1, 5, 2, 3, 4),
            vc.reshape(bp, DEPTH, sp, H_C, DV_C),
            jnp.stack(new_states, axis=1))
```

```python
import functools
import math

import jax
import jax.numpy as jnp
from jax import lax
from jax.experimental import pallas as pl
from jax.experimental.pallas import tpu as pltpu

F32 = jnp.float32
BF16 = jnp.bfloat16

D_MODEL = 1024
DEPTH = 2
GRID_W = 64
HD = 64
SCALE = 1.0 / math.sqrt(HD)
LOG2E = math.log2(math.e)
Q_SCALE = SCALE * LOG2E
H_A = 8
KV_A = 2
WINDOW = 128
D_A = H_A * HD
D_LRU = 512
LRU_BLOCKS = 8
LRU_BW = D_LRU // LRU_BLOCKS
CONV_W = 4
LRU_C = 8.0
H_C = 4
DV_C = 2 * HD
D_C = H_C * DV_C
ROPE_BASE = 10000.0
ROPE_FREQS = HD // 4
EPS = 1e-6
LANES = 128
SUBLANES = 8
VMEM_LIMIT = 56 * 1024 * 1024

_SECTIONS = (("qa", D_A), ("kava", 2 * KV_A * HD), ("ga", D_A), ("xb", D_LRU), ("gb", D_LRU),
             ("qc", H_C * 2 * HD), ("kc", H_C * 2 * HD), ("vc", D_C), ("gc", D_C),
             ("mg_a", D_MODEL), ("mg_b", D_MODEL), ("mg_c", D_MODEL))
W_BLOCK = 256


def _col_blocks(*names):
    start, spans = 0, {}
    for name, width in _SECTIONS:
        spans[name] = (start, start + width)
        start += width
    blocks = []
    for name in names:
        a, b = spans[name]
        assert a % W_BLOCK == 0 and b % W_BLOCK == 0
        blocks += range(a // W_BLOCK, b // W_BLOCK)
    return tuple(blocks)


FRONT_BLOCKS = _col_blocks("qa", "qc", "kc", "kava", "vc", "xb")
TAIL_BLOCKS = _col_blocks("ga", "mg_a", "gb", "mg_b", "gc", "mg_c")
N_BRANCH_COLS = D_A + D_MODEL
N_NORM = D_A + KV_A * HD + 2 * H_C * 2 * HD
N_PLAIN = KV_A * HD + D_C + D_LRU

TM_FRONT = 512
TM_TAIL = 512
TQ_A = 2 * WINDOW
NSUB_A = 4
TQ_C = 512
TS_C = 512
TC_LRU = 512


def _params(sem, vmem=VMEM_LIMIT):
    return pltpu.CompilerParams(dimension_semantics=sem, vmem_limit_bytes=vmem)


def _dot(a, b):
    return jnp.dot(a, b, preferred_element_type=F32)


def _dot_nt(a, b):
    return lax.dot_general(a, b, (((1,), (1,)), ((), ())), preferred_element_type=F32)


def _sigmoid(x):
    return 0.5 * jnp.tanh(0.5 * x) + 0.5


def _permute_cast_kernel(perm_ref, w_ref, o_ref):
    o_ref[...] = w_ref[...].astype(o_ref.dtype)


def _permute_cast(w, blocks):
    nl, k, _ = w.shape
    grid_spec = pltpu.PrefetchScalarGridSpec(
        num_scalar_prefetch=1, grid=(len(blocks),),
        in_specs=[pl.BlockSpec((nl, k, W_BLOCK), lambda j, perm: (0, 0, perm[j]))],
        out_specs=pl.BlockSpec((nl, k, W_BLOCK), lambda j, perm: (0, 0, j)))
    return pl.pallas_call(
        _permute_cast_kernel, grid_spec=grid_spec,
        out_shape=jax.ShapeDtypeStruct((nl, k, len(blocks) * W_BLOCK), BF16),
        compiler_params=_params(("arbitrary",)), name="permute_cast",
    )(jnp.asarray(blocks, jnp.int32), w)


def _mod_kernel(c_ref, w_ref, b_ref, o_ref):
    c = c_ref[...]
    a = c * _sigmoid(c)
    w = w_ref[...]
    a_hi = a.astype(BF16)
    a_lo = (a - a_hi.astype(F32)).astype(BF16)
    w_hi = w.astype(BF16)
    w_lo = (w - w_hi.astype(F32)).astype(BF16)
    o_ref[...] = _dot(a_hi, w_hi) + _dot(a_hi, w_lo) + _dot(a_lo, w_hi) + b_ref[...]


def _modulation(cvecs, mod_w, mod_b):
    tn = 768
    return pl.pallas_call(
        _mod_kernel,
        out_shape=jax.ShapeDtypeStruct((DEPTH, SUBLANES, 3 * D_MODEL), F32),
        grid=(DEPTH, 3 * D_MODEL // tn),
        in_specs=[pl.BlockSpec((SUBLANES, D_MODEL), lambda l, j: (0, 0)),
                  pl.BlockSpec((None, D_MODEL, tn), lambda l, j: (l, 0, j)),
                  pl.BlockSpec((None, 1, tn), lambda l, j: (l, 0, j))],
        out_specs=pl.BlockSpec((None, SUBLANES, tn), lambda l, j: (l, 0, j)),
        compiler_params=_params(("arbitrary", "arbitrary")),
        name="modulation",
    )(cvecs, mod_w, mod_b.reshape(DEPTH, 1, 3 * D_MODEL))


def _normed_input(x, mod, ng):
    ms = jnp.mean(x * x, axis=-1, keepdims=True)
    shift = mod[:, 0:D_MODEL]
    scale = mod[:, D_MODEL:2 * D_MODEL]
    return (x * lax.rsqrt(ms + EPS) * ng) * (1.0 + scale) + shift


def _layer_spec(shape, layer):
    zeros = (0,) * len(shape)
    return pl.BlockSpec((None,) + tuple(shape), lambda *_: (layer,) + zeros)


def _mod_spec(layer, row0, tiles_per_row):
    return pl.BlockSpec((None, None, 1, 3 * D_MODEL), lambda i: (layer, row0 + i // tiles_per_row, 0, 0))


class _Part:
    def __init__(self, name, body, grid, in_specs, args, out_shape, out_specs, scratch=()):
        self.name, self.body, self.grid = name, body, tuple(grid)
        self.in_specs, self.args = list(in_specs), list(args)
        self.out_shape, self.out_specs, self.scratch = list(out_shape), list(out_specs), list(scratch)


def _run(part):
    ni, no = len(part.args), len(part.out_shape)

    def kern(*refs):
        ids = tuple(pl.program_id(k) for k in range(len(part.grid)))
        part.body(refs[:ni], refs[ni:ni + no], refs[ni + no:], ids)

    return pl.pallas_call(
        kern, out_shape=part.out_shape, grid=part.grid, in_specs=part.in_specs, out_specs=part.out_specs,
        scratch_shapes=part.scratch, compiler_params=_params(("arbitrary",) * len(part.grid)), name=part.name)(*part.args)


def _front_body(ins, outs, scratch, ids, *, use_rope, n_prev=0, seq=None):
    x_ref, mod_ref, ng_ref, w_ref, gain_ref = ins[:5]
    qa_ref, ka_ref, qc_ref, kc_ref, va_ref, vc_ref, xb_ref = outs[:7]
    x = x_ref[...]
    tm = x.shape[0]
    hb = _normed_input(x, mod_ref[...], ng_ref[...]).astype(BF16)
    p1 = _dot(hb, w_ref[:, :N_NORM])
    lane = lax.broadcasted_iota(jnp.int32, (tm, LANES), 1)
    first_half = (lane & ROPE_FREQS) == 0
    wide = 2 * LANES
    same_head = (lax.broadcasted_iota(jnp.int32, (wide, wide), 0) // HD
                 == lax.broadcasted_iota(jnp.int32, (wide, wide), 1) // HD)
    ones_bd = jnp.where(same_head, 1.0, 0.0).astype(BF16)
    sums = []
    for c0 in range(0, N_NORM, wide):
        w = min(wide, N_NORM - c0)
        sq = p1[:, c0:c0 + w] * p1[:, c0:c0 + w]
        sq_hi = sq.astype(BF16)
        sq_lo = (sq - sq_hi.astype(F32)).astype(BF16)
        sums.append(_dot(sq_hi, ones_bd[:w, :w]) + _dot(sq_lo, ones_bd[:w, :w]))
    dests = ([(qa_ref, c) for c in range(4)] + [(qc_ref, c) for c in range(4)]
             + [(kc_ref, c) for c in range(4)] + [(ka_ref, 0)])
    for c, (o_ref, oc) in enumerate(dests):
        pc = p1[:, c * LANES:(c + 1) * LANES]
        msq = sums[c // 2][:, (c % 2) * LANES:(c % 2 + 1) * LANES] * (1.0 / HD)
        y = pc * lax.rsqrt(msq + EPS) * gain_ref[:, c * LANES:(c + 1) * LANES]
        if use_rope:
            cos_ref, sin_ref = ins[5:7]
            partner = jnp.where(first_half, pltpu.roll(y, LANES - ROPE_FREQS, 1),
                                pltpu.roll(y, ROPE_FREQS, 1))
            y = y * cos_ref[...] + partner * sin_ref[...]
        o_ref[:, oc * LANES:(oc + 1) * LANES] = y.astype(o_ref.dtype)
    p2 = _dot(hb, w_ref[:, N_NORM:])
    va_ref[...] = p2[:, 0:KV_A * HD].astype(va_ref.dtype)
    vc_ref[...] = p2[:, KV_A * HD:KV_A * HD + D_C].astype(vc_ref.dtype)
    xb_ref[...] = p2[:, KV_A * HD + D_C:]
    if seq is not None:
        prev = ins[len(ins) - 4 * n_prev:]
        layers = [prev[4 * l:4 * l + 4] for l in range(n_prev)] + [(ka_ref, va_ref, kc_ref, vc_ref)]
        for a, st_ref in enumerate(outs[7:]):
            for l, arrays in enumerate(layers):
                for s in range(tm // seq):
                    rows = arrays[a][s * seq:(s + 1) * seq, :]
                    if a < 3:
                        st_ref[s, l] = rows.T
                    else:
                        for h in range(H_C):
                            st_ref[s, l, pl.ds(h, seq, stride=H_C), :] = rows[:, h * DV_C:(h + 1) * DV_C]


def _front_part(x2d, mod4, ng3, w_front, gains3, rope, *, layer, mod_row0, tokens_per_mod, kv_dtype, tm=TM_FRONT,
                stack_seq=None, prev_caches=()):
    n = x2d.shape[0]
    use_rope = rope is not None
    row = lambda i: (i, 0)
    in_specs = [pl.BlockSpec((tm, D_MODEL), row),
                _mod_spec(layer, mod_row0, tokens_per_mod // tm),
                _layer_spec((1, D_MODEL), layer),
                _layer_spec((D_MODEL, N_NORM + N_PLAIN), layer),
                _layer_spec((1, N_NORM), layer)]
    args = [x2d, mod4, ng3, w_front, gains3]
    if use_rope:
        per_seq = rope[0].shape[0] // tm
        in_specs += [pl.BlockSpec((tm, LANES), lambda i: (i % per_seq, 0))] * 2
        args += list(rope)
    widths = (D_A, KV_A * HD, H_C * 2 * HD, H_C * 2 * HD, KV_A * HD, D_C, D_LRU)
    dtypes = (BF16, kv_dtype, BF16, kv_dtype, kv_dtype, kv_dtype, F32)
    out_shape = [jax.ShapeDtypeStruct((n, w), d) for w, d in zip(widths, dtypes)]
    out_specs = [pl.BlockSpec((tm, w), row) for w in widths]
    if stack_seq is not None:
        assert len(prev_caches) == DEPTH - 1 and tm % stack_seq == 0
        for cache in prev_caches:
            in_specs += [pl.BlockSpec((tm, arr.shape[1]), row) for arr in cache]
            args += list(cache)
        per_tile = tm // stack_seq
        cache_w = (KV_A * HD, KV_A * HD, H_C * 2 * HD, D_C)
        shapes = [(DEPTH, w, stack_seq) for w in cache_w[:3]] + [(DEPTH, stack_seq * H_C, DV_C)]
        out_shape += [jax.ShapeDtypeStruct((n // stack_seq,) + s, F32) for s in shapes]
        out_specs += [pl.BlockSpec((per_tile,) + s, lambda i: (i, 0, 0, 0)) for s in shapes]
    body = functools.partial(_front_body, use_rope=use_rope, n_prev=len(prev_caches), seq=stack_seq)
    return _Part("front_rope" if use_rope else "front", body, (n // tm,), in_specs, args, out_shape, out_specs)


def _attn_a_kernel(*refs, tq, nsub, seq_len, banded, has_ctx, layer):
    if has_ctx:
        sink_ref, q_ref, k_ref, v_ref, kc_ref, vc_ref, o_ref = refs
    else:
        sink_ref, q_ref, k_ref, v_ref, o_ref = refs
    i = pl.program_id(1)
    g = H_A // KV_A

    def spread(k, v):
        k = k.astype(F32)
        v = v.astype(F32)
        lo = lax.broadcasted_iota(jnp.int32, k.shape, 1) < HD
        k_sw = pltpu.roll(k, HD, 1)
        v_sw = pltpu.roll(v, HD, 1)
        kd = (jnp.where(lo, k, k_sw).astype(BF16), jnp.where(lo, k_sw, k).astype(BF16))
        vd = ((jnp.where(lo, v, 1.0).astype(BF16), jnp.where(lo, 1.0, v_sw).astype(BF16)),
              (jnp.where(lo, v_sw, 1.0).astype(BF16), jnp.where(lo, 1.0, v).astype(BF16)))
        return kd, vd

    if has_ctx:
        kd_c, vd_c = spread(kc_ref[...], vc_ref[...])
    nwin = tq + 2 * WINDOW if banded else tq
    lo_q = lax.broadcasted_iota(jnp.int32, (tq, LANES), 1) < HD
    zero = jnp.zeros((tq, LANES), BF16)
    for u in range(nsub):
        blk = i * nsub + u
        if banded:
            start = pl.multiple_of(jnp.clip(blk * tq - WINDOW, 0, seq_len - nwin), WINDOW)
            kd, vd = spread(k_ref[pl.ds(start, nwin), :], v_ref[pl.ds(start, nwin), :])
            kpos = start + lax.broadcasted_iota(jnp.int32, (tq, nwin), 1)
            qpos = blk * tq + lax.broadcasted_iota(jnp.int32, (tq, nwin), 0)
            bias = jnp.where(jnp.abs(kpos - qpos) <= WINDOW, 0.0, -jnp.inf)
        else:
            own = pl.multiple_of(blk * tq, tq)
            kd, vd = spread(k_ref[pl.ds(own, tq), :], v_ref[pl.ds(own, tq), :])
        if has_ctx:
            kd = tuple(jnp.concatenate([kd[kv], kd_c[kv]], axis=0) for kv in range(KV_A))
            vd = tuple(tuple(jnp.concatenate([vd[kv][half], vd_c[kv][half]], axis=0) for half in range(2))
                       for kv in range(KV_A))
        q = q_ref[u * tq:(u + 1) * tq, :]
        outs = []
        for h in range(H_A):
            kv, half = h // g, h % 2
            qz = jnp.where(lo_q if half == 0 else jnp.logical_not(lo_q),
                           q[:, (h // 2) * LANES:(h // 2 + 1) * LANES], zero)
            s = _dot_nt(qz, kd[kv])
            if banded:
                s = jnp.concatenate([s[:, :nwin] + bias, s[:, nwin:]], axis=1)
            snk = sink_ref[layer, h] * LOG2E
            m = jnp.maximum(jnp.max(s, axis=-1, keepdims=True), snk)
            e = jnp.exp2(s - m)
            pv = _dot(e.astype(BF16), vd[kv][half])
            outs.append(pv / (pltpu.roll(pv, HD, 1) + jnp.exp2(snk - m)))
        for c in range(H_A // 2):
            pair = jnp.where(lo_q, outs[2 * c], outs[2 * c + 1])
            o_ref[u * tq:(u + 1) * tq, c * LANES:(c + 1) * LANES] = pair.astype(o_ref.dtype)


def _attn_a(sink, q, k, v, k_ctx, v_ctx, *, layer, tq, nsub, banded):
    b, t, _ = q.shape
    has_ctx = k_ctx is not None
    rows = tq * nsub
    kv_spec = pl.BlockSpec((None, t, KV_A * HD), lambda bi, i: (bi, 0, 0))
    in_specs = [pl.BlockSpec(memory_space=pltpu.SMEM),
                pl.BlockSpec((None, rows, D_A), lambda bi, i: (bi, i, 0)), kv_spec, kv_spec]
    args = [sink, q, k, v]
    if has_ctx:
        s = k_ctx.shape[2]
        in_specs += [pl.BlockSpec((None, None, s, KV_A * HD), lambda bi, i: (bi, layer, 0, 0))] * 2
        args += [k_ctx, v_ctx]
    return pl.pallas_call(
        functools.partial(_attn_a_kernel, tq=tq, nsub=nsub, seq_len=t, banded=banded, has_ctx=has_ctx, layer=layer),
        out_shape=jax.ShapeDtypeStruct((b, t, D_A), BF16),
        grid=(b, t // rows),
        in_specs=in_specs,
        out_specs=pl.BlockSpec((None, rows, D_A), lambda bi, i: (bi, i, 0)),
        compiler_params=_params(("arbitrary", "arbitrary")),
        name="attn_a_latent" if has_ctx else "attn_a_context",
    )(*args)


def _attn_c_kernel(*refs, tq, ts, n_loc, has_ctx, lam_init):
    if has_ctx:
        lq1, lk1, lq2, lk2, sub_ref, q_ref, k_ref, v_ref, kc_ref, vc_ref, o_ref = refs
    else:
        lq1, lk1, lq2, lk2, sub_ref, q_ref, k_ref, v_ref, o_ref = refs
    lam = (jnp.exp(jnp.sum(lq1[...] * lk1[...], axis=-1, keepdims=True))
           - jnp.exp(jnp.sum(lq2[...] * lk2[...], axis=-1, keepdims=True)) + lam_init)
    lo = lax.broadcasted_iota(jnp.int32, (tq, LANES), 1) < HD
    zero = jnp.zeros((tq, LANES), BF16)
    heads = [slice(h * LANES, (h + 1) * LANES) for h in range(H_C)]
    qz = []
    for cols in heads:
        q12 = q_ref[:, cols]
        qz.append(jnp.concatenate([jnp.where(lo, q12, zero), jnp.where(lo, zero, q12)], axis=0))

    def update(carry, kt, vt):
        new = []
        ones = jnp.ones((kt.shape[0], LANES), BF16)
        for h, cols in enumerate(heads):
            m, acc = carry[h]
            s = _dot_nt(qz[h], kt[:, cols].astype(BF16))
            m_new = jnp.maximum(m, jnp.max(s, axis=-1, keepdims=True))
            alpha = jnp.exp2(m - m_new)
            p = jnp.exp2(s - m_new)
            v_ones = jnp.concatenate([vt[:, cols].astype(BF16), ones], axis=1)
            acc = alpha * acc + _dot(p.astype(BF16), v_ones)
            new.append((m_new, acc))
        return tuple(new)

    carry = tuple((jnp.full((2 * tq, 1), -jnp.inf, F32), jnp.zeros((2 * tq, 2 * LANES), F32)) for _ in heads)
    for j in range(n_loc):
        carry = update(carry, k_ref[j * ts:(j + 1) * ts, :], v_ref[j * ts:(j + 1) * ts, :])
    if has_ctx:
        carry = update(carry, kc_ref[...], vc_ref[...])
    for h, cols in enumerate(heads):
        _, acc = carry[h]
        num, den = acc[:, :DV_C], acc[:, DV_C:]
        o = num[:tq] / den[:tq] - lam * (num[tq:] / den[tq:])
        o = o * lax.rsqrt(jnp.mean(o * o, axis=-1, keepdims=True) + EPS) * sub_ref[...] * (1.0 - lam_init)
        o_ref[:, cols] = o.astype(o_ref.dtype)


def _attn_c(lam_vecs, subln3, q, k, v, k_ctx, v_ctx, *, layer, tq, ts, lam_init):
    b, t, _ = q.shape
    s_loc = k.shape[1]
    has_ctx = k_ctx is not None
    kv_spec = pl.BlockSpec((None, s_loc, D_C), lambda bi, i: (bi, 0, 0))
    in_specs = ([_layer_spec((1, HD), layer)] * 4 + [_layer_spec((1, DV_C), layer)]
                + [pl.BlockSpec((None, tq, D_C), lambda bi, i: (bi, i, 0)), kv_spec, kv_spec])
    args = list(lam_vecs) + [subln3, q, k, v]
    if has_ctx:
        sc = k_ctx.shape[2]
        in_specs += [pl.BlockSpec((None, None, sc, D_C), lambda bi, i: (bi, layer, 0, 0))] * 2
        args += [k_ctx, v_ctx]
    return pl.pallas_call(
        functools.partial(_attn_c_kernel, tq=tq, ts=ts, n_loc=s_loc // ts, has_ctx=has_ctx, lam_init=lam_init),
        out_shape=jax.ShapeDtypeStruct((b, t, D_C), BF16),
        grid=(b, t // tq),
        in_specs=in_specs,
        out_specs=pl.BlockSpec((None, tq, D_C), lambda bi, i: (bi, i, 0)),
        compiler_params=_params(("arbitrary", "arbitrary")),
        name="attn_c_latent" if has_ctx else "attn_c_context",
    )(*args)


def _lru_body(ins, outs, scratch, ids, *, tc, nc):
    xc_ref, xp_ref, xn_ref, cw_ref, cb_ref, wg_ref, bg_ref, lam_ref, h0_ref, perm_ref, inv_ref = ins
    ob_ref, st_ref = outs
    a_s, b_s, hs, ps, hf_s, xb_s, hcar, last_h, last_p, cin = scratch
    _, p, c = ids
    cidx = c + p * (nc - 1 - 2 * c)
    r0 = pl.multiple_of(cidx * tc, tc)
    ng = tc // SUBLANES

    @pl.when(p == 0)
    def _():
        cur = xc_ref[...]
        prev = jnp.where(cidx > 0, xp_ref[...], 0.0)
        nxt = jnp.where(cidx < nc - 1, xn_ref[...], 0.0)
        hi = cur.astype(BF16)
        rest = cur - hi.astype(F32)
        mid = rest.astype(BF16)
        lo = (rest - mid.astype(F32)).astype(BF16)
        x3 = (_dot(perm_ref[...], hi) + _dot(perm_ref[...], mid) + _dot(perm_ref[...], lo)).reshape(ng, SUBLANES, D_LRU)
        rows = lax.broadcasted_iota(jnp.int32, (SUBLANES, D_LRU), 0)
        before1 = jnp.where(rows == 0, prev[SUBLANES - 1:SUBLANES, :], pltpu.roll(x3[ng - 1], 1, 0))
        before2 = jnp.where(rows == 0, prev[SUBLANES - 2:SUBLANES - 1, :], pltpu.roll(x3[ng - 2], 1, 0))
        after1 = jnp.where(rows == SUBLANES - 1, nxt[0:1, :], pltpu.roll(x3[0], SUBLANES - 1, 0))
        xpad = jnp.concatenate([before2[None], before1[None], x3, after1[None]], axis=0)
        acc = jnp.broadcast_to(cb_ref[...], (ng, SUBLANES, D_LRU))
        for j in range(CONV_W):
            acc = acc + xpad[j:j + ng] * cw_ref[j:j + 1, :]
        xb_s[pl.ds(r0, tc), :] = acc.reshape(tc, D_LRU)

    xb = xb_s[pl.ds(r0, tc), :]
    gm = _dot(xb.astype(BF16), wg_ref[...]) + bg_ref[...]
    r = _sigmoid(gm[:, :D_LRU])
    ig = _sigmoid(gm[:, D_LRU:])
    nl = -lam_ref[...]
    softplus = jnp.maximum(nl, 0.0) + jnp.log1p(jnp.exp(-jnp.abs(nl)))
    decay = LRU_C * softplus
    a = jnp.exp2(r * (decay * (-LOG2E)))
    one_m_a2 = jnp.tanh(r * decay) * (a * a + 1.0)
    bb = jnp.where(one_m_a2 == 0.0, 0.0, one_m_a2 * lax.rsqrt(one_m_a2)) * (ig * xb)

    a_s[...] = a
    b_s[...] = bb

    def step(gi, carry):
        h, pc = carry
        g = gi + p * (ng - 1 - 2 * gi)
        base = pl.multiple_of(g * SUBLANES, SUBLANES)
        a_g = a_s[pl.ds(base, SUBLANES), :]
        h = a_g * h + b_s[pl.ds(base, SUBLANES), :]
        pc = a_g * pc
        hs[pl.ds(base, SUBLANES), :] = h
        ps[pl.ds(base, SUBLANES), :] = pc
        return h, pc

    init = (jnp.zeros((SUBLANES, D_LRU), F32), jnp.ones((SUBLANES, D_LRU), F32))
    last_h[...], last_p[...] = lax.fori_loop(0, ng, step, init, unroll=8)

    @pl.when(c == 0)
    def _():
        hcar[0:1, :] = h0_ref[pl.ds(p, 1), :]

    state = hcar[0:1, :]
    for rr in range(SUBLANES):
        r_in = rr + p * (SUBLANES - 1 - 2 * rr)
        cin[pl.ds(r_in, 1), :] = state
        state = last_p[pl.ds(r_in, 1), :] * state + last_h[pl.ds(r_in, 1), :]
    hcar[0:1, :] = state
    hfull = (hs[...].reshape(ng, SUBLANES, D_LRU)
             + ps[...].reshape(ng, SUBLANES, D_LRU) * cin[...]).reshape(tc, D_LRU)

    @pl.when(p == 0)
    def _():
        hf_s[pl.ds(r0, tc), :] = hfull

    @pl.when(p == 1)
    def _():
        both = (hf_s[pl.ds(r0, tc), :] + hfull).astype(BF16)
        ob_ref[...] = _dot(inv_ref[...], both).astype(ob_ref.dtype)

    @pl.when(c == nc - 1)
    def _():
        st_ref[pl.ds(p, 1), :] = state


def _lru_part(xb, conv_w, conv_b3, wg, bg, lam4, h0, *, layer, h0_layer, tc):
    b, t, _ = xb.shape
    nc = t // tc
    nb8 = t // SUBLANES
    per = tc // SUBLANES
    cidx = lambda p, c: c + p * (nc - 1 - 2 * c)
    in_specs = [
        pl.BlockSpec((None, tc, D_LRU), lambda bi, p, c: (bi, cidx(p, c), 0)),
        pl.BlockSpec((None, SUBLANES, D_LRU), lambda bi, p, c: (bi, jnp.maximum(cidx(p, c) * per - 1, 0), 0)),
        pl.BlockSpec((None, SUBLANES, D_LRU), lambda bi, p, c: (bi, jnp.minimum((cidx(p, c) + 1) * per, nb8 - 1), 0)),
        _layer_spec((CONV_W, D_LRU), layer),
        _layer_spec((1, D_LRU), layer),
        pl.BlockSpec((None, None, D_LRU, 2 * D_LRU), lambda bi, p, c: (layer, p, 0, 0)),
        pl.BlockSpec((None, None, 1, 2 * D_LRU), lambda bi, p, c: (layer, p, 0, 0)),
        pl.BlockSpec((None, None, 1, D_LRU), lambda bi, p, c: (layer, p, 0, 0)),
        pl.BlockSpec((None, None, 2, D_LRU), lambda bi, p, c: (bi, h0_layer, 0, 0)),
        pl.BlockSpec((tc, tc), lambda bi, p, c: (0, 0)),
        pl.BlockSpec((tc, tc), lambda bi, p, c: (0, 0)),
    ]
    j = jnp.arange(tc)
    src = (j % SUBLANES) * (tc // SUBLANES) + j // SUBLANES
    perm = (src[:, None] == j[None, :]).astype(BF16)
    chunk = (tc, D_LRU)
    row8 = (SUBLANES, D_LRU)
    return _Part("lru", functools.partial(_lru_body, tc=tc, nc=nc), (b, 2, nc), in_specs,
                 [xb, xb, xb, conv_w, conv_b3, wg, bg, lam4, h0, perm, perm.T],
                 [jax.ShapeDtypeStruct((b, t, D_LRU), BF16), jax.ShapeDtypeStruct((b, 2, D_LRU), F32)],
                 [pl.BlockSpec((None, tc, D_LRU), lambda bi, p, c: (bi, nc - 1 - p * c, 0)),
                  pl.BlockSpec((None, 2, D_LRU), lambda bi, p, c: (bi, 0, 0))],
                 [pltpu.VMEM(chunk, F32), pltpu.VMEM(chunk, F32), pltpu.VMEM(chunk, F32), pltpu.VMEM(chunk, F32),
                  pltpu.VMEM((t, D_LRU), F32), pltpu.VMEM((t, D_LRU), F32),
                  pltpu.VMEM(row8, F32), pltpu.VMEM(row8, F32), pltpu.VMEM(row8, F32), pltpu.VMEM(row8, F32)])


def _tail_kernel(x_ref, mod_ref, ng_ref, oa_ref, ob_ref, oc_ref, wgm_ref, wbr_ref, wout_ref, o_ref):
    x = x_ref[...]
    mod = mod_ref[...]
    hb = _normed_input(x, mod, ng_ref[...]).astype(BF16)
    y = None
    for br, br_ref in enumerate((oa_ref, ob_ref, oc_ref)):
        gm = _dot(hb, wgm_ref[:, br * N_BRANCH_COLS:(br + 1) * N_BRANCH_COLS])
        gt = gm[:, :D_A]
        u = (br_ref[...].astype(F32) * (gt * _sigmoid(gt))).astype(BF16)
        proj = _dot(u, wbr_ref[br])
        mg = _sigmoid(gm[:, D_A:])
        y = mg * proj if y is None else y + mg * proj
    gate = mod[:, 2 * D_MODEL:]
    o_ref[...] = x + gate * _dot(y.astype(BF16), wout_ref[...])


def _tail_part(x2d, mod4, ng3, oa, ob, oc, wgm, wbr, wout, *, layer, mod_row0, tokens_per_mod, tm=TM_TAIL):
    n = x2d.shape[0]
    row = lambda i: (i, 0)
    once = dict(pipeline_mode=pl.Buffered(1))

    def weight(shape):
        zeros = (0,) * len(shape)
        return pl.BlockSpec((None,) + shape, lambda i: (layer,) + zeros, **once)

    in_specs = [pl.BlockSpec((tm, D_MODEL), row),
                _mod_spec(layer, mod_row0, tokens_per_mod // tm),
                _layer_spec((1, D_MODEL), layer),
                pl.BlockSpec((tm, D_A), row),
                pl.BlockSpec((tm, D_LRU), row),
                pl.BlockSpec((tm, D_C), row),
                weight((D_MODEL, 3 * N_BRANCH_COLS)),
                weight((3, D_A, D_MODEL)),
                weight((D_MODEL, D_MODEL))]
    return _Part("tail", lambda ins, outs, scratch, ids: _tail_kernel(*ins, *outs), (n // tm,), in_specs,
                 [x2d, mod4, ng3, oa, ob, oc, wgm, wbr, wout],
                 [jax.ShapeDtypeStruct((n, D_MODEL), F32)], [pl.BlockSpec((tm, D_MODEL), row)])


def _rope_tables(seq_len):
    pos = jnp.arange(seq_len)
    row = (pos // GRID_W).astype(F32)
    col = (pos % GRID_W).astype(F32)
    inv = jnp.power(ROPE_BASE, -jnp.arange(ROPE_FREQS, dtype=F32) / ROPE_FREQS)
    ang_r = row[:, None] * inv
    ang_c = col[:, None] * inv
    cos = jnp.concatenate([jnp.cos(ang_r)] * 2 + [jnp.cos(ang_c)] * 2, axis=-1)
    sin = jnp.concatenate([-jnp.sin(ang_r), jnp.sin(ang_r), -jnp.sin(ang_c), jnp.sin(ang_c)], axis=-1)
    return jnp.tile(cos, (1, LANES // HD)), jnp.tile(sin, (1, LANES // HD))


def _block_diag(w):
    rows = w.reshape(w.shape[:-3] + (D_LRU, LRU_BW))
    tiled = jnp.tile(rows, (1,) * (rows.ndim - 1) + (LRU_BLOCKS,))
    blk = jnp.arange(D_LRU) // LRU_BW
    return jnp.where(blk[:, None] == blk[None, :], tiled, 0.0)


def kernel(x_prompt, x_sample, cache_a_k, cache_a_v, cache_c_k, cache_c_v, state_lru, c, c_ctx, norm_g, mod_w, mod_b, w_in, qn_a, kn_a, sink_a, conv_w, conv_b, lru_wa, lru_ba, lru_wx, lru_bx, lru_lam, qn_c, kn_c, lam_q1, lam_k1, lam_q2, lam_k2, subln_c, w_br_a, w_br_b, w_br_c, w_out):
    bp, sp, _ = x_prompt.shape
    bs, ss, _ = x_sample.shape
    past = cache_a_k.shape[2]

    cvecs = jnp.concatenate([c, c_ctx[None, :], jnp.zeros((SUBLANES - bs - 1, D_MODEL), F32)], axis=0)
    mod = _modulation(cvecs, mod_w, mod_b)
    mod4 = mod.reshape(DEPTH, SUBLANES, 1, 3 * D_MODEL)
    rope = _rope_tables(ss)

    w_front = _permute_cast(w_in, FRONT_BLOCKS)
    wgm = _permute_cast(w_in, TAIL_BLOCKS)
    wbr = jnp.stack([w_br_a, w_br_b, w_br_c], axis=1).astype(BF16)
    wout = w_out.astype(BF16)
    gains3 = jnp.concatenate([jnp.tile(qn_a * Q_SCALE, (1, H_A)), jnp.tile(qn_c * Q_SCALE, (1, 2 * H_C)),
                              jnp.tile(kn_c, (1, 2 * H_C)), jnp.tile(kn_a, (1, KV_A))],
                             axis=-1)[:, None, :]
    ng3 = norm_g[:, None, :]
    lru_wg = jnp.concatenate([_block_diag(lru_wa), _block_diag(lru_wx)], axis=-1).astype(BF16)
    lru_bg = jnp.concatenate([lru_ba, lru_bx], axis=-1)[:, :, None, :]
    lru_lam4 = lru_lam[:, :, None, :]
    lru_p = (conv_w, conv_b[:, None, :], lru_wg, lru_bg, lru_lam4)
    lam_vecs = [v[:, None, :] for v in (lam_q1, lam_k1, lam_q2, lam_k2)]
    subln3 = subln_c[:, None, :]
    cka = cache_a_k.reshape(bs, DEPTH, past, KV_A * HD)
    cva = cache_a_v.reshape(bs, DEPTH, past, KV_A * HD)
    ckc = cache_c_k.reshape(bs, DEPTH, past, D_C)
    cvc = cache_c_v.reshape(bs, DEPTH, past, D_C)
    zeros_h0 = jnp.zeros((bp, 1, 2, D_LRU), F32)

    xp = x_prompt.reshape(bp * sp, D_MODEL)
    xs = x_sample.reshape(bs * ss, D_MODEL)
    new_caches, new_states = [], []
    for l in range(DEPTH):
        lam_init = 0.8 - 0.6 * math.exp(-0.3 * l)
        ctx_mod = dict(layer=l, mod_row0=bs, tokens_per_mod=bp * sp)
        lat_mod = dict(layer=l, mod_row0=0, tokens_per_mod=ss)

        flat = lambda a: a.reshape(-1, a.shape[-1])
        r3 = lambda a: a.reshape(bp, sp, a.shape[-1])
        stacking = dict(stack_seq=sp, prev_caches=new_caches) if l == DEPTH - 1 else {}
        qa, ka, qc, kc, va, vc, xb, *stacked = _run(
            _front_part(xp, mod4, ng3, w_front, gains3, None, kv_dtype=F32, **ctx_mod, **stacking))
        new_caches.append((ka, va, kc, vc))
        oa = _attn_a(sink_a, r3(qa), r3(ka), r3(va), None, None, layer=l, tq=sp, nsub=1, banded=False)
        ob, st = _run(_lru_part(r3(xb), *lru_p, zeros_h0, layer=l, h0_layer=0, tc=sp))
        new_states.append(st)
        oc = _attn_c(lam_vecs, subln3, r3(qc), r3(kc), r3(vc), None, None, layer=l, tq=sp, ts=sp, lam_init=lam_init)
        (xp,) = _run(_tail_part(xp, mod4, ng3, flat(oa), flat(ob), flat(oc), wgm, wbr, wout, **ctx_mod))

        r3 = lambda a: a.reshape(bs, ss, a.shape[-1])
        qa, ka, qc, kc, va, vc, xb = _run(_front_part(xs, mod4, ng3, w_front, gains3, rope, kv_dtype=BF16, **lat_mod))
        oa = _attn_a(sink_a, r3(qa), r3(ka), r3(va), cka, cva, layer=l, tq=TQ_A, nsub=NSUB_A, banded=True)
        ob, _ = _run(_lru_part(r3(xb), *lru_p, state_lru, layer=l, h0_layer=l, tc=TC_LRU))
        oc = _attn_c(lam_vecs, subln3, r3(qc), r3(kc), r3(vc), ckc, cvc, layer=l, tq=TQ_C, ts=TS_C, lam_init=lam_init)
        (xs,) = _run(_tail_part(xs, mod4, ng3, flat(oa), flat(ob), flat(oc), wgm, wbr, wout, **lat_mod))

    ka_t, va_t, kc_t, vc = stacked
    return (xp.reshape(bp, sp, D_MODEL), xs.reshape(bs, ss, D_MODEL),
            ka_t.reshape(bp, DEPTH, KV_A, HD, sp).transpose(0, 1, 4, 2, 3),
            va_t.reshape(bp, DEPTH, KV_A, HD, sp).transpose(0, 1, 4, 2, 3),
            kc_t.reshape(bp, DEPTH, H_C, 2, HD, sp).transpose(0, 1, 5, 2, 3, 4),
            vc.reshape(bp, DEPTH, sp, H_C, DV_C),
            jnp.stack(new_states, axis=1))
```

```python
import functools
import math

import jax
import jax.numpy as jnp
from jax import lax
from jax.experimental import pallas as pl
from jax.experimental.pallas import tpu as pltpu

F32 = jnp.float32
BF16 = jnp.bfloat16

D_MODEL = 1024
DEPTH = 2
GRID_W = 64
HD = 64
SCALE = 1.0 / math.sqrt(HD)
LOG2E = math.log2(math.e)
Q_SCALE = SCALE * LOG2E
H_A = 8
KV_A = 2
WINDOW = 128
D_A = H_A * HD
D_LRU = 512
LRU_BLOCKS = 8
LRU_BW = D_LRU // LRU_BLOCKS
CONV_W = 4
LRU_C = 8.0
H_C = 4
DV_C = 2 * HD
D_C = H_C * DV_C
ROPE_BASE = 10000.0
ROPE_FREQS = HD // 4
EPS = 1e-6
LANES = 128
SUBLANES = 8
VMEM_LIMIT = 56 * 1024 * 1024

_SECTIONS = (("qa", D_A), ("kava", 2 * KV_A * HD), ("ga", D_A), ("xb", D_LRU), ("gb", D_LRU),
             ("qc", H_C * 2 * HD), ("kc", H_C * 2 * HD), ("vc", D_C), ("gc", D_C),
             ("mg_a", D_MODEL), ("mg_b", D_MODEL), ("mg_c", D_MODEL))
W_BLOCK = 256


def _col_blocks(*names):
    start, spans = 0, {}
    for name, width in _SECTIONS:
        spans[name] = (start, start + width)
        start += width
    blocks = []
    for name in names:
        a, b = spans[name]
        assert a % W_BLOCK == 0 and b % W_BLOCK == 0
        blocks += range(a // W_BLOCK, b // W_BLOCK)
    return tuple(blocks)


FRONT_BLOCKS = _col_blocks("qa", "qc", "kc", "kava", "vc", "xb")
TAIL_BLOCKS = _col_blocks("ga", "mg_a", "gb", "mg_b", "gc", "mg_c")
N_BRANCH_COLS = D_A + D_MODEL
N_NORM = D_A + KV_A * HD + 2 * H_C * 2 * HD
N_PLAIN = KV_A * HD + D_C + D_LRU

TM_FRONT = 512
TM_TAIL = 512
TQ_A = 2 * WINDOW
NSUB_A = 8
TQ_C = 512
TS_C = 512
TC_LRU = 512


def _params(sem, vmem=VMEM_LIMIT):
    return pltpu.CompilerParams(dimension_semantics=sem, vmem_limit_bytes=vmem)


def _dot(a, b):
    return jnp.dot(a, b, preferred_element_type=F32)


def _dot_nt(a, b):
    return lax.dot_general(a, b, (((1,), (1,)), ((), ())), preferred_element_type=F32)


def _sigmoid(x):
    return 0.5 * jnp.tanh(0.5 * x) + 0.5


def _permute_cast_kernel(perm_ref, w_ref, o_ref):
    o_ref[...] = w_ref[...].astype(o_ref.dtype)


def _permute_cast(w, blocks):
    nl, k, _ = w.shape
    grid_spec = pltpu.PrefetchScalarGridSpec(
        num_scalar_prefetch=1, grid=(len(blocks),),
        in_specs=[pl.BlockSpec((nl, k, W_BLOCK), lambda j, perm: (0, 0, perm[j]))],
        out_specs=pl.BlockSpec((nl, k, W_BLOCK), lambda j, perm: (0, 0, j)))
    return pl.pallas_call(
        _permute_cast_kernel, grid_spec=grid_spec,
        out_shape=jax.ShapeDtypeStruct((nl, k, len(blocks) * W_BLOCK), BF16),
        compiler_params=_params(("arbitrary",)), name="permute_cast",
    )(jnp.asarray(blocks, jnp.int32), w)


def _mod_kernel(c_ref, w_ref, b_ref, o_ref):
    c = c_ref[...]
    a = c * _sigmoid(c)
    w = w_ref[...]
    a_hi = a.astype(BF16)
    a_lo = (a - a_hi.astype(F32)).astype(BF16)
    w_hi = w.astype(BF16)
    w_lo = (w - w_hi.astype(F32)).astype(BF16)
    o_ref[...] = _dot(a_hi, w_hi) + _dot(a_hi, w_lo) + _dot(a_lo, w_hi) + b_ref[...]


def _modulation(cvecs, mod_w, mod_b):
    tn = 768
    return pl.pallas_call(
        _mod_kernel,
        out_shape=jax.ShapeDtypeStruct((DEPTH, SUBLANES, 3 * D_MODEL), F32),
        grid=(DEPTH, 3 * D_MODEL // tn),
        in_specs=[pl.BlockSpec((SUBLANES, D_MODEL), lambda l, j: (0, 0)),
                  pl.BlockSpec((None, D_MODEL, tn), lambda l, j: (l, 0, j)),
                  pl.BlockSpec((None, 1, tn), lambda l, j: (l, 0, j))],
        out_specs=pl.BlockSpec((None, SUBLANES, tn), lambda l, j: (l, 0, j)),
        compiler_params=_params(("arbitrary", "arbitrary")),
        name="modulation",
    )(cvecs, mod_w, mod_b.reshape(DEPTH, 1, 3 * D_MODEL))


def _normed_input(x, mod, ng):
    ms = jnp.mean(x * x, axis=-1, keepdims=True)
    shift = mod[:, 0:D_MODEL]
    scale = mod[:, D_MODEL:2 * D_MODEL]
    return (x * lax.rsqrt(ms + EPS) * ng) * (1.0 + scale) + shift


def _layer_spec(shape, layer):
    zeros = (0,) * len(shape)
    return pl.BlockSpec((None,) + tuple(shape), lambda *_: (layer,) + zeros)


def _mod_spec(layer, row0, tiles_per_row):
    return pl.BlockSpec((None, None, 1, 3 * D_MODEL), lambda i: (layer, row0 + i // tiles_per_row, 0, 0))


class _Part:
    def __init__(self, name, body, grid, in_specs, args, out_shape, out_specs, scratch=()):
        self.name, self.body, self.grid = name, body, tuple(grid)
        self.in_specs, self.args = list(in_specs), list(args)
        self.out_shape, self.out_specs, self.scratch = list(out_shape), list(out_specs), list(scratch)


def _run(part):
    ni, no = len(part.args), len(part.out_shape)

    def kern(*refs):
        ids = tuple(pl.program_id(k) for k in range(len(part.grid)))
        part.body(refs[:ni], refs[ni:ni + no], refs[ni + no:], ids)

    return pl.pallas_call(
        kern, out_shape=part.out_shape, grid=part.grid, in_specs=part.in_specs, out_specs=part.out_specs,
        scratch_shapes=part.scratch, compiler_params=_params(("arbitrary",) * len(part.grid)), name=part.name)(*part.args)


def _front_body(ins, outs, scratch, ids, *, use_rope, n_prev=0, seq=None):
    x_ref, mod_ref, ng_ref, w_ref, gain_ref = ins[:5]
    qa_ref, ka_ref, qc_ref, kc_ref, va_ref, vc_ref, xb_ref = outs[:7]
    x = x_ref[...]
    tm = x.shape[0]
    hb = _normed_input(x, mod_ref[...], ng_ref[...]).astype(BF16)
    p1 = _dot(hb, w_ref[:, :N_NORM])
    lane = lax.broadcasted_iota(jnp.int32, (tm, LANES), 1)
    first_half = (lane & ROPE_FREQS) == 0
    wide = 2 * LANES
    same_head = (lax.broadcasted_iota(jnp.int32, (wide, wide), 0) // HD
                 == lax.broadcasted_iota(jnp.int32, (wide, wide), 1) // HD)
    ones_bd = jnp.where(same_head, 1.0, 0.0).astype(BF16)
    n_query = D_A + H_C * 2 * HD
    sums = []
    for c0 in range(0, N_NORM, wide):
        w = min(wide, N_NORM - c0)
        sq = p1[:, c0:c0 + w] * p1[:, c0:c0 + w]
        sq_hi = sq.astype(BF16)
        total = _dot(sq_hi, ones_bd[:w, :w])
        if c0 + w > n_query:
            total = total + _dot((sq - sq_hi.astype(F32)).astype(BF16), ones_bd[:w, :w])
        sums.append(total)
    dests = ([(qa_ref, c) for c in range(4)] + [(qc_ref, c) for c in range(4)]
             + [(kc_ref, c) for c in range(4)] + [(ka_ref, 0)])
    for c, (o_ref, oc) in enumerate(dests):
        pc = p1[:, c * LANES:(c + 1) * LANES]
        msq = sums[c // 2][:, (c % 2) * LANES:(c % 2 + 1) * LANES] * (1.0 / HD)
        y = pc * lax.rsqrt(msq + EPS) * gain_ref[:, c * LANES:(c + 1) * LANES]
        if use_rope:
            cos_ref, sin_ref = ins[5:7]
            partner = jnp.where(first_half, pltpu.roll(y, LANES - ROPE_FREQS, 1),
                                pltpu.roll(y, ROPE_FREQS, 1))
            y = y * cos_ref[...] + partner * sin_ref[...]
        o_ref[:, oc * LANES:(oc + 1) * LANES] = y.astype(o_ref.dtype)
    p2 = _dot(hb, w_ref[:, N_NORM:])
    va_ref[...] = p2[:, 0:KV_A * HD].astype(va_ref.dtype)
    vc_ref[...] = p2[:, KV_A * HD:KV_A * HD + D_C].astype(vc_ref.dtype)
    xb_ref[...] = p2[:, KV_A * HD + D_C:]
    if seq is not None:
        prev = ins[len(ins) - 4 * n_prev:]
        layers = [prev[4 * l:4 * l + 4] for l in range(n_prev)] + [(ka_ref, va_ref, kc_ref, vc_ref)]
        for a, st_ref in enumerate(outs[7:]):
            for l, arrays in enumerate(layers):
                for s in range(tm // seq):
                    rows = arrays[a][s * seq:(s + 1) * seq, :]
                    if a < 3:
                        st_ref[s, l] = rows.T
                    else:
                        for h in range(H_C):
                            st_ref[s, l, pl.ds(h, seq, stride=H_C), :] = rows[:, h * DV_C:(h + 1) * DV_C]


def _front_part(x2d, mod4, ng3, w_front, gains3, rope, *, layer, mod_row0, tokens_per_mod, kv_dtype, tm=TM_FRONT,
                stack_seq=None, prev_caches=()):
    n = x2d.shape[0]
    use_rope = rope is not None
    row = lambda i: (i, 0)
    in_specs = [pl.BlockSpec((tm, D_MODEL), row),
                _mod_spec(layer, mod_row0, tokens_per_mod // tm),
                _layer_spec((1, D_MODEL), layer),
                _layer_spec((D_MODEL, N_NORM + N_PLAIN), layer),
                _layer_spec((1, N_NORM), layer)]
    args = [x2d, mod4, ng3, w_front, gains3]
    if use_rope:
        per_seq = rope[0].shape[0] // tm
        in_specs += [pl.BlockSpec((tm, LANES), lambda i: (i % per_seq, 0))] * 2
        args += list(rope)
    widths = (D_A, KV_A * HD, H_C * 2 * HD, H_C * 2 * HD, KV_A * HD, D_C, D_LRU)
    dtypes = (BF16, kv_dtype, BF16, kv_dtype, kv_dtype, kv_dtype, F32)
    out_shape = [jax.ShapeDtypeStruct((n, w), d) for w, d in zip(widths, dtypes)]
    out_specs = [pl.BlockSpec((tm, w), row) for w in widths]
    if stack_seq is not None:
        assert len(prev_caches) == DEPTH - 1 and tm % stack_seq == 0
        for cache in prev_caches:
            in_specs += [pl.BlockSpec((tm, arr.shape[1]), row) for arr in cache]
            args += list(cache)
        per_tile = tm // stack_seq
        cache_w = (KV_A * HD, KV_A * HD, H_C * 2 * HD, D_C)
        shapes = [(DEPTH, w, stack_seq) for w in cache_w[:3]] + [(DEPTH, stack_seq * H_C, DV_C)]
        out_shape += [jax.ShapeDtypeStruct((n // stack_seq,) + s, F32) for s in shapes]
        out_specs += [pl.BlockSpec((per_tile,) + s, lambda i: (i, 0, 0, 0)) for s in shapes]
    body = functools.partial(_front_body, use_rope=use_rope, n_prev=len(prev_caches), seq=stack_seq)
    return _Part("front_rope" if use_rope else "front", body, (n // tm,), in_specs, args, out_shape, out_specs)


def _attn_a_kernel(*refs, tq, nsub, seq_len, banded, has_ctx, layer):
    if has_ctx:
        sink_ref, q_ref, k_ref, v_ref, kc_ref, vc_ref, o_ref = refs
    else:
        sink_ref, q_ref, k_ref, v_ref, o_ref = refs
    i = pl.program_id(1)
    g = H_A // KV_A

    def spread(k, v):
        k = k.astype(F32)
        v = v.astype(F32)
        lo = lax.broadcasted_iota(jnp.int32, k.shape, 1) < HD
        k_sw = pltpu.roll(k, HD, 1)
        v_sw = pltpu.roll(v, HD, 1)
        kd = (jnp.where(lo, k, k_sw).astype(BF16), jnp.where(lo, k_sw, k).astype(BF16))
        vd = ((jnp.where(lo, v, 1.0).astype(BF16), jnp.where(lo, 1.0, v_sw).astype(BF16)),
              (jnp.where(lo, v_sw, 1.0).astype(BF16), jnp.where(lo, 1.0, v).astype(BF16)))
        return kd, vd

    if has_ctx:
        kd_c, vd_c = spread(kc_ref[...], vc_ref[...])
    nwin = tq + 2 * WINDOW if banded else tq
    lo_q = lax.broadcasted_iota(jnp.int32, (tq, LANES), 1) < HD
    zero = jnp.zeros((tq, LANES), BF16)
    for u in range(nsub):
        blk = i * nsub + u
        if banded:
            start = pl.multiple_of(jnp.clip(blk * tq - WINDOW, 0, seq_len - nwin), WINDOW)
            kd, vd = spread(k_ref[pl.ds(start, nwin), :], v_ref[pl.ds(start, nwin), :])
            kpos = start + lax.broadcasted_iota(jnp.int32, (tq, nwin), 1)
            qpos = blk * tq + lax.broadcasted_iota(jnp.int32, (tq, nwin), 0)
            bias = jnp.where(jnp.abs(kpos - qpos) <= WINDOW, 0.0, -jnp.inf)
        else:
            own = pl.multiple_of(blk * tq, tq)
            kd, vd = spread(k_ref[pl.ds(own, tq), :], v_ref[pl.ds(own, tq), :])
        if has_ctx:
            kd = tuple(jnp.concatenate([kd[kv], kd_c[kv]], axis=0) for kv in range(KV_A))
            vd = tuple(tuple(jnp.concatenate([vd[kv][half], vd_c[kv][half]], axis=0) for half in range(2))
                       for kv in range(KV_A))
        q = q_ref[u * tq:(u + 1) * tq, :]
        outs = []
        for h in range(H_A):
            kv, half = h // g, h % 2
            qz = jnp.where(lo_q if half == 0 else jnp.logical_not(lo_q),
                           q[:, (h // 2) * LANES:(h // 2 + 1) * LANES], zero)
            s = _dot_nt(qz, kd[kv])
            if banded:
                s = jnp.concatenate([s[:, :nwin] + bias, s[:, nwin:]], axis=1)
            snk = sink_ref[layer, h] * LOG2E
            m = jnp.maximum(jnp.max(s, axis=-1, keepdims=True), snk)
            e = jnp.exp2(s - m)
            pv = _dot(e.astype(BF16), vd[kv][half])
            outs.append(pv / (pltpu.roll(pv, HD, 1) + jnp.exp2(snk - m)))
        for c in range(H_A // 2):
            pair = jnp.where(lo_q, outs[2 * c], outs[2 * c + 1])
            o_ref[u * tq:(u + 1) * tq, c * LANES:(c + 1) * LANES] = pair.astype(o_ref.dtype)


def _attn_a(sink, q, k, v, k_ctx, v_ctx, *, layer, tq, nsub, banded):
    b, t, _ = q.shape
    has_ctx = k_ctx is not None
    rows = tq * nsub
    kv_spec = pl.BlockSpec((None, t, KV_A * HD), lambda bi, i: (bi, 0, 0))
    in_specs = [pl.BlockSpec(memory_space=pltpu.SMEM),
                pl.BlockSpec((None, rows, D_A), lambda bi, i: (bi, i, 0)), kv_spec, kv_spec]
    args = [sink, q, k, v]
    if has_ctx:
        s = k_ctx.shape[2]
        in_specs += [pl.BlockSpec((None, None, s, KV_A * HD), lambda bi, i: (bi, layer, 0, 0))] * 2
        args += [k_ctx, v_ctx]
    return pl.pallas_call(
        functools.partial(_attn_a_kernel, tq=tq, nsub=nsub, seq_len=t, banded=banded, has_ctx=has_ctx, layer=layer),
        out_shape=jax.ShapeDtypeStruct((b, t, D_A), BF16),
        grid=(b, t // rows),
        in_specs=in_specs,
        out_specs=pl.BlockSpec((None, rows, D_A), lambda bi, i: (bi, i, 0)),
        compiler_params=_params(("arbitrary", "arbitrary")),
        name="attn_a_latent" if has_ctx else "attn_a_context",
    )(*args)


def _attn_c_kernel(*refs, tq, ts, n_loc, has_ctx, lam_init):
    if has_ctx:
        lq1, lk1, lq2, lk2, sub_ref, q_ref, k_ref, v_ref, kc_ref, vc_ref, o_ref = refs
    else:
        lq1, lk1, lq2, lk2, sub_ref, q_ref, k_ref, v_ref, o_ref = refs
    lam = (jnp.exp(jnp.sum(lq1[...] * lk1[...], axis=-1, keepdims=True))
           - jnp.exp(jnp.sum(lq2[...] * lk2[...], axis=-1, keepdims=True)) + lam_init)
    lo = lax.broadcasted_iota(jnp.int32, (tq, LANES), 1) < HD
    zero = jnp.zeros((tq, LANES), BF16)
    heads = [slice(h * LANES, (h + 1) * LANES) for h in range(H_C)]
    qz = []
    for cols in heads:
        q12 = q_ref[:, cols]
        qz.append(jnp.concatenate([jnp.where(lo, q12, zero), jnp.where(lo, zero, q12)], axis=0))

    def update(carry, kt, vt):
        new = []
        ones = jnp.ones((kt.shape[0], LANES), BF16)
        for h, cols in enumerate(heads):
            m, acc = carry[h]
            s = _dot_nt(qz[h], kt[:, cols].astype(BF16))
            m_new = jnp.maximum(m, jnp.max(s, axis=-1, keepdims=True))
            alpha = jnp.exp2(m - m_new)
            p = jnp.exp2(s - m_new)
            v_ones = jnp.concatenate([vt[:, cols].astype(BF16), ones], axis=1)
            acc = alpha * acc + _dot(p.astype(BF16), v_ones)
            new.append((m_new, acc))
        return tuple(new)

    carry = tuple((jnp.full((2 * tq, 1), -jnp.inf, F32), jnp.zeros((2 * tq, 2 * LANES), F32)) for _ in heads)
    for j in range(n_loc):
        carry = update(carry, k_ref[j * ts:(j + 1) * ts, :], v_ref[j * ts:(j + 1) * ts, :])
    if has_ctx:
        carry = update(carry, kc_ref[...], vc_ref[...])
    for h, cols in enumerate(heads):
        _, acc = carry[h]
        num, den = acc[:, :DV_C], acc[:, DV_C:]
        o = num[:tq] / den[:tq] - lam * (num[tq:] / den[tq:])
        o = o * lax.rsqrt(jnp.mean(o * o, axis=-1, keepdims=True) + EPS) * sub_ref[...] * (1.0 - lam_init)
        o_ref[:, cols] = o.astype(o_ref.dtype)


def _attn_c(lam_vecs, subln3, q, k, v, k_ctx, v_ctx, *, layer, tq, ts, lam_init):
    b, t, _ = q.shape
    s_loc = k.shape[1]
    has_ctx = k_ctx is not None
    kv_spec = pl.BlockSpec((None, s_loc, D_C), lambda bi, i: (bi, 0, 0))
    in_specs = ([_layer_spec((1, HD), layer)] * 4 + [_layer_spec((1, DV_C), layer)]
                + [pl.BlockSpec((None, tq, D_C), lambda bi, i: (bi, i, 0)), kv_spec, kv_spec])
    args = list(lam_vecs) + [subln3, q, k, v]
    if has_ctx:
        sc = k_ctx.shape[2]
        in_specs += [pl.BlockSpec((None, None, sc, D_C), lambda bi, i: (bi, layer, 0, 0))] * 2
        args += [k_ctx, v_ctx]
    return pl.pallas_call(
        functools.partial(_attn_c_kernel, tq=tq, ts=ts, n_loc=s_loc // ts, has_ctx=has_ctx, lam_init=lam_init),
        out_shape=jax.ShapeDtypeStruct((b, t, D_C), BF16),
        grid=(b, t // tq),
        in_specs=in_specs,
        out_specs=pl.BlockSpec((None, tq, D_C), lambda bi, i: (bi, i, 0)),
        compiler_params=_params(("arbitrary", "arbitrary")),
        name="attn_c_latent" if has_ctx else "attn_c_context",
    )(*args)


def _lru_body(ins, outs, scratch, ids, *, tc, nc):
    xc_ref, xp_ref, xn_ref, cw_ref, cb_ref, wg_ref, bg_ref, lam_ref, h0_ref, perm_ref, inv_ref = ins
    ob_ref, st_ref = outs
    a_s, b_s, hs, ps, hf_s, xb_s, hcar, last_h, last_p, cin = scratch
    _, p, c = ids
    cidx = c + p * (nc - 1 - 2 * c)
    r0 = pl.multiple_of(cidx * tc, tc)
    ng = tc // SUBLANES

    @pl.when(p == 0)
    def _():
        cur = xc_ref[...]
        prev = jnp.where(cidx > 0, xp_ref[...], 0.0)
        nxt = jnp.where(cidx < nc - 1, xn_ref[...], 0.0)
        hi = cur.astype(BF16)
        rest = cur - hi.astype(F32)
        mid = rest.astype(BF16)
        lo = (rest - mid.astype(F32)).astype(BF16)
        x3 = (_dot(perm_ref[...], hi) + _dot(perm_ref[...], mid) + _dot(perm_ref[...], lo)).reshape(ng, SUBLANES, D_LRU)
        rows = lax.broadcasted_iota(jnp.int32, (SUBLANES, D_LRU), 0)
        before1 = jnp.where(rows == 0, prev[SUBLANES - 1:SUBLANES, :], pltpu.roll(x3[ng - 1], 1, 0))
        before2 = jnp.where(rows == 0, prev[SUBLANES - 2:SUBLANES - 1, :], pltpu.roll(x3[ng - 2], 1, 0))
        after1 = jnp.where(rows == SUBLANES - 1, nxt[0:1, :], pltpu.roll(x3[0], SUBLANES - 1, 0))
        xpad = jnp.concatenate([before2[None], before1[None], x3, after1[None]], axis=0)
        acc = jnp.broadcast_to(cb_ref[...], (ng, SUBLANES, D_LRU))
        for j in range(CONV_W):
            acc = acc + xpad[j:j + ng] * cw_ref[j:j + 1, :]
        xb_s[pl.ds(r0, tc), :] = acc.reshape(tc, D_LRU)

    xb = xb_s[pl.ds(r0, tc), :]
    gm = _dot(xb.astype(BF16), wg_ref[...]) + bg_ref[...]
    r = _sigmoid(gm[:, :D_LRU])
    ig = _sigmoid(gm[:, D_LRU:])
    nl = -lam_ref[...]
    softplus = jnp.maximum(nl, 0.0) + jnp.log1p(jnp.exp(-jnp.abs(nl)))
    decay = LRU_C * softplus
    a = jnp.exp2(r * (decay * (-LOG2E)))
    one_m_a2 = jnp.tanh(r * decay) * (a * a + 1.0)
    bb = jnp.where(one_m_a2 == 0.0, 0.0, one_m_a2 * lax.rsqrt(one_m_a2)) * (ig * xb)

    a_s[...] = a
    b_s[...] = bb

    def step(gi, carry):
        h, pc = carry
        g = gi + p * (ng - 1 - 2 * gi)
        base = pl.multiple_of(g * SUBLANES, SUBLANES)
        a_g = a_s[pl.ds(base, SUBLANES), :]
        h = a_g * h + b_s[pl.ds(base, SUBLANES), :]
        pc = a_g * pc
        hs[pl.ds(base, SUBLANES), :] = h
        ps[pl.ds(base, SUBLANES), :] = pc
        return h, pc

    init = (jnp.zeros((SUBLANES, D_LRU), F32), jnp.ones((SUBLANES, D_LRU), F32))
    last_h[...], last_p[...] = lax.fori_loop(0, ng, step, init, unroll=8)

    @pl.when(c == 0)
    def _():
        hcar[0:1, :] = h0_ref[pl.ds(p, 1), :]

    state = hcar[0:1, :]
    for rr in range(SUBLANES):
        r_in = rr + p * (SUBLANES - 1 - 2 * rr)
        cin[pl.ds(r_in, 1), :] = state
        state = last_p[pl.ds(r_in, 1), :] * state + last_h[pl.ds(r_in, 1), :]
    hcar[0:1, :] = state
    hfull = (hs[...].reshape(ng, SUBLANES, D_LRU)
             + ps[...].reshape(ng, SUBLANES, D_LRU) * cin[...]).reshape(tc, D_LRU)

    @pl.when(p == 0)
    def _():
        hf_s[pl.ds(r0, tc), :] = hfull

    @pl.when(p == 1)
    def _():
        both = (hf_s[pl.ds(r0, tc), :] + hfull).astype(BF16)
        ob_ref[...] = _dot(inv_ref[...], both).astype(ob_ref.dtype)

    @pl.when(c == nc - 1)
    def _():
        st_ref[pl.ds(p, 1), :] = state


def _lru_part(xb, conv_w, conv_b3, wg, bg, lam4, h0, *, layer, h0_layer, tc):
    b, t, _ = xb.shape
    nc = t // tc
    nb8 = t // SUBLANES
    per = tc // SUBLANES
    cidx = lambda p, c: c + p * (nc - 1 - 2 * c)
    in_specs = [
        pl.BlockSpec((None, tc, D_LRU), lambda bi, p, c: (bi, cidx(p, c), 0)),
        pl.BlockSpec((None, SUBLANES, D_LRU), lambda bi, p, c: (bi, jnp.maximum(cidx(p, c) * per - 1, 0), 0)),
        pl.BlockSpec((None, SUBLANES, D_LRU), lambda bi, p, c: (bi, jnp.minimum((cidx(p, c) + 1) * per, nb8 - 1), 0)),
        _layer_spec((CONV_W, D_LRU), layer),
        _layer_spec((1, D_LRU), layer),
        pl.BlockSpec((None, None, D_LRU, 2 * D_LRU), lambda bi, p, c: (layer, p, 0, 0)),
        pl.BlockSpec((None, None, 1, 2 * D_LRU), lambda bi, p, c: (layer, p, 0, 0)),
        pl.BlockSpec((None, None, 1, D_LRU), lambda bi, p, c: (layer, p, 0, 0)),
        pl.BlockSpec((None, None, 2, D_LRU), lambda bi, p, c: (bi, h0_layer, 0, 0)),
        pl.BlockSpec((tc, tc), lambda bi, p, c: (0, 0)),
        pl.BlockSpec((tc, tc), lambda bi, p, c: (0, 0)),
    ]
    j = jnp.arange(tc)
    src = (j % SUBLANES) * (tc // SUBLANES) + j // SUBLANES
    perm = (src[:, None] == j[None, :]).astype(BF16)
    chunk = (tc, D_LRU)
    row8 = (SUBLANES, D_LRU)
    return _Part("lru", functools.partial(_lru_body, tc=tc, nc=nc), (b, 2, nc), in_specs,
                 [xb, xb, xb, conv_w, conv_b3, wg, bg, lam4, h0, perm, perm.T],
                 [jax.ShapeDtypeStruct((b, t, D_LRU), BF16), jax.ShapeDtypeStruct((b, 2, D_LRU), F32)],
                 [pl.BlockSpec((None, tc, D_LRU), lambda bi, p, c: (bi, nc - 1 - p * c, 0)),
                  pl.BlockSpec((None, 2, D_LRU), lambda bi, p, c: (bi, 0, 0))],
                 [pltpu.VMEM(chunk, F32), pltpu.VMEM(chunk, F32), pltpu.VMEM(chunk, F32), pltpu.VMEM(chunk, F32),
                  pltpu.VMEM((t, D_LRU), F32), pltpu.VMEM((t, D_LRU), F32),
                  pltpu.VMEM(row8, F32), pltpu.VMEM(row8, F32), pltpu.VMEM(row8, F32), pltpu.VMEM(row8, F32)])


def _tail_kernel(x_ref, mod_ref, ng_ref, oa_ref, ob_ref, oc_ref, wgm_ref, wbr_ref, wout_ref, o_ref):
    x = x_ref[...]
    mod = mod_ref[...]
    hb = _normed_input(x, mod, ng_ref[...]).astype(BF16)
    y = None
    for br, br_ref in enumerate((oa_ref, ob_ref, oc_ref)):
        gm = _dot(hb, wgm_ref[:, br * N_BRANCH_COLS:(br + 1) * N_BRANCH_COLS])
        gt = gm[:, :D_A]
        u = (br_ref[...].astype(F32) * (gt * _sigmoid(gt))).astype(BF16)
        proj = _dot(u, wbr_ref[br])
        mg = _sigmoid(gm[:, D_A:])
        y = mg * proj if y is None else y + mg * proj
    gate = mod[:, 2 * D_MODEL:]
    o_ref[...] = x + gate * _dot(y.astype(BF16), wout_ref[...])


def _tail_part(x2d, mod4, ng3, oa, ob, oc, wgm, wbr, wout, *, layer, mod_row0, tokens_per_mod, tm=TM_TAIL):
    n = x2d.shape[0]
    row = lambda i: (i, 0)
    once = dict(pipeline_mode=pl.Buffered(1))

    def weight(shape):
        zeros = (0,) * len(shape)
        return pl.BlockSpec((None,) + shape, lambda i: (layer,) + zeros, **once)

    in_specs = [pl.BlockSpec((tm, D_MODEL), row),
                _mod_spec(layer, mod_row0, tokens_per_mod // tm),
                _layer_spec((1, D_MODEL), layer),
                pl.BlockSpec((tm, D_A), row),
                pl.BlockSpec((tm, D_LRU), row),
                pl.BlockSpec((tm, D_C), row),
                weight((D_MODEL, 3 * N_BRANCH_COLS)),
                weight((3, D_A, D_MODEL)),
                weight((D_MODEL, D_MODEL))]
    return _Part("tail", lambda ins, outs, scratch, ids: _tail_kernel(*ins, *outs), (n // tm,), in_specs,
                 [x2d, mod4, ng3, oa, ob, oc, wgm, wbr, wout],
                 [jax.ShapeDtypeStruct((n, D_MODEL), F32)], [pl.BlockSpec((tm, D_MODEL), row)])


def _rope_tables(seq_len):
    pos = jnp.arange(seq_len)
    row = (pos // GRID_W).astype(F32)
    col = (pos % GRID_W).astype(F32)
    inv = jnp.power(ROPE_BASE, -jnp.arange(ROPE_FREQS, dtype=F32) / ROPE_FREQS)
    ang_r = row[:, None] * inv
    ang_c = col[:, None] * inv
    cos = jnp.concatenate([jnp.cos(ang_r)] * 2 + [jnp.cos(ang_c)] * 2, axis=-1)
    sin = jnp.concatenate([-jnp.sin(ang_r), jnp.sin(ang_r), -jnp.sin(ang_c), jnp.sin(ang_c)], axis=-1)
    return jnp.tile(cos, (1, LANES // HD)), jnp.tile(sin, (1, LANES // HD))


def _block_diag(w):
    rows = w.reshape(w.shape[:-3] + (D_LRU, LRU_BW))
    tiled = jnp.tile(rows, (1,) * (rows.ndim - 1) + (LRU_BLOCKS,))
    blk = jnp.arange(D_LRU) // LRU_BW
    return jnp.where(blk[:, None] == blk[None, :], tiled, 0.0)


def kernel(x_prompt, x_sample, cache_a_k, cache_a_v, cache_c_k, cache_c_v, state_lru, c, c_ctx, norm_g, mod_w, mod_b, w_in, qn_a, kn_a, sink_a, conv_w, conv_b, lru_wa, lru_ba, lru_wx, lru_bx, lru_lam, qn_c, kn_c, lam_q1, lam_k1, lam_q2, lam_k2, subln_c, w_br_a, w_br_b, w_br_c, w_out):
    bp, sp, _ = x_prompt.shape
    bs, ss, _ = x_sample.shape
    past = cache_a_k.shape[2]

    cvecs = jnp.concatenate([c, c_ctx[None, :], jnp.zeros((SUBLANES - bs - 1, D_MODEL), F32)], axis=0)
    mod = _modulation(cvecs, mod_w, mod_b)
    mod4 = mod.reshape(DEPTH, SUBLANES, 1, 3 * D_MODEL)
    rope = _rope_tables(ss)

    w_front = _permute_cast(w_in, FRONT_BLOCKS)
    wgm = _permute_cast(w_in, TAIL_BLOCKS)
    wbr = jnp.stack([w_br_a, w_br_b, w_br_c], axis=1).astype(BF16)
    wout = w_out.astype(BF16)
    gains3 = jnp.concatenate([jnp.tile(qn_a * Q_SCALE, (1, H_A)), jnp.tile(qn_c * Q_SCALE, (1, 2 * H_C)),
                              jnp.tile(kn_c, (1, 2 * H_C)), jnp.tile(kn_a, (1, KV_A))],
                             axis=-1)[:, None, :]
    ng3 = norm_g[:, None, :]
    lru_wg = jnp.concatenate([_block_diag(lru_wa), _block_diag(lru_wx)], axis=-1).astype(BF16)
    lru_bg = jnp.concatenate([lru_ba, lru_bx], axis=-1)[:, :, None, :]
    lru_lam4 = lru_lam[:, :, None, :]
    lru_p = (conv_w, conv_b[:, None, :], lru_wg, lru_bg, lru_lam4)
    lam_vecs = [v[:, None, :] for v in (lam_q1, lam_k1, lam_q2, lam_k2)]
    subln3 = subln_c[:, None, :]
    cka = cache_a_k.reshape(bs, DEPTH, past, KV_A * HD)
    cva = cache_a_v.reshape(bs, DEPTH, past, KV_A * HD)
    ckc = cache_c_k.reshape(bs, DEPTH, past, D_C)
    cvc = cache_c_v.reshape(bs, DEPTH, past, D_C)
    zeros_h0 = jnp.zeros((bp, 1, 2, D_LRU), F32)

    xp = x_prompt.reshape(bp * sp, D_MODEL)
    xs = x_sample.reshape(bs * ss, D_MODEL)
    new_caches, new_states = [], []
    for l in range(DEPTH):
        lam_init = 0.8 - 0.6 * math.exp(-0.3 * l)
        ctx_mod = dict(layer=l, mod_row0=bs, tokens_per_mod=bp * sp)
        lat_mod = dict(layer=l, mod_row0=0, tokens_per_mod=ss)

        flat = lambda a: a.reshape(-1, a.shape[-1])
        r3 = lambda a: a.reshape(bp, sp, a.shape[-1])
        stacking = dict(stack_seq=sp, prev_caches=new_caches) if l == DEPTH - 1 else {}
        qa, ka, qc, kc, va, vc, xb, *stacked = _run(
            _front_part(xp, mod4, ng3, w_front, gains3, None, kv_dtype=F32, **ctx_mod, **stacking))
        new_caches.append((ka, va, kc, vc))
        oa = _attn_a(sink_a, r3(qa), r3(ka), r3(va), None, None, layer=l, tq=sp, nsub=1, banded=False)
        ob, st = _run(_lru_part(r3(xb), *lru_p, zeros_h0, layer=l, h0_layer=0, tc=sp))
        new_states.append(st)
        oc = _attn_c(lam_vecs, subln3, r3(qc), r3(kc), r3(vc), None, None, layer=l, tq=sp, ts=sp, lam_init=lam_init)
        (xp,) = _run(_tail_part(xp, mod4, ng3, flat(oa), flat(ob), flat(oc), wgm, wbr, wout, **ctx_mod))

        r3 = lambda a: a.reshape(bs, ss, a.shape[-1])
        qa, ka, qc, kc, va, vc, xb = _run(_front_part(xs, mod4, ng3, w_front, gains3, rope, kv_dtype=BF16, **lat_mod))
        oa = _attn_a(sink_a, r3(qa), r3(ka), r3(va), cka, cva, layer=l, tq=TQ_A, nsub=NSUB_A, banded=True)
        ob, _ = _run(_lru_part(r3(xb), *lru_p, state_lru, layer=l, h0_layer=l, tc=TC_LRU))
        oc = _attn_c(lam_vecs, subln3, r3(qc), r3(kc), r3(vc), ckc, cvc, layer=l, tq=TQ_C, ts=TS_C, lam_init=lam_init)
        (xs,) = _run(_tail_part(xs, mod4, ng3, flat(oa), flat(ob), flat(oc), wgm, wbr, wout, **lat_mod))

    ka_t, va_t, kc_t, vc = stacked
    return (xp.reshape(bp, sp, D_MODEL), xs.reshape(bs, ss, D_MODEL),
            ka_t.reshape(bp, DEPTH, KV_A, HD, sp).transpose(0, 1, 4, 2, 3),
            va_t.reshape(bp, DEPTH, KV_A, HD, sp).transpose(0, 1, 4, 2, 3),
            kc_t.reshape(bp, DEPTH, H_C, 2, HD, sp).transpose(0, 1, 5, 2, 3, 4),
            vc.reshape(bp, DEPTH, sp, H_C, DV_C),
            jnp.stack(new_states, axis=1))
```

```python
import functools
import math

import jax
import jax.numpy as jnp
from jax import lax
from jax.experimental import pallas as pl
from jax.experimental.pallas import tpu as pltpu

F32 = jnp.float32
BF16 = jnp.bfloat16

D_MODEL = 1024
DEPTH = 2
GRID_W = 64
HD = 64
SCALE = 1.0 / math.sqrt(HD)
LOG2E = math.log2(math.e)
Q_SCALE = SCALE * LOG2E
H_A = 8
KV_A = 2
WINDOW = 128
D_A = H_A * HD
D_LRU = 512
LRU_BLOCKS = 8
LRU_BW = D_LRU // LRU_BLOCKS
CONV_W = 4
LRU_C = 8.0
H_C = 4
DV_C = 2 * HD
D_C = H_C * DV_C
ROPE_BASE = 10000.0
ROPE_FREQS = HD // 4
EPS = 1e-6
LANES = 128
SUBLANES = 8
VMEM_LIMIT = 56 * 1024 * 1024

_SECTIONS = (("qa", D_A), ("kava", 2 * KV_A * HD), ("ga", D_A), ("xb", D_LRU), ("gb", D_LRU),
             ("qc", H_C * 2 * HD), ("kc", H_C * 2 * HD), ("vc", D_C), ("gc", D_C),
             ("mg_a", D_MODEL), ("mg_b", D_MODEL), ("mg_c", D_MODEL))
W_BLOCK = 256


def _col_blocks(*names):
    start, spans = 0, {}
    for name, width in _SECTIONS:
        spans[name] = (start, start + width)
        start += width
    blocks = []
    for name in names:
        a, b = spans[name]
        assert a % W_BLOCK == 0 and b % W_BLOCK == 0
        blocks += range(a // W_BLOCK, b // W_BLOCK)
    return tuple(blocks)


FRONT_BLOCKS = _col_blocks("qa", "qc", "kc", "kava", "vc", "xb")
TAIL_BLOCKS = _col_blocks("ga", "mg_a", "gb", "mg_b", "gc", "mg_c")
N_BRANCH_COLS = D_A + D_MODEL
N_NORM = D_A + KV_A * HD + 2 * H_C * 2 * HD
N_PLAIN = KV_A * HD + D_C + D_LRU

TM_FRONT = 512
TM_TAIL = 512
TQ_A = 2 * WINDOW
NSUB_A = 4
TQ_C = 512
TS_C = 512
TC_LRU = 512


def _params(sem, vmem=VMEM_LIMIT):
    return pltpu.CompilerParams(dimension_semantics=sem, vmem_limit_bytes=vmem)


def _dot(a, b):
    return jnp.dot(a, b, preferred_element_type=F32)


def _dot_nt(a, b):
    return lax.dot_general(a, b, (((1,), (1,)), ((), ())), preferred_element_type=F32)


def _sigmoid(x):
    return 0.5 * jnp.tanh(0.5 * x) + 0.5


def _permute_cast_kernel(perm_ref, w_ref, o_ref):
    o_ref[...] = w_ref[...].astype(o_ref.dtype)


def _permute_cast(w, blocks):
    nl, k, _ = w.shape
    grid_spec = pltpu.PrefetchScalarGridSpec(
        num_scalar_prefetch=1, grid=(len(blocks),),
        in_specs=[pl.BlockSpec((nl, k, W_BLOCK), lambda j, perm: (0, 0, perm[j]))],
        out_specs=pl.BlockSpec((nl, k, W_BLOCK), lambda j, perm: (0, 0, j)))
    return pl.pallas_call(
        _permute_cast_kernel, grid_spec=grid_spec,
        out_shape=jax.ShapeDtypeStruct((nl, k, len(blocks) * W_BLOCK), BF16),
        compiler_params=_params(("arbitrary",)), name="permute_cast",
    )(jnp.asarray(blocks, jnp.int32), w)


def _mod_kernel(c_ref, w_ref, b_ref, o_ref):
    c = c_ref[...]
    a = c * _sigmoid(c)
    w = w_ref[...]
    a_hi = a.astype(BF16)
    a_lo = (a - a_hi.astype(F32)).astype(BF16)
    w_hi = w.astype(BF16)
    w_lo = (w - w_hi.astype(F32)).astype(BF16)
    o_ref[...] = _dot(a_hi, w_hi) + _dot(a_hi, w_lo) + _dot(a_lo, w_hi) + b_ref[...]


def _modulation(cvecs, mod_w, mod_b):
    tn = 768
    return pl.pallas_call(
        _mod_kernel,
        out_shape=jax.ShapeDtypeStruct((DEPTH, SUBLANES, 3 * D_MODEL), F32),
        grid=(DEPTH, 3 * D_MODEL // tn),
        in_specs=[pl.BlockSpec((SUBLANES, D_MODEL), lambda l, j: (0, 0)),
                  pl.BlockSpec((None, D_MODEL, tn), lambda l, j: (l, 0, j)),
                  pl.BlockSpec((None, 1, tn), lambda l, j: (l, 0, j))],
        out_specs=pl.BlockSpec((None, SUBLANES, tn), lambda l, j: (l, 0, j)),
        compiler_params=_params(("arbitrary", "arbitrary")),
        name="modulation",
    )(cvecs, mod_w, mod_b.reshape(DEPTH, 1, 3 * D_MODEL))


def _normed_input(x, mod, ng):
    ms = jnp.mean(x * x, axis=-1, keepdims=True)
    shift = mod[:, 0:D_MODEL]
    scale = mod[:, D_MODEL:2 * D_MODEL]
    return (x * lax.rsqrt(ms + EPS) * ng) * (1.0 + scale) + shift


def _layer_spec(shape, layer):
    zeros = (0,) * len(shape)
    return pl.BlockSpec((None,) + tuple(shape), lambda *_: (layer,) + zeros)


def _mod_spec(layer, row0, tiles_per_row):
    return pl.BlockSpec((None, None, 1, 3 * D_MODEL), lambda i: (layer, row0 + i // tiles_per_row, 0, 0))


class _Part:
    def __init__(self, name, body, grid, in_specs, args, out_shape, out_specs, scratch=()):
        self.name, self.body, self.grid = name, body, tuple(grid)
        self.in_specs, self.args = list(in_specs), list(args)
        self.out_shape, self.out_specs, self.scratch = list(out_shape), list(out_specs), list(scratch)


def _run(part):
    ni, no = len(part.args), len(part.out_shape)

    def kern(*refs):
        ids = tuple(pl.program_id(k) for k in range(len(part.grid)))
        part.body(refs[:ni], refs[ni:ni + no], refs[ni + no:], ids)

    return pl.pallas_call(
        kern, out_shape=part.out_shape, grid=part.grid, in_specs=part.in_specs, out_specs=part.out_specs,
        scratch_shapes=part.scratch, compiler_params=_params(("arbitrary",) * len(part.grid)), name=part.name)(*part.args)


def _front_body(ins, outs, scratch, ids, *, use_rope, n_prev=0, seq=None):
    x_ref, mod_ref, ng_ref, w_ref, gain_ref = ins[:5]
    qa_ref, ka_ref, qc_ref, kc_ref, va_ref, vc_ref, xb_ref = outs[:7]
    x = x_ref[...]
    tm = x.shape[0]
    hb = _normed_input(x, mod_ref[...], ng_ref[...]).astype(BF16)
    p1 = _dot(hb, w_ref[:, :N_NORM])
    lane = lax.broadcasted_iota(jnp.int32, (tm, LANES), 1)
    first_half = (lane & ROPE_FREQS) == 0
    wide = 2 * LANES
    same_head = (lax.broadcasted_iota(jnp.int32, (wide, wide), 0) // HD
                 == lax.broadcasted_iota(jnp.int32, (wide, wide), 1) // HD)
    ones_bd = jnp.where(same_head, 1.0, 0.0).astype(BF16)
    n_query = D_A + H_C * 2 * HD
    sums = []
    for c0 in range(0, N_NORM, wide):
        w = min(wide, N_NORM - c0)
        sq = p1[:, c0:c0 + w] * p1[:, c0:c0 + w]
        sq_hi = sq.astype(BF16)
        total = _dot(sq_hi, ones_bd[:w, :w])
        if c0 + w > n_query:
            total = total + _dot((sq - sq_hi.astype(F32)).astype(BF16), ones_bd[:w, :w])
        sums.append(total)
    dests = ([(qa_ref, c) for c in range(4)] + [(qc_ref, c) for c in range(4)]
             + [(kc_ref, c) for c in range(4)] + [(ka_ref, 0)])
    for c, (o_ref, oc) in enumerate(dests):
        pc = p1[:, c * LANES:(c + 1) * LANES]
        msq = sums[c // 2][:, (c % 2) * LANES:(c % 2 + 1) * LANES] * (1.0 / HD)
        y = pc * lax.rsqrt(msq + EPS) * gain_ref[:, c * LANES:(c + 1) * LANES]
        if use_rope:
            cos_ref, sin_ref = ins[5:7]
            partner = jnp.where(first_half, pltpu.roll(y, LANES - ROPE_FREQS, 1),
                                pltpu.roll(y, ROPE_FREQS, 1))
            y = y * cos_ref[...] + partner * sin_ref[...]
        o_ref[:, oc * LANES:(oc + 1) * LANES] = y.astype(o_ref.dtype)
    p2 = _dot(hb, w_ref[:, N_NORM:])
    va_ref[...] = p2[:, 0:KV_A * HD].astype(va_ref.dtype)
    vc_ref[...] = p2[:, KV_A * HD:KV_A * HD + D_C].astype(vc_ref.dtype)
    xb_ref[...] = p2[:, KV_A * HD + D_C:]
    if seq is not None:
        prev = ins[len(ins) - 4 * n_prev:]
        layers = [prev[4 * l:4 * l + 4] for l in range(n_prev)] + [(ka_ref, va_ref, kc_ref, vc_ref)]
        for a, st_ref in enumerate(outs[7:]):
            for l, arrays in enumerate(layers):
                for s in range(tm // seq):
                    rows = arrays[a][s * seq:(s + 1) * seq, :]
                    if a < 3:
                        st_ref[s, l] = rows.T
                    else:
                        for h in range(H_C):
                            st_ref[s, l, pl.ds(h, seq, stride=H_C), :] = rows[:, h * DV_C:(h + 1) * DV_C]


def _front_part(x2d, mod4, ng3, w_front, gains3, rope, *, layer, mod_row0, tokens_per_mod, kv_dtype, tm=TM_FRONT,
                stack_seq=None, prev_caches=()):
    n = x2d.shape[0]
    use_rope = rope is not None
    row = lambda i: (i, 0)
    in_specs = [pl.BlockSpec((tm, D_MODEL), row),
                _mod_spec(layer, mod_row0, tokens_per_mod // tm),
                _layer_spec((1, D_MODEL), layer),
                _layer_spec((D_MODEL, N_NORM + N_PLAIN), layer),
                _layer_spec((1, N_NORM), layer)]
    args = [x2d, mod4, ng3, w_front, gains3]
    if use_rope:
        per_seq = rope[0].shape[0] // tm
        in_specs += [pl.BlockSpec((tm, LANES), lambda i: (i % per_seq, 0))] * 2
        args += list(rope)
    widths = (D_A, KV_A * HD, H_C * 2 * HD, H_C * 2 * HD, KV_A * HD, D_C, D_LRU)
    dtypes = (BF16, kv_dtype, BF16, kv_dtype, kv_dtype, kv_dtype, F32)
    out_shape = [jax.ShapeDtypeStruct((n, w), d) for w, d in zip(widths, dtypes)]
    out_specs = [pl.BlockSpec((tm, w), row) for w in widths]
    if stack_seq is not None:
        assert len(prev_caches) == DEPTH - 1 and tm % stack_seq == 0
        for cache in prev_caches:
            in_specs += [pl.BlockSpec((tm, arr.shape[1]), row) for arr in cache]
            args += list(cache)
        per_tile = tm // stack_seq
        cache_w = (KV_A * HD, KV_A * HD, H_C * 2 * HD, D_C)
        shapes = [(DEPTH, w, stack_seq) for w in cache_w[:3]] + [(DEPTH, stack_seq * H_C, DV_C)]
        out_shape += [jax.ShapeDtypeStruct((n // stack_seq,) + s, F32) for s in shapes]
        out_specs += [pl.BlockSpec((per_tile,) + s, lambda i: (i, 0, 0, 0)) for s in shapes]
    body = functools.partial(_front_body, use_rope=use_rope, n_prev=len(prev_caches), seq=stack_seq)
    return _Part("front_rope" if use_rope else "front", body, (n // tm,), in_specs, args, out_shape, out_specs)


def _attn_a_kernel(*refs, tq, nsub, seq_len, banded, has_ctx, layer):
    if has_ctx:
        sink_ref, q_ref, k_ref, v_ref, kc_ref, vc_ref, o_ref = refs
    else:
        sink_ref, q_ref, k_ref, v_ref, o_ref = refs
    i = pl.program_id(1)
    g = H_A // KV_A

    def spread(k, v):
        k = k.astype(F32)
        v = v.astype(F32)
        lo = lax.broadcasted_iota(jnp.int32, k.shape, 1) < HD
        k_sw = pltpu.roll(k, HD, 1)
        v_sw = pltpu.roll(v, HD, 1)
        kd = (jnp.where(lo, k, k_sw).astype(BF16), jnp.where(lo, k_sw, k).astype(BF16))
        vd = ((jnp.where(lo, v, 1.0).astype(BF16), jnp.where(lo, 1.0, v_sw).astype(BF16)),
              (jnp.where(lo, v_sw, 1.0).astype(BF16), jnp.where(lo, 1.0, v).astype(BF16)))
        return kd, vd

    if has_ctx:
        kd_c, vd_c = spread(kc_ref[...], vc_ref[...])
    nwin = tq + 2 * WINDOW if banded else tq
    lo_q = lax.broadcasted_iota(jnp.int32, (tq, LANES), 1) < HD
    zero = jnp.zeros((tq, LANES), BF16)
    for u in range(nsub):
        blk = i * nsub + u
        if banded:
            start = pl.multiple_of(jnp.clip(blk * tq - WINDOW, 0, seq_len - nwin), WINDOW)
            kd, vd = spread(k_ref[pl.ds(start, nwin), :], v_ref[pl.ds(start, nwin), :])
            kpos = start + lax.broadcasted_iota(jnp.int32, (tq, nwin), 1)
            qpos = blk * tq + lax.broadcasted_iota(jnp.int32, (tq, nwin), 0)
            bias = jnp.where(jnp.abs(kpos - qpos) <= WINDOW, 0.0, -jnp.inf)
        else:
            own = pl.multiple_of(blk * tq, tq)
            kd, vd = spread(k_ref[pl.ds(own, tq), :], v_ref[pl.ds(own, tq), :])
        if has_ctx:
            kd = tuple(jnp.concatenate([kd[kv], kd_c[kv]], axis=0) for kv in range(KV_A))
            vd = tuple(tuple(jnp.concatenate([vd[kv][half], vd_c[kv][half]], axis=0) for half in range(2))
                       for kv in range(KV_A))
        q = q_ref[u * tq:(u + 1) * tq, :]
        outs = []
        for h in range(H_A):
            kv, half = h // g, h % 2
            qz = jnp.where(lo_q if half == 0 else jnp.logical_not(lo_q),
                           q[:, (h // 2) * LANES:(h // 2 + 1) * LANES], zero)
            s = _dot_nt(qz, kd[kv])
            if banded:
                s = jnp.concatenate([s[:, :nwin] + bias, s[:, nwin:]], axis=1)
            snk = sink_ref[layer, h] * LOG2E
            m = jnp.maximum(jnp.max(s, axis=-1, keepdims=True), snk)
            e = jnp.exp2(s - m)
            pv = _dot(e.astype(BF16), vd[kv][half])
            outs.append(pv / (pltpu.roll(pv, HD, 1) + jnp.exp2(snk - m)))
        for c in range(H_A // 2):
            pair = jnp.where(lo_q, outs[2 * c], outs[2 * c + 1])
            o_ref[u * tq:(u + 1) * tq, c * LANES:(c + 1) * LANES] = pair.astype(o_ref.dtype)


def _attn_a(sink, q, k, v, k_ctx, v_ctx, *, layer, tq, nsub, banded):
    b, t, _ = q.shape
    has_ctx = k_ctx is not None
    rows = tq * nsub
    kv_spec = pl.BlockSpec((None, t, KV_A * HD), lambda bi, i: (bi, 0, 0))
    in_specs = [pl.BlockSpec(memory_space=pltpu.SMEM),
                pl.BlockSpec((None, rows, D_A), lambda bi, i: (bi, i, 0)), kv_spec, kv_spec]
    args = [sink, q, k, v]
    if has_ctx:
        s = k_ctx.shape[2]
        in_specs += [pl.BlockSpec((None, None, s, KV_A * HD), lambda bi, i: (bi, layer, 0, 0))] * 2
        args += [k_ctx, v_ctx]
    return pl.pallas_call(
        functools.partial(_attn_a_kernel, tq=tq, nsub=nsub, seq_len=t, banded=banded, has_ctx=has_ctx, layer=layer),
        out_shape=jax.ShapeDtypeStruct((b, t, D_A), BF16),
        grid=(b, t // rows),
        in_specs=in_specs,
        out_specs=pl.BlockSpec((None, rows, D_A), lambda bi, i: (bi, i, 0)),
        compiler_params=_params(("arbitrary", "arbitrary")),
        name="attn_a_latent" if has_ctx else "attn_a_context",
    )(*args)


def _attn_c_kernel(*refs, tq, ts, n_loc, has_ctx, lam_init):
    if has_ctx:
        lq1, lk1, lq2, lk2, sub_ref, q_ref, k_ref, v_ref, kc_ref, vc_ref, o_ref = refs
    else:
        lq1, lk1, lq2, lk2, sub_ref, q_ref, k_ref, v_ref, o_ref = refs
    lam = (jnp.exp(jnp.sum(lq1[...] * lk1[...], axis=-1, keepdims=True))
           - jnp.exp(jnp.sum(lq2[...] * lk2[...], axis=-1, keepdims=True)) + lam_init)
    lo = lax.broadcasted_iota(jnp.int32, (tq, LANES), 1) < HD
    zero = jnp.zeros((tq, LANES), BF16)
    heads = [slice(h * LANES, (h + 1) * LANES) for h in range(H_C)]
    qz = []
    for cols in heads:
        q12 = q_ref[:, cols]
        qz.append(jnp.concatenate([jnp.where(lo, q12, zero), jnp.where(lo, zero, q12)], axis=0))

    def update(carry, kt, vt):
        new = []
        ones = jnp.ones((kt.shape[0], LANES), BF16)
        for h, cols in enumerate(heads):
            m, acc = carry[h]
            s = _dot_nt(qz[h], kt[:, cols].astype(BF16))
            m_new = jnp.maximum(m, jnp.max(s, axis=-1, keepdims=True))
            alpha = jnp.exp2(m - m_new)
            p = jnp.exp2(s - m_new)
            v_ones = jnp.concatenate([vt[:, cols].astype(BF16), ones], axis=1)
            acc = alpha * acc + _dot(p.astype(BF16), v_ones)
            new.append((m_new, acc))
        return tuple(new)

    carry = tuple((jnp.full((2 * tq, 1), -jnp.inf, F32), jnp.zeros((2 * tq, 2 * LANES), F32)) for _ in heads)
    for j in range(n_loc):
        carry = update(carry, k_ref[j * ts:(j + 1) * ts, :], v_ref[j * ts:(j + 1) * ts, :])
    if has_ctx:
        carry = update(carry, kc_ref[...], vc_ref[...])
    for h, cols in enumerate(heads):
        _, acc = carry[h]
        num, den = acc[:, :DV_C], acc[:, DV_C:]
        o = num[:tq] / den[:tq] - lam * (num[tq:] / den[tq:])
        o = o * lax.rsqrt(jnp.mean(o * o, axis=-1, keepdims=True) + EPS) * sub_ref[...] * (1.0 - lam_init)
        o_ref[:, cols] = o.astype(o_ref.dtype)


def _attn_c(lam_vecs, subln3, q, k, v, k_ctx, v_ctx, *, layer, tq, ts, lam_init):
    b, t, _ = q.shape
    s_loc = k.shape[1]
    has_ctx = k_ctx is not None
    kv_spec = pl.BlockSpec((None, s_loc, D_C), lambda bi, i: (bi, 0, 0))
    in_specs = ([_layer_spec((1, HD), layer)] * 4 + [_layer_spec((1, DV_C), layer)]
                + [pl.BlockSpec((None, tq, D_C), lambda bi, i: (bi, i, 0)), kv_spec, kv_spec])
    args = list(lam_vecs) + [subln3, q, k, v]
    if has_ctx:
        sc = k_ctx.shape[2]
        in_specs += [pl.BlockSpec((None, None, sc, D_C), lambda bi, i: (bi, layer, 0, 0))] * 2
        args += [k_ctx, v_ctx]
    return pl.pallas_call(
        functools.partial(_attn_c_kernel, tq=tq, ts=ts, n_loc=s_loc // ts, has_ctx=has_ctx, lam_init=lam_init),
        out_shape=jax.ShapeDtypeStruct((b, t, D_C), BF16),
        grid=(b, t // tq),
        in_specs=in_specs,
        out_specs=pl.BlockSpec((None, tq, D_C), lambda bi, i: (bi, i, 0)),
        compiler_params=_params(("arbitrary", "arbitrary")),
        name="attn_c_latent" if has_ctx else "attn_c_context",
    )(*args)


def _lru_body(ins, outs, scratch, ids, *, tc, nc):
    xc_ref, xp_ref, xn_ref, cw_ref, cb_ref, wg_ref, bg_ref, lam_ref, h0_ref, perm_ref, inv_ref = ins
    ob_ref, st_ref = outs
    a_s, b_s, hs, ps, hf_s, xb_s, hcar, last_h, last_p, cin = scratch
    _, p, c = ids
    cidx = c + p * (nc - 1 - 2 * c)
    r0 = pl.multiple_of(cidx * tc, tc)
    ng = tc // SUBLANES

    @pl.when(p == 0)
    def _():
        cur = xc_ref[...]
        prev = jnp.where(cidx > 0, xp_ref[...], 0.0)
        nxt = jnp.where(cidx < nc - 1, xn_ref[...], 0.0)
        hi = cur.astype(BF16)
        rest = cur - hi.astype(F32)
        mid = rest.astype(BF16)
        lo = (rest - mid.astype(F32)).astype(BF16)
        x3 = (_dot(perm_ref[...], hi) + _dot(perm_ref[...], mid) + _dot(perm_ref[...], lo)).reshape(ng, SUBLANES, D_LRU)
        rows = lax.broadcasted_iota(jnp.int32, (SUBLANES, D_LRU), 0)
        before1 = jnp.where(rows == 0, prev[SUBLANES - 1:SUBLANES, :], pltpu.roll(x3[ng - 1], 1, 0))
        before2 = jnp.where(rows == 0, prev[SUBLANES - 2:SUBLANES - 1, :], pltpu.roll(x3[ng - 2], 1, 0))
        after1 = jnp.where(rows == SUBLANES - 1, nxt[0:1, :], pltpu.roll(x3[0], SUBLANES - 1, 0))
        xpad = jnp.concatenate([before2[None], before1[None], x3, after1[None]], axis=0)
        acc = jnp.broadcast_to(cb_ref[...], (ng, SUBLANES, D_LRU))
        for j in range(CONV_W):
            acc = acc + xpad[j:j + ng] * cw_ref[j:j + 1, :]
        xb_s[pl.ds(r0, tc), :] = acc.reshape(tc, D_LRU)

    xb = xb_s[pl.ds(r0, tc), :]
    gm = _dot(xb.astype(BF16), wg_ref[...]) + bg_ref[...]
    r = _sigmoid(gm[:, :D_LRU])
    ig = _sigmoid(gm[:, D_LRU:])
    nl = -lam_ref[...]
    softplus = jnp.maximum(nl, 0.0) + jnp.log1p(jnp.exp(-jnp.abs(nl)))
    decay = LRU_C * softplus
    a = jnp.exp2(r * (decay * (-LOG2E)))
    one_m_a2 = jnp.tanh(r * decay) * (a * a + 1.0)
    bb = jnp.where(one_m_a2 == 0.0, 0.0, one_m_a2 * lax.rsqrt(one_m_a2)) * (ig * xb)

    a_s[...] = a
    b_s[...] = bb

    def step(gi, carry):
        h, pc = carry
        g = gi + p * (ng - 1 - 2 * gi)
        base = pl.multiple_of(g * SUBLANES, SUBLANES)
        a_g = a_s[pl.ds(base, SUBLANES), :]
        h = a_g * h + b_s[pl.ds(base, SUBLANES), :]
        pc = a_g * pc
        hs[pl.ds(base, SUBLANES), :] = h
        ps[pl.ds(base, SUBLANES), :] = pc
        return h, pc

    init = (jnp.zeros((SUBLANES, D_LRU), F32), jnp.ones((SUBLANES, D_LRU), F32))
    last_h[...], last_p[...] = lax.fori_loop(0, ng, step, init, unroll=8)

    @pl.when(c == 0)
    def _():
        hcar[0:1, :] = h0_ref[pl.ds(p, 1), :]

    state = hcar[0:1, :]
    for rr in range(SUBLANES):
        r_in = rr + p * (SUBLANES - 1 - 2 * rr)
        cin[pl.ds(r_in, 1), :] = state
        state = last_p[pl.ds(r_in, 1), :] * state + last_h[pl.ds(r_in, 1), :]
    hcar[0:1, :] = state
    hfull = (hs[...].reshape(ng, SUBLANES, D_LRU)
             + ps[...].reshape(ng, SUBLANES, D_LRU) * cin[...]).reshape(tc, D_LRU)

    @pl.when(p == 0)
    def _():
        hf_s[pl.ds(r0, tc), :] = hfull

    @pl.when(p == 1)
    def _():
        both = (hf_s[pl.ds(r0, tc), :] + hfull).astype(BF16)
        ob_ref[...] = _dot(inv_ref[...], both).astype(ob_ref.dtype)

    @pl.when(c == nc - 1)
    def _():
        st_ref[pl.ds(p, 1), :] = state


def _lru_part(xb, conv_w, conv_b3, wg, bg, lam4, h0, *, layer, h0_layer, tc):
    b, t, _ = xb.shape
    nc = t // tc
    nb8 = t // SUBLANES
    per = tc // SUBLANES
    cidx = lambda p, c: c + p * (nc - 1 - 2 * c)
    in_specs = [
        pl.BlockSpec((None, tc, D_LRU), lambda bi, p, c: (bi, cidx(p, c), 0)),
        pl.BlockSpec((None, SUBLANES, D_LRU), lambda bi, p, c: (bi, jnp.maximum(cidx(p, c) * per - 1, 0), 0)),
        pl.BlockSpec((None, SUBLANES, D_LRU), lambda bi, p, c: (bi, jnp.minimum((cidx(p, c) + 1) * per, nb8 - 1), 0)),
        _layer_spec((CONV_W, D_LRU), layer),
        _layer_spec((1, D_LRU), layer),
        pl.BlockSpec((None, None, D_LRU, 2 * D_LRU), lambda bi, p, c: (layer, p, 0, 0)),
        pl.BlockSpec((None, None, 1, 2 * D_LRU), lambda bi, p, c: (layer, p, 0, 0)),
        pl.BlockSpec((None, None, 1, D_LRU), lambda bi, p, c: (layer, p, 0, 0)),
        pl.BlockSpec((None, None, 2, D_LRU), lambda bi, p, c: (bi, h0_layer, 0, 0)),
        pl.BlockSpec((tc, tc), lambda bi, p, c: (0, 0)),
        pl.BlockSpec((tc, tc), lambda bi, p, c: (0, 0)),
    ]
    j = jnp.arange(tc)
    src = (j % SUBLANES) * (tc // SUBLANES) + j // SUBLANES
    perm = (src[:, None] == j[None, :]).astype(BF16)
    chunk = (tc, D_LRU)
    row8 = (SUBLANES, D_LRU)
    return _Part("lru", functools.partial(_lru_body, tc=tc, nc=nc), (b, 2, nc), in_specs,
                 [xb, xb, xb, conv_w, conv_b3, wg, bg, lam4, h0, perm, perm.T],
                 [jax.ShapeDtypeStruct((b, t, D_LRU), BF16), jax.ShapeDtypeStruct((b, 2, D_LRU), F32)],
                 [pl.BlockSpec((None, tc, D_LRU), lambda bi, p, c: (bi, nc - 1 - p * c, 0)),
                  pl.BlockSpec((None, 2, D_LRU), lambda bi, p, c: (bi, 0, 0))],
                 [pltpu.VMEM(chunk, F32), pltpu.VMEM(chunk, F32), pltpu.VMEM(chunk, F32), pltpu.VMEM(chunk, F32),
                  pltpu.VMEM((t, D_LRU), F32), pltpu.VMEM((t, D_LRU), F32),
                  pltpu.VMEM(row8, F32), pltpu.VMEM(row8, F32), pltpu.VMEM(row8, F32), pltpu.VMEM(row8, F32)])


def _tail_kernel(x_ref, mod_ref, ng_ref, oa_ref, ob_ref, oc_ref, wgm_ref, wbr_ref, wout_ref, o_ref):
    x = x_ref[...]
    mod = mod_ref[...]
    hb = _normed_input(x, mod, ng_ref[...]).astype(BF16)
    y = None
    for br, br_ref in enumerate((oa_ref, ob_ref, oc_ref)):
        gm = _dot(hb, wgm_ref[:, br * N_BRANCH_COLS:(br + 1) * N_BRANCH_COLS])
        gt = gm[:, :D_A]
        u = (br_ref[...].astype(F32) * (gt * _sigmoid(gt))).astype(BF16)
        proj = _dot(u, wbr_ref[br])
        mg = _sigmoid(gm[:, D_A:])
        y = mg * proj if y is None else y + mg * proj
    gate = mod[:, 2 * D_MODEL:]
    o_ref[...] = x + gate * _dot(y.astype(BF16), wout_ref[...])


def _tail_part(x2d, mod4, ng3, oa, ob, oc, wgm, wbr, wout, *, layer, mod_row0, tokens_per_mod, tm=TM_TAIL):
    n = x2d.shape[0]
    row = lambda i: (i, 0)
    once = dict(pipeline_mode=pl.Buffered(1))

    def weight(shape):
        zeros = (0,) * len(shape)
        return pl.BlockSpec((None,) + shape, lambda i: (layer,) + zeros, **once)

    in_specs = [pl.BlockSpec((tm, D_MODEL), row),
                _mod_spec(layer, mod_row0, tokens_per_mod // tm),
                _layer_spec((1, D_MODEL), layer),
                pl.BlockSpec((tm, D_A), row),
                pl.BlockSpec((tm, D_LRU), row),
                pl.BlockSpec((tm, D_C), row),
                weight((D_MODEL, 3 * N_BRANCH_COLS)),
                weight((3, D_A, D_MODEL)),
                weight((D_MODEL, D_MODEL))]
    return _Part("tail", lambda ins, outs, scratch, ids: _tail_kernel(*ins, *outs), (n // tm,), in_specs,
                 [x2d, mod4, ng3, oa, ob, oc, wgm, wbr, wout],
                 [jax.ShapeDtypeStruct((n, D_MODEL), F32)], [pl.BlockSpec((tm, D_MODEL), row)])


def _rope_tables(seq_len):
    pos = jnp.arange(seq_len)
    row = (pos // GRID_W).astype(F32)
    col = (pos % GRID_W).astype(F32)
    inv = jnp.power(ROPE_BASE, -jnp.arange(ROPE_FREQS, dtype=F32) / ROPE_FREQS)
    ang_r = row[:, None] * inv
    ang_c = col[:, None] * inv
    cos = jnp.concatenate([jnp.cos(ang_r)] * 2 + [jnp.cos(ang_c)] * 2, axis=-1)
    sin = jnp.concatenate([-jnp.sin(ang_r), jnp.sin(ang_r), -jnp.sin(ang_c), jnp.sin(ang_c)], axis=-1)
    return jnp.tile(cos, (1, LANES // HD)), jnp.tile(sin, (1, LANES // HD))


def _block_diag(w):
    rows = w.reshape(w.shape[:-3] + (D_LRU, LRU_BW))
    tiled = jnp.tile(rows, (1,) * (rows.ndim - 1) + (LRU_BLOCKS,))
    blk = jnp.arange(D_LRU) // LRU_BW
    return jnp.where(blk[:, None] == blk[None, :], tiled, 0.0)


def kernel(x_prompt, x_sample, cache_a_k, cache_a_v, cache_c_k, cache_c_v, state_lru, c, c_ctx, norm_g, mod_w, mod_b, w_in, qn_a, kn_a, sink_a, conv_w, conv_b, lru_wa, lru_ba, lru_wx, lru_bx, lru_lam, qn_c, kn_c, lam_q1, lam_k1, lam_q2, lam_k2, subln_c, w_br_a, w_br_b, w_br_c, w_out):
    bp, sp, _ = x_prompt.shape
    bs, ss, _ = x_sample.shape
    past = cache_a_k.shape[2]

    cvecs = jnp.concatenate([c, c_ctx[None, :], jnp.zeros((SUBLANES - bs - 1, D_MODEL), F32)], axis=0)
    mod = _modulation(cvecs, mod_w, mod_b)
    mod4 = mod.reshape(DEPTH, SUBLANES, 1, 3 * D_MODEL)
    rope = _rope_tables(ss)

    w_front = _permute_cast(w_in, FRONT_BLOCKS)
    wgm = _permute_cast(w_in, TAIL_BLOCKS)
    wbr = jnp.stack([w_br_a, w_br_b, w_br_c], axis=1).astype(BF16)
    wout = w_out.astype(BF16)
    gains3 = jnp.concatenate([jnp.tile(qn_a * Q_SCALE, (1, H_A)), jnp.tile(qn_c * Q_SCALE, (1, 2 * H_C)),
                              jnp.tile(kn_c, (1, 2 * H_C)), jnp.tile(kn_a, (1, KV_A))],
                             axis=-1)[:, None, :]
    ng3 = norm_g[:, None, :]
    lru_wg = jnp.concatenate([_block_diag(lru_wa), _block_diag(lru_wx)], axis=-1).astype(BF16)
    lru_bg = jnp.concatenate([lru_ba, lru_bx], axis=-1)[:, :, None, :]
    lru_lam4 = lru_lam[:, :, None, :]
    lru_p = (conv_w, conv_b[:, None, :], lru_wg, lru_bg, lru_lam4)
    lam_vecs = [v[:, None, :] for v in (lam_q1, lam_k1, lam_q2, lam_k2)]
    subln3 = subln_c[:, None, :]
    cka = cache_a_k.reshape(bs, DEPTH, past, KV_A * HD)
    cva = cache_a_v.reshape(bs, DEPTH, past, KV_A * HD)
    ckc = cache_c_k.reshape(bs, DEPTH, past, D_C)
    cvc = cache_c_v.reshape(bs, DEPTH, past, D_C)
    zeros_h0 = jnp.zeros((bp, 1, 2, D_LRU), F32)

    xp = x_prompt.reshape(bp * sp, D_MODEL)
    xs = x_sample.reshape(bs * ss, D_MODEL)
    new_caches, new_states = [], []
    for l in range(DEPTH):
        lam_init = 0.8 - 0.6 * math.exp(-0.3 * l)
        ctx_mod = dict(layer=l, mod_row0=bs, tokens_per_mod=bp * sp)
        lat_mod = dict(layer=l, mod_row0=0, tokens_per_mod=ss)

        flat = lambda a: a.reshape(-1, a.shape[-1])
        r3 = lambda a: a.reshape(bp, sp, a.shape[-1])
        stacking = dict(stack_seq=sp, prev_caches=new_caches) if l == DEPTH - 1 else {}
        qa, ka, qc, kc, va, vc, xb, *stacked = _run(
            _front_part(xp, mod4, ng3, w_front, gains3, None, kv_dtype=F32, **ctx_mod, **stacking))
        new_caches.append((ka, va, kc, vc))
        oa = _attn_a(sink_a, r3(qa), r3(ka), r3(va), None, None, layer=l, tq=sp, nsub=1, banded=False)
        ob, st = _run(_lru_part(r3(xb), *lru_p, zeros_h0, layer=l, h0_layer=0, tc=sp))
        new_states.append(st)
        oc = _attn_c(lam_vecs, subln3, r3(qc), r3(kc), r3(vc), None, None, layer=l, tq=sp, ts=sp, lam_init=lam_init)
        (xp,) = _run(_tail_part(xp, mod4, ng3, flat(oa), flat(ob), flat(oc), wgm, wbr, wout, **ctx_mod))

        r3 = lambda a: a.reshape(bs, ss, a.shape[-1])
        qa, ka, qc, kc, va, vc, xb = _run(_front_part(xs, mod4, ng3, w_front, gains3, rope, kv_dtype=BF16, **lat_mod))
        oa = _attn_a(sink_a, r3(qa), r3(ka), r3(va), cka, cva, layer=l, tq=TQ_A, nsub=NSUB_A, banded=True)
        ob, _ = _run(_lru_part(r3(xb), *lru_p, state_lru, layer=l, h0_layer=l, tc=TC_LRU))
        oc = _attn_c(lam_vecs, subln3, r3(qc), r3(kc), r3(vc), ckc, cvc, layer=l, tq=TQ_C, ts=TS_C, lam_init=lam_init)
        (xs,) = _run(_tail_part(xs, mod4, ng3, flat(oa), flat(ob), flat(oc), wgm, wbr, wout, **lat_mod))

    ka_t, va_t, kc_t, vc = stacked
    return (xp.reshape(bp, sp, D_MODEL), xs.reshape(bs, ss, D_MODEL),
            ka_t.reshape(bp, DEPTH, KV_A, HD, sp).transpose(0, 1, 4, 2, 3),
            va_t.reshape(bp, DEPTH, KV_A, HD, sp).transpose(0, 1, 4, 2, 3),
            kc_t.reshape(bp, DEPTH, H_C, 2, HD, sp).transpose(0, 1, 5, 2, 3, 4),
            vc.reshape(bp, DEPTH, sp, H_C, DV_C),
            jnp.stack(new_states, axis=1))
```

```python
import functools
import math

import jax
import jax.numpy as jnp
from jax import lax
from jax.experimental import pallas as pl
from jax.experimental.pallas import tpu as pltpu

F32 = jnp.float32
BF16 = jnp.bfloat16

D_MODEL = 1024
DEPTH = 2
GRID_W = 64
HD = 64
SCALE = 1.0 / math.sqrt(HD)
LOG2E = math.log2(math.e)
Q_SCALE = SCALE * LOG2E
H_A = 8
KV_A = 2
WINDOW = 128
D_A = H_A * HD
D_LRU = 512
LRU_BLOCKS = 8
LRU_BW = D_LRU // LRU_BLOCKS
CONV_W = 4
LRU_C = 8.0
H_C = 4
DV_C = 2 * HD
D_C = H_C * DV_C
ROPE_BASE = 10000.0
ROPE_FREQS = HD // 4
EPS = 1e-6
LANES = 128
SUBLANES = 8
VMEM_LIMIT = 56 * 1024 * 1024

_SECTIONS = (("qa", D_A), ("kava", 2 * KV_A * HD), ("ga", D_A), ("xb", D_LRU), ("gb", D_LRU),
             ("qc", H_C * 2 * HD), ("kc", H_C * 2 * HD), ("vc", D_C), ("gc", D_C),
             ("mg_a", D_MODEL), ("mg_b", D_MODEL), ("mg_c", D_MODEL))
W_BLOCK = 256


def _col_blocks(*names):
    start, spans = 0, {}
    for name, width in _SECTIONS:
        spans[name] = (start, start + width)
        start += width
    blocks = []
    for name in names:
        a, b = spans[name]
        assert a % W_BLOCK == 0 and b % W_BLOCK == 0
        blocks += range(a // W_BLOCK, b // W_BLOCK)
    return tuple(blocks)


FRONT_BLOCKS = _col_blocks("qa", "qc", "kc", "kava", "vc", "xb")
TAIL_BLOCKS = _col_blocks("ga", "mg_a", "gb", "mg_b", "gc", "mg_c")
N_BRANCH_COLS = D_A + D_MODEL
N_NORM = D_A + KV_A * HD + 2 * H_C * 2 * HD
N_PLAIN = KV_A * HD + D_C + D_LRU

TM_FRONT = 512
TM_TAIL = 512
TQ_A = 2 * WINDOW
NSUB_A = 4
TQ_C = 512
TS_C = 512
TC_LRU = 512


def _params(sem, vmem=VMEM_LIMIT):
    return pltpu.CompilerParams(dimension_semantics=sem, vmem_limit_bytes=vmem)


def _dot(a, b):
    return jnp.dot(a, b, preferred_element_type=F32)


def _dot_nt(a, b):
    return lax.dot_general(a, b, (((1,), (1,)), ((), ())), preferred_element_type=F32)


def _sigmoid(x):
    return 0.5 * jnp.tanh(0.5 * x) + 0.5


def _permute_cast_kernel(perm_ref, w_ref, o_ref):
    o_ref[...] = w_ref[...].astype(o_ref.dtype)


def _permute_cast(w, blocks):
    nl, k, _ = w.shape
    grid_spec = pltpu.PrefetchScalarGridSpec(
        num_scalar_prefetch=1, grid=(len(blocks),),
        in_specs=[pl.BlockSpec((nl, k, W_BLOCK), lambda j, perm: (0, 0, perm[j]))],
        out_specs=pl.BlockSpec((nl, k, W_BLOCK), lambda j, perm: (0, 0, j)))
    return pl.pallas_call(
        _permute_cast_kernel, grid_spec=grid_spec,
        out_shape=jax.ShapeDtypeStruct((nl, k, len(blocks) * W_BLOCK), BF16),
        compiler_params=_params(("arbitrary",)), name="permute_cast",
    )(jnp.asarray(blocks, jnp.int32), w)


def _mod_kernel(c_ref, w_ref, b_ref, o_ref):
    c = c_ref[...]
    a = c * _sigmoid(c)
    w = w_ref[...]
    a_hi = a.astype(BF16)
    a_lo = (a - a_hi.astype(F32)).astype(BF16)
    w_hi = w.astype(BF16)
    w_lo = (w - w_hi.astype(F32)).astype(BF16)
    o_ref[...] = _dot(a_hi, w_hi) + _dot(a_hi, w_lo) + _dot(a_lo, w_hi) + b_ref[...]


def _modulation(cvecs, mod_w, mod_b):
    tn = 768
    return pl.pallas_call(
        _mod_kernel,
        out_shape=jax.ShapeDtypeStruct((DEPTH, SUBLANES, 3 * D_MODEL), F32),
        grid=(DEPTH, 3 * D_MODEL // tn),
        in_specs=[pl.BlockSpec((SUBLANES, D_MODEL), lambda l, j: (0, 0)),
                  pl.BlockSpec((None, D_MODEL, tn), lambda l, j: (l, 0, j)),
                  pl.BlockSpec((None, 1, tn), lambda l, j: (l, 0, j))],
        out_specs=pl.BlockSpec((None, SUBLANES, tn), lambda l, j: (l, 0, j)),
        compiler_params=_params(("arbitrary", "arbitrary")),
        name="modulation",
    )(cvecs, mod_w, mod_b.reshape(DEPTH, 1, 3 * D_MODEL))


def _normed_input(x, mod, ng):
    ms = jnp.mean(x * x, axis=-1, keepdims=True)
    shift = mod[:, 0:D_MODEL]
    scale = mod[:, D_MODEL:2 * D_MODEL]
    return (x * lax.rsqrt(ms + EPS) * ng) * (1.0 + scale) + shift


def _layer_spec(shape, layer):
    zeros = (0,) * len(shape)
    return pl.BlockSpec((None,) + tuple(shape), lambda *_: (layer,) + zeros)


def _mod_spec(layer, row0, tiles_per_row):
    return pl.BlockSpec((None, None, 1, 3 * D_MODEL), lambda i: (layer, row0 + i // tiles_per_row, 0, 0))


class _Part:
    def __init__(self, name, body, grid, in_specs, args, out_shape, out_specs, scratch=()):
        self.name, self.body, self.grid = name, body, tuple(grid)
        self.in_specs, self.args = list(in_specs), list(args)
        self.out_shape, self.out_specs, self.scratch = list(out_shape), list(out_specs), list(scratch)


def _run(part):
    ni, no = len(part.args), len(part.out_shape)

    def kern(*refs):
        ids = tuple(pl.program_id(k) for k in range(len(part.grid)))
        part.body(refs[:ni], refs[ni:ni + no], refs[ni + no:], ids)

    return pl.pallas_call(
        kern, out_shape=part.out_shape, grid=part.grid, in_specs=part.in_specs, out_specs=part.out_specs,
        scratch_shapes=part.scratch, compiler_params=_params(("arbitrary",) * len(part.grid)), name=part.name)(*part.args)


def _front_body(ins, outs, scratch, ids, *, use_rope, n_prev=0, seq=None):
    x_ref, mod_ref, ng_ref, w_ref, gain_ref = ins[:5]
    qa_ref, ka_ref, qc_ref, kc_ref, va_ref, vc_ref, xb_ref = outs[:7]
    x = x_ref[...]
    tm = x.shape[0]
    hb = _normed_input(x, mod_ref[...], ng_ref[...]).astype(BF16)
    p1 = _dot(hb, w_ref[:, :N_NORM])
    lane = lax.broadcasted_iota(jnp.int32, (tm, LANES), 1)
    first_half = (lane & ROPE_FREQS) == 0
    wide = 2 * LANES
    same_head = (lax.broadcasted_iota(jnp.int32, (wide, wide), 0) // HD
                 == lax.broadcasted_iota(jnp.int32, (wide, wide), 1) // HD)
    ones_bd = jnp.where(same_head, 1.0, 0.0).astype(BF16)
    n_query = D_A + H_C * 2 * HD
    sums = []
    for c0 in range(0, N_NORM, wide):
        w = min(wide, N_NORM - c0)
        sq = p1[:, c0:c0 + w] * p1[:, c0:c0 + w]
        sq_hi = sq.astype(BF16)
        total = _dot(sq_hi, ones_bd[:w, :w])
        if c0 + w > n_query:
            total = total + _dot((sq - sq_hi.astype(F32)).astype(BF16), ones_bd[:w, :w])
        sums.append(total)
    dests = ([(qa_ref, c) for c in range(4)] + [(qc_ref, c) for c in range(4)]
             + [(kc_ref, c) for c in range(4)] + [(ka_ref, 0)])
    for c, (o_ref, oc) in enumerate(dests):
        pc = p1[:, c * LANES:(c + 1) * LANES]
        msq = sums[c // 2][:, (c % 2) * LANES:(c % 2 + 1) * LANES] * (1.0 / HD)
        y = pc * lax.rsqrt(msq + EPS) * gain_ref[:, c * LANES:(c + 1) * LANES]
        if use_rope:
            cos_ref, sin_ref = ins[5:7]
            partner = jnp.where(first_half, pltpu.roll(y, LANES - ROPE_FREQS, 1),
                                pltpu.roll(y, ROPE_FREQS, 1))
            y = y * cos_ref[...] + partner * sin_ref[...]
        o_ref[:, oc * LANES:(oc + 1) * LANES] = y.astype(o_ref.dtype)
    p2 = _dot(hb, w_ref[:, N_NORM:])
    va_ref[...] = p2[:, 0:KV_A * HD].astype(va_ref.dtype)
    vc_ref[...] = p2[:, KV_A * HD:KV_A * HD + D_C].astype(vc_ref.dtype)
    xb_ref[...] = p2[:, KV_A * HD + D_C:]
    if seq is not None:
        prev = ins[len(ins) - 4 * n_prev:]
        layers = [prev[4 * l:4 * l + 4] for l in range(n_prev)] + [(ka_ref, va_ref, kc_ref, vc_ref)]
        for a, st_ref in enumerate(outs[7:]):
            for l, arrays in enumerate(layers):
                for s in range(tm // seq):
                    rows = arrays[a][s * seq:(s + 1) * seq, :]
                    if a < 3:
                        st_ref[s, l] = rows.T
                    else:
                        for h in range(H_C):
                            st_ref[s, l, pl.ds(h, seq, stride=H_C), :] = rows[:, h * DV_C:(h + 1) * DV_C]


def _front_part(x2d, mod4, ng3, w_front, gains3, rope, *, layer, mod_row0, tokens_per_mod, kv_dtype, tm=TM_FRONT,
                stack_seq=None, prev_caches=()):
    n = x2d.shape[0]
    use_rope = rope is not None
    row = lambda i: (i, 0)
    in_specs = [pl.BlockSpec((tm, D_MODEL), row),
                _mod_spec(layer, mod_row0, tokens_per_mod // tm),
                _layer_spec((1, D_MODEL), layer),
                _layer_spec((D_MODEL, N_NORM + N_PLAIN), layer),
                _layer_spec((1, N_NORM), layer)]
    args = [x2d, mod4, ng3, w_front, gains3]
    if use_rope:
        per_seq = rope[0].shape[0] // tm
        in_specs += [pl.BlockSpec((tm, LANES), lambda i: (i % per_seq, 0))] * 2
        args += list(rope)
    widths = (D_A, KV_A * HD, H_C * 2 * HD, H_C * 2 * HD, KV_A * HD, D_C, D_LRU)
    dtypes = (BF16, kv_dtype, BF16, kv_dtype, kv_dtype, kv_dtype, F32)
    out_shape = [jax.ShapeDtypeStruct((n, w), d) for w, d in zip(widths, dtypes)]
    out_specs = [pl.BlockSpec((tm, w), row) for w in widths]
    if stack_seq is not None:
        assert len(prev_caches) == DEPTH - 1 and tm % stack_seq == 0
        for cache in prev_caches:
            in_specs += [pl.BlockSpec((tm, arr.shape[1]), row) for arr in cache]
            args += list(cache)
        per_tile = tm // stack_seq
        cache_w = (KV_A * HD, KV_A * HD, H_C * 2 * HD, D_C)
        shapes = [(DEPTH, w, stack_seq) for w in cache_w[:3]] + [(DEPTH, stack_seq * H_C, DV_C)]
        out_shape += [jax.ShapeDtypeStruct((n // stack_seq,) + s, F32) for s in shapes]
        out_specs += [pl.BlockSpec((per_tile,) + s, lambda i: (i, 0, 0, 0)) for s in shapes]
    body = functools.partial(_front_body, use_rope=use_rope, n_prev=len(prev_caches), seq=stack_seq)
    return _Part("front_rope" if use_rope else "front", body, (n // tm,), in_specs, args, out_shape, out_specs)


def _attn_a_kernel(*refs, tq, nsub, seq_len, banded, has_ctx, layer):
    if has_ctx:
        sink_ref, q_ref, k_ref, v_ref, kc_ref, vc_ref, o_ref = refs
    else:
        sink_ref, q_ref, k_ref, v_ref, o_ref = refs
    i = pl.program_id(1)
    g = H_A // KV_A

    def spread(k, v):
        k = k.astype(F32)
        v = v.astype(F32)
        lo = lax.broadcasted_iota(jnp.int32, k.shape, 1) < HD
        k_sw = pltpu.roll(k, HD, 1)
        v_sw = pltpu.roll(v, HD, 1)
        kd = (jnp.where(lo, k, k_sw).astype(BF16), jnp.where(lo, k_sw, k).astype(BF16))
        vd = ((jnp.where(lo, v, 1.0).astype(BF16), jnp.where(lo, 1.0, v_sw).astype(BF16)),
              (jnp.where(lo, v_sw, 1.0).astype(BF16), jnp.where(lo, 1.0, v).astype(BF16)))
        return kd, vd

    if has_ctx:
        kd_c, vd_c = spread(kc_ref[...], vc_ref[...])
    nwin = tq + 2 * WINDOW if banded else tq
    lo_q = lax.broadcasted_iota(jnp.int32, (tq, LANES), 1) < HD
    zero = jnp.zeros((tq, LANES), BF16)
    for u in range(nsub):
        blk = i * nsub + u
        if banded:
            start = pl.multiple_of(jnp.clip(blk * tq - WINDOW, 0, seq_len - nwin), WINDOW)
            kd, vd = spread(k_ref[pl.ds(start, nwin), :], v_ref[pl.ds(start, nwin), :])
            kpos = start + lax.broadcasted_iota(jnp.int32, (tq, nwin), 1)
            qpos = blk * tq + lax.broadcasted_iota(jnp.int32, (tq, nwin), 0)
            bias = jnp.where(jnp.abs(kpos - qpos) <= WINDOW, 0.0, -jnp.inf)
        else:
            own = pl.multiple_of(blk * tq, tq)
            kd, vd = spread(k_ref[pl.ds(own, tq), :], v_ref[pl.ds(own, tq), :])
        if has_ctx:
            kd = tuple(jnp.concatenate([kd[kv], kd_c[kv]], axis=0) for kv in range(KV_A))
            vd = tuple(tuple(jnp.concatenate([vd[kv][half], vd_c[kv][half]], axis=0) for half in range(2))
                       for kv in range(KV_A))
        q = q_ref[u * tq:(u + 1) * tq, :]
        outs = []
        for h in range(H_A):
            kv, half = h // g, h % 2
            qz = jnp.where(lo_q if half == 0 else jnp.logical_not(lo_q),
                           q[:, (h // 2) * LANES:(h // 2 + 1) * LANES], zero)
            s = _dot_nt(qz, kd[kv])
            if banded:
                s = jnp.concatenate([s[:, :nwin] + bias, s[:, nwin:]], axis=1)
            snk = sink_ref[layer, h] * LOG2E
            m = jnp.maximum(jnp.max(s, axis=-1, keepdims=True), snk)
            e = jnp.exp2(s - m)
            pv = _dot(e.astype(BF16), vd[kv][half])
            outs.append(pv / (pltpu.roll(pv, HD, 1) + jnp.exp2(snk - m)))
        for c in range(H_A // 2):
            pair = jnp.where(lo_q, outs[2 * c], outs[2 * c + 1])
            o_ref[u * tq:(u + 1) * tq, c * LANES:(c + 1) * LANES] = pair.astype(o_ref.dtype)


def _attn_a(sink, q, k, v, k_ctx, v_ctx, *, layer, tq, nsub, banded):
    b, t, _ = q.shape
    has_ctx = k_ctx is not None
    rows = tq * nsub
    kv_spec = pl.BlockSpec((None, t, KV_A * HD), lambda bi, i: (bi, 0, 0))
    in_specs = [pl.BlockSpec(memory_space=pltpu.SMEM),
                pl.BlockSpec((None, rows, D_A), lambda bi, i: (bi, i, 0)), kv_spec, kv_spec]
    args = [sink, q, k, v]
    if has_ctx:
        s = k_ctx.shape[2]
        in_specs += [pl.BlockSpec((None, None, s, KV_A * HD), lambda bi, i: (bi, layer, 0, 0))] * 2
        args += [k_ctx, v_ctx]
    return pl.pallas_call(
        functools.partial(_attn_a_kernel, tq=tq, nsub=nsub, seq_len=t, banded=banded, has_ctx=has_ctx, layer=layer),
        out_shape=jax.ShapeDtypeStruct((b, t, D_A), BF16),
        grid=(b, t // rows),
        in_specs=in_specs,
        out_specs=pl.BlockSpec((None, rows, D_A), lambda bi, i: (bi, i, 0)),
        compiler_params=_params(("arbitrary", "arbitrary")),
        name="attn_a_latent" if has_ctx else "attn_a_context",
    )(*args)


def _attn_c_kernel(*refs, tq, ts, n_loc, has_ctx, lam_init):
    if has_ctx:
        lq1, lk1, lq2, lk2, sub_ref, q_ref, k_ref, v_ref, kc_ref, vc_ref, o_ref = refs
    else:
        lq1, lk1, lq2, lk2, sub_ref, q_ref, k_ref, v_ref, o_ref = refs
    lam = (jnp.exp(jnp.sum(lq1[...] * lk1[...], axis=-1, keepdims=True))
           - jnp.exp(jnp.sum(lq2[...] * lk2[...], axis=-1, keepdims=True)) + lam_init)
    lo = lax.broadcasted_iota(jnp.int32, (tq, LANES), 1) < HD
    zero = jnp.zeros((tq, LANES), BF16)
    heads = [slice(h * LANES, (h + 1) * LANES) for h in range(H_C)]
    qz = []
    for cols in heads:
        q12 = q_ref[:, cols]
        qz.append(jnp.concatenate([jnp.where(lo, q12, zero), jnp.where(lo, zero, q12)], axis=0))

    def update(carry, kt, vt):
        new = []
        ones = jnp.ones((kt.shape[0], LANES), BF16)
        for h, cols in enumerate(heads):
            m, acc = carry[h]
            s = _dot_nt(qz[h], kt[:, cols].astype(BF16))
            m_new = jnp.maximum(m, jnp.max(s, axis=-1, keepdims=True))
            alpha = jnp.exp2(m - m_new)
            p = jnp.exp2(s - m_new)
            v_ones = jnp.concatenate([vt[:, cols].astype(BF16), ones], axis=1)
            acc = alpha * acc + _dot(p.astype(BF16), v_ones)
            new.append((m_new, acc))
        return tuple(new)

    carry = tuple((jnp.full((2 * tq, 1), -jnp.inf, F32), jnp.zeros((2 * tq, 2 * LANES), F32)) for _ in heads)
    for j in range(n_loc):
        carry = update(carry, k_ref[j * ts:(j + 1) * ts, :], v_ref[j * ts:(j + 1) * ts, :])
    if has_ctx:
        carry = update(carry, kc_ref[...], vc_ref[...])
    for h, cols in enumerate(heads):
        _, acc = carry[h]
        num, den = acc[:, :DV_C], acc[:, DV_C:]
        o = num[:tq] / den[:tq] - lam * (num[tq:] / den[tq:])
        o = o * lax.rsqrt(jnp.mean(o * o, axis=-1, keepdims=True) + EPS) * sub_ref[...] * (1.0 - lam_init)
        o_ref[:, cols] = o.astype(o_ref.dtype)


def _attn_c(lam_vecs, subln3, q, k, v, k_ctx, v_ctx, *, layer, tq, ts, lam_init):
    b, t, _ = q.shape
    s_loc = k.shape[1]
    has_ctx = k_ctx is not None
    kv_spec = pl.BlockSpec((None, s_loc, D_C), lambda bi, i: (bi, 0, 0))
    in_specs = ([_layer_spec((1, HD), layer)] * 4 + [_layer_spec((1, DV_C), layer)]
                + [pl.BlockSpec((None, tq, D_C), lambda bi, i: (bi, i, 0)), kv_spec, kv_spec])
    args = list(lam_vecs) + [subln3, q, k, v]
    if has_ctx:
        sc = k_ctx.shape[2]
        in_specs += [pl.BlockSpec((None, None, sc, D_C), lambda bi, i: (bi, layer, 0, 0))] * 2
        args += [k_ctx, v_ctx]
    return pl.pallas_call(
        functools.partial(_attn_c_kernel, tq=tq, ts=ts, n_loc=s_loc // ts, has_ctx=has_ctx, lam_init=lam_init),
        out_shape=jax.ShapeDtypeStruct((b, t, D_C), BF16),
        grid=(b, t // tq),
        in_specs=in_specs,
        out_specs=pl.BlockSpec((None, tq, D_C), lambda bi, i: (bi, i, 0)),
        compiler_params=_params(("arbitrary", "arbitrary")),
        name="attn_c_latent" if has_ctx else "attn_c_context",
    )(*args)


def _lru_body(ins, outs, scratch, ids, *, tc, nc):
    xc_ref, xp_ref, xn_ref, cw_ref, cb_ref, wg_ref, bg_ref, lam_ref, h0_ref, perm_ref, inv_ref = ins
    ob_ref, st_ref = outs
    a_s, b_s, hs, ps, hf_s, xb_s, hcar, last_h, last_p, cin = scratch
    _, p, c = ids
    cidx = c + p * (nc - 1 - 2 * c)
    r0 = pl.multiple_of(cidx * tc, tc)
    ng = tc // SUBLANES

    @pl.when(p == 0)
    def _():
        cur = xc_ref[...]
        prev = jnp.where(cidx > 0, xp_ref[...], 0.0)
        nxt = jnp.where(cidx < nc - 1, xn_ref[...], 0.0)
        hi = cur.astype(BF16)
        rest = cur - hi.astype(F32)
        mid = rest.astype(BF16)
        lo = (rest - mid.astype(F32)).astype(BF16)
        x3 = (_dot(perm_ref[...], hi) + _dot(perm_ref[...], mid) + _dot(perm_ref[...], lo)).reshape(ng, SUBLANES, D_LRU)
        rows = lax.broadcasted_iota(jnp.int32, (SUBLANES, D_LRU), 0)
        before1 = jnp.where(rows == 0, prev[SUBLANES - 1:SUBLANES, :], pltpu.roll(x3[ng - 1], 1, 0))
        before2 = jnp.where(rows == 0, prev[SUBLANES - 2:SUBLANES - 1, :], pltpu.roll(x3[ng - 2], 1, 0))
        after1 = jnp.where(rows == SUBLANES - 1, nxt[0:1, :], pltpu.roll(x3[0], SUBLANES - 1, 0))
        xpad = jnp.concatenate([before2[None], before1[None], x3, after1[None]], axis=0)
        acc = jnp.broadcast_to(cb_ref[...], (ng, SUBLANES, D_LRU))
        for j in range(CONV_W):
            acc = acc + xpad[j:j + ng] * cw_ref[j:j + 1, :]
        xb_s[pl.ds(r0, tc), :] = acc.reshape(tc, D_LRU)

    xb = xb_s[pl.ds(r0, tc), :]
    gm = _dot(xb.astype(BF16), wg_ref[...]) + bg_ref[...]
    r = _sigmoid(gm[:, :D_LRU])
    ig = _sigmoid(gm[:, D_LRU:])
    nl = -lam_ref[...]
    softplus = jnp.maximum(nl, 0.0) + jnp.log1p(jnp.exp(-jnp.abs(nl)))
    decay = LRU_C * softplus
    a = jnp.exp2(r * (decay * (-LOG2E)))
    one_m_a2 = jnp.tanh(r * decay) * (a * a + 1.0)
    bb = jnp.where(one_m_a2 == 0.0, 0.0, one_m_a2 * lax.rsqrt(one_m_a2)) * (ig * xb)

    a_s[...] = a
    b_s[...] = bb

    def step(gi, carry):
        h, pc = carry
        g = gi + p * (ng - 1 - 2 * gi)
        base = pl.multiple_of(g * SUBLANES, SUBLANES)
        a_g = a_s[pl.ds(base, SUBLANES), :]
        h = a_g * h + b_s[pl.ds(base, SUBLANES), :]
        pc = a_g * pc
        hs[pl.ds(base, SUBLANES), :] = h
        ps[pl.ds(base, SUBLANES), :] = pc
        return h, pc

    init = (jnp.zeros((SUBLANES, D_LRU), F32), jnp.ones((SUBLANES, D_LRU), F32))
    last_h[...], last_p[...] = lax.fori_loop(0, ng, step, init, unroll=True)

    @pl.when(c == 0)
    def _():
        hcar[0:1, :] = h0_ref[pl.ds(p, 1), :]

    state = hcar[0:1, :]
    for rr in range(SUBLANES):
        r_in = rr + p * (SUBLANES - 1 - 2 * rr)
        cin[pl.ds(r_in, 1), :] = state
        state = last_p[pl.ds(r_in, 1), :] * state + last_h[pl.ds(r_in, 1), :]
    hcar[0:1, :] = state
    hfull = (hs[...].reshape(ng, SUBLANES, D_LRU)
             + ps[...].reshape(ng, SUBLANES, D_LRU) * cin[...]).reshape(tc, D_LRU)

    @pl.when(p == 0)
    def _():
        hf_s[pl.ds(r0, tc), :] = hfull

    @pl.when(p == 1)
    def _():
        both = (hf_s[pl.ds(r0, tc), :] + hfull).astype(BF16)
        ob_ref[...] = _dot(inv_ref[...], both).astype(ob_ref.dtype)

    @pl.when(c == nc - 1)
    def _():
        st_ref[pl.ds(p, 1), :] = state


def _lru_part(xb, conv_w, conv_b3, wg, bg, lam4, h0, *, layer, h0_layer, tc):
    b, t, _ = xb.shape
    nc = t // tc
    nb8 = t // SUBLANES
    per = tc // SUBLANES
    cidx = lambda p, c: c + p * (nc - 1 - 2 * c)
    in_specs = [
        pl.BlockSpec((None, tc, D_LRU), lambda bi, p, c: (bi, cidx(p, c), 0)),
        pl.BlockSpec((None, SUBLANES, D_LRU), lambda bi, p, c: (bi, jnp.maximum(cidx(p, c) * per - 1, 0), 0)),
        pl.BlockSpec((None, SUBLANES, D_LRU), lambda bi, p, c: (bi, jnp.minimum((cidx(p, c) + 1) * per, nb8 - 1), 0)),
        _layer_spec((CONV_W, D_LRU), layer),
        _layer_spec((1, D_LRU), layer),
        pl.BlockSpec((None, None, D_LRU, 2 * D_LRU), lambda bi, p, c: (layer, p, 0, 0)),
        pl.BlockSpec((None, None, 1, 2 * D_LRU), lambda bi, p, c: (layer, p, 0, 0)),
        pl.BlockSpec((None, None, 1, D_LRU), lambda bi, p, c: (layer, p, 0, 0)),
        pl.BlockSpec((None, None, 2, D_LRU), lambda bi, p, c: (bi, h0_layer, 0, 0)),
        pl.BlockSpec((tc, tc), lambda bi, p, c: (0, 0)),
        pl.BlockSpec((tc, tc), lambda bi, p, c: (0, 0)),
    ]
    j = jnp.arange(tc)
    src = (j % SUBLANES) * (tc // SUBLANES) + j // SUBLANES
    perm = (src[:, None] == j[None, :]).astype(BF16)
    chunk = (tc, D_LRU)
    row8 = (SUBLANES, D_LRU)
    return _Part("lru", functools.partial(_lru_body, tc=tc, nc=nc), (b, 2, nc), in_specs,
                 [xb, xb, xb, conv_w, conv_b3, wg, bg, lam4, h0, perm, perm.T],
                 [jax.ShapeDtypeStruct((b, t, D_LRU), BF16), jax.ShapeDtypeStruct((b, 2, D_LRU), F32)],
                 [pl.BlockSpec((None, tc, D_LRU), lambda bi, p, c: (bi, nc - 1 - p * c, 0)),
                  pl.BlockSpec((None, 2, D_LRU), lambda bi, p, c: (bi, 0, 0))],
                 [pltpu.VMEM(chunk, F32), pltpu.VMEM(chunk, F32), pltpu.VMEM(chunk, F32), pltpu.VMEM(chunk, F32),
                  pltpu.VMEM((t, D_LRU), F32), pltpu.VMEM((t, D_LRU), F32),
                  pltpu.VMEM(row8, F32), pltpu.VMEM(row8, F32), pltpu.VMEM(row8, F32), pltpu.VMEM(row8, F32)])


def _tail_kernel(x_ref, mod_ref, ng_ref, oa_ref, ob_ref, oc_ref, wgm_ref, wbr_ref, wout_ref, o_ref):
    x = x_ref[...]
    mod = mod_ref[...]
    hb = _normed_input(x, mod, ng_ref[...]).astype(BF16)
    y = None
    for br, br_ref in enumerate((oa_ref, ob_ref, oc_ref)):
        gm = _dot(hb, wgm_ref[:, br * N_BRANCH_COLS:(br + 1) * N_BRANCH_COLS])
        gt = gm[:, :D_A]
        u = (br_ref[...].astype(F32) * (gt * _sigmoid(gt))).astype(BF16)
        proj = _dot(u, wbr_ref[br])
        mg = _sigmoid(gm[:, D_A:])
        y = mg * proj if y is None else y + mg * proj
    gate = mod[:, 2 * D_MODEL:]
    o_ref[...] = x + gate * _dot(y.astype(BF16), wout_ref[...])


def _tail_part(x2d, mod4, ng3, oa, ob, oc, wgm, wbr, wout, *, layer, mod_row0, tokens_per_mod, tm=TM_TAIL):
    n = x2d.shape[0]
    row = lambda i: (i, 0)
    once = dict(pipeline_mode=pl.Buffered(1))

    def weight(shape):
        zeros = (0,) * len(shape)
        return pl.BlockSpec((None,) + shape, lambda i: (layer,) + zeros, **once)

    in_specs = [pl.BlockSpec((tm, D_MODEL), row),
                _mod_spec(layer, mod_row0, tokens_per_mod // tm),
                _layer_spec((1, D_MODEL), layer),
                pl.BlockSpec((tm, D_A), row),
                pl.BlockSpec((tm, D_LRU), row),
                pl.BlockSpec((tm, D_C), row),
                weight((D_MODEL, 3 * N_BRANCH_COLS)),
                weight((3, D_A, D_MODEL)),
                weight((D_MODEL, D_MODEL))]
    return _Part("tail", lambda ins, outs, scratch, ids: _tail_kernel(*ins, *outs), (n // tm,), in_specs,
                 [x2d, mod4, ng3, oa, ob, oc, wgm, wbr, wout],
                 [jax.ShapeDtypeStruct((n, D_MODEL), F32)], [pl.BlockSpec((tm, D_MODEL), row)])


def _rope_tables(seq_len):
    pos = jnp.arange(seq_len)
    row = (pos // GRID_W).astype(F32)
    col = (pos % GRID_W).astype(F32)
    inv = jnp.power(ROPE_BASE, -jnp.arange(ROPE_FREQS, dtype=F32) / ROPE_FREQS)
    ang_r = row[:, None] * inv
    ang_c = col[:, None] * inv
    cos = jnp.concatenate([jnp.cos(ang_r)] * 2 + [jnp.cos(ang_c)] * 2, axis=-1)
    sin = jnp.concatenate([-jnp.sin(ang_r), jnp.sin(ang_r), -jnp.sin(ang_c), jnp.sin(ang_c)], axis=-1)
    return jnp.tile(cos, (1, LANES // HD)), jnp.tile(sin, (1, LANES // HD))


def _block_diag(w):
    rows = w.reshape(w.shape[:-3] + (D_LRU, LRU_BW))
    tiled = jnp.tile(rows, (1,) * (rows.ndim - 1) + (LRU_BLOCKS,))
    blk = jnp.arange(D_LRU) // LRU_BW
    return jnp.where(blk[:, None] == blk[None, :], tiled, 0.0)


def kernel(x_prompt, x_sample, cache_a_k, cache_a_v, cache_c_k, cache_c_v, state_lru, c, c_ctx, norm_g, mod_w, mod_b, w_in, qn_a, kn_a, sink_a, conv_w, conv_b, lru_wa, lru_ba, lru_wx, lru_bx, lru_lam, qn_c, kn_c, lam_q1, lam_k1, lam_q2, lam_k2, subln_c, w_br_a, w_br_b, w_br_c, w_out):
    bp, sp, _ = x_prompt.shape
    bs, ss, _ = x_sample.shape
    past = cache_a_k.shape[2]

    cvecs = jnp.concatenate([c, c_ctx[None, :], jnp.zeros((SUBLANES - bs - 1, D_MODEL), F32)], axis=0)
    mod = _modulation(cvecs, mod_w, mod_b)
    mod4 = mod.reshape(DEPTH, SUBLANES, 1, 3 * D_MODEL)
    rope = _rope_tables(ss)

    w_front = _permute_cast(w_in, FRONT_BLOCKS)
    wgm = _permute_cast(w_in, TAIL_BLOCKS)
    wbr = jnp.stack([w_br_a, w_br_b, w_br_c], axis=1).astype(BF16)
    wout = w_out.astype(BF16)
    gains3 = jnp.concatenate([jnp.tile(qn_a * Q_SCALE, (1, H_A)), jnp.tile(qn_c * Q_SCALE, (1, 2 * H_C)),
                              jnp.tile(kn_c, (1, 2 * H_C)), jnp.tile(kn_a, (1, KV_A))],
                             axis=-1)[:, None, :]
    ng3 = norm_g[:, None, :]
    lru_wg = jnp.concatenate([_block_diag(lru_wa), _block_diag(lru_wx)], axis=-1).astype(BF16)
    lru_bg = jnp.concatenate([lru_ba, lru_bx], axis=-1)[:, :, None, :]
    lru_lam4 = lru_lam[:, :, None, :]
    lru_p = (conv_w, conv_b[:, None, :], lru_wg, lru_bg, lru_lam4)
    lam_vecs = [v[:, None, :] for v in (lam_q1, lam_k1, lam_q2, lam_k2)]
    subln3 = subln_c[:, None, :]
    cka = cache_a_k.reshape(bs, DEPTH, past, KV_A * HD)
    cva = cache_a_v.reshape(bs, DEPTH, past, KV_A * HD)
    ckc = cache_c_k.reshape(bs, DEPTH, past, D_C)
    cvc = cache_c_v.reshape(bs, DEPTH, past, D_C)
    zeros_h0 = jnp.zeros((bp, 1, 2, D_LRU), F32)

    xp = x_prompt.reshape(bp * sp, D_MODEL)
    xs = x_sample.reshape(bs * ss, D_MODEL)
    new_caches, new_states = [], []
    for l in range(DEPTH):
        lam_init = 0.8 - 0.6 * math.exp(-0.3 * l)
        ctx_mod = dict(layer=l, mod_row0=bs, tokens_per_mod=bp * sp)
        lat_mod = dict(layer=l, mod_row0=0, tokens_per_mod=ss)

        flat = lambda a: a.reshape(-1, a.shape[-1])
        r3 = lambda a: a.reshape(bp, sp, a.shape[-1])
        stacking = dict(stack_seq=sp, prev_caches=new_caches) if l == DEPTH - 1 else {}
        qa, ka, qc, kc, va, vc, xb, *stacked = _run(
            _front_part(xp, mod4, ng3, w_front, gains3, None, kv_dtype=F32, **ctx_mod, **stacking))
        new_caches.append((ka, va, kc, vc))
        oa = _attn_a(sink_a, r3(qa), r3(ka), r3(va), None, None, layer=l, tq=sp, nsub=1, banded=False)
        ob, st = _run(_lru_part(r3(xb), *lru_p, zeros_h0, layer=l, h0_layer=0, tc=sp))
        new_states.append(st)
        oc = _attn_c(lam_vecs, subln3, r3(qc), r3(kc), r3(vc), None, None, layer=l, tq=sp, ts=sp, lam_init=lam_init)
        (xp,) = _run(_tail_part(xp, mod4, ng3, flat(oa), flat(ob), flat(oc), wgm, wbr, wout, **ctx_mod))

        r3 = lambda a: a.reshape(bs, ss, a.shape[-1])
        qa, ka, qc, kc, va, vc, xb = _run(_front_part(xs, mod4, ng3, w_front, gains3, rope, kv_dtype=BF16, **lat_mod))
        oa = _attn_a(sink_a, r3(qa), r3(ka), r3(va), cka, cva, layer=l, tq=TQ_A, nsub=NSUB_A, banded=True)
        ob, _ = _run(_lru_part(r3(xb), *lru_p, state_lru, layer=l, h0_layer=l, tc=TC_LRU))
        oc = _attn_c(lam_vecs, subln3, r3(qc), r3(kc), r3(vc), ckc, cvc, layer=l, tq=TQ_C, ts=TS_C, lam_init=lam_init)
        (xs,) = _run(_tail_part(xs, mod4, ng3, flat(oa), flat(ob), flat(oc), wgm, wbr, wout, **lat_mod))

    ka_t, va_t, kc_t, vc = stacked
    return (xp.reshape(bp, sp, D_MODEL), xs.reshape(bs, ss, D_MODEL),
            ka_t.reshape(bp, DEPTH, KV_A, HD, sp).transpose(0, 1, 4, 2, 3),
            va_t.reshape(bp, DEPTH, KV_A, HD, sp).transpose(0, 1, 4, 2, 3),
            kc_t.reshape(bp, DEPTH, H_C, 2, HD, sp).transpose(0, 1, 5, 2, 3, 4),
            vc.reshape(bp, DEPTH, sp, H_C, DV_C),
            jnp.stack(new_states, axis=1))
```

```python
import functools
import math

import jax
import jax.numpy as jnp
from jax import lax
from jax.experimental import pallas as pl
from jax.experimental.pallas import tpu as pltpu

F32 = jnp.float32
BF16 = jnp.bfloat16

D_MODEL = 1024
DEPTH = 2
GRID_W = 64
HD = 64
SCALE = 1.0 / math.sqrt(HD)
LOG2E = math.log2(math.e)
Q_SCALE = SCALE * LOG2E
H_A = 8
KV_A = 2
WINDOW = 128
D_A = H_A * HD
D_LRU = 512
LRU_BLOCKS = 8
LRU_BW = D_LRU // LRU_BLOCKS
CONV_W = 4
LRU_C = 8.0
H_C = 4
DV_C = 2 * HD
D_C = H_C * DV_C
ROPE_BASE = 10000.0
ROPE_FREQS = HD // 4
EPS = 1e-6
LANES = 128
SUBLANES = 8
VMEM_LIMIT = 56 * 1024 * 1024

_SECTIONS = (("qa", D_A), ("kava", 2 * KV_A * HD), ("ga", D_A), ("xb", D_LRU), ("gb", D_LRU),
             ("qc", H_C * 2 * HD), ("kc", H_C * 2 * HD), ("vc", D_C), ("gc", D_C),
             ("mg_a", D_MODEL), ("mg_b", D_MODEL), ("mg_c", D_MODEL))
W_BLOCK = 256


def _col_blocks(*names):
    start, spans = 0, {}
    for name, width in _SECTIONS:
        spans[name] = (start, start + width)
        start += width
    blocks = []
    for name in names:
        a, b = spans[name]
        assert a % W_BLOCK == 0 and b % W_BLOCK == 0
        blocks += range(a // W_BLOCK, b // W_BLOCK)
    return tuple(blocks)


FRONT_BLOCKS = _col_blocks("qa", "qc", "kc", "kava", "vc", "xb")
TAIL_BLOCKS = _col_blocks("ga", "mg_a", "gb", "mg_b", "gc", "mg_c")
N_BRANCH_COLS = D_A + D_MODEL
N_NORM = D_A + KV_A * HD + 2 * H_C * 2 * HD
N_PLAIN = KV_A * HD + D_C + D_LRU

TM_FRONT = 512
TM_TAIL = 512
TQ_A = 2 * WINDOW
NSUB_A = 4
TQ_C = 1024
TS_C = 512
TC_LRU = 512


def _params(sem, vmem=VMEM_LIMIT):
    return pltpu.CompilerParams(dimension_semantics=sem, vmem_limit_bytes=vmem)


def _dot(a, b):
    return jnp.dot(a, b, preferred_element_type=F32)


def _dot_nt(a, b):
    return lax.dot_general(a, b, (((1,), (1,)), ((), ())), preferred_element_type=F32)


def _sigmoid(x):
    return 0.5 * jnp.tanh(0.5 * x) + 0.5


def _permute_cast_kernel(perm_ref, w_ref, o_ref):
    o_ref[...] = w_ref[...].astype(o_ref.dtype)


def _permute_cast(w, blocks):
    nl, k, _ = w.shape
    grid_spec = pltpu.PrefetchScalarGridSpec(
        num_scalar_prefetch=1, grid=(len(blocks),),
        in_specs=[pl.BlockSpec((nl, k, W_BLOCK), lambda j, perm: (0, 0, perm[j]))],
        out_specs=pl.BlockSpec((nl, k, W_BLOCK), lambda j, perm: (0, 0, j)))
    return pl.pallas_call(
        _permute_cast_kernel, grid_spec=grid_spec,
        out_shape=jax.ShapeDtypeStruct((nl, k, len(blocks) * W_BLOCK), BF16),
        compiler_params=_params(("arbitrary",)), name="permute_cast",
    )(jnp.asarray(blocks, jnp.int32), w)


def _mod_kernel(c_ref, w_ref, b_ref, o_ref):
    c = c_ref[...]
    a = c * _sigmoid(c)
    w = w_ref[...]
    a_hi = a.astype(BF16)
    a_lo = (a - a_hi.astype(F32)).astype(BF16)
    w_hi = w.astype(BF16)
    w_lo = (w - w_hi.astype(F32)).astype(BF16)
    o_ref[...] = _dot(a_hi, w_hi) + _dot(a_hi, w_lo) + _dot(a_lo, w_hi) + b_ref[...]


def _modulation(cvecs, mod_w, mod_b):
    tn = 768
    return pl.pallas_call(
        _mod_kernel,
        out_shape=jax.ShapeDtypeStruct((DEPTH, SUBLANES, 3 * D_MODEL), F32),
        grid=(DEPTH, 3 * D_MODEL // tn),
        in_specs=[pl.BlockSpec((SUBLANES, D_MODEL), lambda l, j: (0, 0)),
                  pl.BlockSpec((None, D_MODEL, tn), lambda l, j: (l, 0, j)),
                  pl.BlockSpec((None, 1, tn), lambda l, j: (l, 0, j))],
        out_specs=pl.BlockSpec((None, SUBLANES, tn), lambda l, j: (l, 0, j)),
        compiler_params=_params(("arbitrary", "arbitrary")),
        name="modulation",
    )(cvecs, mod_w, mod_b.reshape(DEPTH, 1, 3 * D_MODEL))


def _normed_input(x, mod, ng):
    ms = jnp.mean(x * x, axis=-1, keepdims=True)
    shift = mod[:, 0:D_MODEL]
    scale = mod[:, D_MODEL:2 * D_MODEL]
    return (x * lax.rsqrt(ms + EPS) * ng) * (1.0 + scale) + shift


def _layer_spec(shape, layer):
    zeros = (0,) * len(shape)
    return pl.BlockSpec((None,) + tuple(shape), lambda *_: (layer,) + zeros)


def _mod_spec(layer, row0, tiles_per_row):
    return pl.BlockSpec((None, None, 1, 3 * D_MODEL), lambda i: (layer, row0 + i // tiles_per_row, 0, 0))


class _Part:
    def __init__(self, name, body, grid, in_specs, args, out_shape, out_specs, scratch=()):
        self.name, self.body, self.grid = name, body, tuple(grid)
        self.in_specs, self.args = list(in_specs), list(args)
        self.out_shape, self.out_specs, self.scratch = list(out_shape), list(out_specs), list(scratch)


def _run(part):
    ni, no = len(part.args), len(part.out_shape)

    def kern(*refs):
        ids = tuple(pl.program_id(k) for k in range(len(part.grid)))
        part.body(refs[:ni], refs[ni:ni + no], refs[ni + no:], ids)

    return pl.pallas_call(
        kern, out_shape=part.out_shape, grid=part.grid, in_specs=part.in_specs, out_specs=part.out_specs,
        scratch_shapes=part.scratch, compiler_params=_params(("arbitrary",) * len(part.grid)), name=part.name)(*part.args)


def _front_body(ins, outs, scratch, ids, *, use_rope, n_prev=0, seq=None):
    x_ref, mod_ref, ng_ref, w_ref, gain_ref = ins[:5]
    qa_ref, ka_ref, qc_ref, kc_ref, va_ref, vc_ref, xb_ref = outs[:7]
    x = x_ref[...]
    tm = x.shape[0]
    hb = _normed_input(x, mod_ref[...], ng_ref[...]).astype(BF16)
    p1 = _dot(hb, w_ref[:, :N_NORM])
    lane = lax.broadcasted_iota(jnp.int32, (tm, LANES), 1)
    first_half = (lane & ROPE_FREQS) == 0
    wide = 2 * LANES
    same_head = (lax.broadcasted_iota(jnp.int32, (wide, wide), 0) // HD
                 == lax.broadcasted_iota(jnp.int32, (wide, wide), 1) // HD)
    ones_bd = jnp.where(same_head, 1.0, 0.0).astype(BF16)
    n_query = D_A + H_C * 2 * HD
    sums = []
    for c0 in range(0, N_NORM, wide):
        w = min(wide, N_NORM - c0)
        sq = p1[:, c0:c0 + w] * p1[:, c0:c0 + w]
        sq_hi = sq.astype(BF16)
        total = _dot(sq_hi, ones_bd[:w, :w])
        if c0 + w > n_query:
            total = total + _dot((sq - sq_hi.astype(F32)).astype(BF16), ones_bd[:w, :w])
        sums.append(total)
    dests = ([(qa_ref, c) for c in range(4)] + [(qc_ref, c) for c in range(4)]
             + [(kc_ref, c) for c in range(4)] + [(ka_ref, 0)])
    for c, (o_ref, oc) in enumerate(dests):
        pc = p1[:, c * LANES:(c + 1) * LANES]
        msq = sums[c // 2][:, (c % 2) * LANES:(c % 2 + 1) * LANES] * (1.0 / HD)
        y = pc * lax.rsqrt(msq + EPS) * gain_ref[:, c * LANES:(c + 1) * LANES]
        if use_rope:
            cos_ref, sin_ref = ins[5:7]
            partner = jnp.where(first_half, pltpu.roll(y, LANES - ROPE_FREQS, 1),
                                pltpu.roll(y, ROPE_FREQS, 1))
            y = y * cos_ref[...] + partner * sin_ref[...]
        o_ref[:, oc * LANES:(oc + 1) * LANES] = y.astype(o_ref.dtype)
    p2 = _dot(hb, w_ref[:, N_NORM:])
    va_ref[...] = p2[:, 0:KV_A * HD].astype(va_ref.dtype)
    vc_ref[...] = p2[:, KV_A * HD:KV_A * HD + D_C].astype(vc_ref.dtype)
    xb_ref[...] = p2[:, KV_A * HD + D_C:]
    if seq is not None:
        prev = ins[len(ins) - 4 * n_prev:]
        layers = [prev[4 * l:4 * l + 4] for l in range(n_prev)] + [(ka_ref, va_ref, kc_ref, vc_ref)]
        for a, st_ref in enumerate(outs[7:]):
            for l, arrays in enumerate(layers):
                for s in range(tm // seq):
                    rows = arrays[a][s * seq:(s + 1) * seq, :]
                    if a < 3:
                        st_ref[s, l] = rows.T
                    else:
                        for h in range(H_C):
                            st_ref[s, l, pl.ds(h, seq, stride=H_C), :] = rows[:, h * DV_C:(h + 1) * DV_C]


def _front_part(x2d, mod4, ng3, w_front, gains3, rope, *, layer, mod_row0, tokens_per_mod, kv_dtype, tm=TM_FRONT,
                stack_seq=None, prev_caches=()):
    n = x2d.shape[0]
    use_rope = rope is not None
    row = lambda i: (i, 0)
    in_specs = [pl.BlockSpec((tm, D_MODEL), row),
                _mod_spec(layer, mod_row0, tokens_per_mod // tm),
                _layer_spec((1, D_MODEL), layer),
                _layer_spec((D_MODEL, N_NORM + N_PLAIN), layer),
                _layer_spec((1, N_NORM), layer)]
    args = [x2d, mod4, ng3, w_front, gains3]
    if use_rope:
        per_seq = rope[0].shape[0] // tm
        in_specs += [pl.BlockSpec((tm, LANES), lambda i: (i % per_seq, 0))] * 2
        args += list(rope)
    widths = (D_A, KV_A * HD, H_C * 2 * HD, H_C * 2 * HD, KV_A * HD, D_C, D_LRU)
    dtypes = (BF16, kv_dtype, BF16, kv_dtype, kv_dtype, kv_dtype, F32)
    out_shape = [jax.ShapeDtypeStruct((n, w), d) for w, d in zip(widths, dtypes)]
    out_specs = [pl.BlockSpec((tm, w), row) for w in widths]
    if stack_seq is not None:
        assert len(prev_caches) == DEPTH - 1 and tm % stack_seq == 0
        for cache in prev_caches:
            in_specs += [pl.BlockSpec((tm, arr.shape[1]), row) for arr in cache]
            args += list(cache)
        per_tile = tm // stack_seq
        cache_w = (KV_A * HD, KV_A * HD, H_C * 2 * HD, D_C)
        shapes = [(DEPTH, w, stack_seq) for w in cache_w[:3]] + [(DEPTH, stack_seq * H_C, DV_C)]
        out_shape += [jax.ShapeDtypeStruct((n // stack_seq,) + s, F32) for s in shapes]
        out_specs += [pl.BlockSpec((per_tile,) + s, lambda i: (i, 0, 0, 0)) for s in shapes]
    body = functools.partial(_front_body, use_rope=use_rope, n_prev=len(prev_caches), seq=stack_seq)
    return _Part("front_rope" if use_rope else "front", body, (n // tm,), in_specs, args, out_shape, out_specs)


def _attn_a_kernel(*refs, tq, nsub, seq_len, banded, has_ctx, layer):
    if has_ctx:
        sink_ref, q_ref, k_ref, v_ref, kc_ref, vc_ref, o_ref = refs
    else:
        sink_ref, q_ref, k_ref, v_ref, o_ref = refs
    i = pl.program_id(1)
    g = H_A // KV_A

    def spread(k, v):
        k = k.astype(F32)
        v = v.astype(F32)
        lo = lax.broadcasted_iota(jnp.int32, k.shape, 1) < HD
        k_sw = pltpu.roll(k, HD, 1)
        v_sw = pltpu.roll(v, HD, 1)
        kd = (jnp.where(lo, k, k_sw).astype(BF16), jnp.where(lo, k_sw, k).astype(BF16))
        vd = ((jnp.where(lo, v, 1.0).astype(BF16), jnp.where(lo, 1.0, v_sw).astype(BF16)),
              (jnp.where(lo, v_sw, 1.0).astype(BF16), jnp.where(lo, 1.0, v).astype(BF16)))
        return kd, vd

    if has_ctx:
        kd_c, vd_c = spread(kc_ref[...], vc_ref[...])
    nwin = tq + 2 * WINDOW if banded else tq
    lo_q = lax.broadcasted_iota(jnp.int32, (tq, LANES), 1) < HD
    zero = jnp.zeros((tq, LANES), BF16)
    for u in range(nsub):
        blk = i * nsub + u
        if banded:
            start = pl.multiple_of(jnp.clip(blk * tq - WINDOW, 0, seq_len - nwin), WINDOW)
            kd, vd = spread(k_ref[pl.ds(start, nwin), :], v_ref[pl.ds(start, nwin), :])
            kpos = start + lax.broadcasted_iota(jnp.int32, (tq, nwin), 1)
            qpos = blk * tq + lax.broadcasted_iota(jnp.int32, (tq, nwin), 0)
            bias = jnp.where(jnp.abs(kpos - qpos) <= WINDOW, 0.0, -jnp.inf)
        else:
            own = pl.multiple_of(blk * tq, tq)
            kd, vd = spread(k_ref[pl.ds(own, tq), :], v_ref[pl.ds(own, tq), :])
        if has_ctx:
            kd = tuple(jnp.concatenate([kd[kv], kd_c[kv]], axis=0) for kv in range(KV_A))
            vd = tuple(tuple(jnp.concatenate([vd[kv][half], vd_c[kv][half]], axis=0) for half in range(2))
                       for kv in range(KV_A))
        q = q_ref[u * tq:(u + 1) * tq, :]
        outs = []
        for h in range(H_A):
            kv, half = h // g, h % 2
            qz = jnp.where(lo_q if half == 0 else jnp.logical_not(lo_q),
                           q[:, (h // 2) * LANES:(h // 2 + 1) * LANES], zero)
            s = _dot_nt(qz, kd[kv])
            if banded:
                s = jnp.concatenate([s[:, :nwin] + bias, s[:, nwin:]], axis=1)
            snk = sink_ref[layer, h] * LOG2E
            m = jnp.maximum(jnp.max(s, axis=-1, keepdims=True), snk)
            e = jnp.exp2(s - m)
            pv = _dot(e.astype(BF16), vd[kv][half])
            outs.append(pv / (pltpu.roll(pv, HD, 1) + jnp.exp2(snk - m)))
        for c in range(H_A // 2):
            pair = jnp.where(lo_q, outs[2 * c], outs[2 * c + 1])
            o_ref[u * tq:(u + 1) * tq, c * LANES:(c + 1) * LANES] = pair.astype(o_ref.dtype)


def _attn_a(sink, q, k, v, k_ctx, v_ctx, *, layer, tq, nsub, banded):
    b, t, _ = q.shape
    has_ctx = k_ctx is not None
    rows = tq * nsub
    kv_spec = pl.BlockSpec((None, t, KV_A * HD), lambda bi, i: (bi, 0, 0))
    in_specs = [pl.BlockSpec(memory_space=pltpu.SMEM),
                pl.BlockSpec((None, rows, D_A), lambda bi, i: (bi, i, 0)), kv_spec, kv_spec]
    args = [sink, q, k, v]
    if has_ctx:
        s = k_ctx.shape[2]
        in_specs += [pl.BlockSpec((None, None, s, KV_A * HD), lambda bi, i: (bi, layer, 0, 0))] * 2
        args += [k_ctx, v_ctx]
    return pl.pallas_call(
        functools.partial(_attn_a_kernel, tq=tq, nsub=nsub, seq_len=t, banded=banded, has_ctx=has_ctx, layer=layer),
        out_shape=jax.ShapeDtypeStruct((b, t, D_A), BF16),
        grid=(b, t // rows),
        in_specs=in_specs,
        out_specs=pl.BlockSpec((None, rows, D_A), lambda bi, i: (bi, i, 0)),
        compiler_params=_params(("arbitrary", "arbitrary")),
        name="attn_a_latent" if has_ctx else "attn_a_context",
    )(*args)


def _attn_c_kernel(*refs, tq, ts, n_loc, has_ctx, lam_init):
    if has_ctx:
        lq1, lk1, lq2, lk2, sub_ref, q_ref, k_ref, v_ref, kc_ref, vc_ref, o_ref = refs
    else:
        lq1, lk1, lq2, lk2, sub_ref, q_ref, k_ref, v_ref, o_ref = refs
    lam = (jnp.exp(jnp.sum(lq1[...] * lk1[...], axis=-1, keepdims=True))
           - jnp.exp(jnp.sum(lq2[...] * lk2[...], axis=-1, keepdims=True)) + lam_init)
    lo = lax.broadcasted_iota(jnp.int32, (tq, LANES), 1) < HD
    zero = jnp.zeros((tq, LANES), BF16)
    heads = [slice(h * LANES, (h + 1) * LANES) for h in range(H_C)]
    qz = []
    for cols in heads:
        q12 = q_ref[:, cols]
        qz.append(jnp.concatenate([jnp.where(lo, q12, zero), jnp.where(lo, zero, q12)], axis=0))

    def update(carry, kt, vt):
        new = []
        ones = jnp.ones((kt.shape[0], LANES), BF16)
        for h, cols in enumerate(heads):
            m, acc = carry[h]
            s = _dot_nt(qz[h], kt[:, cols].astype(BF16))
            m_new = jnp.maximum(m, jnp.max(s, axis=-1, keepdims=True))
            alpha = jnp.exp2(m - m_new)
            p = jnp.exp2(s - m_new)
            v_ones = jnp.concatenate([vt[:, cols].astype(BF16), ones], axis=1)
            acc = alpha * acc + _dot(p.astype(BF16), v_ones)
            new.append((m_new, acc))
        return tuple(new)

    carry = tuple((jnp.full((2 * tq, 1), -jnp.inf, F32), jnp.zeros((2 * tq, 2 * LANES), F32)) for _ in heads)
    for j in range(n_loc):
        carry = update(carry, k_ref[j * ts:(j + 1) * ts, :], v_ref[j * ts:(j + 1) * ts, :])
    if has_ctx:
        carry = update(carry, kc_ref[...], vc_ref[...])
    for h, cols in enumerate(heads):
        _, acc = carry[h]
        num, den = acc[:, :DV_C], acc[:, DV_C:]
        o = num[:tq] / den[:tq] - lam * (num[tq:] / den[tq:])
        o = o * lax.rsqrt(jnp.mean(o * o, axis=-1, keepdims=True) + EPS) * sub_ref[...] * (1.0 - lam_init)
        o_ref[:, cols] = o.astype(o_ref.dtype)


def _attn_c(lam_vecs, subln3, q, k, v, k_ctx, v_ctx, *, layer, tq, ts, lam_init):
    b, t, _ = q.shape
    s_loc = k.shape[1]
    has_ctx = k_ctx is not None
    kv_spec = pl.BlockSpec((None, s_loc, D_C), lambda bi, i: (bi, 0, 0))
    in_specs = ([_layer_spec((1, HD), layer)] * 4 + [_layer_spec((1, DV_C), layer)]
                + [pl.BlockSpec((None, tq, D_C), lambda bi, i: (bi, i, 0)), kv_spec, kv_spec])
    args = list(lam_vecs) + [subln3, q, k, v]
    if has_ctx:
        sc = k_ctx.shape[2]
        in_specs += [pl.BlockSpec((None, None, sc, D_C), lambda bi, i: (bi, layer, 0, 0))] * 2
        args += [k_ctx, v_ctx]
    return pl.pallas_call(
        functools.partial(_attn_c_kernel, tq=tq, ts=ts, n_loc=s_loc // ts, has_ctx=has_ctx, lam_init=lam_init),
        out_shape=jax.ShapeDtypeStruct((b, t, D_C), BF16),
        grid=(b, t // tq),
        in_specs=in_specs,
        out_specs=pl.BlockSpec((None, tq, D_C), lambda bi, i: (bi, i, 0)),
        compiler_params=_params(("arbitrary", "arbitrary")),
        name="attn_c_latent" if has_ctx else "attn_c_context",
    )(*args)


def _lru_body(ins, outs, scratch, ids, *, tc, nc):
    xc_ref, xp_ref, xn_ref, cw_ref, cb_ref, wg_ref, bg_ref, lam_ref, h0_ref, perm_ref, inv_ref = ins
    ob_ref, st_ref = outs
    a_s, b_s, hs, ps, hf_s, xb_s, hcar, last_h, last_p, cin = scratch
    _, p, c = ids
    cidx = c + p * (nc - 1 - 2 * c)
    r0 = pl.multiple_of(cidx * tc, tc)
    ng = tc // SUBLANES

    @pl.when(p == 0)
    def _():
        cur = xc_ref[...]
        prev = jnp.where(cidx > 0, xp_ref[...], 0.0)
        nxt = jnp.where(cidx < nc - 1, xn_ref[...], 0.0)
        hi = cur.astype(BF16)
        rest = cur - hi.astype(F32)
        mid = rest.astype(BF16)
        lo = (rest - mid.astype(F32)).astype(BF16)
        x3 = (_dot(perm_ref[...], hi) + _dot(perm_ref[...], mid) + _dot(perm_ref[...], lo)).reshape(ng, SUBLANES, D_LRU)
        rows = lax.broadcasted_iota(jnp.int32, (SUBLANES, D_LRU), 0)
        before1 = jnp.where(rows == 0, prev[SUBLANES - 1:SUBLANES, :], pltpu.roll(x3[ng - 1], 1, 0))
        before2 = jnp.where(rows == 0, prev[SUBLANES - 2:SUBLANES - 1, :], pltpu.roll(x3[ng - 2], 1, 0))
        after1 = jnp.where(rows == SUBLANES - 1, nxt[0:1, :], pltpu.roll(x3[0], SUBLANES - 1, 0))
        xpad = jnp.concatenate([before2[None], before1[None], x3, after1[None]], axis=0)
        acc = jnp.broadcast_to(cb_ref[...], (ng, SUBLANES, D_LRU))
        for j in range(CONV_W):
            acc = acc + xpad[j:j + ng] * cw_ref[j:j + 1, :]
        xb_s[pl.ds(r0, tc), :] = acc.reshape(tc, D_LRU)

    xb = xb_s[pl.ds(r0, tc), :]
    gm = _dot(xb.astype(BF16), wg_ref[...]) + bg_ref[...]
    r = _sigmoid(gm[:, :D_LRU])
    ig = _sigmoid(gm[:, D_LRU:])
    nl = -lam_ref[...]
    softplus = jnp.maximum(nl, 0.0) + jnp.log1p(jnp.exp(-jnp.abs(nl)))
    decay = LRU_C * softplus
    a = jnp.exp2(r * (decay * (-LOG2E)))
    one_m_a2 = jnp.tanh(r * decay) * (a * a + 1.0)
    bb = jnp.where(one_m_a2 == 0.0, 0.0, one_m_a2 * lax.rsqrt(one_m_a2)) * (ig * xb)

    a_s[...] = a
    b_s[...] = bb

    def step(gi, carry):
        h, pc = carry
        g = gi + p * (ng - 1 - 2 * gi)
        base = pl.multiple_of(g * SUBLANES, SUBLANES)
        a_g = a_s[pl.ds(base, SUBLANES), :]
        h = a_g * h + b_s[pl.ds(base, SUBLANES), :]
        pc = a_g * pc
        hs[pl.ds(base, SUBLANES), :] = h
        ps[pl.ds(base, SUBLANES), :] = pc
        return h, pc

    init = (jnp.zeros((SUBLANES, D_LRU), F32), jnp.ones((SUBLANES, D_LRU), F32))
    last_h[...], last_p[...] = lax.fori_loop(0, ng, step, init, unroll=True)

    @pl.when(c == 0)
    def _():
        hcar[0:1, :] = h0_ref[pl.ds(p, 1), :]

    state = hcar[0:1, :]
    for rr in range(SUBLANES):
        r_in = rr + p * (SUBLANES - 1 - 2 * rr)
        cin[pl.ds(r_in, 1), :] = state
        state = last_p[pl.ds(r_in, 1), :] * state + last_h[pl.ds(r_in, 1), :]
    hcar[0:1, :] = state
    hfull = (hs[...].reshape(ng, SUBLANES, D_LRU)
             + ps[...].reshape(ng, SUBLANES, D_LRU) * cin[...]).reshape(tc, D_LRU)

    @pl.when(p == 0)
    def _():
        hf_s[pl.ds(r0, tc), :] = hfull

    @pl.when(p == 1)
    def _():
        both = (hf_s[pl.ds(r0, tc), :] + hfull).astype(BF16)
        ob_ref[...] = _dot(inv_ref[...], both).astype(ob_ref.dtype)

    @pl.when(c == nc - 1)
    def _():
        st_ref[pl.ds(p, 1), :] = state


def _lru_part(xb, conv_w, conv_b3, wg, bg, lam4, h0, *, layer, h0_layer, tc):
    b, t, _ = xb.shape
    nc = t // tc
    nb8 = t // SUBLANES
    per = tc // SUBLANES
    cidx = lambda p, c: c + p * (nc - 1 - 2 * c)
    in_specs = [
        pl.BlockSpec((None, tc, D_LRU), lambda bi, p, c: (bi, cidx(p, c), 0)),
        pl.BlockSpec((None, SUBLANES, D_LRU), lambda bi, p, c: (bi, jnp.maximum(cidx(p, c) * per - 1, 0), 0)),
        pl.BlockSpec((None, SUBLANES, D_LRU), lambda bi, p, c: (bi, jnp.minimum((cidx(p, c) + 1) * per, nb8 - 1), 0)),
        _layer_spec((CONV_W, D_LRU), layer),
        _layer_spec((1, D_LRU), layer),
        pl.BlockSpec((None, None, D_LRU, 2 * D_LRU), lambda bi, p, c: (layer, p, 0, 0)),
        pl.BlockSpec((None, None, 1, 2 * D_LRU), lambda bi, p, c: (layer, p, 0, 0)),
        pl.BlockSpec((None, None, 1, D_LRU), lambda bi, p, c: (layer, p, 0, 0)),
        pl.BlockSpec((None, None, 2, D_LRU), lambda bi, p, c: (bi, h0_layer, 0, 0)),
        pl.BlockSpec((tc, tc), lambda bi, p, c: (0, 0)),
        pl.BlockSpec((tc, tc), lambda bi, p, c: (0, 0)),
    ]
    j = jnp.arange(tc)
    src = (j % SUBLANES) * (tc // SUBLANES) + j // SUBLANES
    perm = (src[:, None] == j[None, :]).astype(BF16)
    chunk = (tc, D_LRU)
    row8 = (SUBLANES, D_LRU)
    return _Part("lru", functools.partial(_lru_body, tc=tc, nc=nc), (b, 2, nc), in_specs,
                 [xb, xb, xb, conv_w, conv_b3, wg, bg, lam4, h0, perm, perm.T],
                 [jax.ShapeDtypeStruct((b, t, D_LRU), BF16), jax.ShapeDtypeStruct((b, 2, D_LRU), F32)],
                 [pl.BlockSpec((None, tc, D_LRU), lambda bi, p, c: (bi, nc - 1 - p * c, 0)),
                  pl.BlockSpec((None, 2, D_LRU), lambda bi, p, c: (bi, 0, 0))],
                 [pltpu.VMEM(chunk, F32), pltpu.VMEM(chunk, F32), pltpu.VMEM(chunk, F32), pltpu.VMEM(chunk, F32),
                  pltpu.VMEM((t, D_LRU), F32), pltpu.VMEM((t, D_LRU), F32),
                  pltpu.VMEM(row8, F32), pltpu.VMEM(row8, F32), pltpu.VMEM(row8, F32), pltpu.VMEM(row8, F32)])


def _tail_kernel(x_ref, mod_ref, ng_ref, oa_ref, ob_ref, oc_ref, wgm_ref, wbr_ref, wout_ref, o_ref):
    x = x_ref[...]
    mod = mod_ref[...]
    hb = _normed_input(x, mod, ng_ref[...]).astype(BF16)
    y = None
    for br, br_ref in enumerate((oa_ref, ob_ref, oc_ref)):
        gm = _dot(hb, wgm_ref[:, br * N_BRANCH_COLS:(br + 1) * N_BRANCH_COLS])
        gt = gm[:, :D_A]
        u = (br_ref[...].astype(F32) * (gt * _sigmoid(gt))).astype(BF16)
        proj = _dot(u, wbr_ref[br])
        mg = _sigmoid(gm[:, D_A:])
        y = mg * proj if y is None else y + mg * proj
    gate = mod[:, 2 * D_MODEL:]
    o_ref[...] = x + gate * _dot(y.astype(BF16), wout_ref[...])


def _tail_part(x2d, mod4, ng3, oa, ob, oc, wgm, wbr, wout, *, layer, mod_row0, tokens_per_mod, tm=TM_TAIL):
    n = x2d.shape[0]
    row = lambda i: (i, 0)
    once = dict(pipeline_mode=pl.Buffered(1))

    def weight(shape):
        zeros = (0,) * len(shape)
        return pl.BlockSpec((None,) + shape, lambda i: (layer,) + zeros, **once)

    in_specs = [pl.BlockSpec((tm, D_MODEL), row),
                _mod_spec(layer, mod_row0, tokens_per_mod // tm),
                _layer_spec((1, D_MODEL), layer),
                pl.BlockSpec((tm, D_A), row),
                pl.BlockSpec((tm, D_LRU), row),
                pl.BlockSpec((tm, D_C), row),
                weight((D_MODEL, 3 * N_BRANCH_COLS)),
                weight((3, D_A, D_MODEL)),
                weight((D_MODEL, D_MODEL))]
    return _Part("tail", lambda ins, outs, scratch, ids: _tail_kernel(*ins, *outs), (n // tm,), in_specs,
                 [x2d, mod4, ng3, oa, ob, oc, wgm, wbr, wout],
                 [jax.ShapeDtypeStruct((n, D_MODEL), F32)], [pl.BlockSpec((tm, D_MODEL), row)])


def _rope_tables(seq_len):
    pos = jnp.arange(seq_len)
    row = (pos // GRID_W).astype(F32)
    col = (pos % GRID_W).astype(F32)
    inv = jnp.power(ROPE_BASE, -jnp.arange(ROPE_FREQS, dtype=F32) / ROPE_FREQS)
    ang_r = row[:, None] * inv
    ang_c = col[:, None] * inv
    cos = jnp.concatenate([jnp.cos(ang_r)] * 2 + [jnp.cos(ang_c)] * 2, axis=-1)
    sin = jnp.concatenate([-jnp.sin(ang_r), jnp.sin(ang_r), -jnp.sin(ang_c), jnp.sin(ang_c)], axis=-1)
    return jnp.tile(cos, (1, LANES // HD)), jnp.tile(sin, (1, LANES // HD))


def _block_diag(w):
    rows = w.reshape(w.shape[:-3] + (D_LRU, LRU_BW))
    tiled = jnp.tile(rows, (1,) * (rows.ndim - 1) + (LRU_BLOCKS,))
    blk = jnp.arange(D_LRU) // LRU_BW
    return jnp.where(blk[:, None] == blk[None, :], tiled, 0.0)


def kernel(x_prompt, x_sample, cache_a_k, cache_a_v, cache_c_k, cache_c_v, state_lru, c, c_ctx, norm_g, mod_w, mod_b, w_in, qn_a, kn_a, sink_a, conv_w, conv_b, lru_wa, lru_ba, lru_wx, lru_bx, lru_lam, qn_c, kn_c, lam_q1, lam_k1, lam_q2, lam_k2, subln_c, w_br_a, w_br_b, w_br_c, w_out):
    bp, sp, _ = x_prompt.shape
    bs, ss, _ = x_sample.shape
    past = cache_a_k.shape[2]

    cvecs = jnp.concatenate([c, c_ctx[None, :], jnp.zeros((SUBLANES - bs - 1, D_MODEL), F32)], axis=0)
    mod = _modulation(cvecs, mod_w, mod_b)
    mod4 = mod.reshape(DEPTH, SUBLANES, 1, 3 * D_MODEL)
    rope = _rope_tables(ss)

    w_front = _permute_cast(w_in, FRONT_BLOCKS)
    wgm = _permute_cast(w_in, TAIL_BLOCKS)
    wbr = jnp.stack([w_br_a, w_br_b, w_br_c], axis=1).astype(BF16)
    wout = w_out.astype(BF16)
    gains3 = jnp.concatenate([jnp.tile(qn_a * Q_SCALE, (1, H_A)), jnp.tile(qn_c * Q_SCALE, (1, 2 * H_C)),
                              jnp.tile(kn_c, (1, 2 * H_C)), jnp.tile(kn_a, (1, KV_A))],
                             axis=-1)[:, None, :]
    ng3 = norm_g[:, None, :]
    lru_wg = jnp.concatenate([_block_diag(lru_wa), _block_diag(lru_wx)], axis=-1).astype(BF16)
    lru_bg = jnp.concatenate([lru_ba, lru_bx], axis=-1)[:, :, None, :]
    lru_lam4 = lru_lam[:, :, None, :]
    lru_p = (conv_w, conv_b[:, None, :], lru_wg, lru_bg, lru_lam4)
    lam_vecs = [v[:, None, :] for v in (lam_q1, lam_k1, lam_q2, lam_k2)]
    subln3 = subln_c[:, None, :]
    cka = cache_a_k.reshape(bs, DEPTH, past, KV_A * HD)
    cva = cache_a_v.reshape(bs, DEPTH, past, KV_A * HD)
    ckc = cache_c_k.reshape(bs, DEPTH, past, D_C)
    cvc = cache_c_v.reshape(bs, DEPTH, past, D_C)
    zeros_h0 = jnp.zeros((bp, 1, 2, D_LRU), F32)

    xp = x_prompt.reshape(bp * sp, D_MODEL)
    xs = x_sample.reshape(bs * ss, D_MODEL)
    new_caches, new_states = [], []
    for l in range(DEPTH):
        lam_init = 0.8 - 0.6 * math.exp(-0.3 * l)
        ctx_mod = dict(layer=l, mod_row0=bs, tokens_per_mod=bp * sp)
        lat_mod = dict(layer=l, mod_row0=0, tokens_per_mod=ss)

        flat = lambda a: a.reshape(-1, a.shape[-1])
        r3 = lambda a: a.reshape(bp, sp, a.shape[-1])
        stacking = dict(stack_seq=sp, prev_caches=new_caches) if l == DEPTH - 1 else {}
        qa, ka, qc, kc, va, vc, xb, *stacked = _run(
            _front_part(xp, mod4, ng3, w_front, gains3, None, kv_dtype=F32, **ctx_mod, **stacking))
        new_caches.append((ka, va, kc, vc))
        oa = _attn_a(sink_a, r3(qa), r3(ka), r3(va), None, None, layer=l, tq=sp, nsub=1, banded=False)
        ob, st = _run(_lru_part(r3(xb), *lru_p, zeros_h0, layer=l, h0_layer=0, tc=sp))
        new_states.append(st)
        oc = _attn_c(lam_vecs, subln3, r3(qc), r3(kc), r3(vc), None, None, layer=l, tq=sp, ts=sp, lam_init=lam_init)
        (xp,) = _run(_tail_part(xp, mod4, ng3, flat(oa), flat(ob), flat(oc), wgm, wbr, wout, **ctx_mod))

        r3 = lambda a: a.reshape(bs, ss, a.shape[-1])
        qa, ka, qc, kc, va, vc, xb = _run(_front_part(xs, mod4, ng3, w_front, gains3, rope, kv_dtype=BF16, **lat_mod))
        oa = _attn_a(sink_a, r3(qa), r3(ka), r3(va), cka, cva, layer=l, tq=TQ_A, nsub=NSUB_A, banded=True)
        ob, _ = _run(_lru_part(r3(xb), *lru_p, state_lru, layer=l, h0_layer=l, tc=TC_LRU))
        oc = _attn_c(lam_vecs, subln3, r3(qc), r3(kc), r3(vc), ckc, cvc, layer=l, tq=TQ_C, ts=TS_C, lam_init=lam_init)
        (xs,) = _run(_tail_part(xs, mod4, ng3, flat(oa), flat(ob), flat(oc), wgm, wbr, wout, **lat_mod))

    ka_t, va_t, kc_t, vc = stacked
    return (xp.reshape(bp, sp, D_MODEL), xs.reshape(bs, ss, D_MODEL),
            ka_t.reshape(bp, DEPTH, KV_A, HD, sp).transpose(0, 1, 4, 2, 3),
            va_t.reshape(bp, DEPTH, KV_A, HD, sp).transpose(0, 1, 4, 2, 3),
            kc_t.reshape(bp, DEPTH, H_C, 2, HD, sp).transpose(0, 1, 5, 2, 3, 4),
            vc.reshape(bp, DEPTH, sp, H_C, DV_C),
            jnp.stack(new_states, axis=1))
```

```python
import functools
import math

import jax
import jax.numpy as jnp
from jax import lax
from jax.experimental import pallas as pl
from jax.experimental.pallas import tpu as pltpu

F32 = jnp.float32
BF16 = jnp.bfloat16

D_MODEL = 1024
DEPTH = 2
GRID_W = 64
HD = 64
SCALE = 1.0 / math.sqrt(HD)
LOG2E = math.log2(math.e)
Q_SCALE = SCALE * LOG2E
H_A = 8
KV_A = 2
WINDOW = 128
D_A = H_A * HD
D_LRU = 512
LRU_BLOCKS = 8
LRU_BW = D_LRU // LRU_BLOCKS
CONV_W = 4
LRU_C = 8.0
H_C = 4
DV_C = 2 * HD
D_C = H_C * DV_C
ROPE_BASE = 10000.0
ROPE_FREQS = HD // 4
EPS = 1e-6
F32_TINY = float(jnp.finfo(jnp.float32).tiny)
LANES = 128
SUBLANES = 8
VMEM_LIMIT = 56 * 1024 * 1024

_SECTIONS = (("qa", D_A), ("kava", 2 * KV_A * HD), ("ga", D_A), ("xb", D_LRU), ("gb", D_LRU),
             ("qc", H_C * 2 * HD), ("kc", H_C * 2 * HD), ("vc", D_C), ("gc", D_C),
             ("mg_a", D_MODEL), ("mg_b", D_MODEL), ("mg_c", D_MODEL))
W_BLOCK = 256


def _col_blocks(*names):
    start, spans = 0, {}
    for name, width in _SECTIONS:
        spans[name] = (start, start + width)
        start += width
    blocks = []
    for name in names:
        a, b = spans[name]
        assert a % W_BLOCK == 0 and b % W_BLOCK == 0
        blocks += range(a // W_BLOCK, b // W_BLOCK)
    return tuple(blocks)


FRONT_BLOCKS = _col_blocks("qa", "qc", "kc", "kava", "vc", "xb")
TAIL_BLOCKS = _col_blocks("ga", "mg_a", "gb", "mg_b", "gc", "mg_c")
N_BRANCH_COLS = D_A + D_MODEL
N_NORM = D_A + KV_A * HD + 2 * H_C * 2 * HD
N_PLAIN = KV_A * HD + D_C + D_LRU

TM_FRONT = 512
TM_TAIL = 512
TQ_A = 2 * WINDOW
NSUB_A = 4
TQ_C = 512
TS_C = 512
TC_LRU = 512


def _params(sem, vmem=VMEM_LIMIT):
    return pltpu.CompilerParams(dimension_semantics=sem, vmem_limit_bytes=vmem)


def _dot(a, b):
    return jnp.dot(a, b, preferred_element_type=F32)


def _dot_nt(a, b):
    return lax.dot_general(a, b, (((1,), (1,)), ((), ())), preferred_element_type=F32)


def _sigmoid(x):
    return 0.5 * jnp.tanh(0.5 * x) + 0.5


def _permute_cast_kernel(perm_ref, w_ref, o_ref):
    o_ref[...] = w_ref[...].astype(o_ref.dtype)


def _permute_cast(w, blocks):
    nl, k, _ = w.shape
    grid_spec = pltpu.PrefetchScalarGridSpec(
        num_scalar_prefetch=1, grid=(len(blocks),),
        in_specs=[pl.BlockSpec((nl, k, W_BLOCK), lambda j, perm: (0, 0, perm[j]))],
        out_specs=pl.BlockSpec((nl, k, W_BLOCK), lambda j, perm: (0, 0, j)))
    return pl.pallas_call(
        _permute_cast_kernel, grid_spec=grid_spec,
        out_shape=jax.ShapeDtypeStruct((nl, k, len(blocks) * W_BLOCK), BF16),
        compiler_params=_params(("arbitrary",)), name="permute_cast",
    )(jnp.asarray(blocks, jnp.int32), w)


def _mod_kernel(c_ref, w_ref, b_ref, o_ref):
    c = c_ref[...]
    a = c * _sigmoid(c)
    w = w_ref[...]
    a_hi = a.astype(BF16)
    a_lo = (a - a_hi.astype(F32)).astype(BF16)
    w_hi = w.astype(BF16)
    w_lo = (w - w_hi.astype(F32)).astype(BF16)
    o_ref[...] = _dot(a_hi, w_hi) + _dot(a_hi, w_lo) + _dot(a_lo, w_hi) + b_ref[...]


def _modulation(cvecs, mod_w, mod_b):
    tn = 768
    return pl.pallas_call(
        _mod_kernel,
        out_shape=jax.ShapeDtypeStruct((DEPTH, SUBLANES, 3 * D_MODEL), F32),
        grid=(DEPTH, 3 * D_MODEL // tn),
        in_specs=[pl.BlockSpec((SUBLANES, D_MODEL), lambda l, j: (0, 0)),
                  pl.BlockSpec((None, D_MODEL, tn), lambda l, j: (l, 0, j)),
                  pl.BlockSpec((None, 1, tn), lambda l, j: (l, 0, j))],
        out_specs=pl.BlockSpec((None, SUBLANES, tn), lambda l, j: (l, 0, j)),
        compiler_params=_params(("arbitrary", "arbitrary")),
        name="modulation",
    )(cvecs, mod_w, mod_b.reshape(DEPTH, 1, 3 * D_MODEL))


def _normed_input(x, mod, ng):
    ms = jnp.mean(x * x, axis=-1, keepdims=True)
    shift = mod[:, 0:D_MODEL]
    scale = mod[:, D_MODEL:2 * D_MODEL]
    return (x * lax.rsqrt(ms + EPS) * ng) * (1.0 + scale) + shift


def _layer_spec(shape, layer):
    zeros = (0,) * len(shape)
    return pl.BlockSpec((None,) + tuple(shape), lambda *_: (layer,) + zeros)


def _mod_spec(layer, row0, tiles_per_row):
    return pl.BlockSpec((None, None, 1, 3 * D_MODEL), lambda i: (layer, row0 + i // tiles_per_row, 0, 0))


class _Part:
    def __init__(self, name, body, grid, in_specs, args, out_shape, out_specs, scratch=()):
        self.name, self.body, self.grid = name, body, tuple(grid)
        self.in_specs, self.args = list(in_specs), list(args)
        self.out_shape, self.out_specs, self.scratch = list(out_shape), list(out_specs), list(scratch)


def _run(part):
    ni, no = len(part.args), len(part.out_shape)

    def kern(*refs):
        ids = tuple(pl.program_id(k) for k in range(len(part.grid)))
        part.body(refs[:ni], refs[ni:ni + no], refs[ni + no:], ids)

    return pl.pallas_call(
        kern, out_shape=part.out_shape, grid=part.grid, in_specs=part.in_specs, out_specs=part.out_specs,
        scratch_shapes=part.scratch, compiler_params=_params(("arbitrary",) * len(part.grid)), name=part.name)(*part.args)


def _front_body(ins, outs, scratch, ids, *, use_rope, n_prev=0, seq=None):
    x_ref, mod_ref, ng_ref, w_ref, gain_ref = ins[:5]
    qa_ref, ka_ref, qc_ref, kc_ref, va_ref, vc_ref, xb_ref = outs[:7]
    x = x_ref[...]
    tm = x.shape[0]
    hb = _normed_input(x, mod_ref[...], ng_ref[...]).astype(BF16)
    p1 = _dot(hb, w_ref[:, :N_NORM])
    lane = lax.broadcasted_iota(jnp.int32, (tm, LANES), 1)
    first_half = (lane & ROPE_FREQS) == 0
    wide = 2 * LANES
    same_head = (lax.broadcasted_iota(jnp.int32, (wide, wide), 0) // HD
                 == lax.broadcasted_iota(jnp.int32, (wide, wide), 1) // HD)
    ones_bd = jnp.where(same_head, 1.0, 0.0).astype(BF16)
    n_query = D_A + H_C * 2 * HD
    sums = []
    for c0 in range(0, N_NORM, wide):
        w = min(wide, N_NORM - c0)
        sq = p1[:, c0:c0 + w] * p1[:, c0:c0 + w]
        sq_hi = sq.astype(BF16)
        total = _dot(sq_hi, ones_bd[:w, :w])
        if c0 + w > n_query:
            total = total + _dot((sq - sq_hi.astype(F32)).astype(BF16), ones_bd[:w, :w])
        sums.append(total)
    dests = ([(qa_ref, c) for c in range(4)] + [(qc_ref, c) for c in range(4)]
             + [(kc_ref, c) for c in range(4)] + [(ka_ref, 0)])
    for c, (o_ref, oc) in enumerate(dests):
        pc = p1[:, c * LANES:(c + 1) * LANES]
        msq = sums[c // 2][:, (c % 2) * LANES:(c % 2 + 1) * LANES] * (1.0 / HD)
        y = pc * lax.rsqrt(msq + EPS) * gain_ref[:, c * LANES:(c + 1) * LANES]
        if use_rope:
            cos_ref, sin_ref = ins[5:7]
            partner = jnp.where(first_half, pltpu.roll(y, LANES - ROPE_FREQS, 1),
                                pltpu.roll(y, ROPE_FREQS, 1))
            y = y * cos_ref[...] + partner * sin_ref[...]
        o_ref[:, oc * LANES:(oc + 1) * LANES] = y.astype(o_ref.dtype)
    p2 = _dot(hb, w_ref[:, N_NORM:])
    va_ref[...] = p2[:, 0:KV_A * HD].astype(va_ref.dtype)
    vc_ref[...] = p2[:, KV_A * HD:KV_A * HD + D_C].astype(vc_ref.dtype)
    xb_ref[...] = p2[:, KV_A * HD + D_C:]
    if seq is not None:
        prev = ins[len(ins) - 4 * n_prev:]
        layers = [prev[4 * l:4 * l + 4] for l in range(n_prev)] + [(ka_ref, va_ref, kc_ref, vc_ref)]
        for a, st_ref in enumerate(outs[7:]):
            for l, arrays in enumerate(layers):
                for s in range(tm // seq):
                    rows = arrays[a][s * seq:(s + 1) * seq, :]
                    if a < 3:
                        st_ref[s, l] = rows.T
                    else:
                        for h in range(H_C):
                            st_ref[s, l, pl.ds(h, seq, stride=H_C), :] = rows[:, h * DV_C:(h + 1) * DV_C]


def _front_part(x2d, mod4, ng3, w_front, gains3, rope, *, layer, mod_row0, tokens_per_mod, kv_dtype, tm=TM_FRONT,
                stack_seq=None, prev_caches=()):
    n = x2d.shape[0]
    use_rope = rope is not None
    row = lambda i: (i, 0)
    in_specs = [pl.BlockSpec((tm, D_MODEL), row),
                _mod_spec(layer, mod_row0, tokens_per_mod // tm),
                _layer_spec((1, D_MODEL), layer),
                _layer_spec((D_MODEL, N_NORM + N_PLAIN), layer),
                _layer_spec((1, N_NORM), layer)]
    args = [x2d, mod4, ng3, w_front, gains3]
    if use_rope:
        per_seq = rope[0].shape[0] // tm
        in_specs += [pl.BlockSpec((tm, LANES), lambda i: (i % per_seq, 0))] * 2
        args += list(rope)
    widths = (D_A, KV_A * HD, H_C * 2 * HD, H_C * 2 * HD, KV_A * HD, D_C, D_LRU)
    dtypes = (BF16, kv_dtype, BF16, kv_dtype, kv_dtype, kv_dtype, F32)
    out_shape = [jax.ShapeDtypeStruct((n, w), d) for w, d in zip(widths, dtypes)]
    out_specs = [pl.BlockSpec((tm, w), row) for w in widths]
    if stack_seq is not None:
        assert len(prev_caches) == DEPTH - 1 and tm % stack_seq == 0
        for cache in prev_caches:
            in_specs += [pl.BlockSpec((tm, arr.shape[1]), row) for arr in cache]
            args += list(cache)
        per_tile = tm // stack_seq
        cache_w = (KV_A * HD, KV_A * HD, H_C * 2 * HD, D_C)
        shapes = [(DEPTH, w, stack_seq) for w in cache_w[:3]] + [(DEPTH, stack_seq * H_C, DV_C)]
        out_shape += [jax.ShapeDtypeStruct((n // stack_seq,) + s, F32) for s in shapes]
        out_specs += [pl.BlockSpec((per_tile,) + s, lambda i: (i, 0, 0, 0)) for s in shapes]
    body = functools.partial(_front_body, use_rope=use_rope, n_prev=len(prev_caches), seq=stack_seq)
    return _Part("front_rope" if use_rope else "front", body, (n // tm,), in_specs, args, out_shape, out_specs)


def _attn_a_kernel(*refs, tq, nsub, seq_len, banded, has_ctx, layer):
    if has_ctx:
        sink_ref, q_ref, k_ref, v_ref, kc_ref, vc_ref, o_ref = refs
    else:
        sink_ref, q_ref, k_ref, v_ref, o_ref = refs
    i = pl.program_id(1)
    g = H_A // KV_A

    def spread(k, v):
        k = k.astype(F32)
        v = v.astype(F32)
        lo = lax.broadcasted_iota(jnp.int32, k.shape, 1) < HD
        k_sw = pltpu.roll(k, HD, 1)
        v_sw = pltpu.roll(v, HD, 1)
        kd = (jnp.where(lo, k, k_sw).astype(BF16), jnp.where(lo, k_sw, k).astype(BF16))
        vd = ((jnp.where(lo, v, 1.0).astype(BF16), jnp.where(lo, 1.0, v_sw).astype(BF16)),
              (jnp.where(lo, v_sw, 1.0).astype(BF16), jnp.where(lo, 1.0, v).astype(BF16)))
        return kd, vd

    if has_ctx:
        kd_c, vd_c = spread(kc_ref[...], vc_ref[...])
    nwin = tq + 2 * WINDOW if banded else tq
    lo_q = lax.broadcasted_iota(jnp.int32, (tq, LANES), 1) < HD
    zero = jnp.zeros((tq, LANES), BF16)
    for u in range(nsub):
        blk = i * nsub + u
        if banded:
            start = pl.multiple_of(jnp.clip(blk * tq - WINDOW, 0, seq_len - nwin), WINDOW)
            kd, vd = spread(k_ref[pl.ds(start, nwin), :], v_ref[pl.ds(start, nwin), :])
            kpos = start + lax.broadcasted_iota(jnp.int32, (tq, nwin), 1)
            qpos = blk * tq + lax.broadcasted_iota(jnp.int32, (tq, nwin), 0)
            bias = jnp.where(jnp.abs(kpos - qpos) <= WINDOW, 0.0, -jnp.inf)
        else:
            own = pl.multiple_of(blk * tq, tq)
            kd, vd = spread(k_ref[pl.ds(own, tq), :], v_ref[pl.ds(own, tq), :])
        if has_ctx:
            kd = tuple(jnp.concatenate([kd[kv], kd_c[kv]], axis=0) for kv in range(KV_A))
            vd = tuple(tuple(jnp.concatenate([vd[kv][half], vd_c[kv][half]], axis=0) for half in range(2))
                       for kv in range(KV_A))
        q = q_ref[u * tq:(u + 1) * tq, :]
        outs = []
        for h in range(H_A):
            kv, half = h // g, h % 2
            qz = jnp.where(lo_q if half == 0 else jnp.logical_not(lo_q),
                           q[:, (h // 2) * LANES:(h // 2 + 1) * LANES], zero)
            s = _dot_nt(qz, kd[kv])
            if banded:
                s = jnp.concatenate([s[:, :nwin] + bias, s[:, nwin:]], axis=1)
            snk = sink_ref[layer, h] * LOG2E
            m = jnp.maximum(jnp.max(s, axis=-1, keepdims=True), snk)
            e = jnp.exp2(s - m)
            pv = _dot(e.astype(BF16), vd[kv][half])
            outs.append(pv / (pltpu.roll(pv, HD, 1) + jnp.exp2(snk - m)))
        for c in range(H_A // 2):
            pair = jnp.where(lo_q, outs[2 * c], outs[2 * c + 1])
            o_ref[u * tq:(u + 1) * tq, c * LANES:(c + 1) * LANES] = pair.astype(o_ref.dtype)


def _attn_a(sink, q, k, v, k_ctx, v_ctx, *, layer, tq, nsub, banded):
    b, t, _ = q.shape
    has_ctx = k_ctx is not None
    rows = tq * nsub
    kv_spec = pl.BlockSpec((None, t, KV_A * HD), lambda bi, i: (bi, 0, 0))
    in_specs = [pl.BlockSpec(memory_space=pltpu.SMEM),
                pl.BlockSpec((None, rows, D_A), lambda bi, i: (bi, i, 0)), kv_spec, kv_spec]
    args = [sink, q, k, v]
    if has_ctx:
        s = k_ctx.shape[2]
        in_specs += [pl.BlockSpec((None, None, s, KV_A * HD), lambda bi, i: (bi, layer, 0, 0))] * 2
        args += [k_ctx, v_ctx]
    return pl.pallas_call(
        functools.partial(_attn_a_kernel, tq=tq, nsub=nsub, seq_len=t, banded=banded, has_ctx=has_ctx, layer=layer),
        out_shape=jax.ShapeDtypeStruct((b, t, D_A), BF16),
        grid=(b, t // rows),
        in_specs=in_specs,
        out_specs=pl.BlockSpec((None, rows, D_A), lambda bi, i: (bi, i, 0)),
        compiler_params=_params(("arbitrary", "arbitrary")),
        name="attn_a_latent" if has_ctx else "attn_a_context",
    )(*args)


def _attn_c_kernel(*refs, tq, ts, n_loc, has_ctx, lam_init):
    if has_ctx:
        lq1, lk1, lq2, lk2, sub_ref, q_ref, k_ref, v_ref, kc_ref, vc_ref, o_ref = refs
    else:
        lq1, lk1, lq2, lk2, sub_ref, q_ref, k_ref, v_ref, o_ref = refs
    lam = (jnp.exp(jnp.sum(lq1[...] * lk1[...], axis=-1, keepdims=True))
           - jnp.exp(jnp.sum(lq2[...] * lk2[...], axis=-1, keepdims=True)) + lam_init)
    lo = lax.broadcasted_iota(jnp.int32, (tq, LANES), 1) < HD
    zero = jnp.zeros((tq, LANES), BF16)
    heads = [slice(h * LANES, (h + 1) * LANES) for h in range(H_C)]
    qz = []
    for cols in heads:
        q12 = q_ref[:, cols]
        qz.append(jnp.concatenate([jnp.where(lo, q12, zero), jnp.where(lo, zero, q12)], axis=0))

    def update(carry, kt, vt):
        new = []
        ones = jnp.ones((kt.shape[0], LANES), BF16)
        for h, cols in enumerate(heads):
            m, acc = carry[h]
            s = _dot_nt(qz[h], kt[:, cols].astype(BF16))
            m_new = jnp.maximum(m, jnp.max(s, axis=-1, keepdims=True))
            alpha = jnp.exp2(m - m_new)
            p = jnp.exp2(s - m_new)
            v_ones = jnp.concatenate([vt[:, cols].astype(BF16), ones], axis=1)
            acc = alpha * acc + _dot(p.astype(BF16), v_ones)
            new.append((m_new, acc))
        return tuple(new)

    carry = tuple((jnp.full((2 * tq, 1), -jnp.inf, F32), jnp.zeros((2 * tq, 2 * LANES), F32)) for _ in heads)
    for j in range(n_loc):
        carry = update(carry, k_ref[j * ts:(j + 1) * ts, :], v_ref[j * ts:(j + 1) * ts, :])
    if has_ctx:
        carry = update(carry, kc_ref[...], vc_ref[...])
    for h, cols in enumerate(heads):
        _, acc = carry[h]
        num, den = acc[:, :DV_C], acc[:, DV_C:]
        o = num[:tq] / den[:tq] - lam * (num[tq:] / den[tq:])
        o = o * lax.rsqrt(jnp.mean(o * o, axis=-1, keepdims=True) + EPS) * sub_ref[...] * (1.0 - lam_init)
        o_ref[:, cols] = o.astype(o_ref.dtype)


def _attn_c(lam_vecs, subln3, q, k, v, k_ctx, v_ctx, *, layer, tq, ts, lam_init):
    b, t, _ = q.shape
    s_loc = k.shape[1]
    has_ctx = k_ctx is not None
    kv_spec = pl.BlockSpec((None, s_loc, D_C), lambda bi, i: (bi, 0, 0))
    in_specs = ([_layer_spec((1, HD), layer)] * 4 + [_layer_spec((1, DV_C), layer)]
                + [pl.BlockSpec((None, tq, D_C), lambda bi, i: (bi, i, 0)), kv_spec, kv_spec])
    args = list(lam_vecs) + [subln3, q, k, v]
    if has_ctx:
        sc = k_ctx.shape[2]
        in_specs += [pl.BlockSpec((None, None, sc, D_C), lambda bi, i: (bi, layer, 0, 0))] * 2
        args += [k_ctx, v_ctx]
    return pl.pallas_call(
        functools.partial(_attn_c_kernel, tq=tq, ts=ts, n_loc=s_loc // ts, has_ctx=has_ctx, lam_init=lam_init),
        out_shape=jax.ShapeDtypeStruct((b, t, D_C), BF16),
        grid=(b, t // tq),
        in_specs=in_specs,
        out_specs=pl.BlockSpec((None, tq, D_C), lambda bi, i: (bi, i, 0)),
        compiler_params=_params(("arbitrary", "arbitrary")),
        name="attn_c_latent" if has_ctx else "attn_c_context",
    )(*args)


def _lru_body(ins, outs, scratch, ids, *, tc, nc):
    xc_ref, xp_ref, xn_ref, cw_ref, cb_ref, wg_ref, bg_ref, lam_ref, h0_ref, perm_ref, inv_ref = ins
    ob_ref, st_ref = outs
    a_s, b_s, hs, ps, hf_s, xb_s, hcar, last_h, last_p, cin = scratch
    _, p, c = ids
    cidx = c + p * (nc - 1 - 2 * c)
    r0 = pl.multiple_of(cidx * tc, tc)
    ng = tc // SUBLANES

    @pl.when(p == 0)
    def _():
        cur = xc_ref[...]
        prev = jnp.where(cidx > 0, xp_ref[...], 0.0)
        nxt = jnp.where(cidx < nc - 1, xn_ref[...], 0.0)
        hi = cur.astype(BF16)
        rest = cur - hi.astype(F32)
        mid = rest.astype(BF16)
        lo = (rest - mid.astype(F32)).astype(BF16)
        x3 = (_dot(perm_ref[...], hi) + _dot(perm_ref[...], mid) + _dot(perm_ref[...], lo)).reshape(ng, SUBLANES, D_LRU)
        rows = lax.broadcasted_iota(jnp.int32, (SUBLANES, D_LRU), 0)
        before1 = jnp.where(rows == 0, prev[SUBLANES - 1:SUBLANES, :], pltpu.roll(x3[ng - 1], 1, 0))
        before2 = jnp.where(rows == 0, prev[SUBLANES - 2:SUBLANES - 1, :], pltpu.roll(x3[ng - 2], 1, 0))
        after1 = jnp.where(rows == SUBLANES - 1, nxt[0:1, :], pltpu.roll(x3[0], SUBLANES - 1, 0))
        xpad = jnp.concatenate([before2[None], before1[None], x3, after1[None]], axis=0)
        acc = jnp.broadcast_to(cb_ref[...], (ng, SUBLANES, D_LRU))
        for j in range(CONV_W):
            acc = acc + xpad[j:j + ng] * cw_ref[j:j + 1, :]
        xb_s[pl.ds(r0, tc), :] = acc.reshape(tc, D_LRU)

    xb = xb_s[pl.ds(r0, tc), :]
    gm = _dot(xb.astype(BF16), wg_ref[...]) + bg_ref[...]
    nl = -lam_ref[...]
    softplus = jnp.maximum(nl, 0.0) + jnp.log1p(jnp.exp(-jnp.abs(nl)))
    half_decay = (0.5 * LRU_C) * softplus
    rd = jnp.tanh(gm[:, :D_LRU]) * half_decay + half_decay
    half_xb = 0.5 * xb
    gated = jnp.tanh(gm[:, D_LRU:]) * half_xb + half_xb
    a = jnp.exp2(rd * (-LOG2E))
    one_m_a2 = jnp.tanh(rd) * (a * a + 1.0)
    bb = one_m_a2 * lax.rsqrt(jnp.maximum(one_m_a2, F32_TINY)) * gated

    a_s[...] = a
    b_s[...] = bb

    def step(gi, carry):
        h, pc = carry
        g = gi + p * (ng - 1 - 2 * gi)
        base = pl.multiple_of(g * SUBLANES, SUBLANES)
        a_g = a_s[pl.ds(base, SUBLANES), :]
        h = a_g * h + b_s[pl.ds(base, SUBLANES), :]
        pc = a_g * pc
        hs[pl.ds(base, SUBLANES), :] = h
        ps[pl.ds(base, SUBLANES), :] = pc
        return h, pc

    init = (jnp.zeros((SUBLANES, D_LRU), F32), jnp.ones((SUBLANES, D_LRU), F32))
    last_h[...], last_p[...] = lax.fori_loop(0, ng, step, init, unroll=True)

    @pl.when(c == 0)
    def _():
        hcar[0:1, :] = h0_ref[pl.ds(p, 1), :]

    state = hcar[0:1, :]
    for rr in range(SUBLANES):
        r_in = rr + p * (SUBLANES - 1 - 2 * rr)
        cin[pl.ds(r_in, 1), :] = state
        state = last_p[pl.ds(r_in, 1), :] * state + last_h[pl.ds(r_in, 1), :]
    hcar[0:1, :] = state
    hfull = (hs[...].reshape(ng, SUBLANES, D_LRU)
             + ps[...].reshape(ng, SUBLANES, D_LRU) * cin[...]).reshape(tc, D_LRU)

    @pl.when(p == 0)
    def _():
        hf_s[pl.ds(r0, tc), :] = hfull

    @pl.when(p == 1)
    def _():
        both = (hf_s[pl.ds(r0, tc), :] + hfull).astype(BF16)
        ob_ref[...] = _dot(inv_ref[...], both).astype(ob_ref.dtype)

    @pl.when(c == nc - 1)
    def _():
        st_ref[pl.ds(p, 1), :] = state


def _lru_part(xb, conv_w, conv_b3, wg, bg, lam4, h0, *, layer, h0_layer, tc):
    b, t, _ = xb.shape
    nc = t // tc
    nb8 = t // SUBLANES
    per = tc // SUBLANES
    cidx = lambda p, c: c + p * (nc - 1 - 2 * c)
    in_specs = [
        pl.BlockSpec((None, tc, D_LRU), lambda bi, p, c: (bi, cidx(p, c), 0)),
        pl.BlockSpec((None, SUBLANES, D_LRU), lambda bi, p, c: (bi, jnp.maximum(cidx(p, c) * per - 1, 0), 0)),
        pl.BlockSpec((None, SUBLANES, D_LRU), lambda bi, p, c: (bi, jnp.minimum((cidx(p, c) + 1) * per, nb8 - 1), 0)),
        _layer_spec((CONV_W, D_LRU), layer),
        _layer_spec((1, D_LRU), layer),
        pl.BlockSpec((None, None, D_LRU, 2 * D_LRU), lambda bi, p, c: (layer, p, 0, 0)),
        pl.BlockSpec((None, None, 1, 2 * D_LRU), lambda bi, p, c: (layer, p, 0, 0)),
        pl.BlockSpec((None, None, 1, D_LRU), lambda bi, p, c: (layer, p, 0, 0)),
        pl.BlockSpec((None, None, 2, D_LRU), lambda bi, p, c: (bi, h0_layer, 0, 0)),
        pl.BlockSpec((tc, tc), lambda bi, p, c: (0, 0)),
        pl.BlockSpec((tc, tc), lambda bi, p, c: (0, 0)),
    ]
    j = jnp.arange(tc)
    src = (j % SUBLANES) * (tc // SUBLANES) + j // SUBLANES
    perm = (src[:, None] == j[None, :]).astype(BF16)
    chunk = (tc, D_LRU)
    row8 = (SUBLANES, D_LRU)
    return _Part("lru", functools.partial(_lru_body, tc=tc, nc=nc), (b, 2, nc), in_specs,
                 [xb, xb, xb, conv_w, conv_b3, wg, bg, lam4, h0, perm, perm.T],
                 [jax.ShapeDtypeStruct((b, t, D_LRU), BF16), jax.ShapeDtypeStruct((b, 2, D_LRU), F32)],
                 [pl.BlockSpec((None, tc, D_LRU), lambda bi, p, c: (bi, nc - 1 - p * c, 0)),
                  pl.BlockSpec((None, 2, D_LRU), lambda bi, p, c: (bi, 0, 0))],
                 [pltpu.VMEM(chunk, F32), pltpu.VMEM(chunk, F32), pltpu.VMEM(chunk, F32), pltpu.VMEM(chunk, F32),
                  pltpu.VMEM((t, D_LRU), F32), pltpu.VMEM((t, D_LRU), F32),
                  pltpu.VMEM(row8, F32), pltpu.VMEM(row8, F32), pltpu.VMEM(row8, F32), pltpu.VMEM(row8, F32)])


def _tail_kernel(x_ref, mod_ref, ng_ref, oa_ref, ob_ref, oc_ref, wgm_ref, wbr_ref, wout_ref, o_ref):
    x = x_ref[...]
    mod = mod_ref[...]
    hb = _normed_input(x, mod, ng_ref[...]).astype(BF16)
    y = None
    for br, br_ref in enumerate((oa_ref, ob_ref, oc_ref)):
        gm = _dot(hb, wgm_ref[:, br * N_BRANCH_COLS:(br + 1) * N_BRANCH_COLS])
        gt = gm[:, :D_A]
        u = (br_ref[...].astype(F32) * (gt * _sigmoid(gt))).astype(BF16)
        proj = _dot(u, wbr_ref[br])
        mg = _sigmoid(gm[:, D_A:])
        y = mg * proj if y is None else y + mg * proj
    gate = mod[:, 2 * D_MODEL:]
    o_ref[...] = x + gate * _dot(y.astype(BF16), wout_ref[...])


def _tail_part(x2d, mod4, ng3, oa, ob, oc, wgm, wbr, wout, *, layer, mod_row0, tokens_per_mod, tm=TM_TAIL):
    n = x2d.shape[0]
    row = lambda i: (i, 0)
    once = dict(pipeline_mode=pl.Buffered(1))

    def weight(shape):
        zeros = (0,) * len(shape)
        return pl.BlockSpec((None,) + shape, lambda i: (layer,) + zeros, **once)

    in_specs = [pl.BlockSpec((tm, D_MODEL), row),
                _mod_spec(layer, mod_row0, tokens_per_mod // tm),
                _layer_spec((1, D_MODEL), layer),
                pl.BlockSpec((tm, D_A), row),
                pl.BlockSpec((tm, D_LRU), row),
                pl.BlockSpec((tm, D_C), row),
                weight((D_MODEL, 3 * N_BRANCH_COLS)),
                weight((3, D_A, D_MODEL)),
                weight((D_MODEL, D_MODEL))]
    return _Part("tail", lambda ins, outs, scratch, ids: _tail_kernel(*ins, *outs), (n // tm,), in_specs,
                 [x2d, mod4, ng3, oa, ob, oc, wgm, wbr, wout],
                 [jax.ShapeDtypeStruct((n, D_MODEL), F32)], [pl.BlockSpec((tm, D_MODEL), row)])


def _rope_tables(seq_len):
    pos = jnp.arange(seq_len)
    row = (pos // GRID_W).astype(F32)
    col = (pos % GRID_W).astype(F32)
    inv = jnp.power(ROPE_BASE, -jnp.arange(ROPE_FREQS, dtype=F32) / ROPE_FREQS)
    ang_r = row[:, None] * inv
    ang_c = col[:, None] * inv
    cos = jnp.concatenate([jnp.cos(ang_r)] * 2 + [jnp.cos(ang_c)] * 2, axis=-1)
    sin = jnp.concatenate([-jnp.sin(ang_r), jnp.sin(ang_r), -jnp.sin(ang_c), jnp.sin(ang_c)], axis=-1)
    return jnp.tile(cos, (1, LANES // HD)), jnp.tile(sin, (1, LANES // HD))


def _block_diag(w):
    rows = w.reshape(w.shape[:-3] + (D_LRU, LRU_BW))
    tiled = jnp.tile(rows, (1,) * (rows.ndim - 1) + (LRU_BLOCKS,))
    blk = jnp.arange(D_LRU) // LRU_BW
    return jnp.where(blk[:, None] == blk[None, :], tiled, 0.0)


def kernel(x_prompt, x_sample, cache_a_k, cache_a_v, cache_c_k, cache_c_v, state_lru, c, c_ctx, norm_g, mod_w, mod_b, w_in, qn_a, kn_a, sink_a, conv_w, conv_b, lru_wa, lru_ba, lru_wx, lru_bx, lru_lam, qn_c, kn_c, lam_q1, lam_k1, lam_q2, lam_k2, subln_c, w_br_a, w_br_b, w_br_c, w_out):
    bp, sp, _ = x_prompt.shape
    bs, ss, _ = x_sample.shape
    past = cache_a_k.shape[2]

    cvecs = jnp.concatenate([c, c_ctx[None, :], jnp.zeros((SUBLANES - bs - 1, D_MODEL), F32)], axis=0)
    mod = _modulation(cvecs, mod_w, mod_b)
    mod4 = mod.reshape(DEPTH, SUBLANES, 1, 3 * D_MODEL)
    rope = _rope_tables(ss)

    w_front = _permute_cast(w_in, FRONT_BLOCKS)
    wgm = _permute_cast(w_in, TAIL_BLOCKS)
    wbr = jnp.stack([w_br_a, w_br_b, w_br_c], axis=1).astype(BF16)
    wout = w_out.astype(BF16)
    gains3 = jnp.concatenate([jnp.tile(qn_a * Q_SCALE, (1, H_A)), jnp.tile(qn_c * Q_SCALE, (1, 2 * H_C)),
                              jnp.tile(kn_c, (1, 2 * H_C)), jnp.tile(kn_a, (1, KV_A))],
                             axis=-1)[:, None, :]
    ng3 = norm_g[:, None, :]
    lru_wg = (0.5 * jnp.concatenate([_block_diag(lru_wa), _block_diag(lru_wx)], axis=-1)).astype(BF16)
    lru_bg = 0.5 * jnp.concatenate([lru_ba, lru_bx], axis=-1)[:, :, None, :]
    lru_lam4 = lru_lam[:, :, None, :]
    lru_p = (conv_w, conv_b[:, None, :], lru_wg, lru_bg, lru_lam4)
    lam_vecs = [v[:, None, :] for v in (lam_q1, lam_k1, lam_q2, lam_k2)]
    subln3 = subln_c[:, None, :]
    cka = cache_a_k.reshape(bs, DEPTH, past, KV_A * HD)
    cva = cache_a_v.reshape(bs, DEPTH, past, KV_A * HD)
    ckc = cache_c_k.reshape(bs, DEPTH, past, D_C)
    cvc = cache_c_v.reshape(bs, DEPTH, past, D_C)
    zeros_h0 = jnp.zeros((bp, 1, 2, D_LRU), F32)

    xp = x_prompt.reshape(bp * sp, D_MODEL)
    xs = x_sample.reshape(bs * ss, D_MODEL)
    new_caches, new_states = [], []
    for l in range(DEPTH):
        lam_init = 0.8 - 0.6 * math.exp(-0.3 * l)
        ctx_mod = dict(layer=l, mod_row0=bs, tokens_per_mod=bp * sp)
        lat_mod = dict(layer=l, mod_row0=0, tokens_per_mod=ss)

        flat = lambda a: a.reshape(-1, a.shape[-1])
        r3 = lambda a: a.reshape(bp, sp, a.shape[-1])
        stacking = dict(stack_seq=sp, prev_caches=new_caches) if l == DEPTH - 1 else {}
        qa, ka, qc, kc, va, vc, xb, *stacked = _run(
            _front_part(xp, mod4, ng3, w_front, gains3, None, kv_dtype=F32, **ctx_mod, **stacking))
        new_caches.append((ka, va, kc, vc))
        oa = _attn_a(sink_a, r3(qa), r3(ka), r3(va), None, None, layer=l, tq=sp, nsub=1, banded=False)
        ob, st = _run(_lru_part(r3(xb), *lru_p, zeros_h0, layer=l, h0_layer=0, tc=sp))
        new_states.append(st)
        oc = _attn_c(lam_vecs, subln3, r3(qc), r3(kc), r3(vc), None, None, layer=l, tq=sp, ts=sp, lam_init=lam_init)
        (xp,) = _run(_tail_part(xp, mod4, ng3, flat(oa), flat(ob), flat(oc), wgm, wbr, wout, **ctx_mod))

        r3 = lambda a: a.reshape(bs, ss, a.shape[-1])
        qa, ka, qc, kc, va, vc, xb = _run(_front_part(xs, mod4, ng3, w_front, gains3, rope, kv_dtype=BF16, **lat_mod))
        oa = _attn_a(sink_a, r3(qa), r3(ka), r3(va), cka, cva, layer=l, tq=TQ_A, nsub=NSUB_A, banded=True)
        ob, _ = _run(_lru_part(r3(xb), *lru_p, state_lru, layer=l, h0_layer=l, tc=TC_LRU))
        oc = _attn_c(lam_vecs, subln3, r3(qc), r3(kc), r3(vc), ckc, cvc, layer=l, tq=TQ_C, ts=TS_C, lam_init=lam_init)
        (xs,) = _run(_tail_part(xs, mod4, ng3, flat(oa), flat(ob), flat(oc), wgm, wbr, wout, **lat_mod))

    ka_t, va_t, kc_t, vc = stacked
    return (xp.reshape(bp, sp, D_MODEL), xs.reshape(bs, ss, D_MODEL),
            ka_t.reshape(bp, DEPTH, KV_A, HD, sp).transpose(0, 1, 4, 2, 3),
            va_t.reshape(bp, DEPTH, KV_A, HD, sp).transpose(0, 1, 4, 2, 3),
            kc_t.reshape(bp, DEPTH, H_C, 2, HD, sp).transpose(0, 1, 5, 2, 3, 4),
            vc.reshape(bp, DEPTH, sp, H_C, DV_C),
            jnp.stack(new_states, axis=1))
```

```python
import functools
import math

import jax
import jax.numpy as jnp
from jax import lax
from jax.experimental import pallas as pl
from jax.experimental.pallas import tpu as pltpu

F32 = jnp.float32
BF16 = jnp.bfloat16

D_MODEL = 1024
DEPTH = 2
GRID_W = 64
HD = 64
SCALE = 1.0 / math.sqrt(HD)
LOG2E = math.log2(math.e)
Q_SCALE = SCALE * LOG2E
H_A = 8
KV_A = 2
WINDOW = 128
D_A = H_A * HD
D_LRU = 512
LRU_BLOCKS = 8
LRU_BW = D_LRU // LRU_BLOCKS
CONV_W = 4
LRU_C = 8.0
H_C = 4
DV_C = 2 * HD
D_C = H_C * DV_C
ROPE_BASE = 10000.0
ROPE_FREQS = HD // 4
EPS = 1e-6
F32_TINY = float(jnp.finfo(jnp.float32).tiny)
LANES = 128
SUBLANES = 8
VMEM_LIMIT = 56 * 1024 * 1024

_SECTIONS = (("qa", D_A), ("kava", 2 * KV_A * HD), ("ga", D_A), ("xb", D_LRU), ("gb", D_LRU),
             ("qc", H_C * 2 * HD), ("kc", H_C * 2 * HD), ("vc", D_C), ("gc", D_C),
             ("mg_a", D_MODEL), ("mg_b", D_MODEL), ("mg_c", D_MODEL))
W_BLOCK = 256


def _col_blocks(*names):
    start, spans = 0, {}
    for name, width in _SECTIONS:
        spans[name] = (start, start + width)
        start += width
    blocks = []
    for name in names:
        a, b = spans[name]
        assert a % W_BLOCK == 0 and b % W_BLOCK == 0
        blocks += range(a // W_BLOCK, b // W_BLOCK)
    return tuple(blocks)


FRONT_BLOCKS = _col_blocks("qa", "qc", "kc", "kava", "vc", "xb")
TAIL_BLOCKS = _col_blocks("ga", "mg_a", "gb", "mg_b", "gc", "mg_c")
N_BRANCH_COLS = D_A + D_MODEL
N_NORM = D_A + KV_A * HD + 2 * H_C * 2 * HD
N_PLAIN = KV_A * HD + D_C + D_LRU

TM_FRONT = 512
TM_TAIL = 512
TQ_A = 2 * WINDOW
NSUB_A = 4
TQ_C = 512
TS_C = 256
TC_LRU = 512


def _params(sem, vmem=VMEM_LIMIT):
    return pltpu.CompilerParams(dimension_semantics=sem, vmem_limit_bytes=vmem)


def _dot(a, b):
    return jnp.dot(a, b, preferred_element_type=F32)


def _dot_nt(a, b):
    return lax.dot_general(a, b, (((1,), (1,)), ((), ())), preferred_element_type=F32)


def _sigmoid(x):
    return 0.5 * jnp.tanh(0.5 * x) + 0.5


def _permute_cast_kernel(perm_ref, w_ref, o_ref):
    o_ref[...] = w_ref[...].astype(o_ref.dtype)


def _permute_cast(w, blocks):
    nl, k, _ = w.shape
    grid_spec = pltpu.PrefetchScalarGridSpec(
        num_scalar_prefetch=1, grid=(len(blocks),),
        in_specs=[pl.BlockSpec((nl, k, W_BLOCK), lambda j, perm: (0, 0, perm[j]))],
        out_specs=pl.BlockSpec((nl, k, W_BLOCK), lambda j, perm: (0, 0, j)))
    return pl.pallas_call(
        _permute_cast_kernel, grid_spec=grid_spec,
        out_shape=jax.ShapeDtypeStruct((nl, k, len(blocks) * W_BLOCK), BF16),
        compiler_params=_params(("arbitrary",)), name="permute_cast",
    )(jnp.asarray(blocks, jnp.int32), w)


def _mod_kernel(c_ref, w_ref, b_ref, o_ref):
    c = c_ref[...]
    a = c * _sigmoid(c)
    w = w_ref[...]
    a_hi = a.astype(BF16)
    a_lo = (a - a_hi.astype(F32)).astype(BF16)
    w_hi = w.astype(BF16)
    w_lo = (w - w_hi.astype(F32)).astype(BF16)
    o_ref[...] = _dot(a_hi, w_hi) + _dot(a_hi, w_lo) + _dot(a_lo, w_hi) + b_ref[...]


def _modulation(cvecs, mod_w, mod_b):
    tn = 768
    return pl.pallas_call(
        _mod_kernel,
        out_shape=jax.ShapeDtypeStruct((DEPTH, SUBLANES, 3 * D_MODEL), F32),
        grid=(DEPTH, 3 * D_MODEL // tn),
        in_specs=[pl.BlockSpec((SUBLANES, D_MODEL), lambda l, j: (0, 0)),
                  pl.BlockSpec((None, D_MODEL, tn), lambda l, j: (l, 0, j)),
                  pl.BlockSpec((None, 1, tn), lambda l, j: (l, 0, j))],
        out_specs=pl.BlockSpec((None, SUBLANES, tn), lambda l, j: (l, 0, j)),
        compiler_params=_params(("arbitrary", "arbitrary")),
        name="modulation",
    )(cvecs, mod_w, mod_b.reshape(DEPTH, 1, 3 * D_MODEL))


def _normed_input(x, mod, ng):
    ms = jnp.mean(x * x, axis=-1, keepdims=True)
    shift = mod[:, 0:D_MODEL]
    scale = mod[:, D_MODEL:2 * D_MODEL]
    return (x * lax.rsqrt(ms + EPS) * ng) * (1.0 + scale) + shift


def _layer_spec(shape, layer):
    zeros = (0,) * len(shape)
    return pl.BlockSpec((None,) + tuple(shape), lambda *_: (layer,) + zeros)


def _mod_spec(layer, row0, tiles_per_row):
    return pl.BlockSpec((None, None, 1, 3 * D_MODEL), lambda i: (layer, row0 + i // tiles_per_row, 0, 0))


class _Part:
    def __init__(self, name, body, grid, in_specs, args, out_shape, out_specs, scratch=()):
        self.name, self.body, self.grid = name, body, tuple(grid)
        self.in_specs, self.args = list(in_specs), list(args)
        self.out_shape, self.out_specs, self.scratch = list(out_shape), list(out_specs), list(scratch)


def _run(part):
    ni, no = len(part.args), len(part.out_shape)

    def kern(*refs):
        ids = tuple(pl.program_id(k) for k in range(len(part.grid)))
        part.body(refs[:ni], refs[ni:ni + no], refs[ni + no:], ids)

    return pl.pallas_call(
        kern, out_shape=part.out_shape, grid=part.grid, in_specs=part.in_specs, out_specs=part.out_specs,
        scratch_shapes=part.scratch, compiler_params=_params(("arbitrary",) * len(part.grid)), name=part.name)(*part.args)


def _front_body(ins, outs, scratch, ids, *, use_rope, n_prev=0, seq=None):
    x_ref, mod_ref, ng_ref, w_ref, gain_ref = ins[:5]
    qa_ref, ka_ref, qc_ref, kc_ref, va_ref, vc_ref, xb_ref = outs[:7]
    x = x_ref[...]
    tm = x.shape[0]
    hb = _normed_input(x, mod_ref[...], ng_ref[...]).astype(BF16)
    p1 = _dot(hb, w_ref[:, :N_NORM])
    lane = lax.broadcasted_iota(jnp.int32, (tm, LANES), 1)
    first_half = (lane & ROPE_FREQS) == 0
    wide = 2 * LANES
    same_head = (lax.broadcasted_iota(jnp.int32, (wide, wide), 0) // HD
                 == lax.broadcasted_iota(jnp.int32, (wide, wide), 1) // HD)
    ones_bd = jnp.where(same_head, 1.0, 0.0).astype(BF16)
    n_query = D_A + H_C * 2 * HD
    sums = []
    for c0 in range(0, N_NORM, wide):
        w = min(wide, N_NORM - c0)
        sq = p1[:, c0:c0 + w] * p1[:, c0:c0 + w]
        sq_hi = sq.astype(BF16)
        total = _dot(sq_hi, ones_bd[:w, :w])
        if c0 + w > n_query:
            total = total + _dot((sq - sq_hi.astype(F32)).astype(BF16), ones_bd[:w, :w])
        sums.append(total)
    dests = ([(qa_ref, c) for c in range(4)] + [(qc_ref, c) for c in range(4)]
             + [(kc_ref, c) for c in range(4)] + [(ka_ref, 0)])
    for c, (o_ref, oc) in enumerate(dests):
        pc = p1[:, c * LANES:(c + 1) * LANES]
        msq = sums[c // 2][:, (c % 2) * LANES:(c % 2 + 1) * LANES] * (1.0 / HD)
        y = pc * lax.rsqrt(msq + EPS) * gain_ref[:, c * LANES:(c + 1) * LANES]
        if use_rope:
            cos_ref, sin_ref = ins[5:7]
            partner = jnp.where(first_half, pltpu.roll(y, LANES - ROPE_FREQS, 1),
                                pltpu.roll(y, ROPE_FREQS, 1))
            y = y * cos_ref[...] + partner * sin_ref[...]
        o_ref[:, oc * LANES:(oc + 1) * LANES] = y.astype(o_ref.dtype)
    p2 = _dot(hb, w_ref[:, N_NORM:])
    va_ref[...] = p2[:, 0:KV_A * HD].astype(va_ref.dtype)
    vc_ref[...] = p2[:, KV_A * HD:KV_A * HD + D_C].astype(vc_ref.dtype)
    xb_ref[...] = p2[:, KV_A * HD + D_C:]
    if seq is not None:
        prev = ins[len(ins) - 4 * n_prev:]
        layers = [prev[4 * l:4 * l + 4] for l in range(n_prev)] + [(ka_ref, va_ref, kc_ref, vc_ref)]
        for a, st_ref in enumerate(outs[7:]):
            for l, arrays in enumerate(layers):
                for s in range(tm // seq):
                    rows = arrays[a][s * seq:(s + 1) * seq, :]
                    if a < 3:
                        st_ref[s, l] = rows.T
                    else:
                        for h in range(H_C):
                            st_ref[s, l, pl.ds(h, seq, stride=H_C), :] = rows[:, h * DV_C:(h + 1) * DV_C]


def _front_part(x2d, mod4, ng3, w_front, gains3, rope, *, layer, mod_row0, tokens_per_mod, kv_dtype, tm=TM_FRONT,
                stack_seq=None, prev_caches=()):
    n = x2d.shape[0]
    use_rope = rope is not None
    row = lambda i: (i, 0)
    in_specs = [pl.BlockSpec((tm, D_MODEL), row),
                _mod_spec(layer, mod_row0, tokens_per_mod // tm),
                _layer_spec((1, D_MODEL), layer),
                _layer_spec((D_MODEL, N_NORM + N_PLAIN), layer),
                _layer_spec((1, N_NORM), layer)]
    args = [x2d, mod4, ng3, w_front, gains3]
    if use_rope:
        per_seq = rope[0].shape[0] // tm
        in_specs += [pl.BlockSpec((tm, LANES), lambda i: (i % per_seq, 0))] * 2
        args += list(rope)
    widths = (D_A, KV_A * HD, H_C * 2 * HD, H_C * 2 * HD, KV_A * HD, D_C, D_LRU)
    dtypes = (BF16, kv_dtype, BF16, kv_dtype, kv_dtype, kv_dtype, F32)
    out_shape = [jax.ShapeDtypeStruct((n, w), d) for w, d in zip(widths, dtypes)]
    out_specs = [pl.BlockSpec((tm, w), row) for w in widths]
    if stack_seq is not None:
        assert len(prev_caches) == DEPTH - 1 and tm % stack_seq == 0
        for cache in prev_caches:
            in_specs += [pl.BlockSpec((tm, arr.shape[1]), row) for arr in cache]
            args += list(cache)
        per_tile = tm // stack_seq
        cache_w = (KV_A * HD, KV_A * HD, H_C * 2 * HD, D_C)
        shapes = [(DEPTH, w, stack_seq) for w in cache_w[:3]] + [(DEPTH, stack_seq * H_C, DV_C)]
        out_shape += [jax.ShapeDtypeStruct((n // stack_seq,) + s, F32) for s in shapes]
        out_specs += [pl.BlockSpec((per_tile,) + s, lambda i: (i, 0, 0, 0)) for s in shapes]
    body = functools.partial(_front_body, use_rope=use_rope, n_prev=len(prev_caches), seq=stack_seq)
    return _Part("front_rope" if use_rope else "front", body, (n // tm,), in_specs, args, out_shape, out_specs)


def _attn_a_kernel(*refs, tq, nsub, seq_len, banded, has_ctx, layer):
    if has_ctx:
        sink_ref, q_ref, k_ref, v_ref, kc_ref, vc_ref, o_ref = refs
    else:
        sink_ref, q_ref, k_ref, v_ref, o_ref = refs
    i = pl.program_id(1)
    g = H_A // KV_A

    def spread(k, v):
        k = k.astype(F32)
        v = v.astype(F32)
        lo = lax.broadcasted_iota(jnp.int32, k.shape, 1) < HD
        k_sw = pltpu.roll(k, HD, 1)
        v_sw = pltpu.roll(v, HD, 1)
        kd = (jnp.where(lo, k, k_sw).astype(BF16), jnp.where(lo, k_sw, k).astype(BF16))
        vd = ((jnp.where(lo, v, 1.0).astype(BF16), jnp.where(lo, 1.0, v_sw).astype(BF16)),
              (jnp.where(lo, v_sw, 1.0).astype(BF16), jnp.where(lo, 1.0, v).astype(BF16)))
        return kd, vd

    if has_ctx:
        kd_c, vd_c = spread(kc_ref[...], vc_ref[...])
    nwin = tq + 2 * WINDOW if banded else tq
    lo_q = lax.broadcasted_iota(jnp.int32, (tq, LANES), 1) < HD
    zero = jnp.zeros((tq, LANES), BF16)
    for u in range(nsub):
        blk = i * nsub + u
        if banded:
            start = pl.multiple_of(jnp.clip(blk * tq - WINDOW, 0, seq_len - nwin), WINDOW)
            kd, vd = spread(k_ref[pl.ds(start, nwin), :], v_ref[pl.ds(start, nwin), :])
            kpos = start + lax.broadcasted_iota(jnp.int32, (tq, nwin), 1)
            qpos = blk * tq + lax.broadcasted_iota(jnp.int32, (tq, nwin), 0)
            bias = jnp.where(jnp.abs(kpos - qpos) <= WINDOW, 0.0, -jnp.inf)
        else:
            own = pl.multiple_of(blk * tq, tq)
            kd, vd = spread(k_ref[pl.ds(own, tq), :], v_ref[pl.ds(own, tq), :])
        if has_ctx:
            kd = tuple(jnp.concatenate([kd[kv], kd_c[kv]], axis=0) for kv in range(KV_A))
            vd = tuple(tuple(jnp.concatenate([vd[kv][half], vd_c[kv][half]], axis=0) for half in range(2))
                       for kv in range(KV_A))
        q = q_ref[u * tq:(u + 1) * tq, :]
        outs = []
        for h in range(H_A):
            kv, half = h // g, h % 2
            qz = jnp.where(lo_q if half == 0 else jnp.logical_not(lo_q),
                           q[:, (h // 2) * LANES:(h // 2 + 1) * LANES], zero)
            s = _dot_nt(qz, kd[kv])
            if banded:
                s = jnp.concatenate([s[:, :nwin] + bias, s[:, nwin:]], axis=1)
            snk = sink_ref[layer, h] * LOG2E
            m = jnp.maximum(jnp.max(s, axis=-1, keepdims=True), snk)
            e = jnp.exp2(s - m)
            pv = _dot(e.astype(BF16), vd[kv][half])
            outs.append(pv / (pltpu.roll(pv, HD, 1) + jnp.exp2(snk - m)))
        for c in range(H_A // 2):
            pair = jnp.where(lo_q, outs[2 * c], outs[2 * c + 1])
            o_ref[u * tq:(u + 1) * tq, c * LANES:(c + 1) * LANES] = pair.astype(o_ref.dtype)


def _attn_a(sink, q, k, v, k_ctx, v_ctx, *, layer, tq, nsub, banded):
    b, t, _ = q.shape
    has_ctx = k_ctx is not None
    rows = tq * nsub
    kv_spec = pl.BlockSpec((None, t, KV_A * HD), lambda bi, i: (bi, 0, 0))
    in_specs = [pl.BlockSpec(memory_space=pltpu.SMEM),
                pl.BlockSpec((None, rows, D_A), lambda bi, i: (bi, i, 0)), kv_spec, kv_spec]
    args = [sink, q, k, v]
    if has_ctx:
        s = k_ctx.shape[2]
        in_specs += [pl.BlockSpec((None, None, s, KV_A * HD), lambda bi, i: (bi, layer, 0, 0))] * 2
        args += [k_ctx, v_ctx]
    return pl.pallas_call(
        functools.partial(_attn_a_kernel, tq=tq, nsub=nsub, seq_len=t, banded=banded, has_ctx=has_ctx, layer=layer),
        out_shape=jax.ShapeDtypeStruct((b, t, D_A), BF16),
        grid=(b, t // rows),
        in_specs=in_specs,
        out_specs=pl.BlockSpec((None, rows, D_A), lambda bi, i: (bi, i, 0)),
        compiler_params=_params(("arbitrary", "arbitrary")),
        name="attn_a_latent" if has_ctx else "attn_a_context",
    )(*args)


def _attn_c_kernel(*refs, tq, ts, n_loc, has_ctx, lam_init):
    if has_ctx:
        lq1, lk1, lq2, lk2, sub_ref, q_ref, k_ref, v_ref, kc_ref, vc_ref, o_ref = refs
    else:
        lq1, lk1, lq2, lk2, sub_ref, q_ref, k_ref, v_ref, o_ref = refs
    lam = (jnp.exp(jnp.sum(lq1[...] * lk1[...], axis=-1, keepdims=True))
           - jnp.exp(jnp.sum(lq2[...] * lk2[...], axis=-1, keepdims=True)) + lam_init)
    lo = lax.broadcasted_iota(jnp.int32, (tq, LANES), 1) < HD
    zero = jnp.zeros((tq, LANES), BF16)
    heads = [slice(h * LANES, (h + 1) * LANES) for h in range(H_C)]
    qz = []
    for cols in heads:
        q12 = q_ref[:, cols]
        qz.append(jnp.concatenate([jnp.where(lo, q12, zero), jnp.where(lo, zero, q12)], axis=0))

    def update(carry, kt, vt):
        new = []
        ones = jnp.ones((kt.shape[0], LANES), BF16)
        for h, cols in enumerate(heads):
            m, acc = carry[h]
            s = _dot_nt(qz[h], kt[:, cols].astype(BF16))
            m_new = jnp.maximum(m, jnp.max(s, axis=-1, keepdims=True))
            alpha = jnp.exp2(m - m_new)
            p = jnp.exp2(s - m_new)
            v_ones = jnp.concatenate([vt[:, cols].astype(BF16), ones], axis=1)
            acc = alpha * acc + _dot(p.astype(BF16), v_ones)
            new.append((m_new, acc))
        return tuple(new)

    carry = tuple((jnp.full((2 * tq, 1), -jnp.inf, F32), jnp.zeros((2 * tq, 2 * LANES), F32)) for _ in heads)
    for j in range(n_loc):
        carry = update(carry, k_ref[j * ts:(j + 1) * ts, :], v_ref[j * ts:(j + 1) * ts, :])
    if has_ctx:
        carry = update(carry, kc_ref[...], vc_ref[...])
    for h, cols in enumerate(heads):
        _, acc = carry[h]
        num, den = acc[:, :DV_C], acc[:, DV_C:]
        o = num[:tq] / den[:tq] - lam * (num[tq:] / den[tq:])
        o = o * lax.rsqrt(jnp.mean(o * o, axis=-1, keepdims=True) + EPS) * sub_ref[...] * (1.0 - lam_init)
        o_ref[:, cols] = o.astype(o_ref.dtype)


def _attn_c(lam_vecs, subln3, q, k, v, k_ctx, v_ctx, *, layer, tq, ts, lam_init):
    b, t, _ = q.shape
    s_loc = k.shape[1]
    has_ctx = k_ctx is not None
    kv_spec = pl.BlockSpec((None, s_loc, D_C), lambda bi, i: (bi, 0, 0))
    in_specs = ([_layer_spec((1, HD), layer)] * 4 + [_layer_spec((1, DV_C), layer)]
                + [pl.BlockSpec((None, tq, D_C), lambda bi, i: (bi, i, 0)), kv_spec, kv_spec])
    args = list(lam_vecs) + [subln3, q, k, v]
    if has_ctx:
        sc = k_ctx.shape[2]
        in_specs += [pl.BlockSpec((None, None, sc, D_C), lambda bi, i: (bi, layer, 0, 0))] * 2
        args += [k_ctx, v_ctx]
    return pl.pallas_call(
        functools.partial(_attn_c_kernel, tq=tq, ts=ts, n_loc=s_loc // ts, has_ctx=has_ctx, lam_init=lam_init),
        out_shape=jax.ShapeDtypeStruct((b, t, D_C), BF16),
        grid=(b, t // tq),
        in_specs=in_specs,
        out_specs=pl.BlockSpec((None, tq, D_C), lambda bi, i: (bi, i, 0)),
        compiler_params=_params(("arbitrary", "arbitrary")),
        name="attn_c_latent" if has_ctx else "attn_c_context",
    )(*args)


def _lru_body(ins, outs, scratch, ids, *, tc, nc):
    xc_ref, xp_ref, xn_ref, cw_ref, cb_ref, wg_ref, bg_ref, lam_ref, h0_ref, perm_ref, inv_ref = ins
    ob_ref, st_ref = outs
    a_s, b_s, hs, ps, hf_s, xb_s, hcar, last_h, last_p, cin = scratch
    _, p, c = ids
    cidx = c + p * (nc - 1 - 2 * c)
    r0 = pl.multiple_of(cidx * tc, tc)
    ng = tc // SUBLANES

    @pl.when(p == 0)
    def _():
        cur = xc_ref[...]
        prev = jnp.where(cidx > 0, xp_ref[...], 0.0)
        nxt = jnp.where(cidx < nc - 1, xn_ref[...], 0.0)
        hi = cur.astype(BF16)
        rest = cur - hi.astype(F32)
        mid = rest.astype(BF16)
        lo = (rest - mid.astype(F32)).astype(BF16)
        x3 = (_dot(perm_ref[...], hi) + _dot(perm_ref[...], mid) + _dot(perm_ref[...], lo)).reshape(ng, SUBLANES, D_LRU)
        rows = lax.broadcasted_iota(jnp.int32, (SUBLANES, D_LRU), 0)
        before1 = jnp.where(rows == 0, prev[SUBLANES - 1:SUBLANES, :], pltpu.roll(x3[ng - 1], 1, 0))
        before2 = jnp.where(rows == 0, prev[SUBLANES - 2:SUBLANES - 1, :], pltpu.roll(x3[ng - 2], 1, 0))
        after1 = jnp.where(rows == SUBLANES - 1, nxt[0:1, :], pltpu.roll(x3[0], SUBLANES - 1, 0))
        xpad = jnp.concatenate([before2[None], before1[None], x3, after1[None]], axis=0)
        acc = jnp.broadcast_to(cb_ref[...], (ng, SUBLANES, D_LRU))
        for j in range(CONV_W):
            acc = acc + xpad[j:j + ng] * cw_ref[j:j + 1, :]
        xb_s[pl.ds(r0, tc), :] = acc.reshape(tc, D_LRU)

    xb = xb_s[pl.ds(r0, tc), :]
    gm = _dot(xb.astype(BF16), wg_ref[...]) + bg_ref[...]
    nl = -lam_ref[...]
    softplus = jnp.maximum(nl, 0.0) + jnp.log1p(jnp.exp(-jnp.abs(nl)))
    half_decay = (0.5 * LRU_C) * softplus
    rd = jnp.tanh(gm[:, :D_LRU]) * half_decay + half_decay
    half_xb = 0.5 * xb
    gated = jnp.tanh(gm[:, D_LRU:]) * half_xb + half_xb
    a = jnp.exp2(rd * (-LOG2E))
    one_m_a2 = jnp.tanh(rd) * (a * a + 1.0)
    bb = one_m_a2 * lax.rsqrt(jnp.maximum(one_m_a2, F32_TINY)) * gated

    a_s[...] = a
    b_s[...] = bb

    def step(gi, carry):
        h, pc = carry
        g = gi + p * (ng - 1 - 2 * gi)
        base = pl.multiple_of(g * SUBLANES, SUBLANES)
        a_g = a_s[pl.ds(base, SUBLANES), :]
        h = a_g * h + b_s[pl.ds(base, SUBLANES), :]
        pc = a_g * pc
        hs[pl.ds(base, SUBLANES), :] = h
        ps[pl.ds(base, SUBLANES), :] = pc
        return h, pc

    init = (jnp.zeros((SUBLANES, D_LRU), F32), jnp.ones((SUBLANES, D_LRU), F32))
    last_h[...], last_p[...] = lax.fori_loop(0, ng, step, init, unroll=True)

    @pl.when(c == 0)
    def _():
        hcar[0:1, :] = h0_ref[pl.ds(p, 1), :]

    state = hcar[0:1, :]
    for rr in range(SUBLANES):
        r_in = rr + p * (SUBLANES - 1 - 2 * rr)
        cin[pl.ds(r_in, 1), :] = state
        state = last_p[pl.ds(r_in, 1), :] * state + last_h[pl.ds(r_in, 1), :]
    hcar[0:1, :] = state
    hfull = (hs[...].reshape(ng, SUBLANES, D_LRU)
             + ps[...].reshape(ng, SUBLANES, D_LRU) * cin[...]).reshape(tc, D_LRU)

    @pl.when(p == 0)
    def _():
        hf_s[pl.ds(r0, tc), :] = hfull

    @pl.when(p == 1)
    def _():
        both = (hf_s[pl.ds(r0, tc), :] + hfull).astype(BF16)
        ob_ref[...] = _dot(inv_ref[...], both).astype(ob_ref.dtype)

    @pl.when(c == nc - 1)
    def _():
        st_ref[pl.ds(p, 1), :] = state


def _lru_part(xb, conv_w, conv_b3, wg, bg, lam4, h0, *, layer, h0_layer, tc):
    b, t, _ = xb.shape
    nc = t // tc
    nb8 = t // SUBLANES
    per = tc // SUBLANES
    cidx = lambda p, c: c + p * (nc - 1 - 2 * c)
    in_specs = [
        pl.BlockSpec((None, tc, D_LRU), lambda bi, p, c: (bi, cidx(p, c), 0)),
        pl.BlockSpec((None, SUBLANES, D_LRU), lambda bi, p, c: (bi, jnp.maximum(cidx(p, c) * per - 1, 0), 0)),
        pl.BlockSpec((None, SUBLANES, D_LRU), lambda bi, p, c: (bi, jnp.minimum((cidx(p, c) + 1) * per, nb8 - 1), 0)),
        _layer_spec((CONV_W, D_LRU), layer),
        _layer_spec((1, D_LRU), layer),
        pl.BlockSpec((None, None, D_LRU, 2 * D_LRU), lambda bi, p, c: (layer, p, 0, 0)),
        pl.BlockSpec((None, None, 1, 2 * D_LRU), lambda bi, p, c: (layer, p, 0, 0)),
        pl.BlockSpec((None, None, 1, D_LRU), lambda bi, p, c: (layer, p, 0, 0)),
        pl.BlockSpec((None, None, 2, D_LRU), lambda bi, p, c: (bi, h0_layer, 0, 0)),
        pl.BlockSpec((tc, tc), lambda bi, p, c: (0, 0)),
        pl.BlockSpec((tc, tc), lambda bi, p, c: (0, 0)),
    ]
    j = jnp.arange(tc)
    src = (j % SUBLANES) * (tc // SUBLANES) + j // SUBLANES
    perm = (src[:, None] == j[None, :]).astype(BF16)
    chunk = (tc, D_LRU)
    row8 = (SUBLANES, D_LRU)
    return _Part("lru", functools.partial(_lru_body, tc=tc, nc=nc), (b, 2, nc), in_specs,
                 [xb, xb, xb, conv_w, conv_b3, wg, bg, lam4, h0, perm, perm.T],
                 [jax.ShapeDtypeStruct((b, t, D_LRU), BF16), jax.ShapeDtypeStruct((b, 2, D_LRU), F32)],
                 [pl.BlockSpec((None, tc, D_LRU), lambda bi, p, c: (bi, nc - 1 - p * c, 0)),
                  pl.BlockSpec((None, 2, D_LRU), lambda bi, p, c: (bi, 0, 0))],
                 [pltpu.VMEM(chunk, F32), pltpu.VMEM(chunk, F32), pltpu.VMEM(chunk, F32), pltpu.VMEM(chunk, F32),
                  pltpu.VMEM((t, D_LRU), F32), pltpu.VMEM((t, D_LRU), F32),
                  pltpu.VMEM(row8, F32), pltpu.VMEM(row8, F32), pltpu.VMEM(row8, F32), pltpu.VMEM(row8, F32)])


def _tail_kernel(x_ref, mod_ref, ng_ref, oa_ref, ob_ref, oc_ref, wgm_ref, wbr_ref, wout_ref, o_ref):
    x = x_ref[...]
    mod = mod_ref[...]
    hb = _normed_input(x, mod, ng_ref[...]).astype(BF16)
    y = None
    for br, br_ref in enumerate((oa_ref, ob_ref, oc_ref)):
        gm = _dot(hb, wgm_ref[:, br * N_BRANCH_COLS:(br + 1) * N_BRANCH_COLS])
        gt = gm[:, :D_A]
        u = (br_ref[...].astype(F32) * (gt * _sigmoid(gt))).astype(BF16)
        proj = _dot(u, wbr_ref[br])
        mg = _sigmoid(gm[:, D_A:])
        y = mg * proj if y is None else y + mg * proj
    gate = mod[:, 2 * D_MODEL:]
    o_ref[...] = x + gate * _dot(y.astype(BF16), wout_ref[...])


def _tail_part(x2d, mod4, ng3, oa, ob, oc, wgm, wbr, wout, *, layer, mod_row0, tokens_per_mod, tm=TM_TAIL):
    n = x2d.shape[0]
    row = lambda i: (i, 0)
    once = dict(pipeline_mode=pl.Buffered(1))

    def weight(shape):
        zeros = (0,) * len(shape)
        return pl.BlockSpec((None,) + shape, lambda i: (layer,) + zeros, **once)

    in_specs = [pl.BlockSpec((tm, D_MODEL), row),
                _mod_spec(layer, mod_row0, tokens_per_mod // tm),
                _layer_spec((1, D_MODEL), layer),
                pl.BlockSpec((tm, D_A), row),
                pl.BlockSpec((tm, D_LRU), row),
                pl.BlockSpec((tm, D_C), row),
                weight((D_MODEL, 3 * N_BRANCH_COLS)),
                weight((3, D_A, D_MODEL)),
                weight((D_MODEL, D_MODEL))]
    return _Part("tail", lambda ins, outs, scratch, ids: _tail_kernel(*ins, *outs), (n // tm,), in_specs,
                 [x2d, mod4, ng3, oa, ob, oc, wgm, wbr, wout],
                 [jax.ShapeDtypeStruct((n, D_MODEL), F32)], [pl.BlockSpec((tm, D_MODEL), row)])


def _rope_tables(seq_len):
    pos = jnp.arange(seq_len)
    row = (pos // GRID_W).astype(F32)
    col = (pos % GRID_W).astype(F32)
    inv = jnp.power(ROPE_BASE, -jnp.arange(ROPE_FREQS, dtype=F32) / ROPE_FREQS)
    ang_r = row[:, None] * inv
    ang_c = col[:, None] * inv
    cos = jnp.concatenate([jnp.cos(ang_r)] * 2 + [jnp.cos(ang_c)] * 2, axis=-1)
    sin = jnp.concatenate([-jnp.sin(ang_r), jnp.sin(ang_r), -jnp.sin(ang_c), jnp.sin(ang_c)], axis=-1)
    return jnp.tile(cos, (1, LANES // HD)), jnp.tile(sin, (1, LANES // HD))


def _block_diag(w):
    rows = w.reshape(w.shape[:-3] + (D_LRU, LRU_BW))
    tiled = jnp.tile(rows, (1,) * (rows.ndim - 1) + (LRU_BLOCKS,))
    blk = jnp.arange(D_LRU) // LRU_BW
    return jnp.where(blk[:, None] == blk[None, :], tiled, 0.0)


def kernel(x_prompt, x_sample, cache_a_k, cache_a_v, cache_c_k, cache_c_v, state_lru, c, c_ctx, norm_g, mod_w, mod_b, w_in, qn_a, kn_a, sink_a, conv_w, conv_b, lru_wa, lru_ba, lru_wx, lru_bx, lru_lam, qn_c, kn_c, lam_q1, lam_k1, lam_q2, lam_k2, subln_c, w_br_a, w_br_b, w_br_c, w_out):
    bp, sp, _ = x_prompt.shape
    bs, ss, _ = x_sample.shape
    past = cache_a_k.shape[2]

    cvecs = jnp.concatenate([c, c_ctx[None, :], jnp.zeros((SUBLANES - bs - 1, D_MODEL), F32)], axis=0)
    mod = _modulation(cvecs, mod_w, mod_b)
    mod4 = mod.reshape(DEPTH, SUBLANES, 1, 3 * D_MODEL)
    rope = _rope_tables(ss)

    w_front = _permute_cast(w_in, FRONT_BLOCKS)
    wgm = _permute_cast(w_in, TAIL_BLOCKS)
    wbr = jnp.stack([w_br_a, w_br_b, w_br_c], axis=1).astype(BF16)
    wout = w_out.astype(BF16)
    gains3 = jnp.concatenate([jnp.tile(qn_a * Q_SCALE, (1, H_A)), jnp.tile(qn_c * Q_SCALE, (1, 2 * H_C)),
                              jnp.tile(kn_c, (1, 2 * H_C)), jnp.tile(kn_a, (1, KV_A))],
                             axis=-1)[:, None, :]
    ng3 = norm_g[:, None, :]
    lru_wg = (0.5 * jnp.concatenate([_block_diag(lru_wa), _block_diag(lru_wx)], axis=-1)).astype(BF16)
    lru_bg = 0.5 * jnp.concatenate([lru_ba, lru_bx], axis=-1)[:, :, None, :]
    lru_lam4 = lru_lam[:, :, None, :]
    lru_p = (conv_w, conv_b[:, None, :], lru_wg, lru_bg, lru_lam4)
    lam_vecs = [v[:, None, :] for v in (lam_q1, lam_k1, lam_q2, lam_k2)]
    subln3 = subln_c[:, None, :]
    cka = cache_a_k.reshape(bs, DEPTH, past, KV_A * HD)
    cva = cache_a_v.reshape(bs, DEPTH, past, KV_A * HD)
    ckc = cache_c_k.reshape(bs, DEPTH, past, D_C)
    cvc = cache_c_v.reshape(bs, DEPTH, past, D_C)
    zeros_h0 = jnp.zeros((bp, 1, 2, D_LRU), F32)

    xp = x_prompt.reshape(bp * sp, D_MODEL)
    xs = x_sample.reshape(bs * ss, D_MODEL)
    new_caches, new_states = [], []
    for l in range(DEPTH):
        lam_init = 0.8 - 0.6 * math.exp(-0.3 * l)
        ctx_mod = dict(layer=l, mod_row0=bs, tokens_per_mod=bp * sp)
        lat_mod = dict(layer=l, mod_row0=0, tokens_per_mod=ss)

        flat = lambda a: a.reshape(-1, a.shape[-1])
        r3 = lambda a: a.reshape(bp, sp, a.shape[-1])
        stacking = dict(stack_seq=sp, prev_caches=new_caches) if l == DEPTH - 1 else {}
        qa, ka, qc, kc, va, vc, xb, *stacked = _run(
            _front_part(xp, mod4, ng3, w_front, gains3, None, kv_dtype=F32, **ctx_mod, **stacking))
        new_caches.append((ka, va, kc, vc))
        oa = _attn_a(sink_a, r3(qa), r3(ka), r3(va), None, None, layer=l, tq=sp, nsub=1, banded=False)
        ob, st = _run(_lru_part(r3(xb), *lru_p, zeros_h0, layer=l, h0_layer=0, tc=sp))
        new_states.append(st)
        oc = _attn_c(lam_vecs, subln3, r3(qc), r3(kc), r3(vc), None, None, layer=l, tq=sp, ts=sp, lam_init=lam_init)
        (xp,) = _run(_tail_part(xp, mod4, ng3, flat(oa), flat(ob), flat(oc), wgm, wbr, wout, **ctx_mod))

        r3 = lambda a: a.reshape(bs, ss, a.shape[-1])
        qa, ka, qc, kc, va, vc, xb = _run(_front_part(xs, mod4, ng3, w_front, gains3, rope, kv_dtype=BF16, **lat_mod))
        oa = _attn_a(sink_a, r3(qa), r3(ka), r3(va), cka, cva, layer=l, tq=TQ_A, nsub=NSUB_A, banded=True)
        ob, _ = _run(_lru_part(r3(xb), *lru_p, state_lru, layer=l, h0_layer=l, tc=TC_LRU))
        oc = _attn_c(lam_vecs, subln3, r3(qc), r3(kc), r3(vc), ckc, cvc, layer=l, tq=TQ_C, ts=TS_C, lam_init=lam_init)
        (xs,) = _run(_tail_part(xs, mod4, ng3, flat(oa), flat(ob), flat(oc), wgm, wbr, wout, **lat_mod))

    ka_t, va_t, kc_t, vc = stacked
    return (xp.reshape(bp, sp, D_MODEL), xs.reshape(bs, ss, D_MODEL),
            ka_t.reshape(bp, DEPTH, KV_A, HD, sp).transpose(0, 1, 4, 2, 3),
            va_t.reshape(bp, DEPTH, KV_A, HD, sp).transpose(0, 1, 4, 2, 3),
            kc_t.reshape(bp, DEPTH, H_C, 2, HD, sp).transpose(0, 1, 5, 2, 3, 4),
            vc.reshape(bp, DEPTH, sp, H_C, DV_C),
            jnp.stack(new_states, axis=1))
```
